```python
import math
import jax, jax.numpy as jnp
from jax import lax
import numpy as np

D_MODEL = 2048
BATCH = 1
SEQ = 8192
DEPTH = 1
DEC_BATCH = 32
DEC_SEQ = 1
PAST_LEN = 8192
PAGE_SIZE = 128

N_HEADS = 8
HEAD_DIM = 128
ATTN_WIDTH = N_HEADS * HEAD_DIM
MOBA_BLOCK = 256
MOBA_TOPK = 3
Q_BLOCK = 64
D_CONV = 1024
CONV_WIDTH = 31
CONV_STATE = CONV_WIDTH - 1
N_MEM = 256
CROSS_HEADS = 4
CROSS_HEAD_DIM = 256
CROSS_WIDTH = CROSS_HEADS * CROSS_HEAD_DIM
N_BRANCH = 3
BRANCH_WIDTH = 1024
NUM_BUCKETS = 32
MAX_DISTANCE = 128
D_FF = -(-8 * D_MODEL // (3 * 256)) * 256
DEEPNORM_ALPHA = (2 * DEPTH) ** 0.25
DEEPNORM_BETA = (8 * DEPTH) ** -0.25
LN_EPS = 1e-5
IN_SIZES = (D_CONV, D_CONV, ATTN_WIDTH, ATTN_WIDTH, ATTN_WIDTH, CROSS_WIDTH, N_BRANCH * D_MODEL)
IN_COLS = sum(IN_SIZES)

kernel_name = "hybrid_conv_moba_memory_decoder_step"


def layer_norm(x, g, b):
    xf = x.astype(jnp.float32)
    mu = jnp.mean(xf, axis=-1, keepdims=True)
    var = jnp.mean(jnp.square(xf - mu), axis=-1, keepdims=True)
    return ((xf - mu) * lax.rsqrt(var + LN_EPS) * g + b).astype(x.dtype)


def t5_bucket(dist):
    max_exact = NUM_BUCKETS // 2
    d = jnp.maximum(dist, 0)
    large = max_exact + (jnp.log(jnp.maximum(d, 1).astype(jnp.float32) / max_exact)
                         / math.log(MAX_DISTANCE / max_exact)
                         * (NUM_BUCKETS - max_exact)).astype(jnp.int32)
    large = jnp.minimum(large, NUM_BUCKETS - 1)
    return jnp.where(d < max_exact, d, large)


def project_in(h, w_in):
    B, T = h.shape[0], h.shape[1]
    z = h @ w_in
    cuts = [int(c) for c in np.cumsum(IN_SIZES)[:-1]]
    a, b, q, k, v, qc, g = jnp.split(z, cuts, axis=-1)
    u = a * jax.nn.sigmoid(b)
    q = q.reshape(B, T, N_HEADS, HEAD_DIM)
    k = k.reshape(B, T, N_HEADS, HEAD_DIM)
    v = v.reshape(B, T, N_HEADS, HEAD_DIM)
    qc = qc.reshape(B, T, CROSS_HEADS, CROSS_HEAD_DIM)
    gates = jax.nn.sigmoid(g.reshape(B, T, N_BRANCH, D_MODEL))
    return u, q, k, v, qc, gates


def conv_branch(u, left, conv_w, conv_b, ln_g, ln_b):
    full = jnp.concatenate([left, u], axis=1)
    y = lax.conv_general_dilated(full, conv_w[:, None, :], window_strides=(1,), padding='VALID',
                                 dimension_numbers=('NWC', 'WIO', 'NWC'),
                                 feature_group_count=D_CONV) + conv_b
    y = jax.nn.silu(layer_norm(y, ln_g, ln_b))
    return y, full[:, -CONV_STATE:]


def pad_to_blocks(parts):
    L = sum(p.shape[1] for p in parts)
    nb = max(-(-L // MOBA_BLOCK), MOBA_TOPK)
    p0 = parts[0]
    zeros = jnp.zeros((p0.shape[0], nb * MOBA_BLOCK - L) + p0.shape[2:], p0.dtype)
    full = jnp.concatenate(list(parts) + [zeros], axis=1)
    return full.reshape((p0.shape[0], nb, MOBA_BLOCK) + p0.shape[2:])


def moba_attention(q, kb, vb, q_pos0, rel_bias):
    B, Tq, H, dh = q.shape
    nb = kb.shape[1]
    k_mean = jnp.mean(kb, axis=2, dtype=jnp.float32)
    qb = min(Q_BLOCK, Tq)
    n_qb = -(-Tq // qb)
    qpad = jnp.pad(q, ((0, 0), (0, n_qb * qb - Tq), (0, 0), (0, 0)))
    q_blocks = qpad.reshape(B, n_qb, qb, H, dh).swapaxes(0, 1)
    pos = (q_pos0 + jnp.arange(n_qb * qb, dtype=jnp.int32)).reshape(n_qb, qb)
    b_idx = jnp.arange(B)[:, None, None, None]
    h_idx = jnp.arange(H)[None, None, :, None]
    bias_hn = rel_bias.T
    slot = jnp.arange(MOBA_TOPK + 1)[None, None, None, :, None]
    offs = jnp.arange(MOBA_BLOCK, dtype=jnp.int32)
    scale = HEAD_DIM ** -0.5

    def attend_block(args):
        qc, t = args
        own = jnp.minimum(t // MOBA_BLOCK, nb - 1)
        gate = jnp.einsum('bqhd,bnhd->bqhn', qc.astype(jnp.float32), k_mean)
        fully_past = jnp.arange(nb)[None, :] < own[:, None]
        gate = jnp.where(fully_past[None, :, None, :], gate, -jnp.inf)
        _, sel = lax.top_k(gate, MOBA_TOPK)
        own_b = jnp.broadcast_to(own[None, :, None, None], (B, qb, H, 1))
        blocks = jnp.concatenate([sel.astype(jnp.int32), own_b.astype(jnp.int32)], axis=-1)
        ks = kb[b_idx, blocks, :, h_idx, :]
        vs = vb[b_idx, blocks, :, h_idx, :]
        kpos = blocks[..., None] * MOBA_BLOCK + offs
        tq = t[None, :, None, None, None]
        valid = jnp.where(slot == MOBA_TOPK, kpos <= tq, slot < own[None, :, None, None, None])
        logits = jnp.einsum('bqhd,bqhjsd->bqhjs', qc, ks,
                            preferred_element_type=jnp.float32) * scale
        logits = logits + bias_hn[h_idx[..., None], t5_bucket(tq - kpos)]
        logits = jnp.where(valid, logits, -jnp.inf)
        p = jax.nn.softmax(logits.reshape(B, qb, H, -1), axis=-1).reshape(logits.shape)
        o = jnp.einsum('bqhjs,bqhjsd->bqhd', p.astype(vs.dtype), vs,
                       preferred_element_type=jnp.float32)
        return o.astype(q.dtype)

    out = lax.map(attend_block, (q_blocks, pos))
    return out.swapaxes(0, 1).reshape(B, n_qb * qb, H, dh)[:, :Tq]


def memory_kv(mem, w_mem_kv):
    B = mem.shape[0]
    mk, mv = jnp.split(mem @ w_mem_kv, 2, axis=-1)
    return (mk.reshape(B, N_MEM, CROSS_HEADS, CROSS_HEAD_DIM),
            mv.reshape(B, N_MEM, CROSS_HEADS, CROSS_HEAD_DIM))


def cross_attention(qc, mem_k, mem_v):
    s = jnp.einsum('bthd,bmhd->bhtm', qc, mem_k,
                   preferred_element_type=jnp.float32) * CROSS_HEAD_DIM ** -0.5
    p = jax.nn.softmax(s, axis=-1)
    o = jnp.einsum('bhtm,bmhd->bthd', p.astype(mem_v.dtype), mem_v,
                   preferred_element_type=jnp.float32)
    return o.astype(qc.dtype)


def finish_layer(h, conv_y, attn, cross, gates, w_branch, w_out, ln1_g, ln1_b,
                 w_gate, w_up, w_down, ln2_g, ln2_b):
    B, T = h.shape[0], h.shape[1]
    br = jnp.stack([conv_y, attn.reshape(B, T, ATTN_WIDTH), cross.reshape(B, T, CROSS_WIDTH)], axis=2)
    proj = jnp.einsum('btnc,ncd->btnd', br, w_branch)
    mix = jnp.sum(gates * proj, axis=2) @ w_out
    h = layer_norm(DEEPNORM_ALPHA * h + mix, ln1_g, ln1_b)
    ffn = (jax.nn.silu(h @ w_gate) * (h @ w_up)) @ w_down
    return layer_norm(DEEPNORM_ALPHA * h + ffn, ln2_g, ln2_b)


def setup_inputs(seed: int = 0) -> dict:
    key = jax.random.key(seed)
    ks = jax.random.split(key, 32)
    n_pages = PAST_LEN // PAGE_SIZE
    n_used = DEC_BATCH * n_pages
    n_pool = -(-5 * n_used // 4)
    f32 = jnp.float32
    nrm = lambda k, s, sc: jax.random.normal(k, s, f32) * sc
    page_table = jax.random.permutation(ks[0], n_pool)[:n_used].reshape(DEC_BATCH, n_pages).astype(jnp.int32)
    return {
        "x_prompt": nrm(ks[1], (BATCH, SEQ, D_MODEL), 1.0),
        "x_sample": nrm(ks[2], (DEC_BATCH, DEC_SEQ, D_MODEL), 1.0),
        "mem_prompt": nrm(ks[3], (BATCH, N_MEM, D_MODEL), 1.0),
        "cache_k": nrm(ks[4], (DEPTH, n_pool, PAGE_SIZE, N_HEADS, HEAD_DIM), 1.0),
        "cache_v": nrm(ks[5], (DEPTH, n_pool, PAGE_SIZE, N_HEADS, HEAD_DIM), 1.0),
        "state_conv": nrm(ks[6], (DEPTH, DEC_BATCH, CONV_STATE, D_CONV), 0.5),
        "cache_mem_k": nrm(ks[7], (DEPTH, DEC_BATCH, N_MEM, CROSS_HEADS, CROSS_HEAD_DIM), 1.0),
        "cache_mem_v": nrm(ks[8], (DEPTH, DEC_BATCH, N_MEM, CROSS_HEADS, CROSS_HEAD_DIM), 1.0),
        "page_table": page_table,
        "w_in": nrm(ks[9], (DEPTH, D_MODEL, IN_COLS), D_MODEL ** -0.5),
        "conv_w": nrm(ks[10], (DEPTH, CONV_WIDTH, D_CONV), CONV_WIDTH ** -0.5),
        "conv_b": nrm(ks[11], (DEPTH, D_CONV), 0.01),
        "conv_ln_g": 1.0 + nrm(ks[12], (DEPTH, D_CONV), 0.01),
        "conv_ln_b": nrm(ks[13], (DEPTH, D_CONV), 0.01),
        "w_mem_kv": nrm(ks[14], (DEPTH, D_MODEL, 2 * CROSS_WIDTH), D_MODEL ** -0.5),
        "rel_bias": nrm(ks[15], (NUM_BUCKETS, N_HEADS), 0.1),
        "w_branch": nrm(ks[16], (DEPTH, N_BRANCH, BRANCH_WIDTH, D_MODEL), BRANCH_WIDTH ** -0.5 * DEEPNORM_BETA),
        "w_out": nrm(ks[17], (DEPTH, D_MODEL, D_MODEL), D_MODEL ** -0.5 * DEEPNORM_BETA),
        "ln1_g": 1.0 + nrm(ks[18], (DEPTH, D_MODEL), 0.01),
        "ln1_b": nrm(ks[19], (DEPTH, D_MODEL), 0.01),
        "w_gate": nrm(ks[20], (DEPTH, D_MODEL, D_FF), D_MODEL ** -0.5 * DEEPNORM_BETA),
        "w_up": nrm(ks[21], (DEPTH, D_MODEL, D_FF), D_MODEL ** -0.5 * DEEPNORM_BETA),
        "w_down": nrm(ks[22], (DEPTH, D_FF, D_MODEL), D_FF ** -0.5 * DEEPNORM_BETA),
        "ln2_g": 1.0 + nrm(ks[23], (DEPTH, D_MODEL), 0.01),
        "ln2_b": nrm(ks[24], (DEPTH, D_MODEL), 0.01),
    }


def reference(x_prompt, x_sample, mem_prompt, cache_k, cache_v, state_conv, cache_mem_k, cache_mem_v,
              page_table, w_in, conv_w, conv_b, conv_ln_g, conv_ln_b, w_mem_kv, rel_bias, w_branch,
              w_out, ln1_g, ln1_b, w_gate, w_up, w_down, ln2_g, ln2_b):
    hp, hs = x_prompt, x_sample
    kp_l, vp_l, cp_l, mkp_l, mvp_l, ks_l, vs_l, cs_l = [], [], [], [], [], [], [], []
    for l in range(DEPTH):
        u, q, k, v, qc, gates = project_in(hp, w_in[l])
        left0 = jnp.zeros((hp.shape[0], CONV_STATE, D_CONV), u.dtype)
        conv_y, conv_new = conv_branch(u, left0, conv_w[l], conv_b[l], conv_ln_g[l], conv_ln_b[l])
        attn = moba_attention(q, pad_to_blocks([k]), pad_to_blocks([v]), 0, rel_bias)
        mk, mv = memory_kv(mem_prompt, w_mem_kv[l])
        cross = cross_attention(qc, mk, mv)
        hp = finish_layer(hp, conv_y, attn, cross, gates, w_branch[l], w_out[l], ln1_g[l], ln1_b[l],
                          w_gate[l], w_up[l], w_down[l], ln2_g[l], ln2_b[l])
        kp_l.append(k); vp_l.append(v); cp_l.append(conv_new); mkp_l.append(mk); mvp_l.append(mv)

        u, q, k, v, qc, gates = project_in(hs, w_in[l])
        conv_y, conv_new = conv_branch(u, state_conv[l], conv_w[l], conv_b[l], conv_ln_g[l], conv_ln_b[l])
        nbatch = hs.shape[0]
        past_k = cache_k[l][page_table].reshape(nbatch, -1, N_HEADS, HEAD_DIM)
        past_v = cache_v[l][page_table].reshape(nbatch, -1, N_HEADS, HEAD_DIM)
        past_len = past_k.shape[1]
        attn = moba_attention(q, pad_to_blocks([past_k, k]), pad_to_blocks([past_v, v]), past_len, rel_bias)
        cross = cross_attention(qc, cache_mem_k[l], cache_mem_v[l])
        hs = finish_layer(hs, conv_y, attn, cross, gates, w_branch[l], w_out[l], ln1_g[l], ln1_b[l],
                          w_gate[l], w_up[l], w_down[l], ln2_g[l], ln2_b[l])
        ks_l.append(k); vs_l.append(v); cs_l.append(conv_new)

    new_k_prompt = jnp.stack(kp_l)
    new_v_prompt = jnp.stack(vp_l)
    new_conv_prompt = jnp.stack(cp_l)
    new_mem_k_prompt = jnp.stack(mkp_l)
    new_mem_v_prompt = jnp.stack(mvp_l)
    new_k_sample = jnp.stack(ks_l)
    new_v_sample = jnp.stack(vs_l)
    new_conv_sample = jnp.stack(cs_l)
    return (hp, hs, new_k_prompt, new_v_prompt, new_conv_prompt, new_mem_k_prompt, new_mem_v_prompt,
            new_k_sample, new_v_sample, new_conv_sample)
```

```python
import functools
import math

import numpy as np
import jax
import jax.numpy as jnp
from jax import lax
from jax.experimental import pallas as pl
from jax.experimental.pallas import tpu as pltpu

F32 = jnp.float32
BF16 = jnp.bfloat16

N_HEADS = 8
HEAD_DIM = 128
ATTN_WIDTH = N_HEADS * HEAD_DIM
MOBA_BLOCK = 256
MOBA_TOPK = 3
D_CONV = 1024
CONV_WIDTH = 31
CONV_STATE = CONV_WIDTH - 1
CROSS_HEADS = 4
CROSS_HEAD_DIM = 256
CROSS_WIDTH = CROSS_HEADS * CROSS_HEAD_DIM
N_BRANCH = 3
BRANCH_WIDTH = 1024
NUM_BUCKETS = 32
MAX_EXACT = NUM_BUCKETS // 2
MAX_DISTANCE = 128
LN_EPS = 1e-5
MASKED = -1e30
LANES = 128
MIB = 2 ** 20

NT_DIMS = (((1,), (1,)), ((), ()))


def _params(semantics, vmem_mib):
    return pltpu.CompilerParams(dimension_semantics=semantics,
                                vmem_limit_bytes=vmem_mib * MIB)


def _layer_norm(y, g, b):
    mu = jnp.mean(y, axis=-1, keepdims=True)
    yc = y - mu
    var = jnp.mean(yc * yc, axis=-1, keepdims=True)
    return yc * lax.rsqrt(var + LN_EPS) * g + b


def _proj_body(*refs, n_w, epilogue):
    x_ref = refs[0]
    w_refs = refs[1:1 + n_w]
    o_refs = refs[1 + n_w:]
    x = x_ref[...]
    z = [jnp.dot(x, w[...], preferred_element_type=F32) for w in w_refs]
    if epilogue == "glu":
        y = z[0] * jax.nn.sigmoid(z[1])
    elif epilogue == "swiglu":
        y = jax.nn.silu(z[0]) * z[1]
    elif epilogue == "sigmoid":
        y = jax.nn.sigmoid(z[0])
    else:
        y = z[0]
    for o in o_refs:
        o[...] = y.astype(o.dtype)


def _proj(x, ws, epilogue, out_dtypes, bm=1024, bn=512, vmem_mib=48):
    m, k = x.shape
    n = ws[0].shape[1]
    bm = min(bm, m)
    bn = min(bn, n)
    while n % bn:
        bn //= 2
    assert m % bm == 0 and bn % LANES == 0
    outs = pl.pallas_call(
        functools.partial(_proj_body, n_w=len(ws), epilogue=epilogue),
        grid=(m // bm, n // bn),
        in_specs=[pl.BlockSpec((bm, k), lambda i, j: (i, 0))]
        + [pl.BlockSpec((k, bn), lambda i, j: (0, j)) for _ in ws],
        out_specs=[pl.BlockSpec((bm, bn), lambda i, j: (i, j)) for _ in out_dtypes],
        out_shape=[jax.ShapeDtypeStruct((m, n), dt) for dt in out_dtypes],
        compiler_params=_params(("parallel", "parallel"), vmem_mib),
        name="proj_" + epilogue,
    )(x, *ws)
    return outs[0] if len(outs) == 1 else outs


def _mm_res_ln_body(a_ref, w_ref, r_ref, g_ref, b_ref, *rest, alpha, n_out):
    o_refs = rest[:n_out]
    acc_ref = rest[n_out]
    kk = pl.program_id(1)

    @pl.when(kk == 0)
    def _():
        acc_ref[...] = jnp.zeros_like(acc_ref)

    acc_ref[...] += jnp.dot(a_ref[...], w_ref[...], preferred_element_type=F32)

    @pl.when(kk == pl.num_programs(1) - 1)
    def _():
        y = alpha * r_ref[...] + acc_ref[...]
        out = _layer_norm(y, g_ref[...], b_ref[...])
        for o in o_refs:
            o[...] = out.astype(o.dtype)


def _mm_res_ln(a, w, res, g, b, alpha, out_dtypes, bm, bk, vmem_mib=48):
    m, k = a.shape
    n = w.shape[1]
    bm = min(bm, m)
    assert m % bm == 0 and k % bk == 0
    outs = pl.pallas_call(
        functools.partial(_mm_res_ln_body, alpha=alpha, n_out=len(out_dtypes)),
        grid=(m // bm, k // bk),
        in_specs=[
            pl.BlockSpec((bm, bk), lambda i, kk: (i, kk)),
            pl.BlockSpec((bk, n), lambda i, kk: (kk, 0)),
            pl.BlockSpec((bm, n), lambda i, kk: (i, 0)),
            pl.BlockSpec((1, n), lambda i, kk: (0, 0)),
            pl.BlockSpec((1, n), lambda i, kk: (0, 0)),
        ],
        out_specs=[pl.BlockSpec((bm, n), lambda i, kk: (i, 0)) for _ in out_dtypes],
        out_shape=[jax.ShapeDtypeStruct((m, n), dt) for dt in out_dtypes],
        scratch_shapes=[pltpu.VMEM((bm, n), F32)],
        compiler_params=_params(("parallel", "arbitrary"), vmem_mib),
        name="mm_res_ln",
    )(a, w, res, g.reshape(1, n), b.reshape(1, n))
    return outs[0] if len(outs) == 1 else outs


def _branch_body(c_ref, a_ref, x_ref, w0_ref, w1_ref, w2_ref, g0_ref, g1_ref, g2_ref, o_ref):
    y = g0_ref[...] * jnp.dot(c_ref[...], w0_ref[...], preferred_element_type=F32)
    y += g1_ref[...] * jnp.dot(a_ref[...], w1_ref[...], preferred_element_type=F32)
    y += g2_ref[...] * jnp.dot(x_ref[...], w2_ref[...], preferred_element_type=F32)
    o_ref[...] = y.astype(o_ref.dtype)


def _branch_mix(conv_y, attn, cross, w_branch, gates, bm=1024, bn=512):
    m = conv_y.shape[0]
    d = w_branch.shape[2]
    bm = min(bm, m)
    nj = d // bn
    br_spec = pl.BlockSpec((bm, BRANCH_WIDTH), lambda i, j: (i, 0))
    return pl.pallas_call(
        _branch_body,
        grid=(m // bm, nj),
        in_specs=[br_spec, br_spec, br_spec]
        + [pl.BlockSpec((None, BRANCH_WIDTH, bn), lambda i, j, n=n: (n, 0, j))
           for n in range(N_BRANCH)]
        + [pl.BlockSpec((bm, bn), lambda i, j, n=n: (i, n * nj + j)) for n in range(N_BRANCH)],
        out_specs=pl.BlockSpec((bm, bn), lambda i, j: (i, j)),
        out_shape=jax.ShapeDtypeStruct((m, d), BF16),
        compiler_params=_params(("parallel", "parallel"), 48),
        name="branch_mix",
    )(conv_y, attn, cross, w_branch, w_branch, w_branch, gates, gates, gates)


def _bias_tables_body(rb_ref, o_ref):
    h = pl.program_id(0)
    shape = (MOBA_BLOCK, MOBA_BLOCK)
    qi = lax.broadcasted_iota(jnp.int32, shape, 0)
    kj = lax.broadcasted_iota(jnp.int32, shape, 1)
    far = rb_ref[NUM_BUCKETS - 1, h]
    o_ref[0, 0] = jnp.zeros(shape, F32)
    for slot, offset in ((1, MOBA_BLOCK), (2, 0)):
        dist = jnp.maximum(qi - kj + offset, 0)
        large = MAX_EXACT + (jnp.log(jnp.maximum(dist, 1).astype(F32) / MAX_EXACT)
                             / math.log(MAX_DISTANCE / MAX_EXACT)
                             * (NUM_BUCKETS - MAX_EXACT)).astype(jnp.int32)
        large = jnp.minimum(large, NUM_BUCKETS - 1)
        bucket = jnp.where(dist < MAX_EXACT, dist, large)
        val = jnp.zeros(shape, F32)
        for bkt in range(NUM_BUCKETS):
            val = jnp.where(bucket == bkt, rb_ref[bkt, h], val)
        val = val - far
        if slot == 2:
            val = jnp.where(kj > qi, MASKED, val)
        o_ref[0, slot] = val


def _bias_tables(rel_bias):
    return pl.pallas_call(
        _bias_tables_body,
        grid=(N_HEADS,),
        in_specs=[pl.BlockSpec(memory_space=pltpu.SMEM)],
        out_specs=pl.BlockSpec((1, 3, MOBA_BLOCK, MOBA_BLOCK), lambda h: (h, 0, 0, 0)),
        out_shape=jax.ShapeDtypeStruct((N_HEADS, 3, MOBA_BLOCK, MOBA_BLOCK), F32),
        compiler_params=_params(("arbitrary",), 32),
        name="bias_tables",
    )(rel_bias)


def _block_mean_body(k_ref, o_ref):
    rows = k_ref.shape[0]
    x = k_ref[...].reshape(rows // MOBA_BLOCK, MOBA_BLOCK, k_ref.shape[1])
    o_ref[...] = jnp.sum(x, axis=1) * (1.0 / MOBA_BLOCK)


def _block_mean(k):
    t, w = k.shape
    nb = t // MOBA_BLOCK
    per = min(8, nb)
    assert nb % per == 0
    return pl.pallas_call(
        _block_mean_body,
        grid=(nb // per,),
        in_specs=[pl.BlockSpec((per * MOBA_BLOCK, w), lambda i: (i, 0))],
        out_specs=pl.BlockSpec((per, w), lambda i: (i, 0)),
        out_shape=jax.ShapeDtypeStruct((nb, w), F32),
        compiler_params=_params(("parallel",), 40),
        name="block_mean",
    )(k)


def _select_topk(gate, index):
    hit_any = jnp.zeros(gate.shape, jnp.bool_)
    for _ in range(MOBA_TOPK):
        top = jnp.max(gate, axis=1, keepdims=True)
        first = jnp.min(jnp.where(gate == top, index, float(LANES)), axis=1, keepdims=True)
        chosen = index == first
        hit_any = hit_any | (chosen & (top > -jnp.inf))
        gate = jnp.where(chosen, -jnp.inf, gate)
    return hit_any


def _moba_prompt_body(q_ref, k_ref, v_ref, km_ref, tab_ref, o_ref, kaug_ref):
    qb_idx = pl.program_id(1)
    t = k_ref.shape[0]

    @pl.when(qb_idx == 0)
    def _():
        kaug_ref[:, :HEAD_DIM] = k_ref[...]
        row_blk = jnp.right_shift(lax.broadcasted_iota(jnp.int32, (t, LANES), 0),
                                  int(math.log2(MOBA_BLOCK)))
        col = lax.broadcasted_iota(jnp.int32, (t, LANES), 1)
        kaug_ref[:, HEAD_DIM:] = jnp.where(row_blk == col, 1.0, 0.0).astype(BF16)

    q = q_ref[...]
    gate = lax.dot_general(q, km_ref[...], NT_DIMS, precision=lax.Precision.HIGHEST,
                           preferred_element_type=F32)
    lane = lax.broadcasted_iota(jnp.int32, gate.shape, 1)
    lane_f = lane.astype(F32)
    gate = jnp.where(lane < qb_idx, gate, -jnp.inf)
    picked = _select_topk(gate, lane_f) | (lane == qb_idx)
    pen = jnp.where(picked, 0.0, MASKED)
    q_aug = jnp.concatenate([(q * (HEAD_DIM ** -0.5)).astype(BF16), pen.astype(BF16)], axis=1)

    def tile(j, carry):
        m, l, acc = carry
        start = pl.multiple_of(j * MOBA_BLOCK, MOBA_BLOCK)
        s = lax.dot_general(q_aug, kaug_ref[pl.ds(start, MOBA_BLOCK), :], NT_DIMS,
                            preferred_element_type=F32)
        s = s + tab_ref[0, jnp.clip(j - (qb_idx - 2), 0, 2)]
        m_new = jnp.maximum(m, jnp.max(s, axis=1, keepdims=True))
        scale = jnp.exp(m - m_new)
        p = jnp.exp(s - m_new)
        l = scale * l + jnp.sum(p, axis=1, keepdims=True)
        acc = scale * acc + jnp.dot(p.astype(BF16), v_ref[pl.ds(start, MOBA_BLOCK), :],
                                    preferred_element_type=F32)
        return m_new, l, acc

    init = (jnp.full((MOBA_BLOCK, 1), 0.1 * MASKED, F32), jnp.zeros((MOBA_BLOCK, 1), F32),
            jnp.zeros((MOBA_BLOCK, HEAD_DIM), F32))
    _, l, acc = lax.fori_loop(0, qb_idx + 1, tile, init)
    o_ref[...] = (acc / l).astype(o_ref.dtype)


def _moba_prompt(q, k_bf, v_bf, k_mean, tables):
    t = q.shape[0]
    nb = t // MOBA_BLOCK
    assert t % MOBA_BLOCK == 0 and MOBA_TOPK <= nb <= LANES
    km = jnp.pad(k_mean, ((0, LANES - nb), (0, 0)))
    return pl.pallas_call(
        _moba_prompt_body,
        grid=(N_HEADS, nb),
        in_specs=[
            pl.BlockSpec((MOBA_BLOCK, HEAD_DIM), lambda h, i: (i, h)),
            pl.BlockSpec((t, HEAD_DIM), lambda h, i: (0, h)),
            pl.BlockSpec((t, HEAD_DIM), lambda h, i: (0, h)),
            pl.BlockSpec((LANES, HEAD_DIM), lambda h, i: (0, h)),
            pl.BlockSpec((1, 3, MOBA_BLOCK, MOBA_BLOCK), lambda h, i: (h, 0, 0, 0)),
        ],
        out_specs=pl.BlockSpec((MOBA_BLOCK, HEAD_DIM), lambda h, i: (i, h)),
        out_shape=jax.ShapeDtypeStruct((t, ATTN_WIDTH), BF16),
        scratch_shapes=[pltpu.VMEM((t, HEAD_DIM + LANES), BF16)],
        compiler_params=_params(("arbitrary", "arbitrary"), 40),
        name="moba_prompt",
    )(q, k_bf, v_bf, km, tables)


PAGES_PER_STEP = 16


def _paged_block_mean_body(pt_ref, *refs):
    pages = refs[:PAGES_PER_STEP]
    o_ref = refs[PAGES_PER_STEP]
    per_block = MOBA_BLOCK // pages[0].shape[1]
    for blk in range(PAGES_PER_STEP // per_block):
        tot = jnp.sum(pages[blk * per_block][0], axis=0, keepdims=True)
        for extra in range(1, per_block):
            tot = tot + jnp.sum(pages[blk * per_block + extra][0], axis=0, keepdims=True)
        o_ref[0, blk:blk + 1, :] = tot * (1.0 / MOBA_BLOCK)


def _paged_block_mean(pool, page_table_flat, n_req, n_pages):
    _, page, w = pool.shape
    per_block = MOBA_BLOCK // page
    blocks_per_step = PAGES_PER_STEP // per_block
    nbp = n_pages // per_block
    assert n_pages % PAGES_PER_STEP == 0 and MOBA_BLOCK % page == 0
    return pl.pallas_call(
        _paged_block_mean_body,
        grid_spec=pltpu.PrefetchScalarGridSpec(
            num_scalar_prefetch=1,
            grid=(n_req, n_pages // PAGES_PER_STEP),
            in_specs=[pl.BlockSpec(
                (1, page, w),
                lambda b, g, pt, i=i: (pt[b * n_pages + g * PAGES_PER_STEP + i], 0, 0))
                for i in range(PAGES_PER_STEP)],
            out_specs=pl.BlockSpec((1, blocks_per_step, w), lambda b, g, pt: (b, g, 0)),
        ),
        out_shape=jax.ShapeDtypeStruct((n_req, nbp, w), F32),
        compiler_params=_params(("parallel", "arbitrary"), 40),
        name="paged_block_mean",
    )(page_table_flat, *([pool] * PAGES_PER_STEP))


def _sample_select_body(q_ref, km_ref, o_ref):
    prod = km_ref[0] * q_ref[0]
    chan = lax.broadcasted_iota(jnp.int32, (ATTN_WIDTH, LANES), 0)
    head = lax.broadcasted_iota(jnp.int32, (ATTN_WIDTH, LANES), 1)
    group = jnp.where(jnp.right_shift(chan, int(math.log2(HEAD_DIM))) == head, 1.0, 0.0)
    gate = jnp.dot(prod, group, precision=lax.Precision.HIGHEST,
                   preferred_element_type=F32)
    row = lax.broadcasted_iota(jnp.int32, gate.shape, 0).astype(F32)
    o_ref[...] = jnp.zeros(o_ref.shape, o_ref.dtype)
    for slot in range(MOBA_TOPK):
        top = jnp.max(gate, axis=0, keepdims=True)
        first = jnp.min(jnp.where(gate == top, row, float(gate.shape[0])), axis=0, keepdims=True)
        o_ref[0, slot:slot + 1, :] = first.astype(jnp.int32)
        gate = jnp.where(row == first, -jnp.inf, gate)


def _sample_select(q3, k_mean):
    n_req, nbp, w = k_mean.shape
    assert nbp >= MOBA_TOPK
    out = pl.pallas_call(
        _sample_select_body,
        grid=(n_req,),
        in_specs=[pl.BlockSpec((1, 1, w), lambda b: (b, 0, 0)),
                  pl.BlockSpec((1, nbp, w), lambda b: (b, 0, 0))],
        out_specs=pl.BlockSpec((1, 8, LANES), lambda b: (b, 0, 0)),
        out_shape=jax.ShapeDtypeStruct((n_req, 8, LANES), jnp.int32),
        compiler_params=_params(("parallel",), 32),
        name="sample_select",
    )(q3, k_mean)
    return out[:, :MOBA_TOPK, :N_HEADS]


def _sample_attn_body(pt_ref, sel_ref, q_ref, kn_ref, vn_ref, tprev_ref, town_ref, *refs,
                      n_slot_pages, last_block):
    k_pages = refs[:n_slot_pages]
    v_pages = refs[n_slot_pages:2 * n_slot_pages]
    o_ref = refs[2 * n_slot_pages]
    b = pl.program_id(0)
    h = pl.program_id(1)
    page = k_pages[0].shape[1]
    per_block = n_slot_pages // MOBA_TOPK
    q = q_ref[0] * (HEAD_DIM ** -0.5)
    q8 = jnp.broadcast_to(q, (8, HEAD_DIM)).astype(BF16)
    prev_row = tprev_ref[0, 0, 0:1, :]
    logits = []
    for slot in range(MOBA_TOPK):
        near = (sel_ref[(b * MOBA_TOPK + slot) * N_HEADS + h] == last_block).astype(F32)
        for i in range(per_block):
            kt = k_pages[slot * per_block + i][0].astype(BF16)
            lg = lax.dot_general(q8, kt, NT_DIMS, preferred_element_type=F32)[0:1, :]
            logits.append(lg + near * prev_row[:, i * page:(i + 1) * page])
    l_own = jnp.sum(q * kn_ref[0], axis=1, keepdims=True) + town_ref[0, 0, 0:1, 0:1]
    m = l_own
    for lg in logits:
        m = jnp.maximum(m, jnp.max(lg, axis=1, keepdims=True))
    p_own = jnp.exp(l_own - m)
    den = p_own
    acc = p_own * vn_ref[0]
    for idx, lg in enumerate(logits):
        p = jnp.exp(lg - m)
        den = den + jnp.sum(p, axis=1, keepdims=True)
        p8 = jnp.broadcast_to(p, (8, page)).astype(BF16)
        acc = acc + jnp.dot(p8, v_pages[idx][0].astype(BF16),
                            preferred_element_type=F32)[0:1, :]
    o_ref[0] = (acc / den).astype(o_ref.dtype)


def _sample_attn(q3, kn3, vn3, pool_k, pool_v, page_table_flat, sel_flat, tables, n_pages):
    n_req = q3.shape[0]
    _, page, _ = pool_k.shape
    per_block = MOBA_BLOCK // page
    n_slot_pages = MOBA_TOPK * per_block
    last_block = n_pages // per_block - 1

    def page_spec(slot, i):
        def index(b, h, pt, sel):
            blk = sel[(b * MOBA_TOPK + slot) * N_HEADS + h]
            return (pt[b * n_pages + blk * per_block + i], 0, h)
        return pl.BlockSpec((1, page, HEAD_DIM), index)

    row_spec = pl.BlockSpec((1, 1, HEAD_DIM), lambda b, h, pt, sel: (b, 0, h))
    page_specs = [page_spec(s, i) for s in range(MOBA_TOPK) for i in range(per_block)]
    return pl.pallas_call(
        functools.partial(_sample_attn_body, n_slot_pages=n_slot_pages, last_block=last_block),
        grid_spec=pltpu.PrefetchScalarGridSpec(
            num_scalar_prefetch=2,
            grid=(n_req, N_HEADS),
            in_specs=[row_spec, row_spec, row_spec,
                      pl.BlockSpec((1, 1, 8, MOBA_BLOCK), lambda b, h, pt, sel: (h, 1, 0, 0)),
                      pl.BlockSpec((1, 1, 8, MOBA_BLOCK), lambda b, h, pt, sel: (h, 2, 0, 0))]
            + page_specs + page_specs,
            out_specs=row_spec,
        ),
        out_shape=jax.ShapeDtypeStruct((n_req, 1, ATTN_WIDTH), BF16),
        compiler_params=_params(("arbitrary", "arbitrary"), 32),
        name="sample_attn",
    )(page_table_flat, sel_flat, q3, kn3, vn3, tables, tables,
      *([pool_k] * n_slot_pages), *([pool_v] * n_slot_pages))


def _cross_prompt_body(q_ref, mk_ref, mv_ref, o_ref):
    for h in range(CROSS_HEADS):
        sl = slice(h * CROSS_HEAD_DIM, (h + 1) * CROSS_HEAD_DIM)
        s = lax.dot_general(q_ref[:, sl], mk_ref[:, sl], NT_DIMS,
                            preferred_element_type=F32) * (CROSS_HEAD_DIM ** -0.5)
        p = jnp.exp(s - jnp.max(s, axis=1, keepdims=True))
        den = jnp.sum(p, axis=1, keepdims=True)
        o = jnp.dot(p.astype(BF16), mv_ref[:, sl], preferred_element_type=F32) / den
        o_ref[:, sl] = o.astype(o_ref.dtype)


def _cross_prompt(qc, mk_bf, mv_bf, bm=512):
    t = qc.shape[0]
    n_mem = mk_bf.shape[0]
    return pl.pallas_call(
        _cross_prompt_body,
        grid=(t // bm,),
        in_specs=[pl.BlockSpec((bm, CROSS_WIDTH), lambda i: (i, 0)),
                  pl.BlockSpec((n_mem, CROSS_WIDTH), lambda i: (0, 0)),
                  pl.BlockSpec((n_mem, CROSS_WIDTH), lambda i: (0, 0))],
        out_specs=pl.BlockSpec((bm, CROSS_WIDTH), lambda i: (i, 0)),
        out_shape=jax.ShapeDtypeStruct((t, CROSS_WIDTH), BF16),
        compiler_params=_params(("parallel",), 32),
        name="cross_prompt",
    )(qc, mk_bf, mv_bf)


def _cross_sample_body(q_ref, mk_ref, mv_ref, o_ref):
    for h in range(CROSS_HEADS):
        sl = slice(h * CROSS_HEAD_DIM, (h + 1) * CROSS_HEAD_DIM)
        q8 = jnp.broadcast_to(q_ref[0, :, sl], (8, CROSS_HEAD_DIM))
        s = lax.dot_general(q8, mk_ref[0, :, sl].astype(BF16), NT_DIMS,
                            preferred_element_type=F32) * (CROSS_HEAD_DIM ** -0.5)
        p = jnp.exp(s - jnp.max(s, axis=1, keepdims=True))
        den = jnp.sum(p, axis=1, keepdims=True)
        o = jnp.dot(p.astype(BF16), mv_ref[0, :, sl].astype(BF16),
                    preferred_element_type=F32) / den
        o_ref[0, :, sl] = o[0:1, :].astype(o_ref.dtype)


def _cross_sample(qc3, mem_k, mem_v):
    n_req, n_mem, _ = mem_k.shape
    row_spec = pl.BlockSpec((1, 1, CROSS_WIDTH), lambda b: (b, 0, 0))
    mem_spec = pl.BlockSpec((1, n_mem, CROSS_WIDTH), lambda b: (b, 0, 0))
    return pl.pallas_call(
        _cross_sample_body,
        grid=(n_req,),
        in_specs=[row_spec, mem_spec, mem_spec],
        out_specs=row_spec,
        out_shape=jax.ShapeDtypeStruct((n_req, 1, CROSS_WIDTH), BF16),
        compiler_params=_params(("parallel",), 32),
        name="cross_sample",
    )(qc3, mem_k, mem_v)


CONV_TILE = 256
CONV_HALO = 32


def _conv_prompt_body(cur_ref, prev_ref, w_ref, cb_ref, g_ref, b_ref, o_ref, buf_ref, y_ref):
    i = pl.program_id(0)
    has_prev = (i > 0).astype(F32)
    buf_ref[0:CONV_HALO, :] = prev_ref[CONV_TILE - CONV_HALO:CONV_TILE, :] * has_prev
    buf_ref[CONV_HALO:CONV_HALO + CONV_TILE, :] = cur_ref[...]
    first = CONV_HALO - CONV_STATE
    for c in range(D_CONV // LANES):
        cs = slice(c * LANES, (c + 1) * LANES)
        acc = jnp.zeros((CONV_TILE, LANES), F32)
        for tap in range(CONV_WIDTH):
            acc = acc + buf_ref[first + tap:first + tap + CONV_TILE, cs] * w_ref[tap:tap + 1, cs]
        y_ref[:, cs] = acc + cb_ref[:, cs]
    y = _layer_norm(y_ref[...], g_ref[...], b_ref[...])
    o_ref[...] = jax.nn.silu(y).astype(o_ref.dtype)


def _conv_prompt(u, conv_w, conv_b, ln_g, ln_b):
    t = u.shape[0]
    vec = pl.BlockSpec((1, D_CONV), lambda i: (0, 0))
    return pl.pallas_call(
        _conv_prompt_body,
        grid=(t // CONV_TILE,),
        in_specs=[pl.BlockSpec((CONV_TILE, D_CONV), lambda i: (i, 0)),
                  pl.BlockSpec((CONV_TILE, D_CONV), lambda i: (jnp.maximum(i - 1, 0), 0)),
                  pl.BlockSpec((CONV_WIDTH, D_CONV), lambda i: (0, 0)),
                  vec, vec, vec],
        out_specs=pl.BlockSpec((CONV_TILE, D_CONV), lambda i: (i, 0)),
        out_shape=jax.ShapeDtypeStruct((t, D_CONV), BF16),
        scratch_shapes=[pltpu.VMEM((CONV_HALO + CONV_TILE, D_CONV), F32),
                        pltpu.VMEM((CONV_TILE, D_CONV), F32)],
        compiler_params=_params(("parallel",), 32),
        name="conv_prompt",
    )(u, u, conv_w, conv_b.reshape(1, D_CONV), ln_g.reshape(1, D_CONV), ln_b.reshape(1, D_CONV))


def _conv_sample_body(st_ref, u_ref, w_ref, cb_ref, g_ref, b_ref, o_ref):
    acc = u_ref[...] * w_ref[CONV_STATE:CONV_WIDTH, :]
    for tap in range(CONV_STATE):
        acc = acc + st_ref[tap] * w_ref[tap:tap + 1, :]
    y = _layer_norm(acc + cb_ref[...], g_ref[...], b_ref[...])
    o_ref[...] = jax.nn.silu(y).astype(o_ref.dtype)


def _conv_sample(state_t, u, conv_w, conv_b, ln_g, ln_b):
    n_req = u.shape[0]
    return pl.pallas_call(
        _conv_sample_body,
        out_shape=jax.ShapeDtypeStruct((n_req, D_CONV), BF16),
        name="conv_sample",
    )(state_t, u, conv_w, conv_b.reshape(1, D_CONV), ln_g.reshape(1, D_CONV),
      ln_b.reshape(1, D_CONV))


def kernel(x_prompt, x_sample, mem_prompt, cache_k, cache_v, state_conv, cache_mem_k, cache_mem_v,
           page_table, w_in, conv_w, conv_b, conv_ln_g, conv_ln_b, w_mem_kv, rel_bias, w_branch,
           w_out, ln1_g, ln1_b, w_gate, w_up, w_down, ln2_g, ln2_b):
    depth = w_in.shape[0]
    assert depth == 1 and x_prompt.shape[0] == 1 and x_sample.shape[1] == 1
    alpha = (2 * depth) ** 0.25
    t, d_model = x_prompt.shape[1], x_prompt.shape[2]
    n_req, n_pages = page_table.shape
    n_pool, page = cache_k.shape[1], cache_k.shape[2]
    n_mem = mem_prompt.shape[1]
    assert (n_pages * page) % MOBA_BLOCK == 0

    sizes = (D_CONV, D_CONV, ATTN_WIDTH, ATTN_WIDTH, ATTN_WIDTH, CROSS_WIDTH, N_BRANCH * d_model)
    cuts = np.concatenate([[0], np.cumsum(sizes)])
    wa, wb, wq, wk, wv, wqc, wg = [w_in[0][:, int(cuts[i]):int(cuts[i + 1])].astype(BF16)
                                   for i in range(len(sizes))]
    wmk = w_mem_kv[0][:, :CROSS_WIDTH].astype(BF16)
    wmv = w_mem_kv[0][:, CROSS_WIDTH:].astype(BF16)
    w_br = w_branch[0].astype(BF16)
    w_o = w_out[0].astype(BF16)
    w_g = w_gate[0].astype(BF16)
    w_u = w_up[0].astype(BF16)
    w_d = w_down[0].astype(BF16)
    d_ff = w_g.shape[1]
    ff_bk = d_ff // 4 if (d_ff // 4) % LANES == 0 else d_ff

    tables = _bias_tables(rel_bias)

    def finish(x, conv_y, attn, cross, gates, bm_out, bm_ffn):
        mix_in = _branch_mix(conv_y, attn, cross, w_br, gates)
        h1, h1_bf = _mm_res_ln(mix_in, w_o, x, ln1_g[0], ln1_b[0], alpha, [F32, BF16],
                               bm=bm_out, bk=d_model)
        act = _proj(h1_bf, [w_g, w_u], "swiglu", [BF16])
        return _mm_res_ln(act, w_d, h1, ln2_g[0], ln2_b[0], alpha, [F32], bm=bm_ffn, bk=ff_bk)

    xp = x_prompt[0]
    xp_bf = xp.astype(BF16)
    u_p = _proj(xp_bf, [wa, wb], "glu", [F32])
    q_p = _proj(xp_bf, [wq], "none", [F32], bn=1024)
    k_p, k_p_bf = _proj(xp_bf, [wk], "none", [F32, BF16], bn=1024)
    v_p, v_p_bf = _proj(xp_bf, [wv], "none", [F32, BF16], bn=1024)
    qc_p = _proj(xp_bf, [wqc], "none", [BF16], bn=1024)
    gates_p = _proj(xp_bf, [wg], "sigmoid", [F32], bn=1024)
    conv_y_p = _conv_prompt(u_p, conv_w[0], conv_b[0], conv_ln_g[0], conv_ln_b[0])
    attn_p = _moba_prompt(q_p, k_p_bf, v_p_bf, _block_mean(k_p), tables)
    mem_bf = mem_prompt[0].astype(BF16)
    mk, mk_bf = _proj(mem_bf, [wmk], "none", [F32, BF16], bn=1024)
    mv, mv_bf = _proj(mem_bf, [wmv], "none", [F32, BF16], bn=1024)
    cross_p = _cross_prompt(qc_p, mk_bf, mv_bf)
    y_p = finish(xp, conv_y_p, attn_p, cross_p, gates_p, bm_out=256, bm_ffn=512)

    xs = x_sample[:, 0]
    xs_bf = xs.astype(BF16)
    u_s = _proj(xs_bf, [wa, wb], "glu", [F32])
    q_s = _proj(xs_bf, [wq], "none", [F32], bn=1024)
    k_s = _proj(xs_bf, [wk], "none", [F32], bn=1024)
    v_s = _proj(xs_bf, [wv], "none", [F32], bn=1024)
    qc_s = _proj(xs_bf, [wqc], "none", [BF16], bn=1024)
    gates_s = _proj(xs_bf, [wg], "sigmoid", [F32], bn=1024)
    conv_y_s = _conv_sample(state_conv[0].transpose(1, 0, 2), u_s, conv_w[0], conv_b[0],
                            conv_ln_g[0], conv_ln_b[0])
    pool_k = cache_k[0].reshape(n_pool, page, ATTN_WIDTH)
    pool_v = cache_v[0].reshape(n_pool, page, ATTN_WIDTH)
    pt_flat = page_table.reshape(-1)
    q_s3 = q_s.reshape(n_req, 1, ATTN_WIDTH)
    k_mean_s = _paged_block_mean(pool_k, pt_flat, n_req, n_pages)
    sel = _sample_select(q_s3, k_mean_s)
    attn_s = _sample_attn(q_s3, k_s.reshape(n_req, 1, ATTN_WIDTH), v_s.reshape(n_req, 1, ATTN_WIDTH),
                          pool_k, pool_v, pt_flat, sel.reshape(-1), tables, n_pages)
    cross_s = _cross_sample(qc_s.reshape(n_req, 1, CROSS_WIDTH),
                            cache_mem_k[0].reshape(n_req, n_mem, CROSS_WIDTH),
                            cache_mem_v[0].reshape(n_req, n_mem, CROSS_WIDTH))
    y_s = finish(xs, conv_y_s, attn_s.reshape(n_req, ATTN_WIDTH),
                 cross_s.reshape(n_req, CROSS_WIDTH), gates_s, bm_out=256, bm_ffn=512)

    new_conv_sample = jnp.concatenate([state_conv[0][:, 1:], u_s[:, None, :]], axis=1)
    return (
        y_p[None],
        y_s[:, None],
        k_p.reshape(1, 1, t, N_HEADS, HEAD_DIM),
        v_p.reshape(1, 1, t, N_HEADS, HEAD_DIM),
        u_p[t - CONV_STATE:][None, None],
        mk.reshape(1, 1, n_mem, CROSS_HEADS, CROSS_HEAD_DIM),
        mv.reshape(1, 1, n_mem, CROSS_HEADS, CROSS_HEAD_DIM),
        k_s.reshape(1, n_req, 1, N_HEADS, HEAD_DIM),
        v_s.reshape(1, n_req, 1, N_HEADS, HEAD_DIM),
        new_conv_sample[None],
    )
```

```python
import functools
import math

import numpy as np
import jax
import jax.numpy as jnp
from jax import lax
from jax.experimental import pallas as pl
from jax.experimental.pallas import tpu as pltpu

F32 = jnp.float32
BF16 = jnp.bfloat16

N_HEADS = 8
HEAD_DIM = 128
ATTN_WIDTH = N_HEADS * HEAD_DIM
MOBA_BLOCK = 256
MOBA_TOPK = 3
D_CONV = 1024
CONV_WIDTH = 31
CONV_STATE = CONV_WIDTH - 1
CROSS_HEADS = 4
CROSS_HEAD_DIM = 256
CROSS_WIDTH = CROSS_HEADS * CROSS_HEAD_DIM
N_BRANCH = 3
BRANCH_WIDTH = 1024
NUM_BUCKETS = 32
MAX_EXACT = NUM_BUCKETS // 2
MAX_DISTANCE = 128
LN_EPS = 1e-5
MASKED = -1e30
LANES = 128
MIB = 2 ** 20

NT_DIMS = (((1,), (1,)), ((), ()))


def _params(semantics, vmem_mib):
    return pltpu.CompilerParams(dimension_semantics=semantics,
                                vmem_limit_bytes=vmem_mib * MIB)


def _layer_norm(y, g, b):
    mu = jnp.mean(y, axis=-1, keepdims=True)
    yc = y - mu
    var = jnp.mean(yc * yc, axis=-1, keepdims=True)
    return yc * lax.rsqrt(var + LN_EPS) * g + b


def _proj_body(*refs, n_w, epilogue):
    x_ref = refs[0]
    w_refs = refs[1:1 + n_w]
    o_refs = refs[1 + n_w:]
    x = x_ref[...]
    z = [jnp.dot(x, w[...], preferred_element_type=F32) for w in w_refs]
    if epilogue == "glu":
        y = z[0] * jax.nn.sigmoid(z[1])
    elif epilogue == "swiglu":
        y = jax.nn.silu(z[0]) * z[1]
    elif epilogue == "sigmoid":
        y = jax.nn.sigmoid(z[0])
    else:
        y = z[0]
    for o in o_refs:
        o[...] = y.astype(o.dtype)


def _proj(x, ws, epilogue, out_dtypes, bm=1024, bn=512, vmem_mib=48):
    m, k = x.shape
    n = ws[0].shape[1]
    bm = min(bm, m)
    bn = min(bn, n)
    while n % bn:
        bn //= 2
    assert m % bm == 0 and bn % LANES == 0
    outs = pl.pallas_call(
        functools.partial(_proj_body, n_w=len(ws), epilogue=epilogue),
        grid=(m // bm, n // bn),
        in_specs=[pl.BlockSpec((bm, k), lambda i, j: (i, 0))]
        + [pl.BlockSpec((k, bn), lambda i, j: (0, j)) for _ in ws],
        out_specs=[pl.BlockSpec((bm, bn), lambda i, j: (i, j)) for _ in out_dtypes],
        out_shape=[jax.ShapeDtypeStruct((m, n), dt) for dt in out_dtypes],
        compiler_params=_params(("parallel", "parallel"), vmem_mib),
        name="proj_" + epilogue,
    )(x, *ws)
    return outs[0] if len(outs) == 1 else outs


def _mm_res_ln_body(a_ref, w_ref, r_ref, g_ref, b_ref, *rest, alpha, n_out):
    o_refs = rest[:n_out]
    acc_ref = rest[n_out]
    kk = pl.program_id(1)

    @pl.when(kk == 0)
    def _():
        acc_ref[...] = jnp.zeros_like(acc_ref)

    acc_ref[...] += jnp.dot(a_ref[...], w_ref[...], preferred_element_type=F32)

    @pl.when(kk == pl.num_programs(1) - 1)
    def _():
        y = alpha * r_ref[...] + acc_ref[...]
        out = _layer_norm(y, g_ref[...], b_ref[...])
        for o in o_refs:
            o[...] = out.astype(o.dtype)


def _mm_res_ln(a, w, res, g, b, alpha, out_dtypes, bm, bk, vmem_mib=48):
    m, k = a.shape
    n = w.shape[1]
    bm = min(bm, m)
    assert m % bm == 0 and k % bk == 0
    outs = pl.pallas_call(
        functools.partial(_mm_res_ln_body, alpha=alpha, n_out=len(out_dtypes)),
        grid=(m // bm, k // bk),
        in_specs=[
            pl.BlockSpec((bm, bk), lambda i, kk: (i, kk)),
            pl.BlockSpec((bk, n), lambda i, kk: (kk, 0)),
            pl.BlockSpec((bm, n), lambda i, kk: (i, 0)),
            pl.BlockSpec((1, n), lambda i, kk: (0, 0)),
            pl.BlockSpec((1, n), lambda i, kk: (0, 0)),
        ],
        out_specs=[pl.BlockSpec((bm, n), lambda i, kk: (i, 0)) for _ in out_dtypes],
        out_shape=[jax.ShapeDtypeStruct((m, n), dt) for dt in out_dtypes],
        scratch_shapes=[pltpu.VMEM((bm, n), F32)],
        compiler_params=_params(("parallel", "arbitrary"), vmem_mib),
        name="mm_res_ln",
    )(a, w, res, g.reshape(1, n), b.reshape(1, n))
    return outs[0] if len(outs) == 1 else outs


def _branch_body(c_ref, a_ref, x_ref, w0_ref, w1_ref, w2_ref, g0_ref, g1_ref, g2_ref, o_ref):
    y = g0_ref[...] * jnp.dot(c_ref[...], w0_ref[...], preferred_element_type=F32)
    y += g1_ref[...] * jnp.dot(a_ref[...], w1_ref[...], preferred_element_type=F32)
    y += g2_ref[...] * jnp.dot(x_ref[...], w2_ref[...], preferred_element_type=F32)
    o_ref[...] = y.astype(o_ref.dtype)


def _branch_mix(conv_y, attn, cross, w_branch, gates, bm=1024, bn=512):
    m = conv_y.shape[0]
    d = w_branch.shape[2]
    bm = min(bm, m)
    nj = d // bn
    br_spec = pl.BlockSpec((bm, BRANCH_WIDTH), lambda i, j: (i, 0))
    return pl.pallas_call(
        _branch_body,
        grid=(m // bm, nj),
        in_specs=[br_spec, br_spec, br_spec]
        + [pl.BlockSpec((None, BRANCH_WIDTH, bn), lambda i, j, n=n: (n, 0, j))
           for n in range(N_BRANCH)]
        + [pl.BlockSpec((bm, bn), lambda i, j, n=n: (i, n * nj + j)) for n in range(N_BRANCH)],
        out_specs=pl.BlockSpec((bm, bn), lambda i, j: (i, j)),
        out_shape=jax.ShapeDtypeStruct((m, d), BF16),
        compiler_params=_params(("parallel", "parallel"), 48),
        name="branch_mix",
    )(conv_y, attn, cross, w_branch, w_branch, w_branch, gates, gates, gates)


TAB_PREV = 0
TAB_OWN = 1


def _bias_tables_body(rb_ref, o_ref):
    h = pl.program_id(0)
    shape = (MOBA_BLOCK, MOBA_BLOCK)
    qi = lax.broadcasted_iota(jnp.int32, shape, 0)
    kj = lax.broadcasted_iota(jnp.int32, shape, 1)
    far = rb_ref[NUM_BUCKETS - 1, h]
    for slot, offset in ((TAB_PREV, MOBA_BLOCK), (TAB_OWN, 0)):
        dist = jnp.maximum(qi - kj + offset, 0)
        large = MAX_EXACT + (jnp.log(jnp.maximum(dist, 1).astype(F32) / MAX_EXACT)
                             / math.log(MAX_DISTANCE / MAX_EXACT)
                             * (NUM_BUCKETS - MAX_EXACT)).astype(jnp.int32)
        large = jnp.minimum(large, NUM_BUCKETS - 1)
        bucket = jnp.where(dist < MAX_EXACT, dist, large)
        val = jnp.zeros(shape, F32)
        for bkt in range(NUM_BUCKETS):
            val = jnp.where(bucket == bkt, rb_ref[bkt, h], val)
        val = val - far
        if slot == TAB_OWN:
            val = jnp.where(kj > qi, MASKED, val)
        o_ref[0, slot] = val


def _bias_tables(rel_bias):
    return pl.pallas_call(
        _bias_tables_body,
        grid=(N_HEADS,),
        in_specs=[pl.BlockSpec(memory_space=pltpu.SMEM)],
        out_specs=pl.BlockSpec((1, 2, MOBA_BLOCK, MOBA_BLOCK), lambda h: (h, 0, 0, 0)),
        out_shape=jax.ShapeDtypeStruct((N_HEADS, 2, MOBA_BLOCK, MOBA_BLOCK), F32),
        compiler_params=_params(("arbitrary",), 32),
        name="bias_tables",
    )(rel_bias)


def _block_mean_body(k_ref, o_ref):
    rows = k_ref.shape[0]
    x = k_ref[...].reshape(rows // MOBA_BLOCK, MOBA_BLOCK, k_ref.shape[1])
    o_ref[...] = jnp.sum(x, axis=1) * (1.0 / MOBA_BLOCK)


def _block_mean(k):
    t, w = k.shape
    nb = t // MOBA_BLOCK
    per = min(8, nb)
    assert nb % per == 0
    return pl.pallas_call(
        _block_mean_body,
        grid=(nb // per,),
        in_specs=[pl.BlockSpec((per * MOBA_BLOCK, w), lambda i: (i, 0))],
        out_specs=pl.BlockSpec((per, w), lambda i: (i, 0)),
        out_shape=jax.ShapeDtypeStruct((nb, w), F32),
        compiler_params=_params(("parallel",), 40),
        name="block_mean",
    )(k)


def _select_topk(gate, index):
    hit_any = jnp.zeros(gate.shape, jnp.bool_)
    for _ in range(MOBA_TOPK):
        top = jnp.max(gate, axis=1, keepdims=True)
        first = jnp.min(jnp.where(gate == top, index, float(LANES)), axis=1, keepdims=True)
        chosen = index == first
        hit_any = hit_any | (chosen & (top > -jnp.inf))
        gate = jnp.where(chosen, -jnp.inf, gate)
    return hit_any


MOBA_CHUNK = 4
MOBA_HEADS_PER_STEP = 2
MOBA_PAD = MOBA_CHUNK * MOBA_BLOCK
KAUG_WIDTH = HEAD_DIM + LANES


def _moba_prep_body(k_ref, v_ref, ka_ref, vp_ref):
    i = pl.program_id(0)
    rows = k_ref.shape[0]
    col = lax.broadcasted_iota(jnp.int32, (rows, LANES), 1)

    @pl.when(i == 0)
    def _():
        pad_hot = jnp.where(col == LANES - 1, 1.0, 0.0).astype(BF16)
        for h in range(N_HEADS):
            ka_ref[:, h * KAUG_WIDTH:h * KAUG_WIDTH + HEAD_DIM] = jnp.zeros((rows, HEAD_DIM), BF16)
            ka_ref[:, h * KAUG_WIDTH + HEAD_DIM:(h + 1) * KAUG_WIDTH] = pad_hot
        vp_ref[...] = jnp.zeros(vp_ref.shape, BF16)

    @pl.when(i > 0)
    def _():
        row = lax.broadcasted_iota(jnp.int32, (rows, LANES), 0) + (i - 1) * rows
        hot = jnp.where(jnp.right_shift(row, int(math.log2(MOBA_BLOCK))) == col,
                        1.0, 0.0).astype(BF16)
        for h in range(N_HEADS):
            ka_ref[:, h * KAUG_WIDTH:h * KAUG_WIDTH + HEAD_DIM] = (
                k_ref[:, h * HEAD_DIM:(h + 1) * HEAD_DIM].astype(BF16))
            ka_ref[:, h * KAUG_WIDTH + HEAD_DIM:(h + 1) * KAUG_WIDTH] = hot
        vp_ref[...] = v_ref[...].astype(BF16)


def _moba_prep(k, v):
    t = k.shape[0]
    rows = MOBA_PAD
    assert t % rows == 0
    data = pl.BlockSpec((rows, ATTN_WIDTH), lambda i: (jnp.maximum(i - 1, 0), 0))
    return pl.pallas_call(
        _moba_prep_body,
        grid=(t // rows + 1,),
        in_specs=[data, data],
        out_specs=[pl.BlockSpec((rows, N_HEADS * KAUG_WIDTH), lambda i: (i, 0)),
                   pl.BlockSpec((rows, ATTN_WIDTH), lambda i: (i, 0))],
        out_shape=[jax.ShapeDtypeStruct((t + rows, N_HEADS * KAUG_WIDTH), BF16),
                   jax.ShapeDtypeStruct((t + rows, ATTN_WIDTH), BF16)],
        compiler_params=_params(("parallel",), 40),
        name="moba_prep",
    )(k, v)


def _moba_prompt_body(q_ref, ka_ref, v_ref, km_ref, tab_ref, o_ref):
    qb_idx = pl.program_id(1)
    heads = range(MOBA_HEADS_PER_STEP)
    span = MOBA_CHUNK * MOBA_BLOCK

    q_aug = []
    for hh in heads:
        q = q_ref[:, hh * HEAD_DIM:(hh + 1) * HEAD_DIM]
        gate = lax.dot_general(q, km_ref[:, hh * HEAD_DIM:(hh + 1) * HEAD_DIM], NT_DIMS,
                               precision=lax.Precision.HIGHEST, preferred_element_type=F32)
        lane = lax.broadcasted_iota(jnp.int32, gate.shape, 1)
        gate = jnp.where(lane < qb_idx, gate, -jnp.inf)
        picked = _select_topk(gate, lane.astype(F32)) | (lane == qb_idx)
        pen = jnp.where(picked, 0.0, MASKED)
        q_aug.append(jnp.concatenate([(q * (HEAD_DIM ** -0.5)).astype(BF16), pen.astype(BF16)],
                                     axis=1))

    def chunk(first_block, carry, bias):
        start = pl.multiple_of(first_block * MOBA_BLOCK, MOBA_BLOCK)
        out = []
        for hh in heads:
            m, l, acc = carry[hh]
            keys = ka_ref[pl.ds(start, span), hh * KAUG_WIDTH:(hh + 1) * KAUG_WIDTH]
            s = lax.dot_general(q_aug[hh], keys, NT_DIMS, preferred_element_type=F32)
            if bias is not None:
                s = s + bias[hh]
            m_new = jnp.maximum(m, jnp.max(s, axis=1, keepdims=True))
            scale = jnp.exp(m - m_new)
            p = jnp.exp(s - m_new)
            l = scale * l + jnp.sum(p, axis=1, keepdims=True)
            vals = v_ref[pl.ds(start, span), hh * HEAD_DIM:(hh + 1) * HEAD_DIM]
            acc = scale * acc + jnp.dot(p.astype(BF16), vals, preferred_element_type=F32)
            out.append((m_new, l, acc))
        return tuple(out)

    init = tuple((jnp.full((MOBA_BLOCK, 1), 0.1 * MASKED, F32), jnp.zeros((MOBA_BLOCK, 1), F32),
                  jnp.zeros((MOBA_BLOCK, HEAD_DIM), F32)) for _ in heads)
    first = qb_idx % MOBA_CHUNK + 1
    carry = lax.fori_loop(0, qb_idx // MOBA_CHUNK,
                          lambda c, cr: chunk(first + c * MOBA_CHUNK, cr, None), init)
    unbiased = jnp.zeros((MOBA_BLOCK, span - 2 * MOBA_BLOCK), F32)
    bias = [jnp.concatenate([unbiased, tab_ref[hh, TAB_PREV], tab_ref[hh, TAB_OWN]], axis=1)
            for hh in heads]
    carry = chunk(qb_idx + 1, carry, bias)
    for hh in heads:
        _, l, acc = carry[hh]
        o_ref[:, hh * HEAD_DIM:(hh + 1) * HEAD_DIM] = (acc / l).astype(o_ref.dtype)


def _moba_prompt(q, k_aug, v_pad, k_mean, tables):
    t = q.shape[0]
    nb = t // MOBA_BLOCK
    hb = MOBA_HEADS_PER_STEP
    assert t % MOBA_BLOCK == 0 and MOBA_TOPK <= nb < LANES and MOBA_CHUNK >= 2
    km = jnp.pad(k_mean, ((0, LANES - nb), (0, 0)))
    return pl.pallas_call(
        _moba_prompt_body,
        grid=(N_HEADS // hb, nb),
        in_specs=[
            pl.BlockSpec((MOBA_BLOCK, hb * HEAD_DIM), lambda h, i: (i, h)),
            pl.BlockSpec((t + MOBA_PAD, hb * KAUG_WIDTH), lambda h, i: (0, h)),
            pl.BlockSpec((t + MOBA_PAD, hb * HEAD_DIM), lambda h, i: (0, h)),
            pl.BlockSpec((LANES, hb * HEAD_DIM), lambda h, i: (0, h)),
            pl.BlockSpec((hb, 2, MOBA_BLOCK, MOBA_BLOCK), lambda h, i: (h, 0, 0, 0)),
        ],
        out_specs=pl.BlockSpec((MOBA_BLOCK, hb * HEAD_DIM), lambda h, i: (i, h)),
        out_shape=jax.ShapeDtypeStruct((t, ATTN_WIDTH), BF16),
        compiler_params=_params(("parallel", "parallel"), 56),
        name="moba_prompt",
    )(q, k_aug, v_pad, km, tables)


PAGES_PER_STEP = 16


def _paged_select_body(pt_ref, q_ref, *refs):
    pages = refs[:PAGES_PER_STEP]
    o_ref = refs[PAGES_PER_STEP]
    km_ref = refs[PAGES_PER_STEP + 1]
    g = pl.program_id(1)
    per_block = MOBA_BLOCK // pages[0].shape[0]
    blocks_per_step = PAGES_PER_STEP // per_block
    for blk in range(blocks_per_step):
        tot = jnp.sum(pages[blk * per_block][...], axis=0)
        for extra in range(1, per_block):
            tot = tot + jnp.sum(pages[blk * per_block + extra][...], axis=0)
        km_ref[g * blocks_per_step + blk] = tot * (1.0 / MOBA_BLOCK)

    @pl.when(g == pl.num_programs(1) - 1)
    def _():
        km = km_ref[...]
        gate = jnp.sum(km * q_ref[0], axis=-1, keepdims=True)
        gate = jnp.broadcast_to(gate, km.shape)
        blk_id = lax.broadcasted_iota(jnp.int32, km.shape, 0).astype(F32)
        for slot in range(MOBA_TOPK):
            top = jnp.max(gate, axis=0, keepdims=True)
            first = jnp.min(jnp.where(gate == top, blk_id, float(km.shape[0])),
                            axis=0, keepdims=True)
            o_ref[0, slot] = first[0].astype(jnp.int32)
            gate = jnp.where(blk_id == first, -jnp.inf, gate)


def _paged_select(q_heads, cache, page_table_flat, n_pages):
    n_req = q_heads.shape[0]
    page = cache.shape[2]
    per_block = MOBA_BLOCK // page
    nbp = n_pages // per_block
    assert n_pages % PAGES_PER_STEP == 0 and MOBA_BLOCK % page == 0 and nbp >= MOBA_TOPK
    out = pl.pallas_call(
        _paged_select_body,
        grid_spec=pltpu.PrefetchScalarGridSpec(
            num_scalar_prefetch=1,
            grid=(n_req, n_pages // PAGES_PER_STEP),
            in_specs=[pl.BlockSpec((1, N_HEADS, HEAD_DIM), lambda b, g, pt: (b, 0, 0))]
            + [pl.BlockSpec(
                (None, None, page, N_HEADS, HEAD_DIM),
                lambda b, g, pt, i=i: (0, pt[b * n_pages + g * PAGES_PER_STEP + i], 0, 0, 0))
               for i in range(PAGES_PER_STEP)],
            out_specs=pl.BlockSpec((1, MOBA_TOPK, N_HEADS, HEAD_DIM),
                                   lambda b, g, pt: (b, 0, 0, 0)),
            scratch_shapes=[pltpu.VMEM((nbp, N_HEADS, HEAD_DIM), F32)],
        ),
        out_shape=jax.ShapeDtypeStruct((n_req, MOBA_TOPK, N_HEADS, HEAD_DIM), jnp.int32),
        compiler_params=_params(("parallel", "arbitrary"), 40),
        name="paged_select",
    )(page_table_flat, q_heads, *([cache] * PAGES_PER_STEP))
    return out[:, :, :, 0]


def _sample_attn_body(pt_ref, sel_ref, q_ref, kn_ref, vn_ref, tprev_ref, town_ref, ck_ref, cv_ref,
                      o_ref, kbuf, vbuf, sem, *, n_pages, per_block):
    step = pl.program_id(0)
    n_steps = pl.num_programs(0)
    page = kbuf.shape[2]
    last_block = n_pages // per_block - 1

    def copies(at_step, buf_slot):
        b = at_step // N_HEADS
        h = at_step % N_HEADS
        out = []
        for slot in range(MOBA_TOPK):
            blk = sel_ref[(b * MOBA_TOPK + slot) * N_HEADS + h]
            for i in range(per_block):
                pg = pt_ref[b * n_pages + blk * per_block + i]
                idx = slot * per_block + i
                out.append(pltpu.make_async_copy(ck_ref.at[0, pg, :, h, :], kbuf.at[buf_slot, idx],
                                                 sem.at[buf_slot, 0]))
                out.append(pltpu.make_async_copy(cv_ref.at[0, pg, :, h, :], vbuf.at[buf_slot, idx],
                                                 sem.at[buf_slot, 1]))
        return out

    @pl.when(step == 0)
    def _():
        for cp in copies(step, 0):
            cp.start()

    @pl.when(step + 1 < n_steps)
    def _():
        for cp in copies(step + 1, (step + 1) % 2):
            cp.start()

    cur = step % 2
    for cp in copies(step, cur):
        cp.wait()

    b = step // N_HEADS
    h = step % N_HEADS
    q = q_ref[0] * (HEAD_DIM ** -0.5)
    q8 = jnp.broadcast_to(q, (8, HEAD_DIM)).astype(BF16)
    prev_row = tprev_ref[0, 0, 0:1, :]
    logits = []
    for slot in range(MOBA_TOPK):
        near = (sel_ref[(b * MOBA_TOPK + slot) * N_HEADS + h] == last_block).astype(F32)
        for i in range(per_block):
            kt = kbuf[cur, slot * per_block + i].astype(BF16)
            lg = lax.dot_general(q8, kt, NT_DIMS, preferred_element_type=F32)[0:1, :]
            logits.append(lg + near * prev_row[:, i * page:(i + 1) * page])
    l_own = jnp.sum(q * kn_ref[0], axis=1, keepdims=True) + town_ref[0, 0, 0:1, 0:1]
    m = l_own
    for lg in logits:
        m = jnp.maximum(m, jnp.max(lg, axis=1, keepdims=True))
    p_own = jnp.exp(l_own - m)
    den = p_own
    acc = p_own * vn_ref[0]
    for idx, lg in enumerate(logits):
        p = jnp.exp(lg - m)
        den = den + jnp.sum(p, axis=1, keepdims=True)
        p8 = jnp.broadcast_to(p, (8, page)).astype(BF16)
        acc = acc + jnp.dot(p8, vbuf[cur, idx].astype(BF16),
                            preferred_element_type=F32)[0:1, :]
    o_ref[0] = (acc / den).astype(o_ref.dtype)


def _sample_attn(q3, kn3, vn3, cache_k, cache_v, page_table_flat, sel_flat, tables, n_pages):
    n_req = q3.shape[0]
    page = cache_k.shape[2]
    per_block = MOBA_BLOCK // page
    n_slot_pages = MOBA_TOPK * per_block

    def row_index(i, pt, sel):
        return (i // N_HEADS, 0, i % N_HEADS)

    row_spec = pl.BlockSpec((1, 1, HEAD_DIM), row_index)
    return pl.pallas_call(
        functools.partial(_sample_attn_body, n_pages=n_pages, per_block=per_block),
        grid_spec=pltpu.PrefetchScalarGridSpec(
            num_scalar_prefetch=2,
            grid=(n_req * N_HEADS,),
            in_specs=[row_spec, row_spec, row_spec,
                      pl.BlockSpec((1, 1, 8, MOBA_BLOCK),
                                   lambda i, pt, sel: (i % N_HEADS, TAB_PREV, 0, 0)),
                      pl.BlockSpec((1, 1, 8, MOBA_BLOCK),
                                   lambda i, pt, sel: (i % N_HEADS, TAB_OWN, 0, 0)),
                      pl.BlockSpec(memory_space=pl.ANY),
                      pl.BlockSpec(memory_space=pl.ANY)],
            out_specs=row_spec,
            scratch_shapes=[pltpu.VMEM((2, n_slot_pages, page, HEAD_DIM), F32),
                            pltpu.VMEM((2, n_slot_pages, page, HEAD_DIM), F32),
                            pltpu.SemaphoreType.DMA((2, 2))],
        ),
        out_shape=jax.ShapeDtypeStruct((n_req, 1, ATTN_WIDTH), BF16),
        compiler_params=_params(("arbitrary",), 32),
        name="sample_attn",
    )(page_table_flat, sel_flat, q3, kn3, vn3, tables, tables, cache_k, cache_v)


def _cross_prompt_body(q_ref, mk_ref, mv_ref, o_ref):
    for h in range(CROSS_HEADS):
        sl = slice(h * CROSS_HEAD_DIM, (h + 1) * CROSS_HEAD_DIM)
        s = lax.dot_general(q_ref[:, sl], mk_ref[:, sl], NT_DIMS,
                            preferred_element_type=F32) * (CROSS_HEAD_DIM ** -0.5)
        p = jnp.exp(s - jnp.max(s, axis=1, keepdims=True))
        den = jnp.sum(p, axis=1, keepdims=True)
        o = jnp.dot(p.astype(BF16), mv_ref[:, sl], preferred_element_type=F32) / den
        o_ref[:, sl] = o.astype(o_ref.dtype)


def _cross_prompt(qc, mk_bf, mv_bf, bm=512):
    t = qc.shape[0]
    n_mem = mk_bf.shape[0]
    return pl.pallas_call(
        _cross_prompt_body,
        grid=(t // bm,),
        in_specs=[pl.BlockSpec((bm, CROSS_WIDTH), lambda i: (i, 0)),
                  pl.BlockSpec((n_mem, CROSS_WIDTH), lambda i: (0, 0)),
                  pl.BlockSpec((n_mem, CROSS_WIDTH), lambda i: (0, 0))],
        out_specs=pl.BlockSpec((bm, CROSS_WIDTH), lambda i: (i, 0)),
        out_shape=jax.ShapeDtypeStruct((t, CROSS_WIDTH), BF16),
        compiler_params=_params(("parallel",), 32),
        name="cross_prompt",
    )(qc, mk_bf, mv_bf)


def _cross_sample_body(q_ref, mk_ref, mv_ref, o_ref):
    for h in range(CROSS_HEADS):
        sl = slice(h * CROSS_HEAD_DIM, (h + 1) * CROSS_HEAD_DIM)
        q8 = jnp.broadcast_to(q_ref[0, :, sl], (8, CROSS_HEAD_DIM))
        s = lax.dot_general(q8, mk_ref[:, h, :].astype(BF16), NT_DIMS,
                            preferred_element_type=F32) * (CROSS_HEAD_DIM ** -0.5)
        p = jnp.exp(s - jnp.max(s, axis=1, keepdims=True))
        den = jnp.sum(p, axis=1, keepdims=True)
        o = jnp.dot(p.astype(BF16), mv_ref[:, h, :].astype(BF16),
                    preferred_element_type=F32) / den
        o_ref[0, :, sl] = o[0:1, :].astype(o_ref.dtype)


def _cross_sample(qc3, mem_k, mem_v):
    _, n_req, n_mem, _, _ = mem_k.shape
    row_spec = pl.BlockSpec((1, 1, CROSS_WIDTH), lambda b: (b, 0, 0))
    mem_spec = pl.BlockSpec((None, None, n_mem, CROSS_HEADS, CROSS_HEAD_DIM),
                            lambda b: (0, b, 0, 0, 0))
    return pl.pallas_call(
        _cross_sample_body,
        grid=(n_req,),
        in_specs=[row_spec, mem_spec, mem_spec],
        out_specs=row_spec,
        out_shape=jax.ShapeDtypeStruct((n_req, 1, CROSS_WIDTH), BF16),
        compiler_params=_params(("parallel",), 32),
        name="cross_sample",
    )(qc3, mem_k, mem_v)


CONV_TILE = 256
CONV_HALO = 32


def _conv_prompt_body(cur_ref, prev_ref, w_ref, cb_ref, g_ref, b_ref, o_ref, buf_ref, y_ref):
    i = pl.program_id(0)
    has_prev = (i > 0).astype(F32)
    buf_ref[0:CONV_HALO, :] = prev_ref[CONV_TILE - CONV_HALO:CONV_TILE, :] * has_prev
    buf_ref[CONV_HALO:CONV_HALO + CONV_TILE, :] = cur_ref[...]
    first = CONV_HALO - CONV_STATE
    for c in range(D_CONV // LANES):
        cs = slice(c * LANES, (c + 1) * LANES)
        acc = jnp.zeros((CONV_TILE, LANES), F32)
        for tap in range(CONV_WIDTH):
            acc = acc + buf_ref[first + tap:first + tap + CONV_TILE, cs] * w_ref[tap:tap + 1, cs]
        y_ref[:, cs] = acc + cb_ref[:, cs]
    y = _layer_norm(y_ref[...], g_ref[...], b_ref[...])
    o_ref[...] = jax.nn.silu(y).astype(o_ref.dtype)


def _conv_prompt(u, conv_w, conv_b, ln_g, ln_b):
    t = u.shape[0]
    vec = pl.BlockSpec((1, D_CONV), lambda i: (0, 0))
    return pl.pallas_call(
        _conv_prompt_body,
        grid=(t // CONV_TILE,),
        in_specs=[pl.BlockSpec((CONV_TILE, D_CONV), lambda i: (i, 0)),
                  pl.BlockSpec((CONV_TILE, D_CONV), lambda i: (jnp.maximum(i - 1, 0), 0)),
                  pl.BlockSpec((CONV_WIDTH, D_CONV), lambda i: (0, 0)),
                  vec, vec, vec],
        out_specs=pl.BlockSpec((CONV_TILE, D_CONV), lambda i: (i, 0)),
        out_shape=jax.ShapeDtypeStruct((t, D_CONV), BF16),
        scratch_shapes=[pltpu.VMEM((CONV_HALO + CONV_TILE, D_CONV), F32),
                        pltpu.VMEM((CONV_TILE, D_CONV), F32)],
        compiler_params=_params(("parallel",), 32),
        name="conv_prompt",
    )(u, u, conv_w, conv_b.reshape(1, D_CONV), ln_g.reshape(1, D_CONV), ln_b.reshape(1, D_CONV))


def _conv_sample_body(st_ref, u_ref, w_ref, cb_ref, g_ref, b_ref, o_ref):
    acc = u_ref[...] * w_ref[CONV_STATE:CONV_WIDTH, :]
    for tap in range(CONV_STATE):
        acc = acc + st_ref[tap] * w_ref[tap:tap + 1, :]
    y = _layer_norm(acc + cb_ref[...], g_ref[...], b_ref[...])
    o_ref[...] = jax.nn.silu(y).astype(o_ref.dtype)


def _conv_sample(state_t, u, conv_w, conv_b, ln_g, ln_b):
    n_req = u.shape[0]
    return pl.pallas_call(
        _conv_sample_body,
        out_shape=jax.ShapeDtypeStruct((n_req, D_CONV), BF16),
        name="conv_sample",
    )(state_t, u, conv_w, conv_b.reshape(1, D_CONV), ln_g.reshape(1, D_CONV),
      ln_b.reshape(1, D_CONV))


def kernel(x_prompt, x_sample, mem_prompt, cache_k, cache_v, state_conv, cache_mem_k, cache_mem_v,
           page_table, w_in, conv_w, conv_b, conv_ln_g, conv_ln_b, w_mem_kv, rel_bias, w_branch,
           w_out, ln1_g, ln1_b, w_gate, w_up, w_down, ln2_g, ln2_b):
    depth = w_in.shape[0]
    assert depth == 1 and x_prompt.shape[0] == 1 and x_sample.shape[1] == 1
    alpha = (2 * depth) ** 0.25
    t, d_model = x_prompt.shape[1], x_prompt.shape[2]
    n_req, n_pages = page_table.shape
    page = cache_k.shape[2]
    n_mem = mem_prompt.shape[1]
    assert (n_pages * page) % MOBA_BLOCK == 0

    sizes = (D_CONV, D_CONV, ATTN_WIDTH, ATTN_WIDTH, ATTN_WIDTH, CROSS_WIDTH, N_BRANCH * d_model)
    cuts = np.concatenate([[0], np.cumsum(sizes)])
    wa, wb, wq, wk, wv, wqc, wg = [w_in[0][:, int(cuts[i]):int(cuts[i + 1])].astype(BF16)
                                   for i in range(len(sizes))]
    wmk = w_mem_kv[0][:, :CROSS_WIDTH].astype(BF16)
    wmv = w_mem_kv[0][:, CROSS_WIDTH:].astype(BF16)
    w_br = w_branch[0].astype(BF16)
    w_o = w_out[0].astype(BF16)
    w_g = w_gate[0].astype(BF16)
    w_u = w_up[0].astype(BF16)
    w_d = w_down[0].astype(BF16)
    d_ff = w_g.shape[1]
    ff_bk = d_ff // 4 if (d_ff // 4) % LANES == 0 else d_ff

    tables = _bias_tables(rel_bias)

    def finish(x, conv_y, attn, cross, gates, bm_out, bm_ffn):
        mix_in = _branch_mix(conv_y, attn, cross, w_br, gates)
        h1, h1_bf = _mm_res_ln(mix_in, w_o, x, ln1_g[0], ln1_b[0], alpha, [F32, BF16],
                               bm=bm_out, bk=d_model)
        act = _proj(h1_bf, [w_g, w_u], "swiglu", [BF16])
        return _mm_res_ln(act, w_d, h1, ln2_g[0], ln2_b[0], alpha, [F32], bm=bm_ffn, bk=ff_bk)

    xp = x_prompt[0]
    xp_bf = xp.astype(BF16)
    u_p = _proj(xp_bf, [wa, wb], "glu", [F32])
    q_p = _proj(xp_bf, [wq], "none", [F32], bn=1024)
    k_p = _proj(xp_bf, [wk], "none", [F32], bn=1024)
    v_p = _proj(xp_bf, [wv], "none", [F32], bn=1024)
    qc_p = _proj(xp_bf, [wqc], "none", [BF16], bn=1024)
    gates_p = _proj(xp_bf, [wg], "sigmoid", [F32], bn=1024)
    conv_y_p = _conv_prompt(u_p, conv_w[0], conv_b[0], conv_ln_g[0], conv_ln_b[0])
    k_aug, v_pad = _moba_prep(k_p, v_p)
    attn_p = _moba_prompt(q_p, k_aug, v_pad, _block_mean(k_p), tables)
    mem_bf = mem_prompt[0].astype(BF16)
    mk, mk_bf = _proj(mem_bf, [wmk], "none", [F32, BF16], bn=1024)
    mv, mv_bf = _proj(mem_bf, [wmv], "none", [F32, BF16], bn=1024)
    cross_p = _cross_prompt(qc_p, mk_bf, mv_bf)
    y_p = finish(xp, conv_y_p, attn_p, cross_p, gates_p, bm_out=256, bm_ffn=512)

    xs = x_sample[:, 0]
    xs_bf = xs.astype(BF16)
    u_s = _proj(xs_bf, [wa, wb], "glu", [F32])
    q_s = _proj(xs_bf, [wq], "none", [F32], bn=1024)
    k_s = _proj(xs_bf, [wk], "none", [F32], bn=1024)
    v_s = _proj(xs_bf, [wv], "none", [F32], bn=1024)
    qc_s = _proj(xs_bf, [wqc], "none", [BF16], bn=1024)
    gates_s = _proj(xs_bf, [wg], "sigmoid", [F32], bn=1024)
    conv_y_s = _conv_sample(state_conv[0].transpose(1, 0, 2), u_s, conv_w[0], conv_b[0],
                            conv_ln_g[0], conv_ln_b[0])
    pt_flat = page_table.reshape(-1)
    sel = _paged_select(q_s.reshape(n_req, N_HEADS, HEAD_DIM), cache_k, pt_flat, n_pages)
    attn_s = _sample_attn(q_s.reshape(n_req, 1, ATTN_WIDTH), k_s.reshape(n_req, 1, ATTN_WIDTH),
                          v_s.reshape(n_req, 1, ATTN_WIDTH), cache_k, cache_v, pt_flat,
                          sel.reshape(-1), tables, n_pages)
    cross_s = _cross_sample(qc_s.reshape(n_req, 1, CROSS_WIDTH), cache_mem_k, cache_mem_v)
    y_s = finish(xs, conv_y_s, attn_s.reshape(n_req, ATTN_WIDTH),
                 cross_s.reshape(n_req, CROSS_WIDTH), gates_s, bm_out=256, bm_ffn=512)

    new_conv_sample = jnp.concatenate([state_conv[0][:, 1:], u_s[:, None, :]], axis=1)
    return (
        y_p[None],
        y_s[:, None],
        k_p.reshape(1, 1, t, N_HEADS, HEAD_DIM),
        v_p.reshape(1, 1, t, N_HEADS, HEAD_DIM),
        u_p[t - CONV_STATE:][None, None],
        mk.reshape(1, 1, n_mem, CROSS_HEADS, CROSS_HEAD_DIM),
        mv.reshape(1, 1, n_mem, CROSS_HEADS, CROSS_HEAD_DIM),
        k_s.reshape(1, n_req, 1, N_HEADS, HEAD_DIM),
        v_s.reshape(1, n_req, 1, N_HEADS, HEAD_DIM),
        new_conv_sample[None],
    )
```

```python
import functools
import math

import numpy as np
import jax
import jax.numpy as jnp
from jax import lax
from jax.experimental import pallas as pl
from jax.experimental.pallas import tpu as pltpu

F32 = jnp.float32
BF16 = jnp.bfloat16

N_HEADS = 8
HEAD_DIM = 128
ATTN_WIDTH = N_HEADS * HEAD_DIM
MOBA_BLOCK = 256
MOBA_TOPK = 3
D_CONV = 1024
CONV_WIDTH = 31
CONV_STATE = CONV_WIDTH - 1
CROSS_HEADS = 4
CROSS_HEAD_DIM = 256
CROSS_WIDTH = CROSS_HEADS * CROSS_HEAD_DIM
N_BRANCH = 3
BRANCH_WIDTH = 1024
NUM_BUCKETS = 32
MAX_EXACT = NUM_BUCKETS // 2
MAX_DISTANCE = 128
LN_EPS = 1e-5
MASKED = -1e30
LANES = 128
MIB = 2 ** 20

NT_DIMS = (((1,), (1,)), ((), ()))


def _params(semantics, vmem_mib):
    return pltpu.CompilerParams(dimension_semantics=semantics,
                                vmem_limit_bytes=vmem_mib * MIB)


def _layer_norm(y, g, b):
    mu = jnp.mean(y, axis=-1, keepdims=True)
    yc = y - mu
    var = jnp.mean(yc * yc, axis=-1, keepdims=True)
    return yc * lax.rsqrt(var + LN_EPS) * g + b


def _proj_body(*refs, n_w, n_out, epilogue):
    x_ref = refs[0]
    w_refs = refs[1:1 + n_w]
    o_refs = refs[1 + n_w:1 + n_w + n_out]
    wbf_refs = refs[1 + n_w + n_out:]

    @pl.when(pl.program_id(1) == 0)
    def _():
        for w, wbf in zip(w_refs, wbf_refs):
            wbf[...] = w[...].astype(BF16)

    x = x_ref[...]
    z = [jnp.dot(x, w[...], preferred_element_type=F32) for w in wbf_refs]
    if epilogue == "glu":
        y = z[0] * jax.nn.sigmoid(z[1])
    elif epilogue == "swiglu":
        y = jax.nn.silu(z[0]) * z[1]
    elif epilogue == "sigmoid":
        y = jax.nn.sigmoid(z[0])
    else:
        y = z[0]
    for o in o_refs:
        o[...] = y.astype(o.dtype)


def _proj(x, ws, n, epilogue, out_dtypes, bm=1024, bn=512, vmem_mib=48):
    m, k = x.shape
    bm = min(bm, m)
    bn = min(bn, n)
    while n % bn or any(c0 % bn for _, c0 in ws):
        bn //= 2
    assert m % bm == 0 and bn % LANES == 0
    outs = pl.pallas_call(
        functools.partial(_proj_body, n_w=len(ws), n_out=len(out_dtypes), epilogue=epilogue),
        grid=(n // bn, m // bm),
        in_specs=[pl.BlockSpec((bm, k), lambda j, i: (i, 0))]
        + [pl.BlockSpec((None, k, bn), lambda j, i, c=c0 // bn: (0, 0, c + j)) for _, c0 in ws],
        out_specs=[pl.BlockSpec((bm, bn), lambda j, i: (i, j)) for _ in out_dtypes],
        out_shape=[jax.ShapeDtypeStruct((m, n), dt) for dt in out_dtypes],
        scratch_shapes=[pltpu.VMEM((k, bn), BF16) for _ in ws],
        compiler_params=_params(("parallel", "arbitrary"), vmem_mib),
        name="proj_" + epilogue,
    )(x, *[w for w, _ in ws])
    return outs[0] if len(outs) == 1 else outs


def _mm_res_ln_body(a_ref, w_ref, r_ref, g_ref, b_ref, *rest, alpha, n_out):
    o_refs = rest[:n_out]
    acc_ref = rest[n_out]
    kk = pl.program_id(1)

    @pl.when(kk == 0)
    def _():
        acc_ref[...] = jnp.zeros_like(acc_ref)

    acc_ref[...] += jnp.dot(a_ref[...], w_ref[...], preferred_element_type=F32)

    @pl.when(kk == pl.num_programs(1) - 1)
    def _():
        y = alpha * r_ref[...] + acc_ref[...]
        out = _layer_norm(y, g_ref[...], b_ref[...])
        for o in o_refs:
            o[...] = out.astype(o.dtype)


def _mm_res_ln(a, w, res, g, b, alpha, out_dtypes, bm, bk, vmem_mib=48):
    m, k = a.shape
    n = w.shape[1]
    bm = min(bm, m)
    assert m % bm == 0 and k % bk == 0
    outs = pl.pallas_call(
        functools.partial(_mm_res_ln_body, alpha=alpha, n_out=len(out_dtypes)),
        grid=(m // bm, k // bk),
        in_specs=[
            pl.BlockSpec((bm, bk), lambda i, kk: (i, kk)),
            pl.BlockSpec((bk, n), lambda i, kk: (kk, 0)),
            pl.BlockSpec((bm, n), lambda i, kk: (i, 0)),
            pl.BlockSpec((1, n), lambda i, kk: (0, 0)),
            pl.BlockSpec((1, n), lambda i, kk: (0, 0)),
        ],
        out_specs=[pl.BlockSpec((bm, n), lambda i, kk: (i, 0)) for _ in out_dtypes],
        out_shape=[jax.ShapeDtypeStruct((m, n), dt) for dt in out_dtypes],
        scratch_shapes=[pltpu.VMEM((bm, n), F32)],
        compiler_params=_params(("parallel", "arbitrary"), vmem_mib),
        name="mm_res_ln",
    )(a, w, res, g.reshape(1, n), b.reshape(1, n))
    return outs[0] if len(outs) == 1 else outs


def _branch_body(*refs):
    br_refs = refs[0:N_BRANCH]
    w_refs = refs[N_BRANCH:2 * N_BRANCH]
    g_refs = refs[2 * N_BRANCH:3 * N_BRANCH]
    o_ref = refs[3 * N_BRANCH]
    wbf_refs = refs[3 * N_BRANCH + 1:]

    @pl.when(pl.program_id(1) == 0)
    def _():
        for w, wbf in zip(w_refs, wbf_refs):
            wbf[...] = w[...].astype(BF16)

    y = None
    for br, wbf, g in zip(br_refs, wbf_refs, g_refs):
        term = g[...] * jnp.dot(br[...], wbf[...], preferred_element_type=F32)
        y = term if y is None else y + term
    o_ref[...] = y.astype(o_ref.dtype)


def _branch_mix(conv_y, attn, cross, w_branch, gates, bm=1024, bn=512):
    m = conv_y.shape[0]
    d = w_branch.shape[3]
    bm = min(bm, m)
    nj = d // bn
    br_spec = pl.BlockSpec((bm, BRANCH_WIDTH), lambda j, i: (i, 0))
    return pl.pallas_call(
        _branch_body,
        grid=(nj, m // bm),
        in_specs=[br_spec, br_spec, br_spec]
        + [pl.BlockSpec((None, None, BRANCH_WIDTH, bn), lambda j, i, n=n: (0, n, 0, j))
           for n in range(N_BRANCH)]
        + [pl.BlockSpec((bm, bn), lambda j, i, n=n: (i, n * nj + j)) for n in range(N_BRANCH)],
        out_specs=pl.BlockSpec((bm, bn), lambda j, i: (i, j)),
        out_shape=jax.ShapeDtypeStruct((m, d), BF16),
        scratch_shapes=[pltpu.VMEM((BRANCH_WIDTH, bn), BF16) for _ in range(N_BRANCH)],
        compiler_params=_params(("parallel", "arbitrary"), 48),
        name="branch_mix",
    )(conv_y, attn, cross, w_branch, w_branch, w_branch, gates, gates, gates)


TAB_PREV = 0
TAB_OWN = 1


def _bias_tables_body(rb_ref, o_ref):
    h = pl.program_id(0)
    shape = (MOBA_BLOCK, MOBA_BLOCK)
    qi = lax.broadcasted_iota(jnp.int32, shape, 0)
    kj = lax.broadcasted_iota(jnp.int32, shape, 1)
    far = rb_ref[NUM_BUCKETS - 1, h]
    for slot, offset in ((TAB_PREV, MOBA_BLOCK), (TAB_OWN, 0)):
        dist = jnp.maximum(qi - kj + offset, 0)
        large = MAX_EXACT + (jnp.log(jnp.maximum(dist, 1).astype(F32) / MAX_EXACT)
                             / math.log(MAX_DISTANCE / MAX_EXACT)
                             * (NUM_BUCKETS - MAX_EXACT)).astype(jnp.int32)
        large = jnp.minimum(large, NUM_BUCKETS - 1)
        bucket = jnp.where(dist < MAX_EXACT, dist, large)
        val = jnp.zeros(shape, F32)
        for bkt in range(NUM_BUCKETS):
            val = jnp.where(bucket == bkt, rb_ref[bkt, h], val)
        val = val - far
        if slot == TAB_OWN:
            val = jnp.where(kj > qi, MASKED, val)
        o_ref[0, slot] = val


def _bias_tables(rel_bias):
    return pl.pallas_call(
        _bias_tables_body,
        grid=(N_HEADS,),
        in_specs=[pl.BlockSpec(memory_space=pltpu.SMEM)],
        out_specs=pl.BlockSpec((1, 2, MOBA_BLOCK, MOBA_BLOCK), lambda h: (h, 0, 0, 0)),
        out_shape=jax.ShapeDtypeStruct((N_HEADS, 2, MOBA_BLOCK, MOBA_BLOCK), F32),
        compiler_params=_params(("arbitrary",), 32),
        name="bias_tables",
    )(rel_bias)


def _block_mean_body(k_ref, o_ref):
    rows = k_ref.shape[0]
    x = k_ref[...].reshape(rows // MOBA_BLOCK, MOBA_BLOCK, k_ref.shape[1])
    o_ref[...] = jnp.sum(x, axis=1) * (1.0 / MOBA_BLOCK)


def _block_mean(k):
    t, w = k.shape
    nb = t // MOBA_BLOCK
    per = min(8, nb)
    assert nb % per == 0
    return pl.pallas_call(
        _block_mean_body,
        grid=(nb // per,),
        in_specs=[pl.BlockSpec((per * MOBA_BLOCK, w), lambda i: (i, 0))],
        out_specs=pl.BlockSpec((per, w), lambda i: (i, 0)),
        out_shape=jax.ShapeDtypeStruct((nb, w), F32),
        compiler_params=_params(("parallel",), 40),
        name="block_mean",
    )(k)


def _select_topk(gate, index):
    hit_any = jnp.zeros(gate.shape, jnp.bool_)
    for _ in range(MOBA_TOPK):
        top = jnp.max(gate, axis=1, keepdims=True)
        first = jnp.min(jnp.where(gate == top, index, float(LANES)), axis=1, keepdims=True)
        chosen = index == first
        hit_any = hit_any | (chosen & (top > -jnp.inf))
        gate = jnp.where(chosen, -jnp.inf, gate)
    return hit_any


MOBA_CHUNK = 4
MOBA_HEADS_PER_STEP = 2
MOBA_PAD = MOBA_CHUNK * MOBA_BLOCK
KAUG_WIDTH = HEAD_DIM + LANES


def _moba_prep_body(k_ref, v_ref, ka_ref, vp_ref):
    i = pl.program_id(0)
    rows = k_ref.shape[0]
    col = lax.broadcasted_iota(jnp.int32, (rows, LANES), 1)

    @pl.when(i == 0)
    def _():
        pad_hot = jnp.where(col == LANES - 1, 1.0, 0.0).astype(BF16)
        for h in range(N_HEADS):
            ka_ref[:, h * KAUG_WIDTH:h * KAUG_WIDTH + HEAD_DIM] = jnp.zeros((rows, HEAD_DIM), BF16)
            ka_ref[:, h * KAUG_WIDTH + HEAD_DIM:(h + 1) * KAUG_WIDTH] = pad_hot
        vp_ref[...] = jnp.zeros(vp_ref.shape, BF16)

    @pl.when(i > 0)
    def _():
        row = lax.broadcasted_iota(jnp.int32, (rows, LANES), 0) + (i - 1) * rows
        hot = jnp.where(jnp.right_shift(row, int(math.log2(MOBA_BLOCK))) == col,
                        1.0, 0.0).astype(BF16)
        for h in range(N_HEADS):
            ka_ref[:, h * KAUG_WIDTH:h * KAUG_WIDTH + HEAD_DIM] = (
                k_ref[:, h * HEAD_DIM:(h + 1) * HEAD_DIM].astype(BF16))
            ka_ref[:, h * KAUG_WIDTH + HEAD_DIM:(h + 1) * KAUG_WIDTH] = hot
        vp_ref[...] = v_ref[...].astype(BF16)


def _moba_prep(k, v):
    t = k.shape[0]
    rows = MOBA_PAD
    assert t % rows == 0
    data = pl.BlockSpec((rows, ATTN_WIDTH), lambda i: (jnp.maximum(i - 1, 0), 0))
    return pl.pallas_call(
        _moba_prep_body,
        grid=(t // rows + 1,),
        in_specs=[data, data],
        out_specs=[pl.BlockSpec((rows, N_HEADS * KAUG_WIDTH), lambda i: (i, 0)),
                   pl.BlockSpec((rows, ATTN_WIDTH), lambda i: (i, 0))],
        out_shape=[jax.ShapeDtypeStruct((t + rows, N_HEADS * KAUG_WIDTH), BF16),
                   jax.ShapeDtypeStruct((t + rows, ATTN_WIDTH), BF16)],
        compiler_params=_params(("parallel",), 40),
        name="moba_prep",
    )(k, v)


SELECT_ROWS = 1024


def _moba_select_body(q_ref, km_ref, o_ref):
    q = q_ref[...]
    gate = lax.dot_general(q, km_ref[...], NT_DIMS, precision=lax.Precision.HIGHEST,
                           preferred_element_type=F32)
    lane = lax.broadcasted_iota(jnp.int32, gate.shape, 1)
    row = lax.broadcasted_iota(jnp.int32, gate.shape, 0) + pl.program_id(0) * q.shape[0]
    own = jnp.right_shift(row, int(math.log2(MOBA_BLOCK)))
    gate = jnp.where(lane < own, gate, -jnp.inf)
    picked = _select_topk(gate, lane.astype(F32)) | (lane == own)
    pen = jnp.where(picked, 0.0, MASKED)
    o_ref[:, :HEAD_DIM] = (q * (HEAD_DIM ** -0.5)).astype(BF16)
    o_ref[:, HEAD_DIM:] = pen.astype(BF16)


def _moba_select(q, k_mean):
    t = q.shape[0]
    nb = t // MOBA_BLOCK
    rows = min(SELECT_ROWS, t)
    assert t % rows == 0 and MOBA_TOPK <= nb < LANES
    km = jnp.pad(k_mean, ((0, LANES - nb), (0, 0)))
    return pl.pallas_call(
        _moba_select_body,
        grid=(t // rows, N_HEADS),
        in_specs=[pl.BlockSpec((rows, HEAD_DIM), lambda i, h: (i, h)),
                  pl.BlockSpec((LANES, HEAD_DIM), lambda i, h: (0, h))],
        out_specs=pl.BlockSpec((rows, KAUG_WIDTH), lambda i, h: (i, h)),
        out_shape=jax.ShapeDtypeStruct((t, N_HEADS * KAUG_WIDTH), BF16),
        compiler_params=_params(("parallel", "parallel"), 40),
        name="moba_select",
    )(q, km)


def _moba_prompt_body(qa_ref, ka_ref, v_ref, tab_ref, o_ref):
    qb_idx = pl.program_id(1)
    heads = range(MOBA_HEADS_PER_STEP)
    span = MOBA_CHUNK * MOBA_BLOCK
    q_aug = [qa_ref[:, hh * KAUG_WIDTH:(hh + 1) * KAUG_WIDTH] for hh in heads]

    def logits(first_block):
        start = pl.multiple_of(first_block * MOBA_BLOCK, MOBA_BLOCK)
        return tuple(
            lax.dot_general(q_aug[hh], ka_ref[pl.ds(start, span), hh * KAUG_WIDTH:(hh + 1) * KAUG_WIDTH],
                            NT_DIMS, preferred_element_type=F32) for hh in heads)

    def absorb(first_block, state, s_all):
        start = pl.multiple_of(first_block * MOBA_BLOCK, MOBA_BLOCK)
        out = []
        for hh in heads:
            m, l, acc = state[hh]
            s = s_all[hh]
            m_new = jnp.maximum(m, jnp.max(s, axis=1, keepdims=True))
            scale = jnp.exp(m - m_new)
            p = jnp.exp(s - m_new)
            l = scale * l + jnp.sum(p, axis=1, keepdims=True)
            vals = v_ref[pl.ds(start, span), hh * HEAD_DIM:(hh + 1) * HEAD_DIM]
            acc = scale * acc + jnp.dot(p.astype(BF16), vals, preferred_element_type=F32)
            out.append((m_new, l, acc))
        return tuple(out)

    init = tuple((jnp.full((MOBA_BLOCK, 1), 0.1 * MASKED, F32), jnp.zeros((MOBA_BLOCK, 1), F32),
                  jnp.zeros((MOBA_BLOCK, HEAD_DIM), F32)) for _ in heads)
    first = qb_idx % MOBA_CHUNK + 1

    def trip(c, carry):
        state, s_cur = carry
        s_next = logits(first + (c + 1) * MOBA_CHUNK)
        return absorb(first + c * MOBA_CHUNK, state, s_cur), s_next

    state, s_last = lax.fori_loop(0, qb_idx // MOBA_CHUNK, trip, (init, logits(first)))
    unbiased = jnp.zeros((MOBA_BLOCK, span - 2 * MOBA_BLOCK), F32)
    s_last = tuple(
        s_last[hh] + jnp.concatenate([unbiased, tab_ref[hh, TAB_PREV], tab_ref[hh, TAB_OWN]], axis=1)
        for hh in heads)
    state = absorb(qb_idx + 1, state, s_last)
    for hh in heads:
        _, l, acc = state[hh]
        o_ref[:, hh * HEAD_DIM:(hh + 1) * HEAD_DIM] = (acc / l).astype(o_ref.dtype)


def _moba_prompt(q_aug, k_aug, v_pad, tables):
    t = q_aug.shape[0]
    nb = t // MOBA_BLOCK
    hb = MOBA_HEADS_PER_STEP
    assert t % MOBA_BLOCK == 0 and MOBA_CHUNK >= 2
    return pl.pallas_call(
        _moba_prompt_body,
        grid=(N_HEADS // hb, nb),
        in_specs=[
            pl.BlockSpec((MOBA_BLOCK, hb * KAUG_WIDTH), lambda h, i: (i, h)),
            pl.BlockSpec((t + MOBA_PAD, hb * KAUG_WIDTH), lambda h, i: (0, h)),
            pl.BlockSpec((t + MOBA_PAD, hb * HEAD_DIM), lambda h, i: (0, h)),
            pl.BlockSpec((hb, 2, MOBA_BLOCK, MOBA_BLOCK), lambda h, i: (h, 0, 0, 0)),
        ],
        out_specs=pl.BlockSpec((MOBA_BLOCK, hb * HEAD_DIM), lambda h, i: (i, h)),
        out_shape=jax.ShapeDtypeStruct((t, ATTN_WIDTH), BF16),
        compiler_params=_params(("parallel", "parallel"), 56),
        name="moba_prompt",
    )(q_aug, k_aug, v_pad, tables)


PAGES_PER_STEP = 16


def _paged_select_body(pt_ref, q_ref, *refs):
    pages = refs[:PAGES_PER_STEP]
    o_ref = refs[PAGES_PER_STEP]
    km_ref = refs[PAGES_PER_STEP + 1]
    g = pl.program_id(1)
    per_block = MOBA_BLOCK // pages[0].shape[0]
    blocks_per_step = PAGES_PER_STEP // per_block
    for blk in range(blocks_per_step):
        tot = jnp.sum(pages[blk * per_block][...], axis=0)
        for extra in range(1, per_block):
            tot = tot + jnp.sum(pages[blk * per_block + extra][...], axis=0)
        km_ref[g * blocks_per_step + blk] = tot * (1.0 / MOBA_BLOCK)

    @pl.when(g == pl.num_programs(1) - 1)
    def _():
        km = km_ref[...]
        gate = jnp.sum(km * q_ref[0], axis=-1, keepdims=True)
        gate = jnp.broadcast_to(gate, km.shape)
        blk_id = lax.broadcasted_iota(jnp.int32, km.shape, 0).astype(F32)
        for slot in range(MOBA_TOPK):
            top = jnp.max(gate, axis=0, keepdims=True)
            first = jnp.min(jnp.where(gate == top, blk_id, float(km.shape[0])),
                            axis=0, keepdims=True)
            o_ref[0, slot] = first[0].astype(jnp.int32)
            gate = jnp.where(blk_id == first, -jnp.inf, gate)


def _paged_select(q_heads, cache, page_table_flat, n_pages):
    n_req = q_heads.shape[0]
    page = cache.shape[2]
    per_block = MOBA_BLOCK // page
    nbp = n_pages // per_block
    assert n_pages % PAGES_PER_STEP == 0 and MOBA_BLOCK % page == 0 and nbp >= MOBA_TOPK
    out = pl.pallas_call(
        _paged_select_body,
        grid_spec=pltpu.PrefetchScalarGridSpec(
            num_scalar_prefetch=1,
            grid=(n_req, n_pages // PAGES_PER_STEP),
            in_specs=[pl.BlockSpec((1, N_HEADS, HEAD_DIM), lambda b, g, pt: (b, 0, 0))]
            + [pl.BlockSpec(
                (None, None, page, N_HEADS, HEAD_DIM),
                lambda b, g, pt, i=i: (0, pt[b * n_pages + g * PAGES_PER_STEP + i], 0, 0, 0))
               for i in range(PAGES_PER_STEP)],
            out_specs=pl.BlockSpec((1, MOBA_TOPK, N_HEADS, HEAD_DIM),
                                   lambda b, g, pt: (b, 0, 0, 0)),
            scratch_shapes=[pltpu.VMEM((nbp, N_HEADS, HEAD_DIM), F32)],
        ),
        out_shape=jax.ShapeDtypeStruct((n_req, MOBA_TOPK, N_HEADS, HEAD_DIM), jnp.int32),
        compiler_params=_params(("parallel", "arbitrary"), 40),
        name="paged_select",
    )(page_table_flat, q_heads, *([cache] * PAGES_PER_STEP))
    return out[:, :, :, 0]


def _sample_attn_body(pt_ref, sel_ref, q_ref, kn_ref, vn_ref, tprev_ref, town_ref, ck_ref, cv_ref,
                      o_ref, kbuf, vbuf, sem, *, n_pages, per_block):
    b = pl.program_id(0)
    page = kbuf.shape[2]
    last_block = n_pages // per_block - 1

    def copies(req, buf_slot):
        out = []
        for h in range(N_HEADS):
            for slot in range(MOBA_TOPK):
                blk = sel_ref[(req * MOBA_TOPK + slot) * N_HEADS + h]
                for i in range(per_block):
                    pg = pt_ref[req * n_pages + blk * per_block + i]
                    idx = (h * MOBA_TOPK + slot) * per_block + i
                    out.append(pltpu.make_async_copy(ck_ref.at[0, pg, :, h, :],
                                                     kbuf.at[buf_slot, idx], sem.at[buf_slot, 0]))
                    out.append(pltpu.make_async_copy(cv_ref.at[0, pg, :, h, :],
                                                     vbuf.at[buf_slot, idx], sem.at[buf_slot, 1]))
        return out

    @pl.when(b == 0)
    def _():
        for cp in copies(b, 0):
            cp.start()

    @pl.when(b + 1 < pl.num_programs(0))
    def _():
        for cp in copies(b + 1, (b + 1) % 2):
            cp.start()

    cur = b % 2
    for cp in copies(b, cur):
        cp.wait()

    for h in range(N_HEADS):
        hs = slice(h * HEAD_DIM, (h + 1) * HEAD_DIM)
        q = q_ref[0, :, hs] * (HEAD_DIM ** -0.5)
        q8 = jnp.broadcast_to(q, (8, HEAD_DIM)).astype(BF16)
        prev_row = tprev_ref[h, 0, 0:1, :]
        logits = []
        for slot in range(MOBA_TOPK):
            near = (sel_ref[(b * MOBA_TOPK + slot) * N_HEADS + h] == last_block).astype(F32)
            for i in range(per_block):
                kt = kbuf[cur, (h * MOBA_TOPK + slot) * per_block + i].astype(BF16)
                lg = lax.dot_general(q8, kt, NT_DIMS, preferred_element_type=F32)[0:1, :]
                logits.append(lg + near * prev_row[:, i * page:(i + 1) * page])
        l_own = (jnp.sum(q * kn_ref[0, :, hs], axis=1, keepdims=True)
                 + town_ref[h, 0, 0:1, 0:1])
        m = l_own
        for lg in logits:
            m = jnp.maximum(m, jnp.max(lg, axis=1, keepdims=True))
        p_own = jnp.exp(l_own - m)
        den = p_own
        acc = p_own * vn_ref[0, :, hs]
        for idx, lg in enumerate(logits):
            p = jnp.exp(lg - m)
            den = den + jnp.sum(p, axis=1, keepdims=True)
            p8 = jnp.broadcast_to(p, (8, page)).astype(BF16)
            acc = acc + jnp.dot(p8, vbuf[cur, h * MOBA_TOPK * per_block + idx].astype(BF16),
                                preferred_element_type=F32)[0:1, :]
        o_ref[0, :, hs] = (acc / den).astype(o_ref.dtype)


def _sample_attn(q3, kn3, vn3, cache_k, cache_v, page_table_flat, sel_flat, tables, n_pages):
    n_req = q3.shape[0]
    page = cache_k.shape[2]
    per_block = MOBA_BLOCK // page
    n_bufs = N_HEADS * MOBA_TOPK * per_block
    row_spec = pl.BlockSpec((1, 1, ATTN_WIDTH), lambda b, pt, sel: (b, 0, 0))
    return pl.pallas_call(
        functools.partial(_sample_attn_body, n_pages=n_pages, per_block=per_block),
        grid_spec=pltpu.PrefetchScalarGridSpec(
            num_scalar_prefetch=2,
            grid=(n_req,),
            in_specs=[row_spec, row_spec, row_spec,
                      pl.BlockSpec((N_HEADS, 1, 8, MOBA_BLOCK),
                                   lambda b, pt, sel: (0, TAB_PREV, 0, 0)),
                      pl.BlockSpec((N_HEADS, 1, 8, MOBA_BLOCK),
                                   lambda b, pt, sel: (0, TAB_OWN, 0, 0)),
                      pl.BlockSpec(memory_space=pl.ANY),
                      pl.BlockSpec(memory_space=pl.ANY)],
            out_specs=row_spec,
            scratch_shapes=[pltpu.VMEM((2, n_bufs, page, HEAD_DIM), F32),
                            pltpu.VMEM((2, n_bufs, page, HEAD_DIM), F32),
                            pltpu.SemaphoreType.DMA((2, 2))],
        ),
        out_shape=jax.ShapeDtypeStruct((n_req, 1, ATTN_WIDTH), BF16),
        compiler_params=_params(("arbitrary",), 40),
        name="sample_attn",
    )(page_table_flat, sel_flat, q3, kn3, vn3, tables, tables, cache_k, cache_v)


def _cross_prompt_body(q_ref, mk_ref, mv_ref, o_ref):
    for h in range(CROSS_HEADS):
        sl = slice(h * CROSS_HEAD_DIM, (h + 1) * CROSS_HEAD_DIM)
        s = lax.dot_general(q_ref[:, sl], mk_ref[:, sl], NT_DIMS,
                            preferred_element_type=F32) * (CROSS_HEAD_DIM ** -0.5)
        p = jnp.exp(s - jnp.max(s, axis=1, keepdims=True))
        den = jnp.sum(p, axis=1, keepdims=True)
        o = jnp.dot(p.astype(BF16), mv_ref[:, sl], preferred_element_type=F32) / den
        o_ref[:, sl] = o.astype(o_ref.dtype)


def _cross_prompt(qc, mk_bf, mv_bf, bm=512):
    t = qc.shape[0]
    n_mem = mk_bf.shape[0]
    return pl.pallas_call(
        _cross_prompt_body,
        grid=(t // bm,),
        in_specs=[pl.BlockSpec((bm, CROSS_WIDTH), lambda i: (i, 0)),
                  pl.BlockSpec((n_mem, CROSS_WIDTH), lambda i: (0, 0)),
                  pl.BlockSpec((n_mem, CROSS_WIDTH), lambda i: (0, 0))],
        out_specs=pl.BlockSpec((bm, CROSS_WIDTH), lambda i: (i, 0)),
        out_shape=jax.ShapeDtypeStruct((t, CROSS_WIDTH), BF16),
        compiler_params=_params(("parallel",), 32),
        name="cross_prompt",
    )(qc, mk_bf, mv_bf)


def _cross_sample_body(q_ref, mk_hbm, mv_hbm, o_ref, kbuf, vbuf, sem):
    b = pl.program_id(0)

    def copies(req, buf_slot):
        out = []
        for h in range(CROSS_HEADS):
            out.append(pltpu.make_async_copy(mk_hbm.at[0, req, :, h, :], kbuf.at[buf_slot, h],
                                             sem.at[buf_slot, 0]))
            out.append(pltpu.make_async_copy(mv_hbm.at[0, req, :, h, :], vbuf.at[buf_slot, h],
                                             sem.at[buf_slot, 1]))
        return out

    @pl.when(b == 0)
    def _():
        for cp in copies(b, 0):
            cp.start()

    @pl.when(b + 1 < pl.num_programs(0))
    def _():
        for cp in copies(b + 1, (b + 1) % 2):
            cp.start()

    cur = b % 2
    for cp in copies(b, cur):
        cp.wait()

    for h in range(CROSS_HEADS):
        sl = slice(h * CROSS_HEAD_DIM, (h + 1) * CROSS_HEAD_DIM)
        q8 = jnp.broadcast_to(q_ref[0, :, sl], (8, CROSS_HEAD_DIM))
        s = lax.dot_general(q8, kbuf[cur, h].astype(BF16), NT_DIMS,
                            preferred_element_type=F32) * (CROSS_HEAD_DIM ** -0.5)
        p = jnp.exp(s - jnp.max(s, axis=1, keepdims=True))
        den = jnp.sum(p, axis=1, keepdims=True)
        o = jnp.dot(p.astype(BF16), vbuf[cur, h].astype(BF16),
                    preferred_element_type=F32) / den
        o_ref[0, :, sl] = o[0:1, :].astype(o_ref.dtype)


def _cross_sample(qc3, mem_k, mem_v):
    _, n_req, n_mem, _, _ = mem_k.shape
    row_spec = pl.BlockSpec((1, 1, CROSS_WIDTH), lambda b: (b, 0, 0))
    any_spec = pl.BlockSpec(memory_space=pl.ANY)
    return pl.pallas_call(
        _cross_sample_body,
        grid=(n_req,),
        in_specs=[row_spec, any_spec, any_spec],
        out_specs=row_spec,
        out_shape=jax.ShapeDtypeStruct((n_req, 1, CROSS_WIDTH), BF16),
        scratch_shapes=[pltpu.VMEM((2, CROSS_HEADS, n_mem, CROSS_HEAD_DIM), F32),
                        pltpu.VMEM((2, CROSS_HEADS, n_mem, CROSS_HEAD_DIM), F32),
                        pltpu.SemaphoreType.DMA((2, 2))],
        compiler_params=_params(("arbitrary",), 32),
        name="cross_sample",
    )(qc3, mem_k, mem_v)


CONV_TILE = 256
CONV_HALO = 32


def _conv_prompt_body(cur_ref, prev_ref, w_ref, cb_ref, g_ref, b_ref, o_ref, buf_ref, y_ref):
    i = pl.program_id(0)
    has_prev = (i > 0).astype(F32)
    buf_ref[0:CONV_HALO, :] = prev_ref[CONV_TILE - CONV_HALO:CONV_TILE, :] * has_prev
    buf_ref[CONV_HALO:CONV_HALO + CONV_TILE, :] = cur_ref[...]
    first = CONV_HALO - CONV_STATE
    for c in range(D_CONV // LANES):
        cs = slice(c * LANES, (c + 1) * LANES)
        acc = jnp.zeros((CONV_TILE, LANES), F32)
        for tap in range(CONV_WIDTH):
            acc = acc + buf_ref[first + tap:first + tap + CONV_TILE, cs] * w_ref[tap:tap + 1, cs]
        y_ref[:, cs] = acc + cb_ref[:, cs]
    y = _layer_norm(y_ref[...], g_ref[...], b_ref[...])
    o_ref[...] = jax.nn.silu(y).astype(o_ref.dtype)


def _conv_prompt(u, conv_w, conv_b, ln_g, ln_b):
    t = u.shape[0]
    vec = pl.BlockSpec((1, D_CONV), lambda i: (0, 0))
    return pl.pallas_call(
        _conv_prompt_body,
        grid=(t // CONV_TILE,),
        in_specs=[pl.BlockSpec((CONV_TILE, D_CONV), lambda i: (i, 0)),
                  pl.BlockSpec((CONV_TILE, D_CONV), lambda i: (jnp.maximum(i - 1, 0), 0)),
                  pl.BlockSpec((CONV_WIDTH, D_CONV), lambda i: (0, 0)),
                  vec, vec, vec],
        out_specs=pl.BlockSpec((CONV_TILE, D_CONV), lambda i: (i, 0)),
        out_shape=jax.ShapeDtypeStruct((t, D_CONV), BF16),
        scratch_shapes=[pltpu.VMEM((CONV_HALO + CONV_TILE, D_CONV), F32),
                        pltpu.VMEM((CONV_TILE, D_CONV), F32)],
        compiler_params=_params(("parallel",), 32),
        name="conv_prompt",
    )(u, u, conv_w, conv_b.reshape(1, D_CONV), ln_g.reshape(1, D_CONV), ln_b.reshape(1, D_CONV))


def _conv_sample_body(st_ref, u_ref, w_ref, cb_ref, g_ref, b_ref, o_ref):
    acc = u_ref[...] * w_ref[CONV_STATE:CONV_WIDTH, :]
    for tap in range(CONV_STATE):
        acc = acc + st_ref[tap] * w_ref[tap:tap + 1, :]
    y = _layer_norm(acc + cb_ref[...], g_ref[...], b_ref[...])
    o_ref[...] = jax.nn.silu(y).astype(o_ref.dtype)


def _conv_sample(state_t, u, conv_w, conv_b, ln_g, ln_b):
    n_req = u.shape[0]
    return pl.pallas_call(
        _conv_sample_body,
        out_shape=jax.ShapeDtypeStruct((n_req, D_CONV), BF16),
        name="conv_sample",
    )(state_t, u, conv_w, conv_b.reshape(1, D_CONV), ln_g.reshape(1, D_CONV),
      ln_b.reshape(1, D_CONV))


def kernel(x_prompt, x_sample, mem_prompt, cache_k, cache_v, state_conv, cache_mem_k, cache_mem_v,
           page_table, w_in, conv_w, conv_b, conv_ln_g, conv_ln_b, w_mem_kv, rel_bias, w_branch,
           w_out, ln1_g, ln1_b, w_gate, w_up, w_down, ln2_g, ln2_b):
    depth = w_in.shape[0]
    assert depth == 1 and x_prompt.shape[0] == 1 and x_sample.shape[1] == 1
    alpha = (2 * depth) ** 0.25
    t, d_model = x_prompt.shape[1], x_prompt.shape[2]
    n_req, n_pages = page_table.shape
    page = cache_k.shape[2]
    n_mem = mem_prompt.shape[1]
    assert (n_pages * page) % MOBA_BLOCK == 0

    sizes = (D_CONV, D_CONV, ATTN_WIDTH, ATTN_WIDTH, ATTN_WIDTH, CROSS_WIDTH, N_BRANCH * d_model)
    col_a, col_b, col_q, col_k, col_v, col_qc, col_g = (
        int(c) for c in np.concatenate([[0], np.cumsum(sizes)[:-1]]))
    w_o = w_out[0].astype(BF16)
    w_d = w_down[0].astype(BF16)
    d_ff = w_gate.shape[2]
    ff_bk = d_ff // 4 if (d_ff // 4) % LANES == 0 else d_ff

    tables = _bias_tables(rel_bias)

    def project_in(x_bf):
        u = _proj(x_bf, [(w_in, col_a), (w_in, col_b)], D_CONV, "glu", [F32])
        q = _proj(x_bf, [(w_in, col_q)], ATTN_WIDTH, "none", [F32], bn=1024)
        k = _proj(x_bf, [(w_in, col_k)], ATTN_WIDTH, "none", [F32], bn=1024)
        v = _proj(x_bf, [(w_in, col_v)], ATTN_WIDTH, "none", [F32], bn=1024)
        qc = _proj(x_bf, [(w_in, col_qc)], CROSS_WIDTH, "none", [BF16], bn=1024)
        gates = _proj(x_bf, [(w_in, col_g)], N_BRANCH * d_model, "sigmoid", [F32], bn=1024)
        return u, q, k, v, qc, gates

    def finish(x, conv_y, attn, cross, gates, bm_out, bm_ffn):
        mix_in = _branch_mix(conv_y, attn, cross, w_branch, gates)
        h1, h1_bf = _mm_res_ln(mix_in, w_o, x, ln1_g[0], ln1_b[0], alpha, [F32, BF16],
                               bm=bm_out, bk=d_model)
        act = _proj(h1_bf, [(w_gate, 0), (w_up, 0)], d_ff, "swiglu", [BF16])
        return _mm_res_ln(act, w_d, h1, ln2_g[0], ln2_b[0], alpha, [F32], bm=bm_ffn, bk=ff_bk)

    xp = x_prompt[0]
    u_p, q_p, k_p, v_p, qc_p, gates_p = project_in(xp.astype(BF16))
    conv_y_p = _conv_prompt(u_p, conv_w[0], conv_b[0], conv_ln_g[0], conv_ln_b[0])
    k_aug, v_pad = _moba_prep(k_p, v_p)
    attn_p = _moba_prompt(_moba_select(q_p, _block_mean(k_p)), k_aug, v_pad, tables)
    mem_bf = mem_prompt[0].astype(BF16)
    mk, mk_bf = _proj(mem_bf, [(w_mem_kv, 0)], CROSS_WIDTH, "none", [F32, BF16], bn=1024)
    mv, mv_bf = _proj(mem_bf, [(w_mem_kv, CROSS_WIDTH)], CROSS_WIDTH, "none", [F32, BF16], bn=1024)
    cross_p = _cross_prompt(qc_p, mk_bf, mv_bf)
    y_p = finish(xp, conv_y_p, attn_p, cross_p, gates_p, bm_out=256, bm_ffn=512)

    xs = x_sample[:, 0]
    u_s, q_s, k_s, v_s, qc_s, gates_s = project_in(xs.astype(BF16))
    conv_y_s = _conv_sample(state_conv[0].transpose(1, 0, 2), u_s, conv_w[0], conv_b[0],
                            conv_ln_g[0], conv_ln_b[0])
    pt_flat = page_table.reshape(-1)
    sel = _paged_select(q_s.reshape(n_req, N_HEADS, HEAD_DIM), cache_k, pt_flat, n_pages)
    attn_s = _sample_attn(q_s.reshape(n_req, 1, ATTN_WIDTH), k_s.reshape(n_req, 1, ATTN_WIDTH),
                          v_s.reshape(n_req, 1, ATTN_WIDTH), cache_k, cache_v, pt_flat,
                          sel.reshape(-1), tables, n_pages)
    cross_s = _cross_sample(qc_s.reshape(n_req, 1, CROSS_WIDTH), cache_mem_k, cache_mem_v)
    y_s = finish(xs, conv_y_s, attn_s.reshape(n_req, ATTN_WIDTH),
                 cross_s.reshape(n_req, CROSS_WIDTH), gates_s, bm_out=256, bm_ffn=512)

    new_conv_sample = jnp.concatenate([state_conv[0][:, 1:], u_s[:, None, :]], axis=1)
    return (
        y_p[None],
        y_s[:, None],
        k_p.reshape(1, 1, t, N_HEADS, HEAD_DIM),
        v_p.reshape(1, 1, t, N_HEADS, HEAD_DIM),
        u_p[t - CONV_STATE:][None, None],
        mk.reshape(1, 1, n_mem, CROSS_HEADS, CROSS_HEAD_DIM),
        mv.reshape(1, 1, n_mem, CROSS_HEADS, CROSS_HEAD_DIM),
        k_s.reshape(1, n_req, 1, N_HEADS, HEAD_DIM),
        v_s.reshape(1, n_req, 1, N_HEADS, HEAD_DIM),
        new_conv_sample[None],
    )
```

```python
import functools
import math

import numpy as np
import jax
import jax.numpy as jnp
from jax import lax
from jax.experimental import pallas as pl
from jax.experimental.pallas import tpu as pltpu

F32 = jnp.float32
BF16 = jnp.bfloat16

N_HEADS = 8
HEAD_DIM = 128
ATTN_WIDTH = N_HEADS * HEAD_DIM
MOBA_BLOCK = 256
MOBA_TOPK = 3
D_CONV = 1024
CONV_WIDTH = 31
CONV_STATE = CONV_WIDTH - 1
CROSS_HEADS = 4
CROSS_HEAD_DIM = 256
CROSS_WIDTH = CROSS_HEADS * CROSS_HEAD_DIM
N_BRANCH = 3
BRANCH_WIDTH = 1024
NUM_BUCKETS = 32
MAX_EXACT = NUM_BUCKETS // 2
MAX_DISTANCE = 128
LN_EPS = 1e-5
MASKED = -1e30
LANES = 128
SUBLANES = 8
MIB = 2 ** 20

NT_DIMS = (((1,), (1,)), ((), ()))


def _params(semantics, vmem_mib):
    return pltpu.CompilerParams(dimension_semantics=semantics,
                                vmem_limit_bytes=vmem_mib * MIB)


def _layer_norm(y, g, b):
    mu = jnp.mean(y, axis=-1, keepdims=True)
    yc = y - mu
    var = jnp.mean(yc * yc, axis=-1, keepdims=True)
    return yc * lax.rsqrt(var + LN_EPS) * g + b


def _proj_body(*refs, n_w, n_out, epilogue):
    x_ref = refs[0]
    w_refs = refs[1:1 + n_w]
    o_refs = refs[1 + n_w:1 + n_w + n_out]
    wbf_refs = refs[1 + n_w + n_out:]

    @pl.when(pl.program_id(1) == 0)
    def _():
        for w, wbf in zip(w_refs, wbf_refs):
            wbf[...] = w[...].astype(BF16)

    x = x_ref[...]
    z = [jnp.dot(x, w[...], preferred_element_type=F32) for w in wbf_refs]
    if epilogue == "glu":
        y = z[0] * jax.nn.sigmoid(z[1])
    elif epilogue == "swiglu":
        y = jax.nn.silu(z[0]) * z[1]
    elif epilogue == "sigmoid":
        y = jax.nn.sigmoid(z[0])
    else:
        y = z[0]
    for o in o_refs:
        o[...] = y.astype(o.dtype)


def _proj(x, ws, n, epilogue, out_dtypes, bm=1024, bn=512, vmem_mib=48):
    m, k = x.shape
    bm = min(bm, m)
    bn = min(bn, n)
    while n % bn or any(c0 % bn for _, c0 in ws):
        bn //= 2
    assert m % bm == 0 and bn % LANES == 0
    outs = pl.pallas_call(
        functools.partial(_proj_body, n_w=len(ws), n_out=len(out_dtypes), epilogue=epilogue),
        grid=(n // bn, m // bm),
        in_specs=[pl.BlockSpec((bm, k), lambda j, i: (i, 0))]
        + [pl.BlockSpec((None, k, bn), lambda j, i, c=c0 // bn: (0, 0, c + j)) for _, c0 in ws],
        out_specs=[pl.BlockSpec((bm, bn), lambda j, i: (i, j)) for _ in out_dtypes],
        out_shape=[jax.ShapeDtypeStruct((m, n), dt) for dt in out_dtypes],
        scratch_shapes=[pltpu.VMEM((k, bn), BF16) for _ in ws],
        compiler_params=_params(("parallel", "arbitrary"), vmem_mib),
        name="proj_" + epilogue,
    )(x, *[w for w, _ in ws])
    return outs[0] if len(outs) == 1 else outs


def _mm_res_ln_body(a_ref, w_ref, r_ref, g_ref, b_ref, *rest, alpha, n_out):
    o_refs = rest[:n_out]
    acc_ref = rest[n_out]
    kk = pl.program_id(1)

    @pl.when(kk == 0)
    def _():
        acc_ref[...] = jnp.zeros_like(acc_ref)

    acc_ref[...] += jnp.dot(a_ref[...], w_ref[...], preferred_element_type=F32)

    @pl.when(kk == pl.num_programs(1) - 1)
    def _():
        y = alpha * r_ref[...] + acc_ref[...]
        out = _layer_norm(y, g_ref[...], b_ref[...])
        for o in o_refs:
            o[...] = out.astype(o.dtype)


def _mm_res_ln(a, w, res, g, b, alpha, out_dtypes, bm, bk, vmem_mib=56):
    m, k = a.shape
    n = w.shape[1]
    bm = min(bm, m)
    assert m % bm == 0 and k % bk == 0
    outs = pl.pallas_call(
        functools.partial(_mm_res_ln_body, alpha=alpha, n_out=len(out_dtypes)),
        grid=(m // bm, k // bk),
        in_specs=[
            pl.BlockSpec((bm, bk), lambda i, kk: (i, kk)),
            pl.BlockSpec((bk, n), lambda i, kk: (kk, 0)),
            pl.BlockSpec((bm, n), lambda i, kk: (i, 0)),
            pl.BlockSpec((1, n), lambda i, kk: (0, 0)),
            pl.BlockSpec((1, n), lambda i, kk: (0, 0)),
        ],
        out_specs=[pl.BlockSpec((bm, n), lambda i, kk: (i, 0)) for _ in out_dtypes],
        out_shape=[jax.ShapeDtypeStruct((m, n), dt) for dt in out_dtypes],
        scratch_shapes=[pltpu.VMEM((bm, n), F32)],
        compiler_params=_params(("parallel", "arbitrary"), vmem_mib),
        name="mm_res_ln",
    )(a, w, res, g.reshape(1, n), b.reshape(1, n))
    return outs[0] if len(outs) == 1 else outs


def _branch_body(*refs):
    br_refs = refs[0:N_BRANCH]
    w_refs = refs[N_BRANCH:2 * N_BRANCH]
    g_refs = refs[2 * N_BRANCH:3 * N_BRANCH]
    o_ref = refs[3 * N_BRANCH]
    wbf_refs = refs[3 * N_BRANCH + 1:]

    @pl.when(pl.program_id(1) == 0)
    def _():
        for w, wbf in zip(w_refs, wbf_refs):
            wbf[...] = w[...].astype(BF16)

    y = None
    for br, wbf, g in zip(br_refs, wbf_refs, g_refs):
        term = g[...] * jnp.dot(br[...], wbf[...], preferred_element_type=F32)
        y = term if y is None else y + term
    o_ref[...] = y.astype(o_ref.dtype)


def _branch_mix(conv_y, attn, cross, w_branch, gates, bm=1024, bn=512):
    m = conv_y.shape[0]
    d = w_branch.shape[3]
    bm = min(bm, m)
    nj = d // bn
    br_spec = pl.BlockSpec((bm, BRANCH_WIDTH), lambda j, i: (i, 0))
    return pl.pallas_call(
        _branch_body,
        grid=(nj, m // bm),
        in_specs=[br_spec, br_spec, br_spec]
        + [pl.BlockSpec((None, None, BRANCH_WIDTH, bn), lambda j, i, n=n: (0, n, 0, j))
           for n in range(N_BRANCH)]
        + [pl.BlockSpec((bm, bn), lambda j, i, n=n: (i, n * nj + j)) for n in range(N_BRANCH)],
        out_specs=pl.BlockSpec((bm, bn), lambda j, i: (i, j)),
        out_shape=jax.ShapeDtypeStruct((m, d), BF16),
        scratch_shapes=[pltpu.VMEM((BRANCH_WIDTH, bn), BF16) for _ in range(N_BRANCH)],
        compiler_params=_params(("parallel", "arbitrary"), 48),
        name="branch_mix",
    )(conv_y, attn, cross, w_branch, w_branch, w_branch, gates, gates, gates)


TAB_PREV = 0
TAB_OWN = 1
TAB_PREV_T = 2
TAB_OWN_T = 3


def _bias_tables_body(rb_ref, o_ref):
    h = pl.program_id(0)
    shape = (MOBA_BLOCK, MOBA_BLOCK)
    far = rb_ref[NUM_BUCKETS - 1, h]
    for slot, offset, q_axis in ((TAB_PREV, MOBA_BLOCK, 0), (TAB_OWN, 0, 0),
                                 (TAB_PREV_T, MOBA_BLOCK, 1), (TAB_OWN_T, 0, 1)):
        qi = lax.broadcasted_iota(jnp.int32, shape, q_axis)
        kj = lax.broadcasted_iota(jnp.int32, shape, 1 - q_axis)
        dist = jnp.maximum(qi - kj + offset, 0)
        large = MAX_EXACT + (jnp.log(jnp.maximum(dist, 1).astype(F32) / MAX_EXACT)
                             / math.log(MAX_DISTANCE / MAX_EXACT)
                             * (NUM_BUCKETS - MAX_EXACT)).astype(jnp.int32)
        large = jnp.minimum(large, NUM_BUCKETS - 1)
        bucket = jnp.where(dist < MAX_EXACT, dist, large)
        val = jnp.zeros(shape, F32)
        for bkt in range(NUM_BUCKETS):
            val = jnp.where(bucket == bkt, rb_ref[bkt, h], val)
        val = val - far
        if slot in (TAB_OWN, TAB_OWN_T):
            val = jnp.where(kj > qi, MASKED, val)
        o_ref[0, slot] = val


def _bias_tables(rel_bias):
    return pl.pallas_call(
        _bias_tables_body,
        grid=(N_HEADS,),
        in_specs=[pl.BlockSpec(memory_space=pltpu.SMEM)],
        out_specs=pl.BlockSpec((1, 4, MOBA_BLOCK, MOBA_BLOCK), lambda h: (h, 0, 0, 0)),
        out_shape=jax.ShapeDtypeStruct((N_HEADS, 4, MOBA_BLOCK, MOBA_BLOCK), F32),
        compiler_params=_params(("arbitrary",), 32),
        name="bias_tables",
    )(rel_bias)


def _block_mean_body(k_ref, o_ref):
    rows = k_ref.shape[0]
    x = k_ref[...].reshape(rows // MOBA_BLOCK, MOBA_BLOCK, k_ref.shape[1])
    o_ref[...] = jnp.sum(x, axis=1) * (1.0 / MOBA_BLOCK)


def _block_mean(k):
    t, w = k.shape
    nb = t // MOBA_BLOCK
    per = min(8, nb)
    assert nb % per == 0
    return pl.pallas_call(
        _block_mean_body,
        grid=(nb // per,),
        in_specs=[pl.BlockSpec((per * MOBA_BLOCK, w), lambda i: (i, 0))],
        out_specs=pl.BlockSpec((per, w), lambda i: (i, 0)),
        out_shape=jax.ShapeDtypeStruct((nb, w), F32),
        compiler_params=_params(("parallel",), 40),
        name="block_mean",
    )(k)


MOBA_CHUNK = 4
MOBA_HEADS_PER_STEP = 2
MOBA_PAD = MOBA_CHUNK * MOBA_BLOCK
KAUG_WIDTH = HEAD_DIM + LANES


def _moba_prep_body(k_ref, v_ref, ka_ref, vt_ref):
    i = pl.program_id(0)
    rows = k_ref.shape[0]
    col = lax.broadcasted_iota(jnp.int32, (rows, LANES), 1)

    @pl.when(i == 0)
    def _():
        pad_hot = jnp.where(col == LANES - 1, 1.0, 0.0).astype(BF16)
        for h in range(N_HEADS):
            ka_ref[:, h * KAUG_WIDTH:h * KAUG_WIDTH + HEAD_DIM] = jnp.zeros((rows, HEAD_DIM), BF16)
            ka_ref[:, h * KAUG_WIDTH + HEAD_DIM:(h + 1) * KAUG_WIDTH] = pad_hot
        vt_ref[...] = jnp.zeros(vt_ref.shape, BF16)

    @pl.when(i > 0)
    def _():
        row = lax.broadcasted_iota(jnp.int32, (rows, LANES), 0) + (i - 1) * rows
        hot = jnp.where(jnp.right_shift(row, int(math.log2(MOBA_BLOCK))) == col,
                        1.0, 0.0).astype(BF16)
        for h in range(N_HEADS):
            ka_ref[:, h * KAUG_WIDTH:h * KAUG_WIDTH + HEAD_DIM] = (
                k_ref[:, h * HEAD_DIM:(h + 1) * HEAD_DIM].astype(BF16))
            ka_ref[:, h * KAUG_WIDTH + HEAD_DIM:(h + 1) * KAUG_WIDTH] = hot
        for blk in range(rows // MOBA_BLOCK):
            vt_ref[blk] = v_ref[blk * MOBA_BLOCK:(blk + 1) * MOBA_BLOCK, :].T.astype(BF16)


def _moba_prep(k, v):
    t = k.shape[0]
    rows = MOBA_PAD
    assert t % rows == 0
    data = pl.BlockSpec((rows, ATTN_WIDTH), lambda i: (jnp.maximum(i - 1, 0), 0))
    return pl.pallas_call(
        _moba_prep_body,
        grid=(t // rows + 1,),
        in_specs=[data, data],
        out_specs=[pl.BlockSpec((rows, N_HEADS * KAUG_WIDTH), lambda i: (i, 0)),
                   pl.BlockSpec((MOBA_CHUNK, ATTN_WIDTH, MOBA_BLOCK), lambda i: (i, 0, 0))],
        out_shape=[jax.ShapeDtypeStruct((t + rows, N_HEADS * KAUG_WIDTH), BF16),
                   jax.ShapeDtypeStruct((t // MOBA_BLOCK + MOBA_CHUNK, ATTN_WIDTH, MOBA_BLOCK),
                                        BF16)],
        compiler_params=_params(("parallel",), 40),
        name="moba_prep",
    )(k, v)


SELECT_ROWS = 1024


def _moba_select_body(q_ref, km_ref, o_ref, *, n_sub):
    q = q_ref[...]
    rows = q.shape[0]
    gate = lax.dot_general(km_ref[...], q, NT_DIMS, precision=lax.Precision.HIGHEST,
                           preferred_element_type=F32)[:n_sub]
    blk = lax.broadcasted_iota(jnp.int32, gate.shape, 0)
    blk_f = blk.astype(F32)
    pos = lax.broadcasted_iota(jnp.int32, gate.shape, 1) + pl.program_id(0) * rows
    own = jnp.right_shift(pos, int(math.log2(MOBA_BLOCK)))
    gate = jnp.where(blk < own, gate, -jnp.inf)
    picked = blk == own
    for _ in range(MOBA_TOPK):
        top = jnp.max(gate, axis=0, keepdims=True)
        first = jnp.min(jnp.where(gate == top, blk_f, float(LANES)), axis=0, keepdims=True)
        chosen = blk_f == first
        picked = picked | (chosen & (top > -jnp.inf))
        gate = jnp.where(chosen, -jnp.inf, gate)
    o_ref[:HEAD_DIM, :] = (q * (HEAD_DIM ** -0.5)).T.astype(BF16)
    o_ref[HEAD_DIM:HEAD_DIM + n_sub, :] = jnp.where(picked, 0.0, MASKED).astype(BF16)
    o_ref[HEAD_DIM + n_sub:, :] = jnp.full((LANES - n_sub, rows), MASKED, BF16)


def _moba_select(q, k_mean):
    t = q.shape[0]
    nb = t // MOBA_BLOCK
    rows = min(SELECT_ROWS, t)
    n_sub = -(-nb // 16) * 16
    assert t % rows == 0 and MOBA_TOPK <= nb and n_sub < LANES
    km = jnp.pad(k_mean, ((0, LANES - nb), (0, 0)))
    return pl.pallas_call(
        functools.partial(_moba_select_body, n_sub=n_sub),
        grid=(t // rows, N_HEADS),
        in_specs=[pl.BlockSpec((rows, HEAD_DIM), lambda i, h: (i, h)),
                  pl.BlockSpec((LANES, HEAD_DIM), lambda i, h: (0, h))],
        out_specs=pl.BlockSpec((KAUG_WIDTH, rows), lambda i, h: (h, i)),
        out_shape=jax.ShapeDtypeStruct((N_HEADS * KAUG_WIDTH, t), BF16),
        compiler_params=_params(("parallel", "parallel"), 40),
        name="moba_select",
    )(q, km)


def _moba_prompt_body(qa_ref, ka_ref, vt_ref, tab_ref, o_ref, s_ref):
    qb_idx = pl.program_id(1)
    heads = range(MOBA_HEADS_PER_STEP)
    tiles = [(hh, g) for hh in heads for g in range(MOBA_CHUNK)]
    q_aug = [qa_ref[hh * KAUG_WIDTH:(hh + 1) * KAUG_WIDTH, :] for hh in heads]

    def logits(hh, g, first_block):
        start = pl.multiple_of((first_block + g) * MOBA_BLOCK, MOBA_BLOCK)
        s_ref[hh * MOBA_CHUNK + g] = jnp.dot(
            ka_ref[pl.ds(start, MOBA_BLOCK), hh * KAUG_WIDTH:(hh + 1) * KAUG_WIDTH],
            q_aug[hh], preferred_element_type=F32)

    def weights(state, bias, next_first):
        m_new, scale, l_new, p = [], [], [], {}

        def tile(hh, g):
            s = s_ref[hh * MOBA_CHUNK + g]
            return s if bias[g] is None else s + tab_ref[hh, bias[g]]

        for hh in heads:
            m, l, _ = state[hh]
            top = m
            for g in range(MOBA_CHUNK):
                top = jnp.maximum(top, jnp.max(tile(hh, g), axis=0, keepdims=True))
            sc = jnp.exp(m - top)
            tot = sc * l
            for g in range(MOBA_CHUNK):
                e = jnp.exp(tile(hh, g) - top)
                if next_first is not None:
                    logits(hh, g, next_first)
                tot = tot + jnp.sum(e, axis=0, keepdims=True)
                p[hh, g] = e.astype(BF16)
            m_new.append(top)
            scale.append(sc)
            l_new.append(tot)
        return m_new, scale, l_new, p

    def accumulate(first_block, state, stats):
        m_new, scale, l_new, p = stats
        out = []
        for hh in heads:
            acc = scale[hh] * state[hh][2]
            for g in range(MOBA_CHUNK):
                acc = acc + jnp.dot(vt_ref[first_block + g, hh * HEAD_DIM:(hh + 1) * HEAD_DIM, :],
                                    p[hh, g], preferred_element_type=F32)
            out.append((m_new[hh], l_new[hh], acc))
        return tuple(out)

    init = tuple((jnp.full((1, MOBA_BLOCK), 0.1 * MASKED, F32), jnp.zeros((1, MOBA_BLOCK), F32),
                  jnp.zeros((HEAD_DIM, MOBA_BLOCK), F32)) for _ in heads)
    first = qb_idx % MOBA_CHUNK + 1

    no_bias = [None] * MOBA_CHUNK
    for hh, g in tiles:
        logits(hh, g, first)

    def trip(c, state):
        stats = weights(state, no_bias, first + (c + 1) * MOBA_CHUNK)
        return accumulate(first + c * MOBA_CHUNK, state, stats)

    state = lax.fori_loop(0, qb_idx // MOBA_CHUNK, trip, init)
    bias = [None] * (MOBA_CHUNK - 2) + [TAB_PREV_T, TAB_OWN_T]
    state = accumulate(qb_idx + 1, state, weights(state, bias, None))
    for hh in heads:
        _, l, acc = state[hh]
        o_ref[:, hh * HEAD_DIM:(hh + 1) * HEAD_DIM] = (acc / l).T.astype(o_ref.dtype)


def _moba_prompt(q_aug_t, k_aug, v_t, tables):
    t = q_aug_t.shape[1]
    nb = t // MOBA_BLOCK
    hb = MOBA_HEADS_PER_STEP
    assert t % MOBA_BLOCK == 0 and MOBA_CHUNK >= 2
    return pl.pallas_call(
        _moba_prompt_body,
        grid=(N_HEADS // hb, nb),
        in_specs=[
            pl.BlockSpec((hb * KAUG_WIDTH, MOBA_BLOCK), lambda h, i: (h, i)),
            pl.BlockSpec((t + MOBA_PAD, hb * KAUG_WIDTH), lambda h, i: (0, h)),
            pl.BlockSpec((nb + MOBA_CHUNK, hb * HEAD_DIM, MOBA_BLOCK), lambda h, i: (0, h, 0)),
            pl.BlockSpec((hb, 4, MOBA_BLOCK, MOBA_BLOCK), lambda h, i: (h, 0, 0, 0)),
        ],
        out_specs=pl.BlockSpec((MOBA_BLOCK, hb * HEAD_DIM), lambda h, i: (i, h)),
        out_shape=jax.ShapeDtypeStruct((t, ATTN_WIDTH), BF16),
        scratch_shapes=[pltpu.VMEM((hb * MOBA_CHUNK, MOBA_BLOCK, MOBA_BLOCK), F32)],
        compiler_params=_params(("parallel", "parallel"), 56),
        name="moba_prompt",
    )(q_aug_t, k_aug, v_t, tables)


PAGES_PER_STEP = 16


def _paged_select_body(pt_ref, q_ref, *refs):
    pages = refs[:PAGES_PER_STEP]
    o_ref = refs[PAGES_PER_STEP]
    km_ref = refs[PAGES_PER_STEP + 1]
    g = pl.program_id(1)
    per_block = MOBA_BLOCK // pages[0].shape[0]
    blocks_per_step = PAGES_PER_STEP // per_block
    for blk in range(blocks_per_step):
        tot = jnp.sum(pages[blk * per_block][...], axis=0)
        for extra in range(1, per_block):
            tot = tot + jnp.sum(pages[blk * per_block + extra][...], axis=0)
        km_ref[g * blocks_per_step + blk] = tot * (1.0 / MOBA_BLOCK)

    @pl.when(g == pl.num_programs(1) - 1)
    def _():
        km = km_ref[...]
        gate = jnp.sum(km * q_ref[0], axis=-1, keepdims=True)
        gate = jnp.broadcast_to(gate, km.shape)
        blk_id = lax.broadcasted_iota(jnp.int32, km.shape, 0).astype(F32)
        for slot in range(MOBA_TOPK):
            top = jnp.max(gate, axis=0, keepdims=True)
            first = jnp.min(jnp.where(gate == top, blk_id, float(km.shape[0])),
                            axis=0, keepdims=True)
            o_ref[0, slot] = first[0].astype(jnp.int32)
            gate = jnp.where(blk_id == first, -jnp.inf, gate)


def _paged_select(q_heads, cache, page_table_flat, n_pages):
    n_req = q_heads.shape[0]
    page = cache.shape[2]
    per_block = MOBA_BLOCK // page
    nbp = n_pages // per_block
    assert n_pages % PAGES_PER_STEP == 0 and MOBA_BLOCK % page == 0 and nbp >= MOBA_TOPK
    out = pl.pallas_call(
        _paged_select_body,
        grid_spec=pltpu.PrefetchScalarGridSpec(
            num_scalar_prefetch=1,
            grid=(n_req, n_pages // PAGES_PER_STEP),
            in_specs=[pl.BlockSpec((1, N_HEADS, HEAD_DIM), lambda b, g, pt: (b, 0, 0))]
            + [pl.BlockSpec(
                (None, None, page, N_HEADS, HEAD_DIM),
                lambda b, g, pt, i=i: (0, pt[b * n_pages + g * PAGES_PER_STEP + i], 0, 0, 0))
               for i in range(PAGES_PER_STEP)],
            out_specs=pl.BlockSpec((1, MOBA_TOPK, N_HEADS, HEAD_DIM),
                                   lambda b, g, pt: (b, 0, 0, 0)),
            scratch_shapes=[pltpu.VMEM((nbp, N_HEADS, HEAD_DIM), F32)],
        ),
        out_shape=jax.ShapeDtypeStruct((n_req, MOBA_TOPK, N_HEADS, HEAD_DIM), jnp.int32),
        compiler_params=_params(("parallel", "arbitrary"), 40),
        name="paged_select",
    )(page_table_flat, q_heads, *([cache] * PAGES_PER_STEP))
    return out[:, :, :, 0]


def _sample_attn_body(pt_ref, sel_ref, q_ref, kn_ref, vn_ref, tprev_ref, town_ref, ck_ref, cv_ref,
                      o_ref, kbuf, vbuf, sem, *, n_pages, per_block):
    b = pl.program_id(0)
    page = kbuf.shape[2]
    last_block = n_pages // per_block - 1

    def copies(req, buf_slot):
        out = []
        for h in range(N_HEADS):
            for slot in range(MOBA_TOPK):
                blk = sel_ref[(req * MOBA_TOPK + slot) * N_HEADS + h]
                for i in range(per_block):
                    pg = pt_ref[req * n_pages + blk * per_block + i]
                    idx = (h * MOBA_TOPK + slot) * per_block + i
                    out.append(pltpu.make_async_copy(ck_ref.at[0, pg, :, h, :],
                                                     kbuf.at[buf_slot, idx], sem.at[buf_slot, 0]))
                    out.append(pltpu.make_async_copy(cv_ref.at[0, pg, :, h, :],
                                                     vbuf.at[buf_slot, idx], sem.at[buf_slot, 1]))
        return out

    @pl.when(b == 0)
    def _():
        for cp in copies(b, 0):
            cp.start()

    @pl.when(b + 1 < pl.num_programs(0))
    def _():
        for cp in copies(b + 1, (b + 1) % 2):
            cp.start()

    cur = b % 2
    for cp in copies(b, cur):
        cp.wait()

    for h in range(N_HEADS):
        hs = slice(h * HEAD_DIM, (h + 1) * HEAD_DIM)
        q = q_ref[0, :, hs] * (HEAD_DIM ** -0.5)
        q8 = jnp.broadcast_to(q, (8, HEAD_DIM)).astype(BF16)
        prev_row = tprev_ref[h, 0, 0:1, :]
        logits = []
        for slot in range(MOBA_TOPK):
            near = (sel_ref[(b * MOBA_TOPK + slot) * N_HEADS + h] == last_block).astype(F32)
            for i in range(per_block):
                kt = kbuf[cur, (h * MOBA_TOPK + slot) * per_block + i].astype(BF16)
                lg = lax.dot_general(q8, kt, NT_DIMS, preferred_element_type=F32)[0:1, :]
                logits.append(lg + near * prev_row[:, i * page:(i + 1) * page])
        l_own = (jnp.sum(q * kn_ref[0, :, hs], axis=1, keepdims=True)
                 + town_ref[h, 0, 0:1, 0:1])
        m = l_own
        for lg in logits:
            m = jnp.maximum(m, jnp.max(lg, axis=1, keepdims=True))
        p_own = jnp.exp(l_own - m)
        den = p_own
        acc = p_own * vn_ref[0, :, hs]
        for idx, lg in enumerate(logits):
            p = jnp.exp(lg - m)
            den = den + jnp.sum(p, axis=1, keepdims=True)
            p8 = jnp.broadcast_to(p, (8, page)).astype(BF16)
            acc = acc + jnp.dot(p8, vbuf[cur, h * MOBA_TOPK * per_block + idx].astype(BF16),
                                preferred_element_type=F32)[0:1, :]
        o_ref[0, :, hs] = (acc / den).astype(o_ref.dtype)


def _sample_attn(q3, kn3, vn3, cache_k, cache_v, page_table_flat, sel_flat, tables, n_pages):
    n_req = q3.shape[0]
    page = cache_k.shape[2]
    per_block = MOBA_BLOCK // page
    n_bufs = N_HEADS * MOBA_TOPK * per_block
    row_spec = pl.BlockSpec((1, 1, ATTN_WIDTH), lambda b, pt, sel: (b, 0, 0))
    return pl.pallas_call(
        functools.partial(_sample_attn_body, n_pages=n_pages, per_block=per_block),
        grid_spec=pltpu.PrefetchScalarGridSpec(
            num_scalar_prefetch=2,
            grid=(n_req,),
            in_specs=[row_spec, row_spec, row_spec,
                      pl.BlockSpec((N_HEADS, 1, 8, MOBA_BLOCK),
                                   lambda b, pt, sel: (0, TAB_PREV, 0, 0)),
                      pl.BlockSpec((N_HEADS, 1, 8, MOBA_BLOCK),
                                   lambda b, pt, sel: (0, TAB_OWN, 0, 0)),
                      pl.BlockSpec(memory_space=pl.ANY),
                      pl.BlockSpec(memory_space=pl.ANY)],
            out_specs=row_spec,
            scratch_shapes=[pltpu.VMEM((2, n_bufs, page, HEAD_DIM), F32),
                            pltpu.VMEM((2, n_bufs, page, HEAD_DIM), F32),
                            pltpu.SemaphoreType.DMA((2, 2))],
        ),
        out_shape=jax.ShapeDtypeStruct((n_req, 1, ATTN_WIDTH), BF16),
        compiler_params=_params(("arbitrary",), 40),
        name="sample_attn",
    )(page_table_flat, sel_flat, q3, kn3, vn3, tables, tables, cache_k, cache_v)


def _cross_prompt_body(q_ref, mk_ref, mv_ref, o_ref):
    for h in range(CROSS_HEADS):
        sl = slice(h * CROSS_HEAD_DIM, (h + 1) * CROSS_HEAD_DIM)
        s = lax.dot_general(q_ref[:, sl], mk_ref[:, sl], NT_DIMS,
                            preferred_element_type=F32) * (CROSS_HEAD_DIM ** -0.5)
        p = jnp.exp(s - jnp.max(s, axis=1, keepdims=True))
        den = jnp.sum(p, axis=1, keepdims=True)
        o = jnp.dot(p.astype(BF16), mv_ref[:, sl], preferred_element_type=F32) / den
        o_ref[:, sl] = o.astype(o_ref.dtype)


def _cross_prompt(qc, mk_bf, mv_bf, bm=512):
    t = qc.shape[0]
    n_mem = mk_bf.shape[0]
    return pl.pallas_call(
        _cross_prompt_body,
        grid=(t // bm,),
        in_specs=[pl.BlockSpec((bm, CROSS_WIDTH), lambda i: (i, 0)),
                  pl.BlockSpec((n_mem, CROSS_WIDTH), lambda i: (0, 0)),
                  pl.BlockSpec((n_mem, CROSS_WIDTH), lambda i: (0, 0))],
        out_specs=pl.BlockSpec((bm, CROSS_WIDTH), lambda i: (i, 0)),
        out_shape=jax.ShapeDtypeStruct((t, CROSS_WIDTH), BF16),
        compiler_params=_params(("parallel",), 32),
        name="cross_prompt",
    )(qc, mk_bf, mv_bf)


def _cross_sample_body(q_ref, mk_hbm, mv_hbm, o_ref, kbuf, vbuf, sem):
    b = pl.program_id(0)

    def copies(req, buf_slot):
        out = []
        for h in range(CROSS_HEADS):
            out.append(pltpu.make_async_copy(mk_hbm.at[0, req, :, h, :], kbuf.at[buf_slot, h],
                                             sem.at[buf_slot, 0]))
            out.append(pltpu.make_async_copy(mv_hbm.at[0, req, :, h, :], vbuf.at[buf_slot, h],
                                             sem.at[buf_slot, 1]))
        return out

    @pl.when(b == 0)
    def _():
        for cp in copies(b, 0):
            cp.start()

    @pl.when(b + 1 < pl.num_programs(0))
    def _():
        for cp in copies(b + 1, (b + 1) % 2):
            cp.start()

    cur = b % 2
    for cp in copies(b, cur):
        cp.wait()

    for h in range(CROSS_HEADS):
        sl = slice(h * CROSS_HEAD_DIM, (h + 1) * CROSS_HEAD_DIM)
        q8 = jnp.broadcast_to(q_ref[0, :, sl], (8, CROSS_HEAD_DIM))
        s = lax.dot_general(q8, kbuf[cur, h].astype(BF16), NT_DIMS,
                            preferred_element_type=F32) * (CROSS_HEAD_DIM ** -0.5)
        p = jnp.exp(s - jnp.max(s, axis=1, keepdims=True))
        den = jnp.sum(p, axis=1, keepdims=True)
        o = jnp.dot(p.astype(BF16), vbuf[cur, h].astype(BF16),
                    preferred_element_type=F32) / den
        o_ref[0, :, sl] = o[0:1, :].astype(o_ref.dtype)


def _cross_sample(qc3, mem_k, mem_v):
    _, n_req, n_mem, _, _ = mem_k.shape
    row_spec = pl.BlockSpec((1, 1, CROSS_WIDTH), lambda b: (b, 0, 0))
    any_spec = pl.BlockSpec(memory_space=pl.ANY)
    return pl.pallas_call(
        _cross_sample_body,
        grid=(n_req,),
        in_specs=[row_spec, any_spec, any_spec],
        out_specs=row_spec,
        out_shape=jax.ShapeDtypeStruct((n_req, 1, CROSS_WIDTH), BF16),
        scratch_shapes=[pltpu.VMEM((2, CROSS_HEADS, n_mem, CROSS_HEAD_DIM), F32),
                        pltpu.VMEM((2, CROSS_HEADS, n_mem, CROSS_HEAD_DIM), F32),
                        pltpu.SemaphoreType.DMA((2, 2))],
        compiler_params=_params(("arbitrary",), 32),
        name="cross_sample",
    )(qc3, mem_k, mem_v)


CONV_TILE = 256
CONV_HALO = 32
CONV_UNROLL = 4


def _conv_prompt_body(cur_ref, prev_ref, w_ref, cb_ref, g_ref, b_ref, o_ref, buf_ref, sh_ref,
                      wt_ref, y_ref):
    i = pl.program_id(0)
    has_prev = (i > 0).astype(F32)
    buf_ref[0:CONV_HALO, :] = prev_ref[CONV_TILE - CONV_HALO:CONV_TILE, :] * has_prev
    buf_ref[CONV_HALO:CONV_HALO + CONV_TILE, :] = cur_ref[...]
    first = CONV_HALO - CONV_STATE
    shifted_rows = sh_ref.shape[1]
    for tap in range(CONV_WIDTH):
        wt_ref[tap] = jnp.broadcast_to(w_ref[tap:tap + 1, :], (SUBLANES, D_CONV))
    for c in range(D_CONV // LANES):
        cs = slice(c * LANES, (c + 1) * LANES)
        for r in range(1, SUBLANES):
            sh_ref[r - 1] = buf_ref[r:r + shifted_rows, cs]
        weights = [wt_ref[tap, :, cs] for tap in range(CONV_WIDTH)]
        bias = jnp.broadcast_to(cb_ref[:, cs], (SUBLANES, LANES))

        def row_tiles(i, carry, cs=cs, weights=weights, bias=bias):
            row0 = pl.multiple_of(i * (CONV_UNROLL * SUBLANES), CONV_UNROLL * SUBLANES)
            acc = [bias] * CONV_UNROLL
            for tap in range(CONV_WIDTH):
                r = (first + tap) % SUBLANES
                for j in range(CONV_UNROLL):
                    rows = pl.ds(row0 + (j * SUBLANES + first + tap - r), SUBLANES)
                    tile = buf_ref[rows, cs] if r == 0 else sh_ref[r - 1, rows, :]
                    acc[j] = acc[j] + tile * weights[tap]
            for j in range(CONV_UNROLL):
                y_ref[pl.ds(row0 + j * SUBLANES, SUBLANES), cs] = acc[j]
            return carry

        lax.fori_loop(0, CONV_TILE // (CONV_UNROLL * SUBLANES), row_tiles, 0)
    y = _layer_norm(y_ref[...], g_ref[...], b_ref[...])
    o_ref[...] = jax.nn.silu(y).astype(o_ref.dtype)


def _conv_prompt(u, conv_w, conv_b, ln_g, ln_b):
    t = u.shape[0]
    vec = pl.BlockSpec((1, D_CONV), lambda i: (0, 0))
    return pl.pallas_call(
        _conv_prompt_body,
        grid=(t // CONV_TILE,),
        in_specs=[pl.BlockSpec((CONV_TILE, D_CONV), lambda i: (i, 0)),
                  pl.BlockSpec((CONV_TILE, D_CONV), lambda i: (jnp.maximum(i - 1, 0), 0)),
                  pl.BlockSpec((CONV_WIDTH, D_CONV), lambda i: (0, 0)),
                  vec, vec, vec],
        out_specs=pl.BlockSpec((CONV_TILE, D_CONV), lambda i: (i, 0)),
        out_shape=jax.ShapeDtypeStruct((t, D_CONV), BF16),
        scratch_shapes=[pltpu.VMEM((CONV_HALO + CONV_TILE, D_CONV), F32),
                        pltpu.VMEM((SUBLANES - 1, CONV_HALO + CONV_TILE - SUBLANES, LANES), F32),
                        pltpu.VMEM((CONV_WIDTH, SUBLANES, D_CONV), F32),
                        pltpu.VMEM((CONV_TILE, D_CONV), F32)],
        compiler_params=_params(("parallel",), 32),
        name="conv_prompt",
    )(u, u, conv_w, conv_b.reshape(1, D_CONV), ln_g.reshape(1, D_CONV), ln_b.reshape(1, D_CONV))


def _conv_sample_body(st_ref, u_ref, w_ref, cb_ref, g_ref, b_ref, o_ref):
    acc = u_ref[...] * w_ref[CONV_STATE:CONV_WIDTH, :]
    for tap in range(CONV_STATE):
        acc = acc + st_ref[tap] * w_ref[tap:tap + 1, :]
    y = _layer_norm(acc + cb_ref[...], g_ref[...], b_ref[...])
    o_ref[...] = jax.nn.silu(y).astype(o_ref.dtype)


def _conv_sample(state_t, u, conv_w, conv_b, ln_g, ln_b):
    n_req = u.shape[0]
    return pl.pallas_call(
        _conv_sample_body,
        out_shape=jax.ShapeDtypeStruct((n_req, D_CONV), BF16),
        name="conv_sample",
    )(state_t, u, conv_w, conv_b.reshape(1, D_CONV), ln_g.reshape(1, D_CONV),
      ln_b.reshape(1, D_CONV))


def kernel(x_prompt, x_sample, mem_prompt, cache_k, cache_v, state_conv, cache_mem_k, cache_mem_v,
           page_table, w_in, conv_w, conv_b, conv_ln_g, conv_ln_b, w_mem_kv, rel_bias, w_branch,
           w_out, ln1_g, ln1_b, w_gate, w_up, w_down, ln2_g, ln2_b):
    depth = w_in.shape[0]
    assert depth == 1 and x_prompt.shape[0] == 1 and x_sample.shape[1] == 1
    alpha = (2 * depth) ** 0.25
    t, d_model = x_prompt.shape[1], x_prompt.shape[2]
    n_req, n_pages = page_table.shape
    page = cache_k.shape[2]
    n_mem = mem_prompt.shape[1]
    assert (n_pages * page) % MOBA_BLOCK == 0

    sizes = (D_CONV, D_CONV, ATTN_WIDTH, ATTN_WIDTH, ATTN_WIDTH, CROSS_WIDTH, N_BRANCH * d_model)
    col_a, col_b, col_q, col_k, col_v, col_qc, col_g = (
        int(c) for c in np.concatenate([[0], np.cumsum(sizes)[:-1]]))
    w_o = w_out[0].astype(BF16)
    w_d = w_down[0].astype(BF16)
    d_ff = w_gate.shape[2]
    ff_bk = 512 if d_ff % 512 == 0 else d_ff

    tables = _bias_tables(rel_bias)

    def project_in(x_bf):
        u = _proj(x_bf, [(w_in, col_a), (w_in, col_b)], D_CONV, "glu", [F32])
        q = _proj(x_bf, [(w_in, col_q)], ATTN_WIDTH, "none", [F32], bn=1024)
        k = _proj(x_bf, [(w_in, col_k)], ATTN_WIDTH, "none", [F32], bn=1024)
        v = _proj(x_bf, [(w_in, col_v)], ATTN_WIDTH, "none", [F32], bn=1024)
        qc = _proj(x_bf, [(w_in, col_qc)], CROSS_WIDTH, "none", [BF16], bn=1024)
        gates = _proj(x_bf, [(w_in, col_g)], N_BRANCH * d_model, "sigmoid", [F32], bn=1024)
        return u, q, k, v, qc, gates

    def finish(x, conv_y, attn, cross, gates, bm_out, bm_ffn):
        mix_in = _branch_mix(conv_y, attn, cross, w_branch, gates)
        h1, h1_bf = _mm_res_ln(mix_in, w_o, x, ln1_g[0], ln1_b[0], alpha, [F32, BF16],
                               bm=bm_out, bk=d_model)
        act = _proj(h1_bf, [(w_gate, 0), (w_up, 0)], d_ff, "swiglu", [BF16])
        return _mm_res_ln(act, w_d, h1, ln2_g[0], ln2_b[0], alpha, [F32], bm=bm_ffn, bk=ff_bk)

    xp = x_prompt[0]
    u_p, q_p, k_p, v_p, qc_p, gates_p = project_in(xp.astype(BF16))
    conv_y_p = _conv_prompt(u_p, conv_w[0], conv_b[0], conv_ln_g[0], conv_ln_b[0])
    k_aug, v_pad = _moba_prep(k_p, v_p)
    attn_p = _moba_prompt(_moba_select(q_p, _block_mean(k_p)), k_aug, v_pad, tables)
    mem_bf = mem_prompt[0].astype(BF16)
    mk, mk_bf = _proj(mem_bf, [(w_mem_kv, 0)], CROSS_WIDTH, "none", [F32, BF16], bn=1024)
    mv, mv_bf = _proj(mem_bf, [(w_mem_kv, CROSS_WIDTH)], CROSS_WIDTH, "none", [F32, BF16], bn=1024)
    cross_p = _cross_prompt(qc_p, mk_bf, mv_bf)
    y_p = finish(xp, conv_y_p, attn_p, cross_p, gates_p, bm_out=512, bm_ffn=1024)

    xs = x_sample[:, 0]
    u_s, q_s, k_s, v_s, qc_s, gates_s = project_in(xs.astype(BF16))
    conv_y_s = _conv_sample(state_conv[0].transpose(1, 0, 2), u_s, conv_w[0], conv_b[0],
                            conv_ln_g[0], conv_ln_b[0])
    pt_flat = page_table.reshape(-1)
    sel = _paged_select(q_s.reshape(n_req, N_HEADS, HEAD_DIM), cache_k, pt_flat, n_pages)
    attn_s = _sample_attn(q_s.reshape(n_req, 1, ATTN_WIDTH), k_s.reshape(n_req, 1, ATTN_WIDTH),
                          v_s.reshape(n_req, 1, ATTN_WIDTH), cache_k, cache_v, pt_flat,
                          sel.reshape(-1), tables, n_pages)
    cross_s = _cross_sample(qc_s.reshape(n_req, 1, CROSS_WIDTH), cache_mem_k, cache_mem_v)
    y_s = finish(xs, conv_y_s, attn_s.reshape(n_req, ATTN_WIDTH),
                 cross_s.reshape(n_req, CROSS_WIDTH), gates_s, bm_out=256, bm_ffn=512)

    new_conv_sample = jnp.concatenate([state_conv[0][:, 1:], u_s[:, None, :]], axis=1)
    return (
        y_p[None],
        y_s[:, None],
        k_p.reshape(1, 1, t, N_HEADS, HEAD_DIM),
        v_p.reshape(1, 1, t, N_HEADS, HEAD_DIM),
        u_p[t - CONV_STATE:][None, None],
        mk.reshape(1, 1, n_mem, CROSS_HEADS, CROSS_HEAD_DIM),
        mv.reshape(1, 1, n_mem, CROSS_HEADS, CROSS_HEAD_DIM),
        k_s.reshape(1, n_req, 1, N_HEADS, HEAD_DIM),
        v_s.reshape(1, n_req, 1, N_HEADS, HEAD_DIM),
        new_conv_sample[None],
    )
```

```python
import functools
import math

import numpy as np
import jax
import jax.numpy as jnp
from jax import lax
from jax.experimental import pallas as pl
from jax.experimental.pallas import tpu as pltpu

F32 = jnp.float32
BF16 = jnp.bfloat16

N_HEADS = 8
HEAD_DIM = 128
ATTN_WIDTH = N_HEADS * HEAD_DIM
MOBA_BLOCK = 256
MOBA_TOPK = 3
D_CONV = 1024
CONV_WIDTH = 31
CONV_STATE = CONV_WIDTH - 1
CROSS_HEADS = 4
CROSS_HEAD_DIM = 256
CROSS_WIDTH = CROSS_HEADS * CROSS_HEAD_DIM
N_BRANCH = 3
BRANCH_WIDTH = 1024
NUM_BUCKETS = 32
MAX_EXACT = NUM_BUCKETS // 2
MAX_DISTANCE = 128
LN_EPS = 1e-5
MASKED = -1e30
LANES = 128
SUBLANES = 8
MIB = 2 ** 20

NT_DIMS = (((1,), (1,)), ((), ()))


def _params(semantics, vmem_mib):
    return pltpu.CompilerParams(dimension_semantics=semantics,
                                vmem_limit_bytes=vmem_mib * MIB)


def _layer_norm(y, g, b):
    mu = jnp.mean(y, axis=-1, keepdims=True)
    yc = y - mu
    var = jnp.mean(yc * yc, axis=-1, keepdims=True)
    return yc * lax.rsqrt(var + LN_EPS) * g + b


def _proj_body(*refs, n_w, n_out, epilogue):
    x_ref = refs[0]
    w_refs = refs[1:1 + n_w]
    o_refs = refs[1 + n_w:1 + n_w + n_out]
    wbf_refs = refs[1 + n_w + n_out:]

    @pl.when(pl.program_id(1) == 0)
    def _():
        for w, wbf in zip(w_refs, wbf_refs):
            wbf[...] = w[...].astype(BF16)

    x = x_ref[...]
    z = [jnp.dot(x, w[...], preferred_element_type=F32) for w in wbf_refs]
    if epilogue == "glu":
        y = z[0] * jax.nn.sigmoid(z[1])
    elif epilogue == "swiglu":
        y = jax.nn.silu(z[0]) * z[1]
    elif epilogue == "sigmoid":
        y = jax.nn.sigmoid(z[0])
    else:
        y = z[0]
    for o in o_refs:
        o[...] = y.astype(o.dtype)


def _proj(x, ws, n, epilogue, out_dtypes, bm=1024, bn=512, vmem_mib=48):
    m, k = x.shape
    bm = min(bm, m)
    bn = min(bn, n)
    while n % bn or any(c0 % bn for _, c0 in ws):
        bn //= 2
    assert m % bm == 0 and bn % LANES == 0
    outs = pl.pallas_call(
        functools.partial(_proj_body, n_w=len(ws), n_out=len(out_dtypes), epilogue=epilogue),
        grid=(n // bn, m // bm),
        in_specs=[pl.BlockSpec((bm, k), lambda j, i: (i, 0))]
        + [pl.BlockSpec((None, k, bn), lambda j, i, c=c0 // bn: (0, 0, c + j)) for _, c0 in ws],
        out_specs=[pl.BlockSpec((bm, bn), lambda j, i: (i, j)) for _ in out_dtypes],
        out_shape=[jax.ShapeDtypeStruct((m, n), dt) for dt in out_dtypes],
        scratch_shapes=[pltpu.VMEM((k, bn), BF16) for _ in ws],
        compiler_params=_params(("parallel", "arbitrary"), vmem_mib),
        name="proj_" + epilogue,
    )(x, *[w for w, _ in ws])
    return outs[0] if len(outs) == 1 else outs


def _mm_res_ln_body(a_ref, w_ref, r_ref, g_ref, b_ref, *rest, alpha, n_out):
    o_refs = rest[:n_out]
    acc_ref = rest[n_out]
    kk = pl.program_id(1)

    @pl.when(kk == 0)
    def _():
        acc_ref[...] = jnp.zeros_like(acc_ref)

    acc_ref[...] += jnp.dot(a_ref[...], w_ref[...], preferred_element_type=F32)

    @pl.when(kk == pl.num_programs(1) - 1)
    def _():
        y = alpha * r_ref[...] + acc_ref[...]
        out = _layer_norm(y, g_ref[...], b_ref[...])
        for o in o_refs:
            o[...] = out.astype(o.dtype)


def _mm_res_ln(a, w, res, g, b, alpha, out_dtypes, bm, bk, vmem_mib=56):
    m, k = a.shape
    n = w.shape[1]
    bm = min(bm, m)
    assert m % bm == 0 and k % bk == 0
    outs = pl.pallas_call(
        functools.partial(_mm_res_ln_body, alpha=alpha, n_out=len(out_dtypes)),
        grid=(m // bm, k // bk),
        in_specs=[
            pl.BlockSpec((bm, bk), lambda i, kk: (i, kk)),
            pl.BlockSpec((bk, n), lambda i, kk: (kk, 0)),
            pl.BlockSpec((bm, n), lambda i, kk: (i, 0)),
            pl.BlockSpec((1, n), lambda i, kk: (0, 0)),
            pl.BlockSpec((1, n), lambda i, kk: (0, 0)),
        ],
        out_specs=[pl.BlockSpec((bm, n), lambda i, kk: (i, 0)) for _ in out_dtypes],
        out_shape=[jax.ShapeDtypeStruct((m, n), dt) for dt in out_dtypes],
        scratch_shapes=[pltpu.VMEM((bm, n), F32)],
        compiler_params=_params(("parallel", "arbitrary"), vmem_mib),
        name="mm_res_ln",
    )(a, w, res, g.reshape(1, n), b.reshape(1, n))
    return outs[0] if len(outs) == 1 else outs


def _branch_body(*refs):
    br_refs = refs[0:N_BRANCH]
    w_refs = refs[N_BRANCH:2 * N_BRANCH]
    g_refs = refs[2 * N_BRANCH:3 * N_BRANCH]
    o_ref = refs[3 * N_BRANCH]
    y = None
    for br, w, g in zip(br_refs, w_refs, g_refs):
        term = g[...] * jnp.dot(br[...], w[...], preferred_element_type=F32)
        y = term if y is None else y + term
    o_ref[...] = y.astype(o_ref.dtype)


def _branch_mix(conv_y, attn, cross, w_branch, gates, bm=1024, bn=1024):
    m = conv_y.shape[0]
    d = w_branch.shape[2]
    bm = min(bm, m)
    bn = min(bn, d)
    nj = d // bn
    br_spec = pl.BlockSpec((bm, BRANCH_WIDTH), lambda i, j: (i, 0))
    return pl.pallas_call(
        _branch_body,
        grid=(m // bm, nj),
        in_specs=[br_spec, br_spec, br_spec]
        + [pl.BlockSpec((None, BRANCH_WIDTH, bn), lambda i, j, n=n: (n, 0, j))
           for n in range(N_BRANCH)]
        + [pl.BlockSpec((bm, bn), lambda i, j, n=n: (i, n * nj + j)) for n in range(N_BRANCH)],
        out_specs=pl.BlockSpec((bm, bn), lambda i, j: (i, j)),
        out_shape=jax.ShapeDtypeStruct((m, d), BF16),
        compiler_params=_params(("parallel", "parallel"), 48),
        name="branch_mix",
    )(conv_y, attn, cross, w_branch, w_branch, w_branch, gates, gates, gates)


TAB_PREV = 0
TAB_OWN = 1
TAB_PREV_T = 2
TAB_OWN_T = 3


def _bias_tables_body(rb_ref, o_ref):
    h = pl.program_id(0)
    shape = (MOBA_BLOCK, MOBA_BLOCK)
    far = rb_ref[NUM_BUCKETS - 1, h]
    for slot, offset, q_axis in ((TAB_PREV, MOBA_BLOCK, 0), (TAB_OWN, 0, 0),
                                 (TAB_PREV_T, MOBA_BLOCK, 1), (TAB_OWN_T, 0, 1)):
        qi = lax.broadcasted_iota(jnp.int32, shape, q_axis)
        kj = lax.broadcasted_iota(jnp.int32, shape, 1 - q_axis)
        dist = jnp.maximum(qi - kj + offset, 0)
        large = MAX_EXACT + (jnp.log(jnp.maximum(dist, 1).astype(F32) / MAX_EXACT)
                             / math.log(MAX_DISTANCE / MAX_EXACT)
                             * (NUM_BUCKETS - MAX_EXACT)).astype(jnp.int32)
        large = jnp.minimum(large, NUM_BUCKETS - 1)
        bucket = jnp.where(dist < MAX_EXACT, dist, large)
        val = jnp.zeros(shape, F32)
        for bkt in range(NUM_BUCKETS):
            val = jnp.where(bucket == bkt, rb_ref[bkt, h], val)
        val = val - far
        if slot in (TAB_OWN, TAB_OWN_T):
            val = jnp.where(kj > qi, MASKED, val)
        o_ref[0, slot] = val


def _bias_tables(rel_bias):
    return pl.pallas_call(
        _bias_tables_body,
        grid=(N_HEADS,),
        in_specs=[pl.BlockSpec(memory_space=pltpu.SMEM)],
        out_specs=pl.BlockSpec((1, 4, MOBA_BLOCK, MOBA_BLOCK), lambda h: (h, 0, 0, 0)),
        out_shape=jax.ShapeDtypeStruct((N_HEADS, 4, MOBA_BLOCK, MOBA_BLOCK), F32),
        compiler_params=_params(("arbitrary",), 32),
        name="bias_tables",
    )(rel_bias)


def _block_mean_body(k_ref, o_ref):
    rows = k_ref.shape[0]
    x = k_ref[...].reshape(rows // MOBA_BLOCK, MOBA_BLOCK, k_ref.shape[1])
    o_ref[...] = jnp.sum(x, axis=1) * (1.0 / MOBA_BLOCK)


def _block_mean(k):
    t, w = k.shape
    nb = t // MOBA_BLOCK
    per = min(8, nb)
    assert nb % per == 0
    return pl.pallas_call(
        _block_mean_body,
        grid=(nb // per,),
        in_specs=[pl.BlockSpec((per * MOBA_BLOCK, w), lambda i: (i, 0))],
        out_specs=pl.BlockSpec((per, w), lambda i: (i, 0)),
        out_shape=jax.ShapeDtypeStruct((nb, w), F32),
        compiler_params=_params(("parallel",), 40),
        name="block_mean",
    )(k)


MOBA_CHUNK = 4
MOBA_HEADS_PER_STEP = 2
MOBA_PAD = MOBA_CHUNK * MOBA_BLOCK
KAUG_WIDTH = HEAD_DIM + LANES


def _moba_prep_body(k_ref, v_ref, ka_ref, vt_ref):
    i = pl.program_id(0)
    rows = k_ref.shape[0]
    col = lax.broadcasted_iota(jnp.int32, (rows, LANES), 1)

    @pl.when(i == 0)
    def _():
        pad_hot = jnp.where(col == LANES - 1, 1.0, 0.0).astype(BF16)
        for h in range(N_HEADS):
            ka_ref[:, h * KAUG_WIDTH:h * KAUG_WIDTH + HEAD_DIM] = jnp.zeros((rows, HEAD_DIM), BF16)
            ka_ref[:, h * KAUG_WIDTH + HEAD_DIM:(h + 1) * KAUG_WIDTH] = pad_hot
        vt_ref[...] = jnp.zeros(vt_ref.shape, BF16)

    @pl.when(i > 0)
    def _():
        row = lax.broadcasted_iota(jnp.int32, (rows, LANES), 0) + (i - 1) * rows
        hot = jnp.where(jnp.right_shift(row, int(math.log2(MOBA_BLOCK))) == col,
                        1.0, 0.0).astype(BF16)
        for h in range(N_HEADS):
            ka_ref[:, h * KAUG_WIDTH:h * KAUG_WIDTH + HEAD_DIM] = (
                k_ref[:, h * HEAD_DIM:(h + 1) * HEAD_DIM].astype(BF16))
            ka_ref[:, h * KAUG_WIDTH + HEAD_DIM:(h + 1) * KAUG_WIDTH] = hot
        for blk in range(rows // MOBA_BLOCK):
            vt_ref[blk] = v_ref[blk * MOBA_BLOCK:(blk + 1) * MOBA_BLOCK, :].T.astype(BF16)


def _moba_prep(k, v):
    t = k.shape[0]
    rows = MOBA_PAD
    assert t % rows == 0
    data = pl.BlockSpec((rows, ATTN_WIDTH), lambda i: (jnp.maximum(i - 1, 0), 0))
    return pl.pallas_call(
        _moba_prep_body,
        grid=(t // rows + 1,),
        in_specs=[data, data],
        out_specs=[pl.BlockSpec((rows, N_HEADS * KAUG_WIDTH), lambda i: (i, 0)),
                   pl.BlockSpec((MOBA_CHUNK, ATTN_WIDTH, MOBA_BLOCK), lambda i: (i, 0, 0))],
        out_shape=[jax.ShapeDtypeStruct((t + rows, N_HEADS * KAUG_WIDTH), BF16),
                   jax.ShapeDtypeStruct((t // MOBA_BLOCK + MOBA_CHUNK, ATTN_WIDTH, MOBA_BLOCK),
                                        BF16)],
        compiler_params=_params(("parallel",), 40),
        name="moba_prep",
    )(k, v)


SELECT_ROWS = 1024


def _moba_select_body(q_ref, km_ref, o_ref, *, n_sub):
    q = q_ref[...]
    rows = q.shape[0]
    gate = lax.dot_general(km_ref[...], q, NT_DIMS, precision=lax.Precision.HIGHEST,
                           preferred_element_type=F32)[:n_sub]
    blk = lax.broadcasted_iota(jnp.int32, gate.shape, 0)
    blk_f = blk.astype(F32)
    pos = lax.broadcasted_iota(jnp.int32, gate.shape, 1) + pl.program_id(0) * rows
    own = jnp.right_shift(pos, int(math.log2(MOBA_BLOCK)))
    gate = jnp.where(blk < own, gate, -jnp.inf)
    picked = blk == own
    for _ in range(MOBA_TOPK):
        top = jnp.max(gate, axis=0, keepdims=True)
        first = jnp.min(jnp.where(gate == top, blk_f, float(LANES)), axis=0, keepdims=True)
        chosen = blk_f == first
        picked = picked | (chosen & (top > -jnp.inf))
        gate = jnp.where(chosen, -jnp.inf, gate)
    o_ref[:HEAD_DIM, :] = (q * (HEAD_DIM ** -0.5)).T.astype(BF16)
    o_ref[HEAD_DIM:HEAD_DIM + n_sub, :] = jnp.where(picked, 0.0, MASKED).astype(BF16)
    o_ref[HEAD_DIM + n_sub:, :] = jnp.full((LANES - n_sub, rows), MASKED, BF16)


def _moba_select(q, k_mean):
    t = q.shape[0]
    nb = t // MOBA_BLOCK
    rows = min(SELECT_ROWS, t)
    n_sub = -(-nb // 16) * 16
    assert t % rows == 0 and MOBA_TOPK <= nb and n_sub < LANES
    km = jnp.pad(k_mean, ((0, LANES - nb), (0, 0)))
    return pl.pallas_call(
        functools.partial(_moba_select_body, n_sub=n_sub),
        grid=(t // rows, N_HEADS),
        in_specs=[pl.BlockSpec((rows, HEAD_DIM), lambda i, h: (i, h)),
                  pl.BlockSpec((LANES, HEAD_DIM), lambda i, h: (0, h))],
        out_specs=pl.BlockSpec((KAUG_WIDTH, rows), lambda i, h: (h, i)),
        out_shape=jax.ShapeDtypeStruct((N_HEADS * KAUG_WIDTH, t), BF16),
        compiler_params=_params(("parallel", "parallel"), 40),
        name="moba_select",
    )(q, km)


def _moba_prompt_body(pt_ref, qa_ref, ka_ref, vt_ref, tab_ref, ck_ref, o_ref, km_ref, s_ref,
                      page_buf, page_sem, *, pages_per_step, per_block):
    qb_idx = pl.program_id(1)
    step = pl.program_id(0) * pl.num_programs(1) + qb_idx
    n_steps = pl.num_programs(0) * pl.num_programs(1)

    def page_copies(at_step, slot):
        return [pltpu.make_async_copy(ck_ref.at[0, pt_ref[at_step * pages_per_step + j]],
                                      page_buf.at[slot, j], page_sem.at[slot])
                for j in range(pages_per_step)]

    @pl.when(step == 0)
    def _():
        for cp in page_copies(step, 0):
            cp.start()

    @pl.when(step + 1 < n_steps)
    def _():
        for cp in page_copies(step + 1, (step + 1) % 2):
            cp.start()

    heads = range(MOBA_HEADS_PER_STEP)
    tiles = [(hh, g) for hh in heads for g in range(MOBA_CHUNK)]
    q_aug = [qa_ref[hh * KAUG_WIDTH:(hh + 1) * KAUG_WIDTH, :] for hh in heads]

    def logits(hh, g, first_block):
        start = pl.multiple_of((first_block + g) * MOBA_BLOCK, MOBA_BLOCK)
        s_ref[hh * MOBA_CHUNK + g] = jnp.dot(
            ka_ref[pl.ds(start, MOBA_BLOCK), hh * KAUG_WIDTH:(hh + 1) * KAUG_WIDTH],
            q_aug[hh], preferred_element_type=F32)

    def weights(state, bias, next_first):
        m_new, scale, l_new, p = [], [], [], {}

        def tile(hh, g):
            s = s_ref[hh * MOBA_CHUNK + g]
            return s if bias[g] is None else s + tab_ref[hh, bias[g]]

        for hh in heads:
            m, l, _ = state[hh]
            top = m
            for g in range(MOBA_CHUNK):
                top = jnp.maximum(top, jnp.max(tile(hh, g), axis=0, keepdims=True))
            sc = jnp.exp(m - top)
            tot = sc * l
            for g in range(MOBA_CHUNK):
                e = jnp.exp(tile(hh, g) - top)
                if next_first is not None:
                    logits(hh, g, next_first)
                tot = tot + jnp.sum(e, axis=0, keepdims=True)
                p[hh, g] = e.astype(BF16)
            m_new.append(top)
            scale.append(sc)
            l_new.append(tot)
        return m_new, scale, l_new, p

    def accumulate(first_block, state, stats):
        m_new, scale, l_new, p = stats
        out = []
        for hh in heads:
            acc = scale[hh] * state[hh][2]
            for g in range(MOBA_CHUNK):
                acc = acc + jnp.dot(vt_ref[first_block + g, hh * HEAD_DIM:(hh + 1) * HEAD_DIM, :],
                                    p[hh, g], preferred_element_type=F32)
            out.append((m_new[hh], l_new[hh], acc))
        return tuple(out)

    init = tuple((jnp.full((1, MOBA_BLOCK), 0.1 * MASKED, F32), jnp.zeros((1, MOBA_BLOCK), F32),
                  jnp.zeros((HEAD_DIM, MOBA_BLOCK), F32)) for _ in heads)
    first = qb_idx % MOBA_CHUNK + 1

    no_bias = [None] * MOBA_CHUNK
    for hh, g in tiles:
        logits(hh, g, first)

    def trip(c, state):
        stats = weights(state, no_bias, first + (c + 1) * MOBA_CHUNK)
        return accumulate(first + c * MOBA_CHUNK, state, stats)

    state = lax.fori_loop(0, qb_idx // MOBA_CHUNK, trip, init)
    bias = [None] * (MOBA_CHUNK - 2) + [0, 1]
    state = accumulate(qb_idx + 1, state, weights(state, bias, None))
    for hh in heads:
        _, l, acc = state[hh]
        o_ref[:, hh * HEAD_DIM:(hh + 1) * HEAD_DIM] = (acc / l).T.astype(o_ref.dtype)

    cur = step % 2
    for cp in page_copies(step, cur):
        cp.wait()
    for blk in range(pages_per_step // per_block):
        tot = jnp.sum(page_buf[cur, blk * per_block], axis=0)
        for extra in range(1, per_block):
            tot = tot + jnp.sum(page_buf[cur, blk * per_block + extra], axis=0)
        km_ref[0, blk] = tot * (1.0 / MOBA_BLOCK)


def _moba_prompt(q_aug_t, k_aug, v_t, tables, cache_k, page_table_flat):
    t = q_aug_t.shape[1]
    nb = t // MOBA_BLOCK
    hb = MOBA_HEADS_PER_STEP
    n_steps = (N_HEADS // hb) * nb
    page = cache_k.shape[2]
    per_block = MOBA_BLOCK // page
    total_pages = page_table_flat.shape[0]
    pages_per_step = total_pages // n_steps
    assert t % MOBA_BLOCK == 0 and MOBA_CHUNK >= 2 and MOBA_BLOCK % page == 0
    assert total_pages % n_steps == 0 and pages_per_step % per_block == 0
    blocks_per_step = pages_per_step // per_block
    attn, k_mean = pl.pallas_call(
        functools.partial(_moba_prompt_body, pages_per_step=pages_per_step, per_block=per_block),
        grid_spec=pltpu.PrefetchScalarGridSpec(
            num_scalar_prefetch=1,
            grid=(N_HEADS // hb, nb),
            in_specs=[
                pl.BlockSpec((hb * KAUG_WIDTH, MOBA_BLOCK), lambda h, i, pt: (h, i)),
                pl.BlockSpec((t + MOBA_PAD, hb * KAUG_WIDTH), lambda h, i, pt: (0, h)),
                pl.BlockSpec((nb + MOBA_CHUNK, hb * HEAD_DIM, MOBA_BLOCK),
                             lambda h, i, pt: (0, h, 0)),
                pl.BlockSpec((hb, 2, MOBA_BLOCK, MOBA_BLOCK),
                             lambda h, i, pt: (h, TAB_PREV_T // 2, 0, 0)),
                pl.BlockSpec(memory_space=pl.ANY),
            ],
            out_specs=[
                pl.BlockSpec((MOBA_BLOCK, hb * HEAD_DIM), lambda h, i, pt: (i, h)),
                pl.BlockSpec((1, blocks_per_step, N_HEADS, HEAD_DIM),
                             lambda h, i, pt: (h * nb + i, 0, 0, 0)),
            ],
            scratch_shapes=[
                pltpu.VMEM((hb * MOBA_CHUNK, MOBA_BLOCK, MOBA_BLOCK), F32),
                pltpu.VMEM((2, pages_per_step, page, N_HEADS, HEAD_DIM), F32),
                pltpu.SemaphoreType.DMA((2,)),
            ],
        ),
        out_shape=[jax.ShapeDtypeStruct((t, ATTN_WIDTH), BF16),
                   jax.ShapeDtypeStruct((n_steps, blocks_per_step, N_HEADS, HEAD_DIM), F32)],
        compiler_params=_params(("arbitrary", "arbitrary"), 58),
        name="moba_prompt",
    )(page_table_flat, q_aug_t, k_aug, v_t, tables, cache_k)
    return attn, k_mean.reshape(n_steps * blocks_per_step, N_HEADS, HEAD_DIM)


def _sample_select_body(q_ref, km_ref, o_ref):
    km = km_ref[0]
    gate = jnp.sum(km * q_ref[0], axis=-1, keepdims=True)
    gate = jnp.broadcast_to(gate, km.shape)
    blk_id = lax.broadcasted_iota(jnp.int32, km.shape, 0).astype(F32)
    for slot in range(MOBA_TOPK):
        top = jnp.max(gate, axis=0, keepdims=True)
        first = jnp.min(jnp.where(gate == top, blk_id, float(km.shape[0])),
                        axis=0, keepdims=True)
        o_ref[0, slot] = first[0].astype(jnp.int32)
        gate = jnp.where(blk_id == first, -jnp.inf, gate)


def _sample_select(q_heads, k_mean):
    n_req, nbp = k_mean.shape[0], k_mean.shape[1]
    assert nbp >= MOBA_TOPK
    out = pl.pallas_call(
        _sample_select_body,
        grid=(n_req,),
        in_specs=[pl.BlockSpec((1, N_HEADS, HEAD_DIM), lambda b: (b, 0, 0)),
                  pl.BlockSpec((1, nbp, N_HEADS, HEAD_DIM), lambda b: (b, 0, 0, 0))],
        out_specs=pl.BlockSpec((1, MOBA_TOPK, N_HEADS, HEAD_DIM), lambda b: (b, 0, 0, 0)),
        out_shape=jax.ShapeDtypeStruct((n_req, MOBA_TOPK, N_HEADS, HEAD_DIM), jnp.int32),
        compiler_params=_params(("parallel",), 32),
        name="sample_select",
    )(q_heads, k_mean)
    return out[:, :, :, 0]


def _sample_attn_body(pt_ref, sel_ref, q_ref, kn_ref, vn_ref, tprev_ref, town_ref, ck_ref, cv_ref,
                      o_ref, kbuf, vbuf, sem, *, n_pages, per_block):
    b = pl.program_id(0)
    page = kbuf.shape[2]
    last_block = n_pages // per_block - 1

    def copies(req, buf_slot):
        out = []
        for h in range(N_HEADS):
            for slot in range(MOBA_TOPK):
                blk = sel_ref[(req * MOBA_TOPK + slot) * N_HEADS + h]
                for i in range(per_block):
                    pg = pt_ref[req * n_pages + blk * per_block + i]
                    idx = (h * MOBA_TOPK + slot) * per_block + i
                    out.append(pltpu.make_async_copy(ck_ref.at[0, pg, :, h, :],
                                                     kbuf.at[buf_slot, idx], sem.at[buf_slot, 0]))
                    out.append(pltpu.make_async_copy(cv_ref.at[0, pg, :, h, :],
                                                     vbuf.at[buf_slot, idx], sem.at[buf_slot, 1]))
        return out

    @pl.when(b == 0)
    def _():
        for cp in copies(b, 0):
            cp.start()

    @pl.when(b + 1 < pl.num_programs(0))
    def _():
        for cp in copies(b + 1, (b + 1) % 2):
            cp.start()

    cur = b % 2
    for cp in copies(b, cur):
        cp.wait()

    for h in range(N_HEADS):
        hs = slice(h * HEAD_DIM, (h + 1) * HEAD_DIM)
        q = q_ref[0, :, hs] * (HEAD_DIM ** -0.5)
        q8 = jnp.broadcast_to(q, (8, HEAD_DIM)).astype(BF16)
        prev_row = tprev_ref[h, 0, 0:1, :]
        logits = []
        for slot in range(MOBA_TOPK):
            near = (sel_ref[(b * MOBA_TOPK + slot) * N_HEADS + h] == last_block).astype(F32)
            for i in range(per_block):
                kt = kbuf[cur, (h * MOBA_TOPK + slot) * per_block + i].astype(BF16)
                lg = lax.dot_general(q8, kt, NT_DIMS, preferred_element_type=F32)[0:1, :]
                logits.append(lg + near * prev_row[:, i * page:(i + 1) * page])
        l_own = (jnp.sum(q * kn_ref[0, :, hs], axis=1, keepdims=True)
                 + town_ref[h, 0, 0:1, 0:1])
        m = l_own
        for lg in logits:
            m = jnp.maximum(m, jnp.max(lg, axis=1, keepdims=True))
        p_own = jnp.exp(l_own - m)
        den = p_own
        acc = p_own * vn_ref[0, :, hs]
        for idx, lg in enumerate(logits):
            p = jnp.exp(lg - m)
            den = den + jnp.sum(p, axis=1, keepdims=True)
            p8 = jnp.broadcast_to(p, (8, page)).astype(BF16)
            acc = acc + jnp.dot(p8, vbuf[cur, h * MOBA_TOPK * per_block + idx].astype(BF16),
                                preferred_element_type=F32)[0:1, :]
        o_ref[0, :, hs] = (acc / den).astype(o_ref.dtype)


def _sample_attn(q3, kn3, vn3, cache_k, cache_v, page_table_flat, sel_flat, tables, n_pages):
    n_req = q3.shape[0]
    page = cache_k.shape[2]
    per_block = MOBA_BLOCK // page
    n_bufs = N_HEADS * MOBA_TOPK * per_block
    row_spec = pl.BlockSpec((1, 1, ATTN_WIDTH), lambda b, pt, sel: (b, 0, 0))
    return pl.pallas_call(
        functools.partial(_sample_attn_body, n_pages=n_pages, per_block=per_block),
        grid_spec=pltpu.PrefetchScalarGridSpec(
            num_scalar_prefetch=2,
            grid=(n_req,),
            in_specs=[row_spec, row_spec, row_spec,
                      pl.BlockSpec((N_HEADS, 1, 8, MOBA_BLOCK),
                                   lambda b, pt, sel: (0, TAB_PREV, 0, 0)),
                      pl.BlockSpec((N_HEADS, 1, 8, MOBA_BLOCK),
                                   lambda b, pt, sel: (0, TAB_OWN, 0, 0)),
                      pl.BlockSpec(memory_space=pl.ANY),
                      pl.BlockSpec(memory_space=pl.ANY)],
            out_specs=row_spec,
            scratch_shapes=[pltpu.VMEM((2, n_bufs, page, HEAD_DIM), F32),
                            pltpu.VMEM((2, n_bufs, page, HEAD_DIM), F32),
                            pltpu.SemaphoreType.DMA((2, 2))],
        ),
        out_shape=jax.ShapeDtypeStruct((n_req, 1, ATTN_WIDTH), BF16),
        compiler_params=_params(("arbitrary",), 40),
        name="sample_attn",
    )(page_table_flat, sel_flat, q3, kn3, vn3, tables, tables, cache_k, cache_v)


def _cross_prompt_body(q_ref, mk_ref, mv_ref, o_ref):
    for h in range(CROSS_HEADS):
        sl = slice(h * CROSS_HEAD_DIM, (h + 1) * CROSS_HEAD_DIM)
        s = lax.dot_general(q_ref[:, sl], mk_ref[:, sl], NT_DIMS,
                            preferred_element_type=F32) * (CROSS_HEAD_DIM ** -0.5)
        p = jnp.exp(s - jnp.max(s, axis=1, keepdims=True))
        den = jnp.sum(p, axis=1, keepdims=True)
        o = jnp.dot(p.astype(BF16), mv_ref[:, sl], preferred_element_type=F32) / den
        o_ref[:, sl] = o.astype(o_ref.dtype)


def _cross_prompt(qc, mk_bf, mv_bf, bm=512):
    t = qc.shape[0]
    n_mem = mk_bf.shape[0]
    return pl.pallas_call(
        _cross_prompt_body,
        grid=(t // bm,),
        in_specs=[pl.BlockSpec((bm, CROSS_WIDTH), lambda i: (i, 0)),
                  pl.BlockSpec((n_mem, CROSS_WIDTH), lambda i: (0, 0)),
                  pl.BlockSpec((n_mem, CROSS_WIDTH), lambda i: (0, 0))],
        out_specs=pl.BlockSpec((bm, CROSS_WIDTH), lambda i: (i, 0)),
        out_shape=jax.ShapeDtypeStruct((t, CROSS_WIDTH), BF16),
        compiler_params=_params(("parallel",), 32),
        name="cross_prompt",
    )(qc, mk_bf, mv_bf)


def _cross_sample_body(q_ref, mk_hbm, mv_hbm, o_ref, kbuf, vbuf, sem):
    b = pl.program_id(0)

    def copies(req, buf_slot):
        out = []
        for h in range(CROSS_HEADS):
            out.append(pltpu.make_async_copy(mk_hbm.at[0, req, :, h, :], kbuf.at[buf_slot, h],
                                             sem.at[buf_slot, 0]))
            out.append(pltpu.make_async_copy(mv_hbm.at[0, req, :, h, :], vbuf.at[buf_slot, h],
                                             sem.at[buf_slot, 1]))
        return out

    @pl.when(b == 0)
    def _():
        for cp in copies(b, 0):
            cp.start()

    @pl.when(b + 1 < pl.num_programs(0))
    def _():
        for cp in copies(b + 1, (b + 1) % 2):
            cp.start()

    cur = b % 2
    for cp in copies(b, cur):
        cp.wait()

    for h in range(CROSS_HEADS):
        sl = slice(h * CROSS_HEAD_DIM, (h + 1) * CROSS_HEAD_DIM)
        q8 = jnp.broadcast_to(q_ref[0, :, sl], (8, CROSS_HEAD_DIM))
        s = lax.dot_general(q8, kbuf[cur, h].astype(BF16), NT_DIMS,
                            preferred_element_type=F32) * (CROSS_HEAD_DIM ** -0.5)
        p = jnp.exp(s - jnp.max(s, axis=1, keepdims=True))
        den = jnp.sum(p, axis=1, keepdims=True)
        o = jnp.dot(p.astype(BF16), vbuf[cur, h].astype(BF16),
                    preferred_element_type=F32) / den
        o_ref[0, :, sl] = o[0:1, :].astype(o_ref.dtype)


def _cross_sample(qc3, mem_k, mem_v):
    _, n_req, n_mem, _, _ = mem_k.shape
    row_spec = pl.BlockSpec((1, 1, CROSS_WIDTH), lambda b: (b, 0, 0))
    any_spec = pl.BlockSpec(memory_space=pl.ANY)
    return pl.pallas_call(
        _cross_sample_body,
        grid=(n_req,),
        in_specs=[row_spec, any_spec, any_spec],
        out_specs=row_spec,
        out_shape=jax.ShapeDtypeStruct((n_req, 1, CROSS_WIDTH), BF16),
        scratch_shapes=[pltpu.VMEM((2, CROSS_HEADS, n_mem, CROSS_HEAD_DIM), F32),
                        pltpu.VMEM((2, CROSS_HEADS, n_mem, CROSS_HEAD_DIM), F32),
                        pltpu.SemaphoreType.DMA((2, 2))],
        compiler_params=_params(("arbitrary",), 32),
        name="cross_sample",
    )(qc3, mem_k, mem_v)


CONV_TILE = 256
CONV_HALO = 32
CONV_UNROLL = 4


def _conv_prompt_body(cur_ref, prev_ref, w_ref, cb_ref, g_ref, b_ref, o_ref, buf_ref, sh_ref,
                      wt_ref, y_ref):
    i = pl.program_id(0)
    has_prev = (i > 0).astype(F32)
    buf_ref[0:CONV_HALO, :] = prev_ref[CONV_TILE - CONV_HALO:CONV_TILE, :] * has_prev
    buf_ref[CONV_HALO:CONV_HALO + CONV_TILE, :] = cur_ref[...]
    first = CONV_HALO - CONV_STATE
    shifted_rows = sh_ref.shape[1]
    for tap in range(CONV_WIDTH):
        wt_ref[tap] = jnp.broadcast_to(w_ref[tap:tap + 1, :], (SUBLANES, D_CONV))
    for c in range(D_CONV // LANES):
        cs = slice(c * LANES, (c + 1) * LANES)
        for r in range(1, SUBLANES):
            sh_ref[r - 1] = buf_ref[r:r + shifted_rows, cs]
        weights = [wt_ref[tap, :, cs] for tap in range(CONV_WIDTH)]
        bias = jnp.broadcast_to(cb_ref[:, cs], (SUBLANES, LANES))

        def row_tiles(i, carry, cs=cs, weights=weights, bias=bias):
            row0 = pl.multiple_of(i * (CONV_UNROLL * SUBLANES), CONV_UNROLL * SUBLANES)
            acc = [bias] * CONV_UNROLL
            for tap in range(CONV_WIDTH):
                r = (first + tap) % SUBLANES
                for j in range(CONV_UNROLL):
                    rows = pl.ds(row0 + (j * SUBLANES + first + tap - r), SUBLANES)
                    tile = buf_ref[rows, cs] if r == 0 else sh_ref[r - 1, rows, :]
                    acc[j] = acc[j] + tile * weights[tap]
            for j in range(CONV_UNROLL):
                y_ref[pl.ds(row0 + j * SUBLANES, SUBLANES), cs] = acc[j]
            return carry

        lax.fori_loop(0, CONV_TILE // (CONV_UNROLL * SUBLANES), row_tiles, 0)
    y = _layer_norm(y_ref[...], g_ref[...], b_ref[...])
    o_ref[...] = jax.nn.silu(y).astype(o_ref.dtype)


def _conv_prompt(u, conv_w, conv_b, ln_g, ln_b):
    t = u.shape[0]
    vec = pl.BlockSpec((1, D_CONV), lambda i: (0, 0))
    return pl.pallas_call(
        _conv_prompt_body,
        grid=(t // CONV_TILE,),
        in_specs=[pl.BlockSpec((CONV_TILE, D_CONV), lambda i: (i, 0)),
                  pl.BlockSpec((CONV_TILE, D_CONV), lambda i: (jnp.maximum(i - 1, 0), 0)),
                  pl.BlockSpec((CONV_WIDTH, D_CONV), lambda i: (0, 0)),
                  vec, vec, vec],
        out_specs=pl.BlockSpec((CONV_TILE, D_CONV), lambda i: (i, 0)),
        out_shape=jax.ShapeDtypeStruct((t, D_CONV), BF16),
        scratch_shapes=[pltpu.VMEM((CONV_HALO + CONV_TILE, D_CONV), F32),
                        pltpu.VMEM((SUBLANES - 1, CONV_HALO + CONV_TILE - SUBLANES, LANES), F32),
                        pltpu.VMEM((CONV_WIDTH, SUBLANES, D_CONV), F32),
                        pltpu.VMEM((CONV_TILE, D_CONV), F32)],
        compiler_params=_params(("parallel",), 32),
        name="conv_prompt",
    )(u, u, conv_w, conv_b.reshape(1, D_CONV), ln_g.reshape(1, D_CONV), ln_b.reshape(1, D_CONV))


def _conv_sample_body(st_ref, u_ref, w_ref, cb_ref, g_ref, b_ref, o_ref):
    acc = u_ref[...] * w_ref[CONV_STATE:CONV_WIDTH, :]
    for tap in range(CONV_STATE):
        acc = acc + st_ref[tap] * w_ref[tap:tap + 1, :]
    y = _layer_norm(acc + cb_ref[...], g_ref[...], b_ref[...])
    o_ref[...] = jax.nn.silu(y).astype(o_ref.dtype)


def _conv_sample(state_t, u, conv_w, conv_b, ln_g, ln_b):
    n_req = u.shape[0]
    return pl.pallas_call(
        _conv_sample_body,
        out_shape=jax.ShapeDtypeStruct((n_req, D_CONV), BF16),
        name="conv_sample",
    )(state_t, u, conv_w, conv_b.reshape(1, D_CONV), ln_g.reshape(1, D_CONV),
      ln_b.reshape(1, D_CONV))


def kernel(x_prompt, x_sample, mem_prompt, cache_k, cache_v, state_conv, cache_mem_k, cache_mem_v,
           page_table, w_in, conv_w, conv_b, conv_ln_g, conv_ln_b, w_mem_kv, rel_bias, w_branch,
           w_out, ln1_g, ln1_b, w_gate, w_up, w_down, ln2_g, ln2_b):
    depth = w_in.shape[0]
    assert depth == 1 and x_prompt.shape[0] == 1 and x_sample.shape[1] == 1
    alpha = (2 * depth) ** 0.25
    t, d_model = x_prompt.shape[1], x_prompt.shape[2]
    n_req, n_pages = page_table.shape
    page = cache_k.shape[2]
    n_mem = mem_prompt.shape[1]
    assert (n_pages * page) % MOBA_BLOCK == 0

    sizes = (D_CONV, D_CONV, ATTN_WIDTH, ATTN_WIDTH, ATTN_WIDTH, CROSS_WIDTH, N_BRANCH * d_model)
    col_a, col_b, col_q, col_k, col_v, col_qc, col_g = (
        int(c) for c in np.concatenate([[0], np.cumsum(sizes)[:-1]]))
    w_o = w_out[0].astype(BF16)
    w_d = w_down[0].astype(BF16)
    w_br = w_branch[0].astype(BF16)
    d_ff = w_gate.shape[2]
    ff_bk = 512 if d_ff % 512 == 0 else d_ff

    tables = _bias_tables(rel_bias)

    def project_in(x_bf):
        u = _proj(x_bf, [(w_in, col_a), (w_in, col_b)], D_CONV, "glu", [F32])
        q = _proj(x_bf, [(w_in, col_q)], ATTN_WIDTH, "none", [F32], bn=1024)
        k = _proj(x_bf, [(w_in, col_k)], ATTN_WIDTH, "none", [F32], bn=1024)
        v = _proj(x_bf, [(w_in, col_v)], ATTN_WIDTH, "none", [F32], bn=1024)
        qc = _proj(x_bf, [(w_in, col_qc)], CROSS_WIDTH, "none", [BF16], bn=1024)
        gates = _proj(x_bf, [(w_in, col_g)], N_BRANCH * d_model, "sigmoid", [BF16], bn=1024)
        return u, q, k, v, qc, gates

    def finish(x, conv_y, attn, cross, gates, bm_out, bm_ffn):
        mix_in = _branch_mix(conv_y, attn, cross, w_br, gates)
        h1, h1_bf = _mm_res_ln(mix_in, w_o, x, ln1_g[0], ln1_b[0], alpha, [F32, BF16],
                               bm=bm_out, bk=d_model)
        act = _proj(h1_bf, [(w_gate, 0), (w_up, 0)], d_ff, "swiglu", [BF16])
        return _mm_res_ln(act, w_d, h1, ln2_g[0], ln2_b[0], alpha, [F32], bm=bm_ffn, bk=ff_bk)

    xp = x_prompt[0]
    u_p, q_p, k_p, v_p, qc_p, gates_p = project_in(xp.astype(BF16))
    conv_y_p = _conv_prompt(u_p, conv_w[0], conv_b[0], conv_ln_g[0], conv_ln_b[0])
    k_aug, v_pad = _moba_prep(k_p, v_p)
    pt_flat = page_table.reshape(-1)
    attn_p, k_mean_s = _moba_prompt(_moba_select(q_p, _block_mean(k_p)), k_aug, v_pad, tables,
                                    cache_k, pt_flat)
    mem_bf = mem_prompt[0].astype(BF16)
    mk, mk_bf = _proj(mem_bf, [(w_mem_kv, 0)], CROSS_WIDTH, "none", [F32, BF16], bn=1024)
    mv, mv_bf = _proj(mem_bf, [(w_mem_kv, CROSS_WIDTH)], CROSS_WIDTH, "none", [F32, BF16], bn=1024)
    cross_p = _cross_prompt(qc_p, mk_bf, mv_bf)
    y_p = finish(xp, conv_y_p, attn_p, cross_p, gates_p, bm_out=512, bm_ffn=1024)

    xs = x_sample[:, 0]
    u_s, q_s, k_s, v_s, qc_s, gates_s = project_in(xs.astype(BF16))
    conv_y_s = _conv_sample(state_conv[0].transpose(1, 0, 2), u_s, conv_w[0], conv_b[0],
                            conv_ln_g[0], conv_ln_b[0])
    sel = _sample_select(q_s.reshape(n_req, N_HEADS, HEAD_DIM),
                         k_mean_s.reshape(n_req, -1, N_HEADS, HEAD_DIM))
    attn_s = _sample_attn(q_s.reshape(n_req, 1, ATTN_WIDTH), k_s.reshape(n_req, 1, ATTN_WIDTH),
                          v_s.reshape(n_req, 1, ATTN_WIDTH), cache_k, cache_v, pt_flat,
                          sel.reshape(-1), tables, n_pages)
    cross_s = _cross_sample(qc_s.reshape(n_req, 1, CROSS_WIDTH), cache_mem_k, cache_mem_v)
    y_s = finish(xs, conv_y_s, attn_s.reshape(n_req, ATTN_WIDTH),
                 cross_s.reshape(n_req, CROSS_WIDTH), gates_s, bm_out=256, bm_ffn=512)

    new_conv_sample = jnp.concatenate([state_conv[0][:, 1:], u_s[:, None, :]], axis=1)
    return (
        y_p[None],
        y_s[:, None],
        k_p.reshape(1, 1, t, N_HEADS, HEAD_DIM),
        v_p.reshape(1, 1, t, N_HEADS, HEAD_DIM),
        u_p[t - CONV_STATE:][None, None],
        mk.reshape(1, 1, n_mem, CROSS_HEADS, CROSS_HEAD_DIM),
        mv.reshape(1, 1, n_mem, CROSS_HEADS, CROSS_HEAD_DIM),
        k_s.reshape(1, n_req, 1, N_HEADS, HEAD_DIM),
        v_s.reshape(1, n_req, 1, N_HEADS, HEAD_DIM),
        new_conv_sample[None],
    )
```

```python
import functools
import math

import numpy as np
import jax
import jax.numpy as jnp
from jax import lax
from jax.experimental import pallas as pl
from jax.experimental.pallas import tpu as pltpu

F32 = jnp.float32
BF16 = jnp.bfloat16

N_HEADS = 8
HEAD_DIM = 128
ATTN_WIDTH = N_HEADS * HEAD_DIM
MOBA_BLOCK = 256
MOBA_TOPK = 3
D_CONV = 1024
CONV_WIDTH = 31
CONV_STATE = CONV_WIDTH - 1
CROSS_HEADS = 4
CROSS_HEAD_DIM = 256
CROSS_WIDTH = CROSS_HEADS * CROSS_HEAD_DIM
N_BRANCH = 3
BRANCH_WIDTH = 1024
NUM_BUCKETS = 32
MAX_EXACT = NUM_BUCKETS // 2
MAX_DISTANCE = 128
LN_EPS = 1e-5
MASKED = -1e30
LANES = 128
SUBLANES = 8
MIB = 2 ** 20

NT_DIMS = (((1,), (1,)), ((), ()))


def _params(semantics, vmem_mib):
    return pltpu.CompilerParams(dimension_semantics=semantics,
                                vmem_limit_bytes=vmem_mib * MIB)


def _layer_norm(y, g, b):
    mu = jnp.mean(y, axis=-1, keepdims=True)
    yc = y - mu
    var = jnp.mean(yc * yc, axis=-1, keepdims=True)
    return yc * lax.rsqrt(var + LN_EPS) * g + b


def _proj_body(*refs, n_w, n_out, epilogue, tail):
    x_ref = refs[0]
    refs = refs[1:]
    if tail:
        xs_ref = refs[0]
        refs = refs[1:]
    w_refs = refs[:n_w]
    o_refs = refs[n_w:n_w + n_out]
    refs = refs[n_w + n_out:]
    if tail:
        os_refs = refs[:n_out]
        refs = refs[n_out:]
    wbf_refs = refs
    i = pl.program_id(1)

    @pl.when(i == 0)
    def _():
        for w, wbf in zip(w_refs, wbf_refs):
            wbf[...] = w[...].astype(BF16)

    def apply(x, outs):
        z = [jnp.dot(x, w[...], preferred_element_type=F32) for w in wbf_refs]
        if epilogue == "glu":
            y = z[0] * jax.nn.sigmoid(z[1])
        elif epilogue == "swiglu":
            y = jax.nn.silu(z[0]) * z[1]
        elif epilogue == "sigmoid":
            y = jax.nn.sigmoid(z[0])
        else:
            y = z[0]
        for o in outs:
            o[...] = y.astype(o.dtype)

    apply(x_ref[...], o_refs)

    if tail:
        @pl.when(i == pl.num_programs(1) - 1)
        def _():
            apply(xs_ref[...], os_refs)


def _proj(x, xs, ws, n, epilogue, out_dtypes, bm=1024, bn=512, vmem_mib=48):
    m, k = x.shape
    tail = xs is not None
    bm = min(bm, m)
    bn = min(bn, n)
    while n % bn or any(c0 % bn for _, c0 in ws):
        bn //= 2
    assert m % bm == 0 and bn % LANES == 0
    in_specs = [pl.BlockSpec((bm, k), lambda j, i: (i, 0))]
    out_specs = [pl.BlockSpec((bm, bn), lambda j, i: (i, j)) for _ in out_dtypes]
    out_shape = [jax.ShapeDtypeStruct((m, n), dt) for dt in out_dtypes]
    operands = [x]
    if tail:
        ms = xs.shape[0]
        in_specs.append(pl.BlockSpec((ms, k), lambda j, i: (0, 0)))
        out_specs += [pl.BlockSpec((ms, bn), lambda j, i: (0, j)) for _ in out_dtypes]
        out_shape += [jax.ShapeDtypeStruct((ms, n), dt) for dt in out_dtypes]
        operands.append(xs)
    in_specs += [pl.BlockSpec((None, k, bn), lambda j, i, c=c0 // bn: (0, 0, c + j))
                 for _, c0 in ws]
    return pl.pallas_call(
        functools.partial(_proj_body, n_w=len(ws), n_out=len(out_dtypes), epilogue=epilogue,
                          tail=tail),
        grid=(n // bn, m // bm),
        in_specs=in_specs,
        out_specs=out_specs,
        out_shape=out_shape,
        scratch_shapes=[pltpu.VMEM((k, bn), BF16) for _ in ws],
        compiler_params=_params(("parallel", "arbitrary"), vmem_mib),
        name="proj_" + epilogue,
    )(*operands, *[w for w, _ in ws])


def _mm_res_ln_body(a_ref, as_ref, w_ref, r_ref, rs_ref, g_ref, b_ref, *rest, alpha, n_out):
    o_refs = rest[:n_out]
    os_refs = rest[n_out:2 * n_out]
    acc_ref, accs_ref = rest[2 * n_out], rest[2 * n_out + 1]
    i = pl.program_id(0)
    kk = pl.program_id(1)
    last_i = i == pl.num_programs(0) - 1
    last_k = kk == pl.num_programs(1) - 1

    def finish(acc, res, outs):
        out = _layer_norm(alpha * res[...] + acc[...], g_ref[...], b_ref[...])
        for o in outs:
            o[...] = out.astype(o.dtype)

    @pl.when(kk == 0)
    def _():
        acc_ref[...] = jnp.zeros_like(acc_ref)

    acc_ref[...] += jnp.dot(a_ref[...], w_ref[...], preferred_element_type=F32)

    @pl.when(last_k)
    def _():
        finish(acc_ref, r_ref, o_refs)

    @pl.when(last_i & (kk == 0))
    def _():
        accs_ref[...] = jnp.zeros_like(accs_ref)

    @pl.when(last_i)
    def _():
        accs_ref[...] += jnp.dot(as_ref[...], w_ref[...], preferred_element_type=F32)

    @pl.when(last_i & last_k)
    def _():
        finish(accs_ref, rs_ref, os_refs)


def _mm_res_ln(a, a_s, w, res, res_s, g, b, alpha, out_dtypes, bm, bk, vmem_mib=56):
    m, k = a.shape
    ms = a_s.shape[0]
    n = w.shape[1]
    bm = min(bm, m)
    assert m % bm == 0 and k % bk == 0
    return pl.pallas_call(
        functools.partial(_mm_res_ln_body, alpha=alpha, n_out=len(out_dtypes)),
        grid=(m // bm, k // bk),
        in_specs=[
            pl.BlockSpec((bm, bk), lambda i, kk: (i, kk)),
            pl.BlockSpec((ms, bk), lambda i, kk: (0, kk)),
            pl.BlockSpec((bk, n), lambda i, kk: (kk, 0)),
            pl.BlockSpec((bm, n), lambda i, kk: (i, 0)),
            pl.BlockSpec((ms, n), lambda i, kk: (0, 0)),
            pl.BlockSpec((1, n), lambda i, kk: (0, 0)),
            pl.BlockSpec((1, n), lambda i, kk: (0, 0)),
        ],
        out_specs=[pl.BlockSpec((bm, n), lambda i, kk: (i, 0)) for _ in out_dtypes]
        + [pl.BlockSpec((ms, n), lambda i, kk: (0, 0)) for _ in out_dtypes],
        out_shape=[jax.ShapeDtypeStruct((m, n), dt) for dt in out_dtypes]
        + [jax.ShapeDtypeStruct((ms, n), dt) for dt in out_dtypes],
        scratch_shapes=[pltpu.VMEM((bm, n), F32), pltpu.VMEM((ms, n), F32)],
        compiler_params=_params(("arbitrary", "arbitrary"), vmem_mib),
        name="mm_res_ln",
    )(a, a_s, w, res, res_s, g.reshape(1, n), b.reshape(1, n))


def _branch_body(*refs):
    nb = N_BRANCH
    br_refs, brs_refs, w_refs = refs[0:nb], refs[nb:2 * nb], refs[2 * nb:3 * nb]
    g_refs, gs_refs = refs[3 * nb:4 * nb], refs[4 * nb:5 * nb]
    o_ref, os_ref = refs[5 * nb], refs[5 * nb + 1]

    def mix(branches, gates, out):
        y = None
        for br, w, g in zip(branches, w_refs, gates):
            term = g[...] * jnp.dot(br[...], w[...], preferred_element_type=F32)
            y = term if y is None else y + term
        out[...] = y.astype(out.dtype)

    mix(br_refs, g_refs, o_ref)

    @pl.when(pl.program_id(1) == pl.num_programs(1) - 1)
    def _():
        mix(brs_refs, gs_refs, os_ref)


def _branch_mix(branches, branches_s, w_branch, gates, gates_s, bm=1024, bn=1024):
    m = branches[0].shape[0]
    ms = branches_s[0].shape[0]
    d = w_branch.shape[2]
    bm = min(bm, m)
    bn = min(bn, d)
    nj = d // bn
    ids = range(N_BRANCH)
    return pl.pallas_call(
        _branch_body,
        grid=(nj, m // bm),
        in_specs=[pl.BlockSpec((bm, BRANCH_WIDTH), lambda j, i: (i, 0)) for _ in ids]
        + [pl.BlockSpec((ms, BRANCH_WIDTH), lambda j, i: (0, 0)) for _ in ids]
        + [pl.BlockSpec((None, BRANCH_WIDTH, bn), lambda j, i, n=n: (n, 0, j)) for n in ids]
        + [pl.BlockSpec((bm, bn), lambda j, i, n=n: (i, n * nj + j)) for n in ids]
        + [pl.BlockSpec((ms, bn), lambda j, i, n=n: (0, n * nj + j)) for n in ids],
        out_specs=[pl.BlockSpec((bm, bn), lambda j, i: (i, j)),
                   pl.BlockSpec((ms, bn), lambda j, i: (0, j))],
        out_shape=[jax.ShapeDtypeStruct((m, d), BF16), jax.ShapeDtypeStruct((ms, d), BF16)],
        compiler_params=_params(("parallel", "arbitrary"), 48),
        name="branch_mix",
    )(*branches, *branches_s, w_branch, w_branch, w_branch, gates, gates, gates,
      gates_s, gates_s, gates_s)


TAB_PREV = 0
TAB_OWN = 1
TAB_PREV_T = 2
TAB_OWN_T = 3


def _bias_tables_body(rb_ref, o_ref):
    h = pl.program_id(0)
    shape = (MOBA_BLOCK, MOBA_BLOCK)
    far = rb_ref[NUM_BUCKETS - 1, h]
    for slot, offset, q_axis in ((TAB_PREV, MOBA_BLOCK, 0), (TAB_OWN, 0, 0),
                                 (TAB_PREV_T, MOBA_BLOCK, 1), (TAB_OWN_T, 0, 1)):
        qi = lax.broadcasted_iota(jnp.int32, shape, q_axis)
        kj = lax.broadcasted_iota(jnp.int32, shape, 1 - q_axis)
        dist = jnp.maximum(qi - kj + offset, 0)
        large = MAX_EXACT + (jnp.log(jnp.maximum(dist, 1).astype(F32) / MAX_EXACT)
                             / math.log(MAX_DISTANCE / MAX_EXACT)
                             * (NUM_BUCKETS - MAX_EXACT)).astype(jnp.int32)
        large = jnp.minimum(large, NUM_BUCKETS - 1)
        bucket = jnp.where(dist < MAX_EXACT, dist, large)
        val = jnp.zeros(shape, F32)
        for bkt in range(NUM_BUCKETS):
            val = jnp.where(bucket == bkt, rb_ref[bkt, h], val)
        val = val - far
        if slot in (TAB_OWN, TAB_OWN_T):
            val = jnp.where(kj > qi, MASKED, val)
        o_ref[0, slot] = val


def _bias_tables(rel_bias):
    return pl.pallas_call(
        _bias_tables_body,
        grid=(N_HEADS,),
        in_specs=[pl.BlockSpec(memory_space=pltpu.SMEM)],
        out_specs=pl.BlockSpec((1, 4, MOBA_BLOCK, MOBA_BLOCK), lambda h: (h, 0, 0, 0)),
        out_shape=jax.ShapeDtypeStruct((N_HEADS, 4, MOBA_BLOCK, MOBA_BLOCK), F32),
        compiler_params=_params(("arbitrary",), 32),
        name="bias_tables",
    )(rel_bias)


def _block_mean_body(k_ref, o_ref):
    rows = k_ref.shape[0]
    x = k_ref[...].reshape(rows // MOBA_BLOCK, MOBA_BLOCK, k_ref.shape[1])
    o_ref[...] = jnp.sum(x, axis=1) * (1.0 / MOBA_BLOCK)


def _block_mean(k):
    t, w = k.shape
    nb = t // MOBA_BLOCK
    per = min(8, nb)
    assert nb % per == 0
    return pl.pallas_call(
        _block_mean_body,
        grid=(nb // per,),
        in_specs=[pl.BlockSpec((per * MOBA_BLOCK, w), lambda i: (i, 0))],
        out_specs=pl.BlockSpec((per, w), lambda i: (i, 0)),
        out_shape=jax.ShapeDtypeStruct((nb, w), F32),
        compiler_params=_params(("parallel",), 40),
        name="block_mean",
    )(k)


MOBA_CHUNK = 4
MOBA_HEADS_PER_STEP = 2
MOBA_PAD = MOBA_CHUNK * MOBA_BLOCK
KAUG_WIDTH = HEAD_DIM + LANES


def _moba_prep_body(k_ref, v_ref, ka_ref, vt_ref):
    i = pl.program_id(0)
    rows = k_ref.shape[0]
    col = lax.broadcasted_iota(jnp.int32, (rows, LANES), 1)

    @pl.when(i == 0)
    def _():
        pad_hot = jnp.where(col == LANES - 1, 1.0, 0.0).astype(BF16)
        for h in range(N_HEADS):
            ka_ref[:, h * KAUG_WIDTH:h * KAUG_WIDTH + HEAD_DIM] = jnp.zeros((rows, HEAD_DIM), BF16)
            ka_ref[:, h * KAUG_WIDTH + HEAD_DIM:(h + 1) * KAUG_WIDTH] = pad_hot
        vt_ref[...] = jnp.zeros(vt_ref.shape, BF16)

    @pl.when(i > 0)
    def _():
        row = lax.broadcasted_iota(jnp.int32, (rows, LANES), 0) + (i - 1) * rows
        hot = jnp.where(jnp.right_shift(row, int(math.log2(MOBA_BLOCK))) == col,
                        1.0, 0.0).astype(BF16)
        for h in range(N_HEADS):
            ka_ref[:, h * KAUG_WIDTH:h * KAUG_WIDTH + HEAD_DIM] = (
                k_ref[:, h * HEAD_DIM:(h + 1) * HEAD_DIM].astype(BF16))
            ka_ref[:, h * KAUG_WIDTH + HEAD_DIM:(h + 1) * KAUG_WIDTH] = hot
        for blk in range(rows // MOBA_BLOCK):
            vt_ref[blk] = v_ref[blk * MOBA_BLOCK:(blk + 1) * MOBA_BLOCK, :].T.astype(BF16)


def _moba_prep(k, v):
    t = k.shape[0]
    rows = MOBA_PAD
    assert t % rows == 0
    data = pl.BlockSpec((rows, ATTN_WIDTH), lambda i: (jnp.maximum(i - 1, 0), 0))
    return pl.pallas_call(
        _moba_prep_body,
        grid=(t // rows + 1,),
        in_specs=[data, data],
        out_specs=[pl.BlockSpec((rows, N_HEADS * KAUG_WIDTH), lambda i: (i, 0)),
                   pl.BlockSpec((MOBA_CHUNK, ATTN_WIDTH, MOBA_BLOCK), lambda i: (i, 0, 0))],
        out_shape=[jax.ShapeDtypeStruct((t + rows, N_HEADS * KAUG_WIDTH), BF16),
                   jax.ShapeDtypeStruct((t // MOBA_BLOCK + MOBA_CHUNK, ATTN_WIDTH, MOBA_BLOCK),
                                        BF16)],
        compiler_params=_params(("parallel",), 40),
        name="moba_prep",
    )(k, v)


SELECT_ROWS = 1024


def _moba_select_body(q_ref, km_ref, o_ref, *, n_sub):
    q = q_ref[...]
    rows = q.shape[0]
    gate = lax.dot_general(km_ref[...], q, NT_DIMS, precision=lax.Precision.HIGHEST,
                           preferred_element_type=F32)[:n_sub]
    blk = lax.broadcasted_iota(jnp.int32, gate.shape, 0)
    blk_f = blk.astype(F32)
    pos = lax.broadcasted_iota(jnp.int32, gate.shape, 1) + pl.program_id(0) * rows
    own = jnp.right_shift(pos, int(math.log2(MOBA_BLOCK)))
    gate = jnp.where(blk < own, gate, -jnp.inf)
    picked = blk == own
    for _ in range(MOBA_TOPK):
        top = jnp.max(gate, axis=0, keepdims=True)
        first = jnp.min(jnp.where(gate == top, blk_f, float(LANES)), axis=0, keepdims=True)
        chosen = blk_f == first
        picked = picked | (chosen & (top > -jnp.inf))
        gate = jnp.where(chosen, -jnp.inf, gate)
    o_ref[:HEAD_DIM, :] = (q * (HEAD_DIM ** -0.5)).T.astype(BF16)
    o_ref[HEAD_DIM:HEAD_DIM + n_sub, :] = jnp.where(picked, 0.0, MASKED).astype(BF16)
    o_ref[HEAD_DIM + n_sub:, :] = jnp.full((LANES - n_sub, rows), MASKED, BF16)


def _moba_select(q, k_mean):
    t = q.shape[0]
    nb = t // MOBA_BLOCK
    rows = min(SELECT_ROWS, t)
    n_sub = -(-nb // 16) * 16
    assert t % rows == 0 and MOBA_TOPK <= nb and n_sub < LANES
    km = jnp.pad(k_mean, ((0, LANES - nb), (0, 0)))
    return pl.pallas_call(
        functools.partial(_moba_select_body, n_sub=n_sub),
        grid=(t // rows, N_HEADS),
        in_specs=[pl.BlockSpec((rows, HEAD_DIM), lambda i, h: (i, h)),
                  pl.BlockSpec((LANES, HEAD_DIM), lambda i, h: (0, h))],
        out_specs=pl.BlockSpec((KAUG_WIDTH, rows), lambda i, h: (h, i)),
        out_shape=jax.ShapeDtypeStruct((N_HEADS * KAUG_WIDTH, t), BF16),
        compiler_params=_params(("parallel", "parallel"), 40),
        name="moba_select",
    )(q, km)


def _moba_prompt_body(pt_ref, qa_ref, ka_ref, vt_ref, tab_ref, ck_ref, o_ref, km_ref, s_ref,
                      page_buf, page_sem, *, pages_per_step, per_block):
    qb_idx = pl.program_id(1)
    step = pl.program_id(0) * pl.num_programs(1) + qb_idx
    n_steps = pl.num_programs(0) * pl.num_programs(1)

    def page_copies(at_step, slot):
        return [pltpu.make_async_copy(ck_ref.at[0, pt_ref[at_step * pages_per_step + j]],
                                      page_buf.at[slot, j], page_sem.at[slot])
                for j in range(pages_per_step)]

    @pl.when(step == 0)
    def _():
        for cp in page_copies(step, 0):
            cp.start()

    @pl.when(step + 1 < n_steps)
    def _():
        for cp in page_copies(step + 1, (step + 1) % 2):
            cp.start()

    cur = step % 2
    for cp in page_copies(step, cur):
        cp.wait()

    def block_means(blocks):
        for blk in blocks:
            tot = jnp.sum(page_buf[cur, blk * per_block], axis=0)
            for extra in range(1, per_block):
                tot = tot + jnp.sum(page_buf[cur, blk * per_block + extra], axis=0)
            km_ref[0, blk] = tot * (1.0 / MOBA_BLOCK)

    n_blocks = pages_per_step // per_block
    heads = range(MOBA_HEADS_PER_STEP)
    tiles = [(hh, g) for hh in heads for g in range(MOBA_CHUNK)]
    q_aug = [qa_ref[hh * KAUG_WIDTH:(hh + 1) * KAUG_WIDTH, :] for hh in heads]

    def logits(hh, g, first_block):
        start = pl.multiple_of((first_block + g) * MOBA_BLOCK, MOBA_BLOCK)
        s_ref[hh * MOBA_CHUNK + g] = jnp.dot(
            ka_ref[pl.ds(start, MOBA_BLOCK), hh * KAUG_WIDTH:(hh + 1) * KAUG_WIDTH],
            q_aug[hh], preferred_element_type=F32)

    def weights(state, bias, next_first):
        m_new, scale, l_new, p = [], [], [], {}

        def tile(hh, g):
            s = s_ref[hh * MOBA_CHUNK + g]
            return s if bias[g] is None else s + tab_ref[hh, bias[g]]

        for hh in heads:
            m, l, _ = state[hh]
            top = m
            for g in range(MOBA_CHUNK):
                top = jnp.maximum(top, jnp.max(tile(hh, g), axis=0, keepdims=True))
            sc = jnp.exp(m - top)
            tot = sc * l
            for g in range(MOBA_CHUNK):
                e = jnp.exp(tile(hh, g) - top)
                if next_first is not None:
                    logits(hh, g, next_first)
                tot = tot + jnp.sum(e, axis=0, keepdims=True)
                p[hh, g] = e.astype(BF16)
            m_new.append(top)
            scale.append(sc)
            l_new.append(tot)
        return m_new, scale, l_new, p

    def accumulate(first_block, state, stats):
        m_new, scale, l_new, p = stats
        out = []
        for hh in heads:
            acc = scale[hh] * state[hh][2]
            for g in range(MOBA_CHUNK):
                acc = acc + jnp.dot(vt_ref[first_block + g, hh * HEAD_DIM:(hh + 1) * HEAD_DIM, :],
                                    p[hh, g], preferred_element_type=F32)
            out.append((m_new[hh], l_new[hh], acc))
        return tuple(out)

    init = tuple((jnp.full((1, MOBA_BLOCK), 0.1 * MASKED, F32), jnp.zeros((1, MOBA_BLOCK), F32),
                  jnp.zeros((HEAD_DIM, MOBA_BLOCK), F32)) for _ in heads)
    first = qb_idx % MOBA_CHUNK + 1

    no_bias = [None] * MOBA_CHUNK
    for hh, g in tiles:
        logits(hh, g, first)
    block_means(range(0, n_blocks // 2))

    def trip(c, state):
        stats = weights(state, no_bias, first + (c + 1) * MOBA_CHUNK)
        return accumulate(first + c * MOBA_CHUNK, state, stats)

    state = lax.fori_loop(0, qb_idx // MOBA_CHUNK, trip, init)
    bias = [None] * (MOBA_CHUNK - 2) + [0, 1]
    stats = weights(state, bias, None)
    block_means(range(n_blocks // 2, n_blocks))
    state = accumulate(qb_idx + 1, state, stats)
    for hh in heads:
        _, l, acc = state[hh]
        o_ref[:, hh * HEAD_DIM:(hh + 1) * HEAD_DIM] = (acc / l).T.astype(o_ref.dtype)


def _moba_prompt(q_aug_t, k_aug, v_t, tables, cache_k, page_table_flat):
    t = q_aug_t.shape[1]
    nb = t // MOBA_BLOCK
    hb = MOBA_HEADS_PER_STEP
    n_steps = (N_HEADS // hb) * nb
    page = cache_k.shape[2]
    per_block = MOBA_BLOCK // page
    total_pages = page_table_flat.shape[0]
    pages_per_step = total_pages // n_steps
    assert t % MOBA_BLOCK == 0 and MOBA_CHUNK >= 2 and MOBA_BLOCK % page == 0
    assert total_pages % n_steps == 0 and pages_per_step % per_block == 0
    blocks_per_step = pages_per_step // per_block
    attn, k_mean = pl.pallas_call(
        functools.partial(_moba_prompt_body, pages_per_step=pages_per_step, per_block=per_block),
        grid_spec=pltpu.PrefetchScalarGridSpec(
            num_scalar_prefetch=1,
            grid=(N_HEADS // hb, nb),
            in_specs=[
                pl.BlockSpec((hb * KAUG_WIDTH, MOBA_BLOCK), lambda h, i, pt: (h, i)),
                pl.BlockSpec((t + MOBA_PAD, hb * KAUG_WIDTH), lambda h, i, pt: (0, h)),
                pl.BlockSpec((nb + MOBA_CHUNK, hb * HEAD_DIM, MOBA_BLOCK),
                             lambda h, i, pt: (0, h, 0)),
                pl.BlockSpec((hb, 2, MOBA_BLOCK, MOBA_BLOCK),
                             lambda h, i, pt: (h, TAB_PREV_T // 2, 0, 0)),
                pl.BlockSpec(memory_space=pl.ANY),
            ],
            out_specs=[
                pl.BlockSpec((MOBA_BLOCK, hb * HEAD_DIM), lambda h, i, pt: (i, h)),
                pl.BlockSpec((1, blocks_per_step, N_HEADS, HEAD_DIM),
                             lambda h, i, pt: (h * nb + i, 0, 0, 0)),
            ],
            scratch_shapes=[
                pltpu.VMEM((hb * MOBA_CHUNK, MOBA_BLOCK, MOBA_BLOCK), F32),
                pltpu.VMEM((2, pages_per_step, page, N_HEADS, HEAD_DIM), F32),
                pltpu.SemaphoreType.DMA((2,)),
            ],
        ),
        out_shape=[jax.ShapeDtypeStruct((t, ATTN_WIDTH), BF16),
                   jax.ShapeDtypeStruct((n_steps, blocks_per_step, N_HEADS, HEAD_DIM), F32)],
        compiler_params=_params(("arbitrary", "arbitrary"), 58),
        name="moba_prompt",
    )(page_table_flat, q_aug_t, k_aug, v_t, tables, cache_k)
    return attn, k_mean.reshape(n_steps * blocks_per_step, N_HEADS, HEAD_DIM)


def _sample_select_body(q_ref, km_ref, o_ref):
    km = km_ref[0]
    gate = jnp.sum(km * q_ref[0], axis=-1, keepdims=True)
    gate = jnp.broadcast_to(gate, km.shape)
    blk_id = lax.broadcasted_iota(jnp.int32, km.shape, 0).astype(F32)
    for slot in range(MOBA_TOPK):
        top = jnp.max(gate, axis=0, keepdims=True)
        first = jnp.min(jnp.where(gate == top, blk_id, float(km.shape[0])),
                        axis=0, keepdims=True)
        o_ref[0, slot] = first[0].astype(jnp.int32)
        gate = jnp.where(blk_id == first, -jnp.inf, gate)


def _sample_select(q_heads, k_mean):
    n_req, nbp = k_mean.shape[0], k_mean.shape[1]
    assert nbp >= MOBA_TOPK
    out = pl.pallas_call(
        _sample_select_body,
        grid=(n_req,),
        in_specs=[pl.BlockSpec((1, N_HEADS, HEAD_DIM), lambda b: (b, 0, 0)),
                  pl.BlockSpec((1, nbp, N_HEADS, HEAD_DIM), lambda b: (b, 0, 0, 0))],
        out_specs=pl.BlockSpec((1, MOBA_TOPK, N_HEADS, HEAD_DIM), lambda b: (b, 0, 0, 0)),
        out_shape=jax.ShapeDtypeStruct((n_req, MOBA_TOPK, N_HEADS, HEAD_DIM), jnp.int32),
        compiler_params=_params(("parallel",), 32),
        name="sample_select",
    )(q_heads, k_mean)
    return out[:, :, :, 0]


def _sample_attn_body(pt_ref, sel_ref, q_ref, kn_ref, vn_ref, tprev_ref, town_ref, ck_ref, cv_ref,
                      o_ref, kbuf, vbuf, sem, *, n_pages, per_block):
    b = pl.program_id(0)
    page = kbuf.shape[2]
    last_block = n_pages // per_block - 1

    def copies(req, buf_slot):
        out = []
        for h in range(N_HEADS):
            for slot in range(MOBA_TOPK):
                blk = sel_ref[(req * MOBA_TOPK + slot) * N_HEADS + h]
                for i in range(per_block):
                    pg = pt_ref[req * n_pages + blk * per_block + i]
                    idx = (h * MOBA_TOPK + slot) * per_block + i
                    out.append(pltpu.make_async_copy(ck_ref.at[0, pg, :, h, :],
                                                     kbuf.at[buf_slot, idx], sem.at[buf_slot, 0]))
                    out.append(pltpu.make_async_copy(cv_ref.at[0, pg, :, h, :],
                                                     vbuf.at[buf_slot, idx], sem.at[buf_slot, 1]))
        return out

    @pl.when(b == 0)
    def _():
        for cp in copies(b, 0):
            cp.start()

    @pl.when(b + 1 < pl.num_programs(0))
    def _():
        for cp in copies(b + 1, (b + 1) % 2):
            cp.start()

    cur = b % 2
    for cp in copies(b, cur):
        cp.wait()

    for h in range(N_HEADS):
        hs = slice(h * HEAD_DIM, (h + 1) * HEAD_DIM)
        q = q_ref[0, :, hs] * (HEAD_DIM ** -0.5)
        q8 = jnp.broadcast_to(q, (8, HEAD_DIM)).astype(BF16)
        prev_row = tprev_ref[h, 0, 0:1, :]
        logits = []
        for slot in range(MOBA_TOPK):
            near = (sel_ref[(b * MOBA_TOPK + slot) * N_HEADS + h] == last_block).astype(F32)
            for i in range(per_block):
                kt = kbuf[cur, (h * MOBA_TOPK + slot) * per_block + i].astype(BF16)
                lg = lax.dot_general(q8, kt, NT_DIMS, preferred_element_type=F32)[0:1, :]
                logits.append(lg + near * prev_row[:, i * page:(i + 1) * page])
        l_own = (jnp.sum(q * kn_ref[0, :, hs], axis=1, keepdims=True)
                 + town_ref[h, 0, 0:1, 0:1])
        m = l_own
        for lg in logits:
            m = jnp.maximum(m, jnp.max(lg, axis=1, keepdims=True))
        p_own = jnp.exp(l_own - m)
        den = p_own
        acc = p_own * vn_ref[0, :, hs]
        for idx, lg in enumerate(logits):
            p = jnp.exp(lg - m)
            den = den + jnp.sum(p, axis=1, keepdims=True)
            p8 = jnp.broadcast_to(p, (8, page)).astype(BF16)
            acc = acc + jnp.dot(p8, vbuf[cur, h * MOBA_TOPK * per_block + idx].astype(BF16),
                                preferred_element_type=F32)[0:1, :]
        o_ref[0, :, hs] = (acc / den).astype(o_ref.dtype)


def _sample_attn(q3, kn3, vn3, cache_k, cache_v, page_table_flat, sel_flat, tables, n_pages):
    n_req = q3.shape[0]
    page = cache_k.shape[2]
    per_block = MOBA_BLOCK // page
    n_bufs = N_HEADS * MOBA_TOPK * per_block
    row_spec = pl.BlockSpec((1, 1, ATTN_WIDTH), lambda b, pt, sel: (b, 0, 0))
    return pl.pallas_call(
        functools.partial(_sample_attn_body, n_pages=n_pages, per_block=per_block),
        grid_spec=pltpu.PrefetchScalarGridSpec(
            num_scalar_prefetch=2,
            grid=(n_req,),
            in_specs=[row_spec, row_spec, row_spec,
                      pl.BlockSpec((N_HEADS, 1, 8, MOBA_BLOCK),
                                   lambda b, pt, sel: (0, TAB_PREV, 0, 0)),
                      pl.BlockSpec((N_HEADS, 1, 8, MOBA_BLOCK),
                                   lambda b, pt, sel: (0, TAB_OWN, 0, 0)),
                      pl.BlockSpec(memory_space=pl.ANY),
                      pl.BlockSpec(memory_space=pl.ANY)],
            out_specs=row_spec,
            scratch_shapes=[pltpu.VMEM((2, n_bufs, page, HEAD_DIM), F32),
                            pltpu.VMEM((2, n_bufs, page, HEAD_DIM), F32),
                            pltpu.SemaphoreType.DMA((2, 2))],
        ),
        out_shape=jax.ShapeDtypeStruct((n_req, 1, ATTN_WIDTH), BF16),
        compiler_params=_params(("arbitrary",), 40),
        name="sample_attn",
    )(page_table_flat, sel_flat, q3, kn3, vn3, tables, tables, cache_k, cache_v)


def _cross_prompt_body(q_ref, mk_ref, mv_ref, o_ref):
    for h in range(CROSS_HEADS):
        sl = slice(h * CROSS_HEAD_DIM, (h + 1) * CROSS_HEAD_DIM)
        s = lax.dot_general(q_ref[:, sl], mk_ref[:, sl], NT_DIMS,
                            preferred_element_type=F32) * (CROSS_HEAD_DIM ** -0.5)
        p = jnp.exp(s - jnp.max(s, axis=1, keepdims=True))
        den = jnp.sum(p, axis=1, keepdims=True)
        o = jnp.dot(p.astype(BF16), mv_ref[:, sl], preferred_element_type=F32) / den
        o_ref[:, sl] = o.astype(o_ref.dtype)


def _cross_prompt(qc, mk_bf, mv_bf, bm=512):
    t = qc.shape[0]
    n_mem = mk_bf.shape[0]
    return pl.pallas_call(
        _cross_prompt_body,
        grid=(t // bm,),
        in_specs=[pl.BlockSpec((bm, CROSS_WIDTH), lambda i: (i, 0)),
                  pl.BlockSpec((n_mem, CROSS_WIDTH), lambda i: (0, 0)),
                  pl.BlockSpec((n_mem, CROSS_WIDTH), lambda i: (0, 0))],
        out_specs=pl.BlockSpec((bm, CROSS_WIDTH), lambda i: (i, 0)),
        out_shape=jax.ShapeDtypeStruct((t, CROSS_WIDTH), BF16),
        compiler_params=_params(("parallel",), 32),
        name="cross_prompt",
    )(qc, mk_bf, mv_bf)


def _cross_sample_body(q_ref, mk_hbm, mv_hbm, o_ref, kbuf, vbuf, sem):
    b = pl.program_id(0)

    def copies(req, buf_slot):
        out = []
        for h in range(CROSS_HEADS):
            out.append(pltpu.make_async_copy(mk_hbm.at[0, req, :, h, :], kbuf.at[buf_slot, h],
                                             sem.at[buf_slot, 0]))
            out.append(pltpu.make_async_copy(mv_hbm.at[0, req, :, h, :], vbuf.at[buf_slot, h],
                                             sem.at[buf_slot, 1]))
        return out

    @pl.when(b == 0)
    def _():
        for cp in copies(b, 0):
            cp.start()

    @pl.when(b + 1 < pl.num_programs(0))
    def _():
        for cp in copies(b + 1, (b + 1) % 2):
            cp.start()

    cur = b % 2
    for cp in copies(b, cur):
        cp.wait()

    for h in range(CROSS_HEADS):
        sl = slice(h * CROSS_HEAD_DIM, (h + 1) * CROSS_HEAD_DIM)
        q8 = jnp.broadcast_to(q_ref[0, :, sl], (8, CROSS_HEAD_DIM))
        s = lax.dot_general(q8, kbuf[cur, h].astype(BF16), NT_DIMS,
                            preferred_element_type=F32) * (CROSS_HEAD_DIM ** -0.5)
        p = jnp.exp(s - jnp.max(s, axis=1, keepdims=True))
        den = jnp.sum(p, axis=1, keepdims=True)
        o = jnp.dot(p.astype(BF16), vbuf[cur, h].astype(BF16),
                    preferred_element_type=F32) / den
        o_ref[0, :, sl] = o[0:1, :].astype(o_ref.dtype)


def _cross_sample(qc3, mem_k, mem_v):
    _, n_req, n_mem, _, _ = mem_k.shape
    row_spec = pl.BlockSpec((1, 1, CROSS_WIDTH), lambda b: (b, 0, 0))
    any_spec = pl.BlockSpec(memory_space=pl.ANY)
    return pl.pallas_call(
        _cross_sample_body,
        grid=(n_req,),
        in_specs=[row_spec, any_spec, any_spec],
        out_specs=row_spec,
        out_shape=jax.ShapeDtypeStruct((n_req, 1, CROSS_WIDTH), BF16),
        scratch_shapes=[pltpu.VMEM((2, CROSS_HEADS, n_mem, CROSS_HEAD_DIM), F32),
                        pltpu.VMEM((2, CROSS_HEADS, n_mem, CROSS_HEAD_DIM), F32),
                        pltpu.SemaphoreType.DMA((2, 2))],
        compiler_params=_params(("arbitrary",), 32),
        name="cross_sample",
    )(qc3, mem_k, mem_v)


CONV_TILE = 256
CONV_HALO = 32
CONV_UNROLL = 4


def _conv_prompt_body(cur_ref, prev_ref, w_ref, cb_ref, g_ref, b_ref, o_ref, buf_ref, sh_ref,
                      wt_ref, y_ref):
    i = pl.program_id(0)
    has_prev = (i > 0).astype(F32)
    buf_ref[0:CONV_HALO, :] = prev_ref[CONV_TILE - CONV_HALO:CONV_TILE, :] * has_prev
    buf_ref[CONV_HALO:CONV_HALO + CONV_TILE, :] = cur_ref[...]
    first = CONV_HALO - CONV_STATE
    shifted_rows = sh_ref.shape[1]
    for tap in range(CONV_WIDTH):
        wt_ref[tap] = jnp.broadcast_to(w_ref[tap:tap + 1, :], (SUBLANES, D_CONV))
    for c in range(D_CONV // LANES):
        cs = slice(c * LANES, (c + 1) * LANES)
        for r in range(1, SUBLANES):
            sh_ref[r - 1] = buf_ref[r:r + shifted_rows, cs]
        weights = [wt_ref[tap, :, cs] for tap in range(CONV_WIDTH)]
        bias = jnp.broadcast_to(cb_ref[:, cs], (SUBLANES, LANES))

        def row_tiles(i, carry, cs=cs, weights=weights, bias=bias):
            row0 = pl.multiple_of(i * (CONV_UNROLL * SUBLANES), CONV_UNROLL * SUBLANES)
            acc = [bias] * CONV_UNROLL
            for tap in range(CONV_WIDTH):
                r = (first + tap) % SUBLANES
                for j in range(CONV_UNROLL):
                    rows = pl.ds(row0 + (j * SUBLANES + first + tap - r), SUBLANES)
                    tile = buf_ref[rows, cs] if r == 0 else sh_ref[r - 1, rows, :]
                    acc[j] = acc[j] + tile * weights[tap]
            for j in range(CONV_UNROLL):
                y_ref[pl.ds(row0 + j * SUBLANES, SUBLANES), cs] = acc[j]
            return carry

        lax.fori_loop(0, CONV_TILE // (CONV_UNROLL * SUBLANES), row_tiles, 0)
    y = _layer_norm(y_ref[...], g_ref[...], b_ref[...])
    o_ref[...] = jax.nn.silu(y).astype(o_ref.dtype)


def _conv_prompt(u, conv_w, conv_b, ln_g, ln_b):
    t = u.shape[0]
    vec = pl.BlockSpec((1, D_CONV), lambda i: (0, 0))
    return pl.pallas_call(
        _conv_prompt_body,
        grid=(t // CONV_TILE,),
        in_specs=[pl.BlockSpec((CONV_TILE, D_CONV), lambda i: (i, 0)),
                  pl.BlockSpec((CONV_TILE, D_CONV), lambda i: (jnp.maximum(i - 1, 0), 0)),
                  pl.BlockSpec((CONV_WIDTH, D_CONV), lambda i: (0, 0)),
                  vec, vec, vec],
        out_specs=pl.BlockSpec((CONV_TILE, D_CONV), lambda i: (i, 0)),
        out_shape=jax.ShapeDtypeStruct((t, D_CONV), BF16),
        scratch_shapes=[pltpu.VMEM((CONV_HALO + CONV_TILE, D_CONV), F32),
                        pltpu.VMEM((SUBLANES - 1, CONV_HALO + CONV_TILE - SUBLANES, LANES), F32),
                        pltpu.VMEM((CONV_WIDTH, SUBLANES, D_CONV), F32),
                        pltpu.VMEM((CONV_TILE, D_CONV), F32)],
        compiler_params=_params(("parallel",), 32),
        name="conv_prompt",
    )(u, u, conv_w, conv_b.reshape(1, D_CONV), ln_g.reshape(1, D_CONV), ln_b.reshape(1, D_CONV))


def _conv_sample_body(st_ref, u_ref, w_ref, cb_ref, g_ref, b_ref, o_ref):
    acc = u_ref[...] * w_ref[CONV_STATE:CONV_WIDTH, :]
    for tap in range(CONV_STATE):
        acc = acc + st_ref[tap] * w_ref[tap:tap + 1, :]
    y = _layer_norm(acc + cb_ref[...], g_ref[...], b_ref[...])
    o_ref[...] = jax.nn.silu(y).astype(o_ref.dtype)


def _conv_sample(state_t, u, conv_w, conv_b, ln_g, ln_b):
    n_req = u.shape[0]
    return pl.pallas_call(
        _conv_sample_body,
        out_shape=jax.ShapeDtypeStruct((n_req, D_CONV), BF16),
        name="conv_sample",
    )(state_t, u, conv_w, conv_b.reshape(1, D_CONV), ln_g.reshape(1, D_CONV),
      ln_b.reshape(1, D_CONV))


def kernel(x_prompt, x_sample, mem_prompt, cache_k, cache_v, state_conv, cache_mem_k, cache_mem_v,
           page_table, w_in, conv_w, conv_b, conv_ln_g, conv_ln_b, w_mem_kv, rel_bias, w_branch,
           w_out, ln1_g, ln1_b, w_gate, w_up, w_down, ln2_g, ln2_b):
    depth = w_in.shape[0]
    assert depth == 1 and x_prompt.shape[0] == 1 and x_sample.shape[1] == 1
    alpha = (2 * depth) ** 0.25
    t, d_model = x_prompt.shape[1], x_prompt.shape[2]
    n_req, n_pages = page_table.shape
    page = cache_k.shape[2]
    n_mem = mem_prompt.shape[1]
    assert (n_pages * page) % MOBA_BLOCK == 0

    sizes = (D_CONV, D_CONV, ATTN_WIDTH, ATTN_WIDTH, ATTN_WIDTH, CROSS_WIDTH, N_BRANCH * d_model)
    col_a, col_b, col_q, col_k, col_v, col_qc, col_g = (
        int(c) for c in np.concatenate([[0], np.cumsum(sizes)[:-1]]))
    w_o = w_out[0].astype(BF16)
    w_d = w_down[0].astype(BF16)
    w_br = w_branch[0].astype(BF16)
    d_ff = w_gate.shape[2]
    ff_bk = 512 if d_ff % 512 == 0 else d_ff

    tables = _bias_tables(rel_bias)

    xp = x_prompt[0]
    xs = x_sample[:, 0]
    xp_bf = xp.astype(BF16)
    xs_bf = xs.astype(BF16)
    u_p, u_s = _proj(xp_bf, xs_bf, [(w_in, col_a), (w_in, col_b)], D_CONV, "glu", [F32])
    q_p, q_s = _proj(xp_bf, xs_bf, [(w_in, col_q)], ATTN_WIDTH, "none", [F32], bn=1024)
    k_p, k_s = _proj(xp_bf, xs_bf, [(w_in, col_k)], ATTN_WIDTH, "none", [F32], bn=1024)
    v_p, v_s = _proj(xp_bf, xs_bf, [(w_in, col_v)], ATTN_WIDTH, "none", [F32], bn=1024)
    qc_p, qc_s = _proj(xp_bf, xs_bf, [(w_in, col_qc)], CROSS_WIDTH, "none", [BF16], bn=1024)
    gates_p, gates_s = _proj(xp_bf, xs_bf, [(w_in, col_g)], N_BRANCH * d_model, "sigmoid",
                             [BF16], bn=1024)

    conv_y_p = _conv_prompt(u_p, conv_w[0], conv_b[0], conv_ln_g[0], conv_ln_b[0])
    k_aug, v_pad = _moba_prep(k_p, v_p)
    pt_flat = page_table.reshape(-1)
    attn_p, k_mean_s = _moba_prompt(_moba_select(q_p, _block_mean(k_p)), k_aug, v_pad, tables,
                                    cache_k, pt_flat)
    mem_bf = mem_prompt[0].astype(BF16)
    mk, mk_bf = _proj(mem_bf, None, [(w_mem_kv, 0)], CROSS_WIDTH, "none", [F32, BF16], bn=1024)
    mv, mv_bf = _proj(mem_bf, None, [(w_mem_kv, CROSS_WIDTH)], CROSS_WIDTH, "none", [F32, BF16],
                      bn=1024)
    cross_p = _cross_prompt(qc_p, mk_bf, mv_bf)

    conv_y_s = _conv_sample(state_conv[0].transpose(1, 0, 2), u_s, conv_w[0], conv_b[0],
                            conv_ln_g[0], conv_ln_b[0])
    sel = _sample_select(q_s.reshape(n_req, N_HEADS, HEAD_DIM),
                         k_mean_s.reshape(n_req, -1, N_HEADS, HEAD_DIM))
    attn_s = _sample_attn(q_s.reshape(n_req, 1, ATTN_WIDTH), k_s.reshape(n_req, 1, ATTN_WIDTH),
                          v_s.reshape(n_req, 1, ATTN_WIDTH), cache_k, cache_v, pt_flat,
                          sel.reshape(-1), tables, n_pages)
    cross_s = _cross_sample(qc_s.reshape(n_req, 1, CROSS_WIDTH), cache_mem_k, cache_mem_v)

    mix_p, mix_s = _branch_mix(
        (conv_y_p, attn_p, cross_p),
        (conv_y_s, attn_s.reshape(n_req, ATTN_WIDTH), cross_s.reshape(n_req, CROSS_WIDTH)),
        w_br, gates_p, gates_s)
    h1_p, h1_p_bf, h1_s, h1_s_bf = _mm_res_ln(mix_p, mix_s, w_o, xp, xs, ln1_g[0], ln1_b[0], alpha,
                                              [F32, BF16], bm=512, bk=d_model)
    act_p, act_s = _proj(h1_p_bf, h1_s_bf, [(w_gate, 0), (w_up, 0)], d_ff, "swiglu", [BF16])
    y_p, y_s = _mm_res_ln(act_p, act_s, w_d, h1_p, h1_s, ln2_g[0], ln2_b[0], alpha, [F32],
                          bm=1024, bk=ff_bk)

    new_conv_sample = jnp.concatenate([state_conv[0][:, 1:], u_s[:, None, :]], axis=1)
    return (
        y_p[None],
        y_s[:, None],
        k_p.reshape(1, 1, t, N_HEADS, HEAD_DIM),
        v_p.reshape(1, 1, t, N_HEADS, HEAD_DIM),
        u_p[t - CONV_STATE:][None, None],
        mk.reshape(1, 1, n_mem, CROSS_HEADS, CROSS_HEAD_DIM),
        mv.reshape(1, 1, n_mem, CROSS_HEADS, CROSS_HEAD_DIM),
        k_s.reshape(1, n_req, 1, N_HEADS, HEAD_DIM),
        v_s.reshape(1, n_req, 1, N_HEADS, HEAD_DIM),
        new_conv_sample[None],
    )
```

```python
import functools
import math

import numpy as np
import jax
import jax.numpy as jnp
from jax import lax
from jax.experimental import pallas as pl
from jax.experimental.pallas import tpu as pltpu

F32 = jnp.float32
BF16 = jnp.bfloat16

N_HEADS = 8
HEAD_DIM = 128
ATTN_WIDTH = N_HEADS * HEAD_DIM
MOBA_BLOCK = 256
MOBA_TOPK = 3
D_CONV = 1024
CONV_WIDTH = 31
CONV_STATE = CONV_WIDTH - 1
CROSS_HEADS = 4
CROSS_HEAD_DIM = 256
CROSS_WIDTH = CROSS_HEADS * CROSS_HEAD_DIM
N_BRANCH = 3
BRANCH_WIDTH = 1024
NUM_BUCKETS = 32
MAX_EXACT = NUM_BUCKETS // 2
MAX_DISTANCE = 128
LN_EPS = 1e-5
MASKED = -1e30
LANES = 128
SUBLANES = 8
MIB = 2 ** 20
LOG2_E = math.log2(math.e)

NT_DIMS = (((1,), (1,)), ((), ()))


def _params(semantics, vmem_mib):
    return pltpu.CompilerParams(dimension_semantics=semantics,
                                vmem_limit_bytes=vmem_mib * MIB)


def _layer_norm(y, g, b):
    mu = jnp.mean(y, axis=-1, keepdims=True)
    yc = y - mu
    var = jnp.mean(yc * yc, axis=-1, keepdims=True)
    return yc * lax.rsqrt(var + LN_EPS) * g + b


def _proj_body(*refs, n_w, n_out, epilogue, tail):
    x_ref = refs[0]
    refs = refs[1:]
    if tail:
        xs_ref = refs[0]
        refs = refs[1:]
    w_refs = refs[:n_w]
    o_refs = refs[n_w:n_w + n_out]
    refs = refs[n_w + n_out:]
    if tail:
        os_refs = refs[:n_out]
        refs = refs[n_out:]
    wbf_refs = refs
    i = pl.program_id(1)

    @pl.when(i == 0)
    def _():
        for w, wbf in zip(w_refs, wbf_refs):
            wbf[...] = w[...].astype(BF16)

    def apply(x, outs):
        z = [jnp.dot(x, w[...], preferred_element_type=F32) for w in wbf_refs]
        if epilogue == "glu":
            y = z[0] * jax.nn.sigmoid(z[1])
        elif epilogue == "swiglu":
            y = jax.nn.silu(z[0]) * z[1]
        elif epilogue == "sigmoid":
            y = jax.nn.sigmoid(z[0])
        else:
            y = z[0]
        for o in outs:
            o[...] = y.astype(o.dtype)

    apply(x_ref[...], o_refs)

    if tail:
        @pl.when(i == pl.num_programs(1) - 1)
        def _():
            apply(xs_ref[...], os_refs)


def _proj(x, xs, ws, n, epilogue, out_dtypes, bm=1024, bn=512, vmem_mib=48):
    m, k = x.shape
    tail = xs is not None
    bm = min(bm, m)
    bn = min(bn, n)
    while n % bn or any(c0 % bn for _, c0 in ws):
        bn //= 2
    assert m % bm == 0 and bn % LANES == 0
    in_specs = [pl.BlockSpec((bm, k), lambda j, i: (i, 0))]
    out_specs = [pl.BlockSpec((bm, bn), lambda j, i: (i, j)) for _ in out_dtypes]
    out_shape = [jax.ShapeDtypeStruct((m, n), dt) for dt in out_dtypes]
    operands = [x]
    if tail:
        ms = xs.shape[0]
        in_specs.append(pl.BlockSpec((ms, k), lambda j, i: (0, 0)))
        out_specs += [pl.BlockSpec((ms, bn), lambda j, i: (0, j)) for _ in out_dtypes]
        out_shape += [jax.ShapeDtypeStruct((ms, n), dt) for dt in out_dtypes]
        operands.append(xs)
    in_specs += [pl.BlockSpec((None, k, bn), lambda j, i, c=c0 // bn: (0, 0, c + j))
                 for _, c0 in ws]
    return pl.pallas_call(
        functools.partial(_proj_body, n_w=len(ws), n_out=len(out_dtypes), epilogue=epilogue,
                          tail=tail),
        grid=(n // bn, m // bm),
        in_specs=in_specs,
        out_specs=out_specs,
        out_shape=out_shape,
        scratch_shapes=[pltpu.VMEM((k, bn), BF16) for _ in ws],
        compiler_params=_params(("parallel", "arbitrary"), vmem_mib),
        name="proj_" + epilogue,
    )(*operands, *[w for w, _ in ws])


def _mm_res_ln_body(a_ref, as_ref, w_ref, r_ref, rs_ref, g_ref, b_ref, *rest, alpha, n_out):
    o_refs = rest[:n_out]
    os_refs = rest[n_out:2 * n_out]
    acc_ref, accs_ref = rest[2 * n_out], rest[2 * n_out + 1]
    i = pl.program_id(0)
    kk = pl.program_id(1)
    last_i = i == pl.num_programs(0) - 1
    last_k = kk == pl.num_programs(1) - 1

    def finish(acc, res, outs):
        out = _layer_norm(alpha * res[...] + acc[...], g_ref[...], b_ref[...])
        for o in outs:
            o[...] = out.astype(o.dtype)

    @pl.when(kk == 0)
    def _():
        acc_ref[...] = jnp.zeros_like(acc_ref)

    acc_ref[...] += jnp.dot(a_ref[...], w_ref[...], preferred_element_type=F32)

    @pl.when(last_k)
    def _():
        finish(acc_ref, r_ref, o_refs)

    @pl.when(last_i & (kk == 0))
    def _():
        accs_ref[...] = jnp.zeros_like(accs_ref)

    @pl.when(last_i)
    def _():
        accs_ref[...] += jnp.dot(as_ref[...], w_ref[...], preferred_element_type=F32)

    @pl.when(last_i & last_k)
    def _():
        finish(accs_ref, rs_ref, os_refs)


def _mm_res_ln(a, a_s, w, res, res_s, g, b, alpha, out_dtypes, bm, bk, vmem_mib=56):
    m, k = a.shape
    ms = a_s.shape[0]
    n = w.shape[1]
    bm = min(bm, m)
    assert m % bm == 0 and k % bk == 0
    return pl.pallas_call(
        functools.partial(_mm_res_ln_body, alpha=alpha, n_out=len(out_dtypes)),
        grid=(m // bm, k // bk),
        in_specs=[
            pl.BlockSpec((bm, bk), lambda i, kk: (i, kk)),
            pl.BlockSpec((ms, bk), lambda i, kk: (0, kk)),
            pl.BlockSpec((bk, n), lambda i, kk: (kk, 0)),
            pl.BlockSpec((bm, n), lambda i, kk: (i, 0)),
            pl.BlockSpec((ms, n), lambda i, kk: (0, 0)),
            pl.BlockSpec((1, n), lambda i, kk: (0, 0)),
            pl.BlockSpec((1, n), lambda i, kk: (0, 0)),
        ],
        out_specs=[pl.BlockSpec((bm, n), lambda i, kk: (i, 0)) for _ in out_dtypes]
        + [pl.BlockSpec((ms, n), lambda i, kk: (0, 0)) for _ in out_dtypes],
        out_shape=[jax.ShapeDtypeStruct((m, n), dt) for dt in out_dtypes]
        + [jax.ShapeDtypeStruct((ms, n), dt) for dt in out_dtypes],
        scratch_shapes=[pltpu.VMEM((bm, n), F32), pltpu.VMEM((ms, n), F32)],
        compiler_params=_params(("arbitrary", "arbitrary"), vmem_mib),
        name="mm_res_ln",
    )(a, a_s, w, res, res_s, g.reshape(1, n), b.reshape(1, n))


def _branch_body(*refs):
    nb = N_BRANCH
    br_refs, brs_refs, w_refs = refs[0:nb], refs[nb:2 * nb], refs[2 * nb:3 * nb]
    g_refs, gs_refs = refs[3 * nb:4 * nb], refs[4 * nb:5 * nb]
    o_ref, os_ref = refs[5 * nb], refs[5 * nb + 1]

    def mix(branches, gates, out):
        y = None
        for br, w, g in zip(branches, w_refs, gates):
            term = g[...] * jnp.dot(br[...], w[...], preferred_element_type=F32)
            y = term if y is None else y + term
        out[...] = y.astype(out.dtype)

    mix(br_refs, g_refs, o_ref)

    @pl.when(pl.program_id(1) == pl.num_programs(1) - 1)
    def _():
        mix(brs_refs, gs_refs, os_ref)


def _branch_mix(branches, branches_s, w_branch, gates, gates_s, bm=1024, bn=1024):
    m = branches[0].shape[0]
    ms = branches_s[0].shape[0]
    d = w_branch.shape[2]
    bm = min(bm, m)
    bn = min(bn, d)
    nj = d // bn
    ids = range(N_BRANCH)
    return pl.pallas_call(
        _branch_body,
        grid=(nj, m // bm),
        in_specs=[pl.BlockSpec((bm, BRANCH_WIDTH), lambda j, i: (i, 0)) for _ in ids]
        + [pl.BlockSpec((ms, BRANCH_WIDTH), lambda j, i: (0, 0)) for _ in ids]
        + [pl.BlockSpec((None, BRANCH_WIDTH, bn), lambda j, i, n=n: (n, 0, j)) for n in ids]
        + [pl.BlockSpec((bm, bn), lambda j, i, n=n: (i, n * nj + j)) for n in ids]
        + [pl.BlockSpec((ms, bn), lambda j, i, n=n: (0, n * nj + j)) for n in ids],
        out_specs=[pl.BlockSpec((bm, bn), lambda j, i: (i, j)),
                   pl.BlockSpec((ms, bn), lambda j, i: (0, j))],
        out_shape=[jax.ShapeDtypeStruct((m, d), BF16), jax.ShapeDtypeStruct((ms, d), BF16)],
        compiler_params=_params(("parallel", "arbitrary"), 48),
        name="branch_mix",
    )(*branches, *branches_s, w_branch, w_branch, w_branch, gates, gates, gates,
      gates_s, gates_s, gates_s)


TAB_PREV = 0
TAB_OWN = 1
TAB_PREV_T = 2
TAB_OWN_T = 3


def _bias_tables_body(rb_ref, o_ref):
    h = pl.program_id(0)
    shape = (MOBA_BLOCK, MOBA_BLOCK)
    far = rb_ref[NUM_BUCKETS - 1, h]
    for slot, offset, q_axis in ((TAB_PREV, MOBA_BLOCK, 0), (TAB_OWN, 0, 0),
                                 (TAB_PREV_T, MOBA_BLOCK, 1), (TAB_OWN_T, 0, 1)):
        qi = lax.broadcasted_iota(jnp.int32, shape, q_axis)
        kj = lax.broadcasted_iota(jnp.int32, shape, 1 - q_axis)
        dist = jnp.maximum(qi - kj + offset, 0)
        large = MAX_EXACT + (jnp.log(jnp.maximum(dist, 1).astype(F32) / MAX_EXACT)
                             / math.log(MAX_DISTANCE / MAX_EXACT)
                             * (NUM_BUCKETS - MAX_EXACT)).astype(jnp.int32)
        large = jnp.minimum(large, NUM_BUCKETS - 1)
        bucket = jnp.where(dist < MAX_EXACT, dist, large)
        val = jnp.zeros(shape, F32)
        for bkt in range(NUM_BUCKETS):
            val = jnp.where(bucket == bkt, rb_ref[bkt, h], val)
        val = val - far
        if slot in (TAB_PREV_T, TAB_OWN_T):
            val = val * LOG2_E
        if slot in (TAB_OWN, TAB_OWN_T):
            val = jnp.where(kj > qi, MASKED, val)
        o_ref[0, slot] = val


def _bias_tables(rel_bias):
    return pl.pallas_call(
        _bias_tables_body,
        grid=(N_HEADS,),
        in_specs=[pl.BlockSpec(memory_space=pltpu.SMEM)],
        out_specs=pl.BlockSpec((1, 4, MOBA_BLOCK, MOBA_BLOCK), lambda h: (h, 0, 0, 0)),
        out_shape=jax.ShapeDtypeStruct((N_HEADS, 4, MOBA_BLOCK, MOBA_BLOCK), F32),
        compiler_params=_params(("arbitrary",), 32),
        name="bias_tables",
    )(rel_bias)


def _block_mean_body(k_ref, o_ref):
    rows = k_ref.shape[0]
    x = k_ref[...].reshape(rows // MOBA_BLOCK, MOBA_BLOCK, k_ref.shape[1])
    o_ref[...] = jnp.sum(x, axis=1) * (1.0 / MOBA_BLOCK)


def _block_mean(k):
    t, w = k.shape
    nb = t // MOBA_BLOCK
    per = min(8, nb)
    assert nb % per == 0
    return pl.pallas_call(
        _block_mean_body,
        grid=(nb // per,),
        in_specs=[pl.BlockSpec((per * MOBA_BLOCK, w), lambda i: (i, 0))],
        out_specs=pl.BlockSpec((per, w), lambda i: (i, 0)),
        out_shape=jax.ShapeDtypeStruct((nb, w), F32),
        compiler_params=_params(("parallel",), 40),
        name="block_mean",
    )(k)


MOBA_CHUNK = 4
MOBA_HEADS_PER_STEP = 2
MOBA_PAD = MOBA_CHUNK * MOBA_BLOCK
KAUG_WIDTH = HEAD_DIM + LANES


def _moba_prep_body(k_ref, v_ref, ka_ref, vt_ref):
    i = pl.program_id(0)
    rows = k_ref.shape[0]
    col = lax.broadcasted_iota(jnp.int32, (rows, LANES), 1)

    @pl.when(i == 0)
    def _():
        pad_hot = jnp.where(col == LANES - 1, 1.0, 0.0).astype(BF16)
        for h in range(N_HEADS):
            ka_ref[:, h * KAUG_WIDTH:h * KAUG_WIDTH + HEAD_DIM] = jnp.zeros((rows, HEAD_DIM), BF16)
            ka_ref[:, h * KAUG_WIDTH + HEAD_DIM:(h + 1) * KAUG_WIDTH] = pad_hot
        vt_ref[...] = jnp.zeros(vt_ref.shape, BF16)

    @pl.when(i > 0)
    def _():
        row = lax.broadcasted_iota(jnp.int32, (rows, LANES), 0) + (i - 1) * rows
        hot = jnp.where(jnp.right_shift(row, int(math.log2(MOBA_BLOCK))) == col,
                        1.0, 0.0).astype(BF16)
        for h in range(N_HEADS):
            ka_ref[:, h * KAUG_WIDTH:h * KAUG_WIDTH + HEAD_DIM] = (
                k_ref[:, h * HEAD_DIM:(h + 1) * HEAD_DIM].astype(BF16))
            ka_ref[:, h * KAUG_WIDTH + HEAD_DIM:(h + 1) * KAUG_WIDTH] = hot
        for blk in range(rows // MOBA_BLOCK):
            vt_ref[blk] = v_ref[blk * MOBA_BLOCK:(blk + 1) * MOBA_BLOCK, :].T.astype(BF16)


def _moba_prep(k, v):
    t = k.shape[0]
    rows = MOBA_PAD
    assert t % rows == 0
    data = pl.BlockSpec((rows, ATTN_WIDTH), lambda i: (jnp.maximum(i - 1, 0), 0))
    return pl.pallas_call(
        _moba_prep_body,
        grid=(t // rows + 1,),
        in_specs=[data, data],
        out_specs=[pl.BlockSpec((rows, N_HEADS * KAUG_WIDTH), lambda i: (i, 0)),
                   pl.BlockSpec((MOBA_CHUNK, ATTN_WIDTH, MOBA_BLOCK), lambda i: (i, 0, 0))],
        out_shape=[jax.ShapeDtypeStruct((t + rows, N_HEADS * KAUG_WIDTH), BF16),
                   jax.ShapeDtypeStruct((t // MOBA_BLOCK + MOBA_CHUNK, ATTN_WIDTH, MOBA_BLOCK),
                                        BF16)],
        compiler_params=_params(("parallel",), 40),
        name="moba_prep",
    )(k, v)


SELECT_ROWS = 1024


def _moba_select_body(q_ref, km_ref, o_ref, *, n_sub):
    q = q_ref[...]
    rows = q.shape[0]
    gate = lax.dot_general(km_ref[...], q, NT_DIMS, precision=lax.Precision.HIGHEST,
                           preferred_element_type=F32)[:n_sub]
    blk = lax.broadcasted_iota(jnp.int32, gate.shape, 0)
    blk_f = blk.astype(F32)
    pos = lax.broadcasted_iota(jnp.int32, gate.shape, 1) + pl.program_id(0) * rows
    own = jnp.right_shift(pos, int(math.log2(MOBA_BLOCK)))
    gate = jnp.where(blk < own, gate, -jnp.inf)
    picked = blk == own
    for _ in range(MOBA_TOPK):
        top = jnp.max(gate, axis=0, keepdims=True)
        first = jnp.min(jnp.where(gate == top, blk_f, float(LANES)), axis=0, keepdims=True)
        chosen = blk_f == first
        picked = picked | (chosen & (top > -jnp.inf))
        gate = jnp.where(chosen, -jnp.inf, gate)
    o_ref[:HEAD_DIM, :] = (q * (HEAD_DIM ** -0.5 * LOG2_E)).T.astype(BF16)
    o_ref[HEAD_DIM:HEAD_DIM + n_sub, :] = jnp.where(picked, 0.0, MASKED).astype(BF16)
    o_ref[HEAD_DIM + n_sub:, :] = jnp.full((LANES - n_sub, rows), MASKED, BF16)


def _moba_select(q, k_mean):
    t = q.shape[0]
    nb = t // MOBA_BLOCK
    rows = min(SELECT_ROWS, t)
    n_sub = -(-nb // 16) * 16
    assert t % rows == 0 and MOBA_TOPK <= nb and n_sub < LANES
    km = jnp.pad(k_mean, ((0, LANES - nb), (0, 0)))
    return pl.pallas_call(
        functools.partial(_moba_select_body, n_sub=n_sub),
        grid=(t // rows, N_HEADS),
        in_specs=[pl.BlockSpec((rows, HEAD_DIM), lambda i, h: (i, h)),
                  pl.BlockSpec((LANES, HEAD_DIM), lambda i, h: (0, h))],
        out_specs=pl.BlockSpec((KAUG_WIDTH, rows), lambda i, h: (h, i)),
        out_shape=jax.ShapeDtypeStruct((N_HEADS * KAUG_WIDTH, t), BF16),
        compiler_params=_params(("parallel", "parallel"), 40),
        name="moba_select",
    )(q, km)


def _moba_prompt_body(pt_ref, qa_ref, qn_ref, ka_ref, vt_ref, tab_ref, ck_ref, o_ref, km_ref, s_ref,
                      page_buf, page_sem, *, pages_per_step, per_block):
    qb_idx = pl.program_id(1)
    step = pl.program_id(0) * pl.num_programs(1) + qb_idx
    n_steps = pl.num_programs(0) * pl.num_programs(1)

    def page_copies(at_step, slot):
        return [pltpu.make_async_copy(ck_ref.at[0, pt_ref[at_step * pages_per_step + j]],
                                      page_buf.at[slot, j], page_sem.at[slot])
                for j in range(pages_per_step)]

    @pl.when(step == 0)
    def _():
        for cp in page_copies(step, 0):
            cp.start()

    @pl.when(step + 1 < n_steps)
    def _():
        for cp in page_copies(step + 1, (step + 1) % 2):
            cp.start()

    cur = step % 2
    for cp in page_copies(step, cur):
        cp.wait()

    def block_means(blocks):
        for blk in blocks:
            tot = jnp.sum(page_buf[cur, blk * per_block], axis=0)
            for extra in range(1, per_block):
                tot = tot + jnp.sum(page_buf[cur, blk * per_block + extra], axis=0)
            km_ref[0, blk] = tot * (1.0 / MOBA_BLOCK)

    n_blocks = pages_per_step // per_block
    heads = range(MOBA_HEADS_PER_STEP)
    tiles = [(hh, g) for hh in heads for g in range(MOBA_CHUNK)]
    q_aug = [qa_ref[hh * KAUG_WIDTH:(hh + 1) * KAUG_WIDTH, :] for hh in heads]
    q_aug_next = [qn_ref[hh * KAUG_WIDTH:(hh + 1) * KAUG_WIDTH, :] for hh in heads]

    def logits(hh, g, first_block, queries):
        start = pl.multiple_of((first_block + g) * MOBA_BLOCK, MOBA_BLOCK)
        s_ref[hh * MOBA_CHUNK + g] = jnp.dot(
            ka_ref[pl.ds(start, MOBA_BLOCK), hh * KAUG_WIDTH:(hh + 1) * KAUG_WIDTH],
            queries[hh], preferred_element_type=F32)

    def weights(state, bias, next_first, next_queries):
        m_new, scale, l_new, p = [], [], [], {}

        def tile(hh, g):
            s = s_ref[hh * MOBA_CHUNK + g]
            return s if bias[g] is None else s + tab_ref[hh, bias[g]]

        for hh in heads:
            m, l, _ = state[hh]
            top = m
            for g in range(MOBA_CHUNK):
                top = jnp.maximum(top, jnp.max(tile(hh, g), axis=0, keepdims=True))
            sc = jnp.exp2(m - top)
            tot = sc * l
            for g in range(MOBA_CHUNK):
                e = jnp.exp2(tile(hh, g) - top)
                logits(hh, g, next_first, next_queries)
                tot = tot + jnp.sum(e, axis=0, keepdims=True)
                p[hh, g] = e.astype(BF16)
            m_new.append(top)
            scale.append(sc)
            l_new.append(tot)
        return m_new, scale, l_new, p

    def accumulate(first_block, state, stats):
        m_new, scale, l_new, p = stats
        out = []
        for hh in heads:
            acc = scale[hh] * state[hh][2]
            for g in range(MOBA_CHUNK):
                acc = acc + jnp.dot(vt_ref[first_block + g, hh * HEAD_DIM:(hh + 1) * HEAD_DIM, :],
                                    p[hh, g], preferred_element_type=F32)
            out.append((m_new[hh], l_new[hh], acc))
        return tuple(out)

    init = tuple((jnp.full((1, MOBA_BLOCK), 0.1 * MASKED, F32), jnp.zeros((1, MOBA_BLOCK), F32),
                  jnp.zeros((HEAD_DIM, MOBA_BLOCK), F32)) for _ in heads)
    first = qb_idx % MOBA_CHUNK + 1

    @pl.when(qb_idx == 0)
    def _():
        for hh, g in tiles:
            logits(hh, g, first, q_aug)

    no_bias = [None] * MOBA_CHUNK

    def trip(c, state):
        stats = weights(state, no_bias, first + (c + 1) * MOBA_CHUNK, q_aug)
        return accumulate(first + c * MOBA_CHUNK, state, stats)

    state = lax.fori_loop(0, qb_idx // MOBA_CHUNK, trip, init)
    bias = [None] * (MOBA_CHUNK - 2) + [0, 1]
    block_means(range(0, n_blocks // 2))
    stats = weights(state, bias, (qb_idx + 1) % MOBA_CHUNK + 1, q_aug_next)
    block_means(range(n_blocks // 2, n_blocks))
    state = accumulate(qb_idx + 1, state, stats)
    for hh in heads:
        _, l, acc = state[hh]
        o_ref[:, hh * HEAD_DIM:(hh + 1) * HEAD_DIM] = (acc / l).T.astype(o_ref.dtype)


def _moba_prompt(q_aug_t, k_aug, v_t, tables, cache_k, page_table_flat):
    t = q_aug_t.shape[1]
    nb = t // MOBA_BLOCK
    hb = MOBA_HEADS_PER_STEP
    n_steps = (N_HEADS // hb) * nb
    page = cache_k.shape[2]
    per_block = MOBA_BLOCK // page
    total_pages = page_table_flat.shape[0]
    pages_per_step = total_pages // n_steps
    assert t % MOBA_BLOCK == 0 and MOBA_CHUNK >= 2 and MOBA_BLOCK % page == 0
    assert total_pages % n_steps == 0 and pages_per_step % per_block == 0
    blocks_per_step = pages_per_step // per_block
    attn, k_mean = pl.pallas_call(
        functools.partial(_moba_prompt_body, pages_per_step=pages_per_step, per_block=per_block),
        grid_spec=pltpu.PrefetchScalarGridSpec(
            num_scalar_prefetch=1,
            grid=(N_HEADS // hb, nb),
            in_specs=[
                pl.BlockSpec((hb * KAUG_WIDTH, MOBA_BLOCK), lambda h, i, pt: (h, i)),
                pl.BlockSpec((hb * KAUG_WIDTH, MOBA_BLOCK),
                             lambda h, i, pt: (h, jnp.minimum(i + 1, nb - 1))),
                pl.BlockSpec((t + MOBA_PAD, hb * KAUG_WIDTH), lambda h, i, pt: (0, h)),
                pl.BlockSpec((nb + MOBA_CHUNK, hb * HEAD_DIM, MOBA_BLOCK),
                             lambda h, i, pt: (0, h, 0)),
                pl.BlockSpec((hb, 2, MOBA_BLOCK, MOBA_BLOCK),
                             lambda h, i, pt: (h, TAB_PREV_T // 2, 0, 0)),
                pl.BlockSpec(memory_space=pl.ANY),
            ],
            out_specs=[
                pl.BlockSpec((MOBA_BLOCK, hb * HEAD_DIM), lambda h, i, pt: (i, h)),
                pl.BlockSpec((1, blocks_per_step, N_HEADS, HEAD_DIM),
                             lambda h, i, pt: (h * nb + i, 0, 0, 0)),
            ],
            scratch_shapes=[
                pltpu.VMEM((hb * MOBA_CHUNK, MOBA_BLOCK, MOBA_BLOCK), F32),
                pltpu.VMEM((2, pages_per_step, page, N_HEADS, HEAD_DIM), F32),
                pltpu.SemaphoreType.DMA((2,)),
            ],
        ),
        out_shape=[jax.ShapeDtypeStruct((t, ATTN_WIDTH), BF16),
                   jax.ShapeDtypeStruct((n_steps, blocks_per_step, N_HEADS, HEAD_DIM), F32)],
        compiler_params=_params(("arbitrary", "arbitrary"), 58),
        name="moba_prompt",
    )(page_table_flat, q_aug_t, q_aug_t, k_aug, v_t, tables, cache_k)
    return attn, k_mean.reshape(n_steps * blocks_per_step, N_HEADS, HEAD_DIM)


def _sample_select_body(q_ref, km_ref, o_ref):
    km = km_ref[0]
    gate = jnp.sum(km * q_ref[0], axis=-1, keepdims=True)
    gate = jnp.broadcast_to(gate, km.shape)
    blk_id = lax.broadcasted_iota(jnp.int32, km.shape, 0).astype(F32)
    for slot in range(MOBA_TOPK):
        top = jnp.max(gate, axis=0, keepdims=True)
        first = jnp.min(jnp.where(gate == top, blk_id, float(km.shape[0])),
                        axis=0, keepdims=True)
        o_ref[0, slot] = first[0].astype(jnp.int32)
        gate = jnp.where(blk_id == first, -jnp.inf, gate)


def _sample_select(q_heads, k_mean):
    n_req, nbp = k_mean.shape[0], k_mean.shape[1]
    assert nbp >= MOBA_TOPK
    out = pl.pallas_call(
        _sample_select_body,
        grid=(n_req,),
        in_specs=[pl.BlockSpec((1, N_HEADS, HEAD_DIM), lambda b: (b, 0, 0)),
                  pl.BlockSpec((1, nbp, N_HEADS, HEAD_DIM), lambda b: (b, 0, 0, 0))],
        out_specs=pl.BlockSpec((1, MOBA_TOPK, N_HEADS, HEAD_DIM), lambda b: (b, 0, 0, 0)),
        out_shape=jax.ShapeDtypeStruct((n_req, MOBA_TOPK, N_HEADS, HEAD_DIM), jnp.int32),
        compiler_params=_params(("parallel",), 32),
        name="sample_select",
    )(q_heads, k_mean)
    return out[:, :, :, 0]


def _sample_attn_body(pt_ref, sel_ref, q_ref, kn_ref, vn_ref, tprev_ref, town_ref, ck_ref, cv_ref,
                      o_ref, kbuf, vbuf, sem, *, n_pages, per_block):
    b = pl.program_id(0)
    page = kbuf.shape[2]
    last_block = n_pages // per_block - 1

    def copies(req, buf_slot):
        out = []
        for h in range(N_HEADS):
            for slot in range(MOBA_TOPK):
                blk = sel_ref[(req * MOBA_TOPK + slot) * N_HEADS + h]
                for i in range(per_block):
                    pg = pt_ref[req * n_pages + blk * per_block + i]
                    idx = (h * MOBA_TOPK + slot) * per_block + i
                    out.append(pltpu.make_async_copy(ck_ref.at[0, pg, :, h, :],
                                                     kbuf.at[buf_slot, idx], sem.at[buf_slot, 0]))
                    out.append(pltpu.make_async_copy(cv_ref.at[0, pg, :, h, :],
                                                     vbuf.at[buf_slot, idx], sem.at[buf_slot, 1]))
        return out

    @pl.when(b == 0)
    def _():
        for cp in copies(b, 0):
            cp.start()

    @pl.when(b + 1 < pl.num_programs(0))
    def _():
        for cp in copies(b + 1, (b + 1) % 2):
            cp.start()

    cur = b % 2
    for cp in copies(b, cur):
        cp.wait()

    hs = [slice(h * HEAD_DIM, (h + 1) * HEAD_DIM) for h in range(N_HEADS)]
    n_tiles = MOBA_TOPK * per_block
    q = [q_ref[0, :, hs[h]] * (HEAD_DIM ** -0.5) for h in range(N_HEADS)]
    logits = {}
    for h in range(N_HEADS):
        q8 = jnp.broadcast_to(q[h], (8, HEAD_DIM)).astype(BF16)
        prev_row = tprev_ref[h, 0, 0:1, :]
        for slot in range(MOBA_TOPK):
            near = (sel_ref[(b * MOBA_TOPK + slot) * N_HEADS + h] == last_block).astype(F32)
            for i in range(per_block):
                idx = slot * per_block + i
                kt = kbuf[cur, h * n_tiles + idx].astype(BF16)
                lg = lax.dot_general(q8, kt, NT_DIMS, preferred_element_type=F32)[0:1, :]
                logits[h, idx] = lg + near * prev_row[:, i * page:(i + 1) * page]
    p_own, den, weights = {}, {}, {}
    for h in range(N_HEADS):
        l_own = (jnp.sum(q[h] * kn_ref[0, :, hs[h]], axis=1, keepdims=True)
                 + town_ref[h, 0, 0:1, 0:1])
        m = l_own
        for idx in range(n_tiles):
            m = jnp.maximum(m, jnp.max(logits[h, idx], axis=1, keepdims=True))
        p_own[h] = jnp.exp(l_own - m)
        tot = p_own[h]
        for idx in range(n_tiles):
            p = jnp.exp(logits[h, idx] - m)
            tot = tot + jnp.sum(p, axis=1, keepdims=True)
            weights[h, idx] = jnp.broadcast_to(p, (8, page)).astype(BF16)
        den[h] = tot
    for h in range(N_HEADS):
        acc = p_own[h] * vn_ref[0, :, hs[h]]
        for idx in range(n_tiles):
            acc = acc + jnp.dot(weights[h, idx], vbuf[cur, h * n_tiles + idx].astype(BF16),
                                preferred_element_type=F32)[0:1, :]
        o_ref[0, :, hs[h]] = (acc / den[h]).astype(o_ref.dtype)


def _sample_attn(q3, kn3, vn3, cache_k, cache_v, page_table_flat, sel_flat, tables, n_pages):
    n_req = q3.shape[0]
    page = cache_k.shape[2]
    per_block = MOBA_BLOCK // page
    n_bufs = N_HEADS * MOBA_TOPK * per_block
    row_spec = pl.BlockSpec((1, 1, ATTN_WIDTH), lambda b, pt, sel: (b, 0, 0))
    return pl.pallas_call(
        functools.partial(_sample_attn_body, n_pages=n_pages, per_block=per_block),
        grid_spec=pltpu.PrefetchScalarGridSpec(
            num_scalar_prefetch=2,
            grid=(n_req,),
            in_specs=[row_spec, row_spec, row_spec,
                      pl.BlockSpec((N_HEADS, 1, 8, MOBA_BLOCK),
                                   lambda b, pt, sel: (0, TAB_PREV, 0, 0)),
                      pl.BlockSpec((N_HEADS, 1, 8, MOBA_BLOCK),
                                   lambda b, pt, sel: (0, TAB_OWN, 0, 0)),
                      pl.BlockSpec(memory_space=pl.ANY),
                      pl.BlockSpec(memory_space=pl.ANY)],
            out_specs=row_spec,
            scratch_shapes=[pltpu.VMEM((2, n_bufs, page, HEAD_DIM), F32),
                            pltpu.VMEM((2, n_bufs, page, HEAD_DIM), F32),
                            pltpu.SemaphoreType.DMA((2, 2))],
        ),
        out_shape=jax.ShapeDtypeStruct((n_req, 1, ATTN_WIDTH), BF16),
        compiler_params=_params(("arbitrary",), 40),
        name="sample_attn",
    )(page_table_flat, sel_flat, q3, kn3, vn3, tables, tables, cache_k, cache_v)


def _cross_prompt_body(q_ref, mk_ref, mv_ref, o_ref):
    for h in range(CROSS_HEADS):
        sl = slice(h * CROSS_HEAD_DIM, (h + 1) * CROSS_HEAD_DIM)
        s = lax.dot_general(q_ref[:, sl], mk_ref[:, sl], NT_DIMS,
                            preferred_element_type=F32) * (CROSS_HEAD_DIM ** -0.5)
        p = jnp.exp(s - jnp.max(s, axis=1, keepdims=True))
        den = jnp.sum(p, axis=1, keepdims=True)
        o = jnp.dot(p.astype(BF16), mv_ref[:, sl], preferred_element_type=F32) / den
        o_ref[:, sl] = o.astype(o_ref.dtype)


def _cross_prompt(qc, mk_bf, mv_bf, bm=512):
    t = qc.shape[0]
    n_mem = mk_bf.shape[0]
    return pl.pallas_call(
        _cross_prompt_body,
        grid=(t // bm,),
        in_specs=[pl.BlockSpec((bm, CROSS_WIDTH), lambda i: (i, 0)),
                  pl.BlockSpec((n_mem, CROSS_WIDTH), lambda i: (0, 0)),
                  pl.BlockSpec((n_mem, CROSS_WIDTH), lambda i: (0, 0))],
        out_specs=pl.BlockSpec((bm, CROSS_WIDTH), lambda i: (i, 0)),
        out_shape=jax.ShapeDtypeStruct((t, CROSS_WIDTH), BF16),
        compiler_params=_params(("parallel",), 32),
        name="cross_prompt",
    )(qc, mk_bf, mv_bf)


def _cross_sample_body(q_ref, mk_hbm, mv_hbm, o_ref, kbuf, vbuf, sem):
    b = pl.program_id(0)

    def copies(req, buf_slot):
        out = []
        for h in range(CROSS_HEADS):
            out.append(pltpu.make_async_copy(mk_hbm.at[0, req, :, h, :], kbuf.at[buf_slot, h],
                                             sem.at[buf_slot, 0]))
            out.append(pltpu.make_async_copy(mv_hbm.at[0, req, :, h, :], vbuf.at[buf_slot, h],
                                             sem.at[buf_slot, 1]))
        return out

    @pl.when(b == 0)
    def _():
        for cp in copies(b, 0):
            cp.start()

    @pl.when(b + 1 < pl.num_programs(0))
    def _():
        for cp in copies(b + 1, (b + 1) % 2):
            cp.start()

    cur = b % 2
    for cp in copies(b, cur):
        cp.wait()

    for h in range(CROSS_HEADS):
        sl = slice(h * CROSS_HEAD_DIM, (h + 1) * CROSS_HEAD_DIM)
        q8 = jnp.broadcast_to(q_ref[0, :, sl], (8, CROSS_HEAD_DIM))
        s = lax.dot_general(q8, kbuf[cur, h].astype(BF16), NT_DIMS,
                            preferred_element_type=F32) * (CROSS_HEAD_DIM ** -0.5)
        p = jnp.exp(s - jnp.max(s, axis=1, keepdims=True))
        den = jnp.sum(p, axis=1, keepdims=True)
        o = jnp.dot(p.astype(BF16), vbuf[cur, h].astype(BF16),
                    preferred_element_type=F32) / den
        o_ref[0, :, sl] = o[0:1, :].astype(o_ref.dtype)


def _cross_sample(qc3, mem_k, mem_v):
    _, n_req, n_mem, _, _ = mem_k.shape
    row_spec = pl.BlockSpec((1, 1, CROSS_WIDTH), lambda b: (b, 0, 0))
    any_spec = pl.BlockSpec(memory_space=pl.ANY)
    return pl.pallas_call(
        _cross_sample_body,
        grid=(n_req,),
        in_specs=[row_spec, any_spec, any_spec],
        out_specs=row_spec,
        out_shape=jax.ShapeDtypeStruct((n_req, 1, CROSS_WIDTH), BF16),
        scratch_shapes=[pltpu.VMEM((2, CROSS_HEADS, n_mem, CROSS_HEAD_DIM), F32),
                        pltpu.VMEM((2, CROSS_HEADS, n_mem, CROSS_HEAD_DIM), F32),
                        pltpu.SemaphoreType.DMA((2, 2))],
        compiler_params=_params(("arbitrary",), 32),
        name="cross_sample",
    )(qc3, mem_k, mem_v)


CONV_TILE = 256
CONV_HALO = 32
CONV_UNROLL = 4


def _conv_prompt_body(cur_ref, prev_ref, w_ref, cb_ref, g_ref, b_ref, o_ref, buf_ref, sh_ref,
                      wt_ref, y_ref):
    i = pl.program_id(0)
    has_prev = (i > 0).astype(F32)
    buf_ref[0:CONV_HALO, :] = prev_ref[CONV_TILE - CONV_HALO:CONV_TILE, :] * has_prev
    buf_ref[CONV_HALO:CONV_HALO + CONV_TILE, :] = cur_ref[...]
    first = CONV_HALO - CONV_STATE
    shifted_rows = sh_ref.shape[1]
    for tap in range(CONV_WIDTH):
        wt_ref[tap] = jnp.broadcast_to(w_ref[tap:tap + 1, :], (SUBLANES, D_CONV))
    for c in range(D_CONV // LANES):
        cs = slice(c * LANES, (c + 1) * LANES)
        for r in range(1, SUBLANES):
            sh_ref[r - 1] = buf_ref[r:r + shifted_rows, cs]
        weights = [wt_ref[tap, :, cs] for tap in range(CONV_WIDTH)]
        bias = jnp.broadcast_to(cb_ref[:, cs], (SUBLANES, LANES))

        def row_tiles(i, carry, cs=cs, weights=weights, bias=bias):
            row0 = pl.multiple_of(i * (CONV_UNROLL * SUBLANES), CONV_UNROLL * SUBLANES)
            acc = [bias] * CONV_UNROLL
            for tap in range(CONV_WIDTH):
                r = (first + tap) % SUBLANES
                for j in range(CONV_UNROLL):
                    rows = pl.ds(row0 + (j * SUBLANES + first + tap - r), SUBLANES)
                    tile = buf_ref[rows, cs] if r == 0 else sh_ref[r - 1, rows, :]
                    acc[j] = acc[j] + tile * weights[tap]
            for j in range(CONV_UNROLL):
                y_ref[pl.ds(row0 + j * SUBLANES, SUBLANES), cs] = acc[j]
            return carry

        lax.fori_loop(0, CONV_TILE // (CONV_UNROLL * SUBLANES), row_tiles, 0)
    y = _layer_norm(y_ref[...], g_ref[...], b_ref[...])
    o_ref[...] = jax.nn.silu(y).astype(o_ref.dtype)


def _conv_prompt(u, conv_w, conv_b, ln_g, ln_b):
    t = u.shape[0]
    vec = pl.BlockSpec((1, D_CONV), lambda i: (0, 0))
    return pl.pallas_call(
        _conv_prompt_body,
        grid=(t // CONV_TILE,),
        in_specs=[pl.BlockSpec((CONV_TILE, D_CONV), lambda i: (i, 0)),
                  pl.BlockSpec((CONV_TILE, D_CONV), lambda i: (jnp.maximum(i - 1, 0), 0)),
                  pl.BlockSpec((CONV_WIDTH, D_CONV), lambda i: (0, 0)),
                  vec, vec, vec],
        out_specs=pl.BlockSpec((CONV_TILE, D_CONV), lambda i: (i, 0)),
        out_shape=jax.ShapeDtypeStruct((t, D_CONV), BF16),
        scratch_shapes=[pltpu.VMEM((CONV_HALO + CONV_TILE, D_CONV), F32),
                        pltpu.VMEM((SUBLANES - 1, CONV_HALO + CONV_TILE - SUBLANES, LANES), F32),
                        pltpu.VMEM((CONV_WIDTH, SUBLANES, D_CONV), F32),
                        pltpu.VMEM((CONV_TILE, D_CONV), F32)],
        compiler_params=_params(("parallel",), 32),
        name="conv_prompt",
    )(u, u, conv_w, conv_b.reshape(1, D_CONV), ln_g.reshape(1, D_CONV), ln_b.reshape(1, D_CONV))


def _conv_sample_body(st_ref, u_ref, w_ref, cb_ref, g_ref, b_ref, o_ref):
    acc = u_ref[...] * w_ref[CONV_STATE:CONV_WIDTH, :]
    for tap in range(CONV_STATE):
        acc = acc + st_ref[tap] * w_ref[tap:tap + 1, :]
    y = _layer_norm(acc + cb_ref[...], g_ref[...], b_ref[...])
    o_ref[...] = jax.nn.silu(y).astype(o_ref.dtype)


def _conv_sample(state_t, u, conv_w, conv_b, ln_g, ln_b):
    n_req = u.shape[0]
    return pl.pallas_call(
        _conv_sample_body,
        out_shape=jax.ShapeDtypeStruct((n_req, D_CONV), BF16),
        name="conv_sample",
    )(state_t, u, conv_w, conv_b.reshape(1, D_CONV), ln_g.reshape(1, D_CONV),
      ln_b.reshape(1, D_CONV))


def kernel(x_prompt, x_sample, mem_prompt, cache_k, cache_v, state_conv, cache_mem_k, cache_mem_v,
           page_table, w_in, conv_w, conv_b, conv_ln_g, conv_ln_b, w_mem_kv, rel_bias, w_branch,
           w_out, ln1_g, ln1_b, w_gate, w_up, w_down, ln2_g, ln2_b):
    depth = w_in.shape[0]
    assert depth == 1 and x_prompt.shape[0] == 1 and x_sample.shape[1] == 1
    alpha = (2 * depth) ** 0.25
    t, d_model = x_prompt.shape[1], x_prompt.shape[2]
    n_req, n_pages = page_table.shape
    page = cache_k.shape[2]
    n_mem = mem_prompt.shape[1]
    assert (n_pages * page) % MOBA_BLOCK == 0

    sizes = (D_CONV, D_CONV, ATTN_WIDTH, ATTN_WIDTH, ATTN_WIDTH, CROSS_WIDTH, N_BRANCH * d_model)
    col_a, col_b, col_q, col_k, col_v, col_qc, col_g = (
        int(c) for c in np.concatenate([[0], np.cumsum(sizes)[:-1]]))
    w_o = w_out[0].astype(BF16)
    w_d = w_down[0].astype(BF16)
    w_br = w_branch[0].astype(BF16)
    d_ff = w_gate.shape[2]
    ff_bk = 512 if d_ff % 512 == 0 else d_ff

    tables = _bias_tables(rel_bias)

    xp = x_prompt[0]
    xs = x_sample[:, 0]
    xp_bf = xp.astype(BF16)
    xs_bf = xs.astype(BF16)
    u_p, u_s = _proj(xp_bf, xs_bf, [(w_in, col_a), (w_in, col_b)], D_CONV, "glu", [F32])
    q_p, q_s = _proj(xp_bf, xs_bf, [(w_in, col_q)], ATTN_WIDTH, "none", [F32], bn=1024)
    k_p, k_s = _proj(xp_bf, xs_bf, [(w_in, col_k)], ATTN_WIDTH, "none", [F32], bn=1024)
    v_p, v_s = _proj(xp_bf, xs_bf, [(w_in, col_v)], ATTN_WIDTH, "none", [F32], bn=1024)
    qc_p, qc_s = _proj(xp_bf, xs_bf, [(w_in, col_qc)], CROSS_WIDTH, "none", [BF16], bn=1024)
    gates_p, gates_s = _proj(xp_bf, xs_bf, [(w_in, col_g)], N_BRANCH * d_model, "sigmoid",
                             [BF16], bn=1024)

    conv_y_p = _conv_prompt(u_p, conv_w[0], conv_b[0], conv_ln_g[0], conv_ln_b[0])
    k_aug, v_pad = _moba_prep(k_p, v_p)
    pt_flat = page_table.reshape(-1)
    attn_p, k_mean_s = _moba_prompt(_moba_select(q_p, _block_mean(k_p)), k_aug, v_pad, tables,
                                    cache_k, pt_flat)
    mem_bf = mem_prompt[0].astype(BF16)
    mk, mk_bf = _proj(mem_bf, None, [(w_mem_kv, 0)], CROSS_WIDTH, "none", [F32, BF16], bn=1024)
    mv, mv_bf = _proj(mem_bf, None, [(w_mem_kv, CROSS_WIDTH)], CROSS_WIDTH, "none", [F32, BF16],
                      bn=1024)
    cross_p = _cross_prompt(qc_p, mk_bf, mv_bf)

    conv_y_s = _conv_sample(state_conv[0].transpose(1, 0, 2), u_s, conv_w[0], conv_b[0],
                            conv_ln_g[0], conv_ln_b[0])
    sel = _sample_select(q_s.reshape(n_req, N_HEADS, HEAD_DIM),
                         k_mean_s.reshape(n_req, -1, N_HEADS, HEAD_DIM))
    attn_s = _sample_attn(q_s.reshape(n_req, 1, ATTN_WIDTH), k_s.reshape(n_req, 1, ATTN_WIDTH),
                          v_s.reshape(n_req, 1, ATTN_WIDTH), cache_k, cache_v, pt_flat,
                          sel.reshape(-1), tables, n_pages)
    cross_s = _cross_sample(qc_s.reshape(n_req, 1, CROSS_WIDTH), cache_mem_k, cache_mem_v)

    mix_p, mix_s = _branch_mix(
        (conv_y_p, attn_p, cross_p),
        (conv_y_s, attn_s.reshape(n_req, ATTN_WIDTH), cross_s.reshape(n_req, CROSS_WIDTH)),
        w_br, gates_p, gates_s)
    h1_p, h1_p_bf, h1_s, h1_s_bf = _mm_res_ln(mix_p, mix_s, w_o, xp, xs, ln1_g[0], ln1_b[0], alpha,
                                              [F32, BF16], bm=512, bk=d_model)
    act_p, act_s = _proj(h1_p_bf, h1_s_bf, [(w_gate, 0), (w_up, 0)], d_ff, "swiglu", [BF16])
    y_p, y_s = _mm_res_ln(act_p, act_s, w_d, h1_p, h1_s, ln2_g[0], ln2_b[0], alpha, [F32],
                          bm=1024, bk=ff_bk)

    new_conv_sample = jnp.concatenate([state_conv[0][:, 1:], u_s[:, None, :]], axis=1)
    return (
        y_p[None],
        y_s[:, None],
        k_p.reshape(1, 1, t, N_HEADS, HEAD_DIM),
        v_p.reshape(1, 1, t, N_HEADS, HEAD_DIM),
        u_p[t - CONV_STATE:][None, None],
        mk.reshape(1, 1, n_mem, CROSS_HEADS, CROSS_HEAD_DIM),
        mv.reshape(1, 1, n_mem, CROSS_HEADS, CROSS_HEAD_DIM),
        k_s.reshape(1, n_req, 1, N_HEADS, HEAD_DIM),
        v_s.reshape(1, n_req, 1, N_HEADS, HEAD_DIM),
        new_conv_sample[None],
    )
```

```python
import functools
import math

import numpy as np
import jax
import jax.numpy as jnp
from jax import lax
from jax.experimental import pallas as pl
from jax.experimental.pallas import tpu as pltpu

F32 = jnp.float32
BF16 = jnp.bfloat16

N_HEADS = 8
HEAD_DIM = 128
ATTN_WIDTH = N_HEADS * HEAD_DIM
MOBA_BLOCK = 256
MOBA_TOPK = 3
D_CONV = 1024
CONV_WIDTH = 31
CONV_STATE = CONV_WIDTH - 1
CROSS_HEADS = 4
CROSS_HEAD_DIM = 256
CROSS_WIDTH = CROSS_HEADS * CROSS_HEAD_DIM
N_BRANCH = 3
BRANCH_WIDTH = 1024
NUM_BUCKETS = 32
MAX_EXACT = NUM_BUCKETS // 2
MAX_DISTANCE = 128
LN_EPS = 1e-5
MASKED = -1e30
LANES = 128
SUBLANES = 8
MIB = 2 ** 20
LOG2_E = math.log2(math.e)

NT_DIMS = (((1,), (1,)), ((), ()))


def _params(semantics, vmem_mib):
    return pltpu.CompilerParams(dimension_semantics=semantics,
                                vmem_limit_bytes=vmem_mib * MIB)


def _layer_norm(y, g, b):
    mu = jnp.mean(y, axis=-1, keepdims=True)
    yc = y - mu
    var = jnp.mean(yc * yc, axis=-1, keepdims=True)
    return yc * lax.rsqrt(var + LN_EPS) * g + b


def _proj_body(*refs, n_w, n_out, epilogue, tail, keep_input):
    x_ref = refs[0]
    refs = refs[1:]
    if tail:
        xs_ref = refs[0]
        refs = refs[1:]
    w_refs = refs[:n_w]
    o_refs = refs[n_w:n_w + n_out]
    refs = refs[n_w + n_out:]
    if tail:
        os_refs = refs[:n_out]
        refs = refs[n_out:]
    if keep_input:
        x_copy_ref = refs[0]
        refs = refs[1:]
        if tail:
            xs_copy_ref = refs[0]
            refs = refs[1:]
    wbf_refs = refs
    i = pl.program_id(1)

    @pl.when(i == 0)
    def _():
        for w, wbf in zip(w_refs, wbf_refs):
            wbf[...] = w[...].astype(BF16)

    def apply(x, outs):
        z = [jnp.dot(x, w[...], preferred_element_type=F32) for w in wbf_refs]
        if epilogue == "glu":
            y = z[0] * jax.nn.sigmoid(z[1])
        elif epilogue == "swiglu":
            y = jax.nn.silu(z[0]) * z[1]
        elif epilogue == "sigmoid":
            y = jax.nn.sigmoid(z[0])
        else:
            y = z[0]
        for o in outs:
            o[...] = y.astype(o.dtype)

    x = x_ref[...].astype(BF16)
    if keep_input:
        x_copy_ref[...] = x
    apply(x, o_refs)

    if tail:
        @pl.when(i == pl.num_programs(1) - 1)
        def _():
            xs = xs_ref[...].astype(BF16)
            if keep_input:
                xs_copy_ref[...] = xs
            apply(xs, os_refs)


def _proj(x, xs, ws, n, epilogue, out_dtypes, bm=1024, bn=512, vmem_mib=48, keep_input=False):
    m, k = x.shape
    tail = xs is not None
    bm = min(bm, m)
    bn = min(bn, n)
    while n % bn or any(c0 % bn for _, c0 in ws):
        bn //= 2
    assert m % bm == 0 and bn % LANES == 0
    assert not keep_input or bn == n
    in_specs = [pl.BlockSpec((bm, k), lambda j, i: (i, 0))]
    out_specs = [pl.BlockSpec((bm, bn), lambda j, i: (i, j)) for _ in out_dtypes]
    out_shape = [jax.ShapeDtypeStruct((m, n), dt) for dt in out_dtypes]
    operands = [x]
    if tail:
        ms = xs.shape[0]
        in_specs.append(pl.BlockSpec((ms, k), lambda j, i: (0, 0)))
        out_specs += [pl.BlockSpec((ms, bn), lambda j, i: (0, j)) for _ in out_dtypes]
        out_shape += [jax.ShapeDtypeStruct((ms, n), dt) for dt in out_dtypes]
        operands.append(xs)
    if keep_input:
        out_specs.append(pl.BlockSpec((bm, k), lambda j, i: (i, 0)))
        out_shape.append(jax.ShapeDtypeStruct((m, k), BF16))
        if tail:
            out_specs.append(pl.BlockSpec((ms, k), lambda j, i: (0, 0)))
            out_shape.append(jax.ShapeDtypeStruct((ms, k), BF16))
    in_specs += [pl.BlockSpec((None, k, bn), lambda j, i, c=c0 // bn: (0, 0, c + j))
                 for _, c0 in ws]
    return pl.pallas_call(
        functools.partial(_proj_body, n_w=len(ws), n_out=len(out_dtypes), epilogue=epilogue,
                          tail=tail, keep_input=keep_input),
        grid=(n // bn, m // bm),
        in_specs=in_specs,
        out_specs=out_specs,
        out_shape=out_shape,
        scratch_shapes=[pltpu.VMEM((k, bn), BF16) for _ in ws],
        compiler_params=_params(("arbitrary" if keep_input else "parallel", "arbitrary"), vmem_mib),
        name="proj_" + epilogue,
    )(*operands, *[w for w, _ in ws])


def _mm_res_ln_body(a_ref, as_ref, w_ref, r_ref, rs_ref, g_ref, b_ref, *o_refs, alpha, n_out):
    def apply(a, res, outs):
        y = alpha * res[...] + jnp.dot(a[...], w_ref[...], preferred_element_type=F32)
        out = _layer_norm(y, g_ref[...], b_ref[...])
        for o in outs:
            o[...] = out.astype(o.dtype)

    apply(a_ref, r_ref, o_refs[:n_out])

    @pl.when(pl.program_id(0) == pl.num_programs(0) - 1)
    def _():
        apply(as_ref, rs_ref, o_refs[n_out:])


def _mm_res_ln(a, a_s, w, res, res_s, g, b, alpha, out_dtypes, bm, vmem_mib=58):
    m, k = a.shape
    ms = a_s.shape[0]
    n = w.shape[1]
    bm = min(bm, m)
    assert m % bm == 0
    return pl.pallas_call(
        functools.partial(_mm_res_ln_body, alpha=alpha, n_out=len(out_dtypes)),
        grid=(m // bm,),
        in_specs=[
            pl.BlockSpec((bm, k), lambda i: (i, 0)),
            pl.BlockSpec((ms, k), lambda i: (0, 0)),
            pl.BlockSpec((k, n), lambda i: (0, 0), pipeline_mode=pl.Buffered(1)),
            pl.BlockSpec((bm, n), lambda i: (i, 0)),
            pl.BlockSpec((ms, n), lambda i: (0, 0)),
            pl.BlockSpec((1, n), lambda i: (0, 0)),
            pl.BlockSpec((1, n), lambda i: (0, 0)),
        ],
        out_specs=[pl.BlockSpec((bm, n), lambda i: (i, 0)) for _ in out_dtypes]
        + [pl.BlockSpec((ms, n), lambda i: (0, 0)) for _ in out_dtypes],
        out_shape=[jax.ShapeDtypeStruct((m, n), dt) for dt in out_dtypes]
        + [jax.ShapeDtypeStruct((ms, n), dt) for dt in out_dtypes],
        compiler_params=_params(("arbitrary",), vmem_mib),
        name="mm_res_ln",
    )(a, a_s, w, res, res_s, g.reshape(1, n), b.reshape(1, n))


def _branch_body(*refs):
    nb = N_BRANCH
    br_refs, brs_refs, w_refs = refs[0:nb], refs[nb:2 * nb], refs[2 * nb:3 * nb]
    g_refs, gs_refs = refs[3 * nb:4 * nb], refs[4 * nb:5 * nb]
    o_ref, os_ref = refs[5 * nb], refs[5 * nb + 1]

    def mix(branches, gates, out):
        y = None
        for br, w, g in zip(branches, w_refs, gates):
            term = g[...] * jnp.dot(br[...], w[...], preferred_element_type=F32)
            y = term if y is None else y + term
        out[...] = y.astype(out.dtype)

    mix(br_refs, g_refs, o_ref)

    @pl.when(pl.program_id(1) == pl.num_programs(1) - 1)
    def _():
        mix(brs_refs, gs_refs, os_ref)


def _branch_mix(branches, branches_s, w_branch, gates, gates_s, bm=1024, bn=1024):
    m = branches[0].shape[0]
    ms = branches_s[0].shape[0]
    d = w_branch.shape[2]
    bm = min(bm, m)
    bn = min(bn, d)
    nj = d // bn
    ids = range(N_BRANCH)
    return pl.pallas_call(
        _branch_body,
        grid=(nj, m // bm),
        in_specs=[pl.BlockSpec((bm, BRANCH_WIDTH), lambda j, i: (i, 0)) for _ in ids]
        + [pl.BlockSpec((ms, BRANCH_WIDTH), lambda j, i: (0, 0)) for _ in ids]
        + [pl.BlockSpec((None, BRANCH_WIDTH, bn), lambda j, i, n=n: (n, 0, j)) for n in ids]
        + [pl.BlockSpec((bm, bn), lambda j, i, n=n: (i, n * nj + j)) for n in ids]
        + [pl.BlockSpec((ms, bn), lambda j, i, n=n: (0, n * nj + j)) for n in ids],
        out_specs=[pl.BlockSpec((bm, bn), lambda j, i: (i, j)),
                   pl.BlockSpec((ms, bn), lambda j, i: (0, j))],
        out_shape=[jax.ShapeDtypeStruct((m, d), BF16), jax.ShapeDtypeStruct((ms, d), BF16)],
        compiler_params=_params(("parallel", "arbitrary"), 48),
        name="branch_mix",
    )(*branches, *branches_s, w_branch, w_branch, w_branch, gates, gates, gates,
      gates_s, gates_s, gates_s)


TAB_PREV = 0
TAB_OWN = 1
TAB_PREV_T = 2
TAB_OWN_T = 3


def _bias_tables_body(rb_ref, o_ref):
    h = pl.program_id(0)
    shape = (MOBA_BLOCK, MOBA_BLOCK)
    far = rb_ref[NUM_BUCKETS - 1, h]
    for slot, offset, q_axis in ((TAB_PREV, MOBA_BLOCK, 0), (TAB_OWN, 0, 0),
                                 (TAB_PREV_T, MOBA_BLOCK, 1), (TAB_OWN_T, 0, 1)):
        qi = lax.broadcasted_iota(jnp.int32, shape, q_axis)
        kj = lax.broadcasted_iota(jnp.int32, shape, 1 - q_axis)
        dist = jnp.maximum(qi - kj + offset, 0)
        large = MAX_EXACT + (jnp.log(jnp.maximum(dist, 1).astype(F32) / MAX_EXACT)
                             / math.log(MAX_DISTANCE / MAX_EXACT)
                             * (NUM_BUCKETS - MAX_EXACT)).astype(jnp.int32)
        large = jnp.minimum(large, NUM_BUCKETS - 1)
        bucket = jnp.where(dist < MAX_EXACT, dist, large)
        val = jnp.zeros(shape, F32)
        for bkt in range(NUM_BUCKETS):
            val = jnp.where(bucket == bkt, rb_ref[bkt, h], val)
        val = val - far
        if slot in (TAB_PREV_T, TAB_OWN_T):
            val = val * LOG2_E
        if slot in (TAB_OWN, TAB_OWN_T):
            val = jnp.where(kj > qi, MASKED, val)
        o_ref[0, slot] = val


def _bias_tables(rel_bias):
    return pl.pallas_call(
        _bias_tables_body,
        grid=(N_HEADS,),
        in_specs=[pl.BlockSpec(memory_space=pltpu.SMEM)],
        out_specs=pl.BlockSpec((1, 4, MOBA_BLOCK, MOBA_BLOCK), lambda h: (h, 0, 0, 0)),
        out_shape=jax.ShapeDtypeStruct((N_HEADS, 4, MOBA_BLOCK, MOBA_BLOCK), F32),
        compiler_params=_params(("arbitrary",), 32),
        name="bias_tables",
    )(rel_bias)


def _block_mean_body(k_ref, o_ref):
    rows = k_ref.shape[0]
    x = k_ref[...].reshape(rows // MOBA_BLOCK, MOBA_BLOCK, k_ref.shape[1])
    o_ref[...] = jnp.sum(x, axis=1) * (1.0 / MOBA_BLOCK)


def _block_mean(k):
    t, w = k.shape
    nb = t // MOBA_BLOCK
    per = min(8, nb)
    assert nb % per == 0
    return pl.pallas_call(
        _block_mean_body,
        grid=(nb // per,),
        in_specs=[pl.BlockSpec((per * MOBA_BLOCK, w), lambda i: (i, 0))],
        out_specs=pl.BlockSpec((per, w), lambda i: (i, 0)),
        out_shape=jax.ShapeDtypeStruct((nb, w), F32),
        compiler_params=_params(("parallel",), 40),
        name="block_mean",
    )(k)


MOBA_CHUNK = 4
MOBA_HEADS_PER_STEP = 2
MOBA_PAD = MOBA_CHUNK * MOBA_BLOCK
KAUG_WIDTH = HEAD_DIM + LANES


def _moba_prep_body(k_ref, v_ref, ka_ref, vt_ref):
    i = pl.program_id(0)
    rows = k_ref.shape[0]
    col = lax.broadcasted_iota(jnp.int32, (rows, LANES), 1)

    @pl.when(i == 0)
    def _():
        pad_hot = jnp.where(col == LANES - 1, 1.0, 0.0).astype(BF16)
        for h in range(N_HEADS):
            ka_ref[:, h * KAUG_WIDTH:h * KAUG_WIDTH + HEAD_DIM] = jnp.zeros((rows, HEAD_DIM), BF16)
            ka_ref[:, h * KAUG_WIDTH + HEAD_DIM:(h + 1) * KAUG_WIDTH] = pad_hot
        vt_ref[...] = jnp.zeros(vt_ref.shape, BF16)

    @pl.when(i > 0)
    def _():
        row = lax.broadcasted_iota(jnp.int32, (rows, LANES), 0) + (i - 1) * rows
        hot = jnp.where(jnp.right_shift(row, int(math.log2(MOBA_BLOCK))) == col,
                        1.0, 0.0).astype(BF16)
        for h in range(N_HEADS):
            ka_ref[:, h * KAUG_WIDTH:h * KAUG_WIDTH + HEAD_DIM] = (
                k_ref[:, h * HEAD_DIM:(h + 1) * HEAD_DIM].astype(BF16))
            ka_ref[:, h * KAUG_WIDTH + HEAD_DIM:(h + 1) * KAUG_WIDTH] = hot
        for blk in range(rows // MOBA_BLOCK):
            vt_ref[blk] = v_ref[blk * MOBA_BLOCK:(blk + 1) * MOBA_BLOCK, :].T.astype(BF16)


def _moba_prep(k, v):
    t = k.shape[0]
    rows = MOBA_PAD
    assert t % rows == 0
    data = pl.BlockSpec((rows, ATTN_WIDTH), lambda i: (jnp.maximum(i - 1, 0), 0))
    return pl.pallas_call(
        _moba_prep_body,
        grid=(t // rows + 1,),
        in_specs=[data, data],
        out_specs=[pl.BlockSpec((rows, N_HEADS * KAUG_WIDTH), lambda i: (i, 0)),
                   pl.BlockSpec((MOBA_CHUNK, ATTN_WIDTH, MOBA_BLOCK), lambda i: (i, 0, 0))],
        out_shape=[jax.ShapeDtypeStruct((t + rows, N_HEADS * KAUG_WIDTH), BF16),
                   jax.ShapeDtypeStruct((t // MOBA_BLOCK + MOBA_CHUNK, ATTN_WIDTH, MOBA_BLOCK),
                                        BF16)],
        compiler_params=_params(("parallel",), 40),
        name="moba_prep",
    )(k, v)


SELECT_ROWS = 1024


def _moba_select_body(q_ref, km_ref, o_ref, *, n_sub):
    q = q_ref[...]
    rows = q.shape[0]
    gate = lax.dot_general(km_ref[...], q, NT_DIMS, precision=lax.Precision.HIGHEST,
                           preferred_element_type=F32)[:n_sub]
    blk = lax.broadcasted_iota(jnp.int32, gate.shape, 0)
    blk_f = blk.astype(F32)
    pos = lax.broadcasted_iota(jnp.int32, gate.shape, 1) + pl.program_id(0) * rows
    own = jnp.right_shift(pos, int(math.log2(MOBA_BLOCK)))
    gate = jnp.where(blk < own, gate, -jnp.inf)
    picked = blk == own
    for _ in range(MOBA_TOPK):
        top = jnp.max(gate, axis=0, keepdims=True)
        first = jnp.min(jnp.where(gate == top, blk_f, float(LANES)), axis=0, keepdims=True)
        chosen = blk_f == first
        picked = picked | (chosen & (top > -jnp.inf))
        gate = jnp.where(chosen, -jnp.inf, gate)
    o_ref[:HEAD_DIM, :] = (q * (HEAD_DIM ** -0.5 * LOG2_E)).T.astype(BF16)
    o_ref[HEAD_DIM:HEAD_DIM + n_sub, :] = jnp.where(picked, 0.0, MASKED).astype(BF16)
    o_ref[HEAD_DIM + n_sub:, :] = jnp.full((LANES - n_sub, rows), MASKED, BF16)


def _moba_select(q, k_mean):
    t = q.shape[0]
    nb = t // MOBA_BLOCK
    rows = min(SELECT_ROWS, t)
    n_sub = -(-nb // 16) * 16
    assert t % rows == 0 and MOBA_TOPK <= nb and n_sub < LANES
    km = jnp.pad(k_mean, ((0, LANES - nb), (0, 0)))
    return pl.pallas_call(
        functools.partial(_moba_select_body, n_sub=n_sub),
        grid=(t // rows, N_HEADS),
        in_specs=[pl.BlockSpec((rows, HEAD_DIM), lambda i, h: (i, h)),
                  pl.BlockSpec((LANES, HEAD_DIM), lambda i, h: (0, h))],
        out_specs=pl.BlockSpec((KAUG_WIDTH, rows), lambda i, h: (h, i)),
        out_shape=jax.ShapeDtypeStruct((N_HEADS * KAUG_WIDTH, t), BF16),
        compiler_params=_params(("parallel", "parallel"), 40),
        name="moba_select",
    )(q, km)


def _moba_prompt_body(pt_ref, qa_ref, qn_ref, ka_ref, vt_ref, tab_ref, ck_ref, o_ref, km_ref, s_ref,
                      page_buf, page_sem, *, pages_per_step, per_block):
    qb_idx = pl.program_id(1)
    step = pl.program_id(0) * pl.num_programs(1) + qb_idx
    n_steps = pl.num_programs(0) * pl.num_programs(1)

    def page_copies(at_step, slot):
        return [pltpu.make_async_copy(ck_ref.at[0, pt_ref[at_step * pages_per_step + j]],
                                      page_buf.at[slot, j], page_sem.at[slot])
                for j in range(pages_per_step)]

    @pl.when(step == 0)
    def _():
        for cp in page_copies(step, 0):
            cp.start()

    @pl.when(step + 1 < n_steps)
    def _():
        for cp in page_copies(step + 1, (step + 1) % 2):
            cp.start()

    cur = step % 2
    for cp in page_copies(step, cur):
        cp.wait()

    def block_means(blocks):
        for blk in blocks:
            tot = jnp.sum(page_buf[cur, blk * per_block], axis=0)
            for extra in range(1, per_block):
                tot = tot + jnp.sum(page_buf[cur, blk * per_block + extra], axis=0)
            km_ref[0, blk] = tot * (1.0 / MOBA_BLOCK)

    n_blocks = pages_per_step // per_block
    heads = range(MOBA_HEADS_PER_STEP)
    tiles = [(hh, g) for hh in heads for g in range(MOBA_CHUNK)]
    q_aug = [qa_ref[hh * KAUG_WIDTH:(hh + 1) * KAUG_WIDTH, :] for hh in heads]
    q_aug_next = [qn_ref[hh * KAUG_WIDTH:(hh + 1) * KAUG_WIDTH, :] for hh in heads]

    def logits(hh, g, first_block, queries):
        start = pl.multiple_of((first_block + g) * MOBA_BLOCK, MOBA_BLOCK)
        s_ref[hh * MOBA_CHUNK + g] = jnp.dot(
            ka_ref[pl.ds(start, MOBA_BLOCK), hh * KAUG_WIDTH:(hh + 1) * KAUG_WIDTH],
            queries[hh], preferred_element_type=F32)

    def weights(state, bias, next_first, next_queries):
        m_new, scale, l_new, p = [], [], [], {}

        def tile(hh, g):
            s = s_ref[hh * MOBA_CHUNK + g]
            return s if bias[g] is None else s + tab_ref[hh, bias[g]]

        for hh in heads:
            m, l, _ = state[hh]
            top = m
            for g in range(MOBA_CHUNK):
                top = jnp.maximum(top, jnp.max(tile(hh, g), axis=0, keepdims=True))
            sc = jnp.exp2(m - top)
            tot = sc * l
            for g in range(MOBA_CHUNK):
                e = jnp.exp2(tile(hh, g) - top)
                logits(hh, g, next_first, next_queries)
                tot = tot + jnp.sum(e, axis=0, keepdims=True)
                p[hh, g] = e.astype(BF16)
            m_new.append(top)
            scale.append(sc)
            l_new.append(tot)
        return m_new, scale, l_new, p

    def accumulate(first_block, state, stats):
        m_new, scale, l_new, p = stats
        out = []
        for hh in heads:
            acc = scale[hh] * state[hh][2]
            for g in range(MOBA_CHUNK):
                acc = acc + jnp.dot(vt_ref[first_block + g, hh * HEAD_DIM:(hh + 1) * HEAD_DIM, :],
                                    p[hh, g], preferred_element_type=F32)
            out.append((m_new[hh], l_new[hh], acc))
        return tuple(out)

    init = tuple((jnp.full((1, MOBA_BLOCK), 0.1 * MASKED, F32), jnp.zeros((1, MOBA_BLOCK), F32),
                  jnp.zeros((HEAD_DIM, MOBA_BLOCK), F32)) for _ in heads)
    first = qb_idx % MOBA_CHUNK + 1

    @pl.when(qb_idx == 0)
    def _():
        for hh, g in tiles:
            logits(hh, g, first, q_aug)

    no_bias = [None] * MOBA_CHUNK

    def trip(c, state):
        stats = weights(state, no_bias, first + (c + 1) * MOBA_CHUNK, q_aug)
        return accumulate(first + c * MOBA_CHUNK, state, stats)

    state = lax.fori_loop(0, qb_idx // MOBA_CHUNK, trip, init)
    bias = [None] * (MOBA_CHUNK - 2) + [0, 1]
    block_means(range(0, n_blocks // 2))
    stats = weights(state, bias, (qb_idx + 1) % MOBA_CHUNK + 1, q_aug_next)
    block_means(range(n_blocks // 2, n_blocks))
    state = accumulate(qb_idx + 1, state, stats)
    for hh in heads:
        _, l, acc = state[hh]
        o_ref[:, hh * HEAD_DIM:(hh + 1) * HEAD_DIM] = (acc / l).T.astype(o_ref.dtype)


def _moba_prompt(q_aug_t, k_aug, v_t, tables, cache_k, page_table_flat):
    t = q_aug_t.shape[1]
    nb = t // MOBA_BLOCK
    hb = MOBA_HEADS_PER_STEP
    n_steps = (N_HEADS // hb) * nb
    page = cache_k.shape[2]
    per_block = MOBA_BLOCK // page
    total_pages = page_table_flat.shape[0]
    pages_per_step = total_pages // n_steps
    assert t % MOBA_BLOCK == 0 and MOBA_CHUNK >= 2 and MOBA_BLOCK % page == 0
    assert total_pages % n_steps == 0 and pages_per_step % per_block == 0
    blocks_per_step = pages_per_step // per_block
    attn, k_mean = pl.pallas_call(
        functools.partial(_moba_prompt_body, pages_per_step=pages_per_step, per_block=per_block),
        grid_spec=pltpu.PrefetchScalarGridSpec(
            num_scalar_prefetch=1,
            grid=(N_HEADS // hb, nb),
            in_specs=[
                pl.BlockSpec((hb * KAUG_WIDTH, MOBA_BLOCK), lambda h, i, pt: (h, i)),
                pl.BlockSpec((hb * KAUG_WIDTH, MOBA_BLOCK),
                             lambda h, i, pt: (h, jnp.minimum(i + 1, nb - 1))),
                pl.BlockSpec((t + MOBA_PAD, hb * KAUG_WIDTH), lambda h, i, pt: (0, h)),
                pl.BlockSpec((nb + MOBA_CHUNK, hb * HEAD_DIM, MOBA_BLOCK),
                             lambda h, i, pt: (0, h, 0)),
                pl.BlockSpec((hb, 2, MOBA_BLOCK, MOBA_BLOCK),
                             lambda h, i, pt: (h, TAB_PREV_T // 2, 0, 0)),
                pl.BlockSpec(memory_space=pl.ANY),
            ],
            out_specs=[
                pl.BlockSpec((MOBA_BLOCK, hb * HEAD_DIM), lambda h, i, pt: (i, h)),
                pl.BlockSpec((1, blocks_per_step, N_HEADS, HEAD_DIM),
                             lambda h, i, pt: (h * nb + i, 0, 0, 0)),
            ],
            scratch_shapes=[
                pltpu.VMEM((hb * MOBA_CHUNK, MOBA_BLOCK, MOBA_BLOCK), F32),
                pltpu.VMEM((2, pages_per_step, page, N_HEADS, HEAD_DIM), F32),
                pltpu.SemaphoreType.DMA((2,)),
            ],
        ),
        out_shape=[jax.ShapeDtypeStruct((t, ATTN_WIDTH), BF16),
                   jax.ShapeDtypeStruct((n_steps, blocks_per_step, N_HEADS, HEAD_DIM), F32)],
        compiler_params=_params(("arbitrary", "arbitrary"), 58),
        name="moba_prompt",
    )(page_table_flat, q_aug_t, q_aug_t, k_aug, v_t, tables, cache_k)
    return attn, k_mean.reshape(n_steps * blocks_per_step, N_HEADS, HEAD_DIM)


def _sample_select_body(q_ref, km_ref, o_ref):
    km = km_ref[0]
    gate = jnp.sum(km * q_ref[0], axis=-1, keepdims=True)
    gate = jnp.broadcast_to(gate, km.shape)
    blk_id = lax.broadcasted_iota(jnp.int32, km.shape, 0).astype(F32)
    for slot in range(MOBA_TOPK):
        top = jnp.max(gate, axis=0, keepdims=True)
        first = jnp.min(jnp.where(gate == top, blk_id, float(km.shape[0])),
                        axis=0, keepdims=True)
        o_ref[0, slot] = first[0].astype(jnp.int32)
        gate = jnp.where(blk_id == first, -jnp.inf, gate)


def _sample_select(q_heads, k_mean):
    n_req, nbp = k_mean.shape[0], k_mean.shape[1]
    assert nbp >= MOBA_TOPK
    out = pl.pallas_call(
        _sample_select_body,
        grid=(n_req,),
        in_specs=[pl.BlockSpec((1, N_HEADS, HEAD_DIM), lambda b: (b, 0, 0)),
                  pl.BlockSpec((1, nbp, N_HEADS, HEAD_DIM), lambda b: (b, 0, 0, 0))],
        out_specs=pl.BlockSpec((1, MOBA_TOPK, N_HEADS, HEAD_DIM), lambda b: (b, 0, 0, 0)),
        out_shape=jax.ShapeDtypeStruct((n_req, MOBA_TOPK, N_HEADS, HEAD_DIM), jnp.int32),
        compiler_params=_params(("parallel",), 32),
        name="sample_select",
    )(q_heads, k_mean)
    return out[:, :, :, 0]


def _sample_attn_body(pt_ref, sel_ref, q_ref, kn_ref, vn_ref, tprev_ref, town_ref, ck_ref, cv_ref,
                      o_ref, kbuf, vbuf, sem, *, n_pages, per_block):
    b = pl.program_id(0)
    page = kbuf.shape[2]
    last_block = n_pages // per_block - 1

    def copies(req, buf_slot):
        out = []
        for h in range(N_HEADS):
            for slot in range(MOBA_TOPK):
                blk = sel_ref[(req * MOBA_TOPK + slot) * N_HEADS + h]
                for i in range(per_block):
                    pg = pt_ref[req * n_pages + blk * per_block + i]
                    idx = (h * MOBA_TOPK + slot) * per_block + i
                    out.append(pltpu.make_async_copy(ck_ref.at[0, pg, :, h, :],
                                                     kbuf.at[buf_slot, idx], sem.at[buf_slot, 0]))
                    out.append(pltpu.make_async_copy(cv_ref.at[0, pg, :, h, :],
                                                     vbuf.at[buf_slot, idx], sem.at[buf_slot, 1]))
        return out

    @pl.when(b == 0)
    def _():
        for cp in copies(b, 0):
            cp.start()

    @pl.when(b + 1 < pl.num_programs(0))
    def _():
        for cp in copies(b + 1, (b + 1) % 2):
            cp.start()

    cur = b % 2
    for cp in copies(b, cur):
        cp.wait()

    hs = [slice(h * HEAD_DIM, (h + 1) * HEAD_DIM) for h in range(N_HEADS)]
    n_tiles = MOBA_TOPK * per_block
    q = [q_ref[0, :, hs[h]] * (HEAD_DIM ** -0.5) for h in range(N_HEADS)]
    logits = {}
    for h in range(N_HEADS):
        q8 = jnp.broadcast_to(q[h], (8, HEAD_DIM)).astype(BF16)
        prev_row = tprev_ref[h, 0, 0:1, :]
        for slot in range(MOBA_TOPK):
            near = (sel_ref[(b * MOBA_TOPK + slot) * N_HEADS + h] == last_block).astype(F32)
            for i in range(per_block):
                idx = slot * per_block + i
                kt = kbuf[cur, h * n_tiles + idx].astype(BF16)
                lg = lax.dot_general(q8, kt, NT_DIMS, preferred_element_type=F32)[0:1, :]
                logits[h, idx] = lg + near * prev_row[:, i * page:(i + 1) * page]
    p_own, den, weights = {}, {}, {}
    for h in range(N_HEADS):
        l_own = (jnp.sum(q[h] * kn_ref[0, :, hs[h]], axis=1, keepdims=True)
                 + town_ref[h, 0, 0:1, 0:1])
        m = l_own
        for idx in range(n_tiles):
            m = jnp.maximum(m, jnp.max(logits[h, idx], axis=1, keepdims=True))
        p_own[h] = jnp.exp(l_own - m)
        tot = p_own[h]
        for idx in range(n_tiles):
            p = jnp.exp(logits[h, idx] - m)
            tot = tot + jnp.sum(p, axis=1, keepdims=True)
            weights[h, idx] = jnp.broadcast_to(p, (8, page)).astype(BF16)
        den[h] = tot
    for h in range(N_HEADS):
        acc = p_own[h] * vn_ref[0, :, hs[h]]
        for idx in range(n_tiles):
            acc = acc + jnp.dot(weights[h, idx], vbuf[cur, h * n_tiles + idx].astype(BF16),
                                preferred_element_type=F32)[0:1, :]
        o_ref[0, :, hs[h]] = (acc / den[h]).astype(o_ref.dtype)


def _sample_attn(q3, kn3, vn3, cache_k, cache_v, page_table_flat, sel_flat, tables, n_pages):
    n_req = q3.shape[0]
    page = cache_k.shape[2]
    per_block = MOBA_BLOCK // page
    n_bufs = N_HEADS * MOBA_TOPK * per_block
    row_spec = pl.BlockSpec((1, 1, ATTN_WIDTH), lambda b, pt, sel: (b, 0, 0))
    return pl.pallas_call(
        functools.partial(_sample_attn_body, n_pages=n_pages, per_block=per_block),
        grid_spec=pltpu.PrefetchScalarGridSpec(
            num_scalar_prefetch=2,
            grid=(n_req,),
            in_specs=[row_spec, row_spec, row_spec,
                      pl.BlockSpec((N_HEADS, 1, 8, MOBA_BLOCK),
                                   lambda b, pt, sel: (0, TAB_PREV, 0, 0)),
                      pl.BlockSpec((N_HEADS, 1, 8, MOBA_BLOCK),
                                   lambda b, pt, sel: (0, TAB_OWN, 0, 0)),
                      pl.BlockSpec(memory_space=pl.ANY),
                      pl.BlockSpec(memory_space=pl.ANY)],
            out_specs=row_spec,
            scratch_shapes=[pltpu.VMEM((2, n_bufs, page, HEAD_DIM), F32),
                            pltpu.VMEM((2, n_bufs, page, HEAD_DIM), F32),
                            pltpu.SemaphoreType.DMA((2, 2))],
        ),
        out_shape=jax.ShapeDtypeStruct((n_req, 1, ATTN_WIDTH), BF16),
        compiler_params=_params(("arbitrary",), 40),
        name="sample_attn",
    )(page_table_flat, sel_flat, q3, kn3, vn3, tables, tables, cache_k, cache_v)


def _cross_prompt_body(q_ref, mk_ref, mv_ref, o_ref):
    for h in range(CROSS_HEADS):
        sl = slice(h * CROSS_HEAD_DIM, (h + 1) * CROSS_HEAD_DIM)
        s = lax.dot_general(q_ref[:, sl], mk_ref[:, sl], NT_DIMS,
                            preferred_element_type=F32) * (CROSS_HEAD_DIM ** -0.5)
        p = jnp.exp(s - jnp.max(s, axis=1, keepdims=True))
        den = jnp.sum(p, axis=1, keepdims=True)
        o = jnp.dot(p.astype(BF16), mv_ref[:, sl], preferred_element_type=F32) / den
        o_ref[:, sl] = o.astype(o_ref.dtype)


def _cross_prompt(qc, mk_bf, mv_bf, bm=512):
    t = qc.shape[0]
    n_mem = mk_bf.shape[0]
    return pl.pallas_call(
        _cross_prompt_body,
        grid=(t // bm,),
        in_specs=[pl.BlockSpec((bm, CROSS_WIDTH), lambda i: (i, 0)),
                  pl.BlockSpec((n_mem, CROSS_WIDTH), lambda i: (0, 0)),
                  pl.BlockSpec((n_mem, CROSS_WIDTH), lambda i: (0, 0))],
        out_specs=pl.BlockSpec((bm, CROSS_WIDTH), lambda i: (i, 0)),
        out_shape=jax.ShapeDtypeStruct((t, CROSS_WIDTH), BF16),
        compiler_params=_params(("parallel",), 32),
        name="cross_prompt",
    )(qc, mk_bf, mv_bf)


def _cross_sample_body(q_ref, mk_hbm, mv_hbm, o_ref, kbuf, vbuf, sem):
    b = pl.program_id(0)

    def copies(req, buf_slot):
        out = []
        for h in range(CROSS_HEADS):
            out.append(pltpu.make_async_copy(mk_hbm.at[0, req, :, h, :], kbuf.at[buf_slot, h],
                                             sem.at[buf_slot, 0]))
            out.append(pltpu.make_async_copy(mv_hbm.at[0, req, :, h, :], vbuf.at[buf_slot, h],
                                             sem.at[buf_slot, 1]))
        return out

    @pl.when(b == 0)
    def _():
        for cp in copies(b, 0):
            cp.start()

    @pl.when(b + 1 < pl.num_programs(0))
    def _():
        for cp in copies(b + 1, (b + 1) % 2):
            cp.start()

    cur = b % 2
    for cp in copies(b, cur):
        cp.wait()

    for h in range(CROSS_HEADS):
        sl = slice(h * CROSS_HEAD_DIM, (h + 1) * CROSS_HEAD_DIM)
        q8 = jnp.broadcast_to(q_ref[0, :, sl], (8, CROSS_HEAD_DIM))
        s = lax.dot_general(q8, kbuf[cur, h].astype(BF16), NT_DIMS,
                            preferred_element_type=F32) * (CROSS_HEAD_DIM ** -0.5)
        p = jnp.exp(s - jnp.max(s, axis=1, keepdims=True))
        den = jnp.sum(p, axis=1, keepdims=True)
        o = jnp.dot(p.astype(BF16), vbuf[cur, h].astype(BF16),
                    preferred_element_type=F32) / den
        o_ref[0, :, sl] = o[0:1, :].astype(o_ref.dtype)


def _cross_sample(qc3, mem_k, mem_v):
    _, n_req, n_mem, _, _ = mem_k.shape
    row_spec = pl.BlockSpec((1, 1, CROSS_WIDTH), lambda b: (b, 0, 0))
    any_spec = pl.BlockSpec(memory_space=pl.ANY)
    return pl.pallas_call(
        _cross_sample_body,
        grid=(n_req,),
        in_specs=[row_spec, any_spec, any_spec],
        out_specs=row_spec,
        out_shape=jax.ShapeDtypeStruct((n_req, 1, CROSS_WIDTH), BF16),
        scratch_shapes=[pltpu.VMEM((2, CROSS_HEADS, n_mem, CROSS_HEAD_DIM), F32),
                        pltpu.VMEM((2, CROSS_HEADS, n_mem, CROSS_HEAD_DIM), F32),
                        pltpu.SemaphoreType.DMA((2, 2))],
        compiler_params=_params(("arbitrary",), 32),
        name="cross_sample",
    )(qc3, mem_k, mem_v)


CONV_TILE = 256
CONV_HALO = 32
CONV_UNROLL = 4


def _conv_prompt_body(cur_ref, prev_ref, w_ref, cb_ref, g_ref, b_ref, o_ref, buf_ref, sh_ref,
                      wt_ref, y_ref):
    i = pl.program_id(0)
    has_prev = (i > 0).astype(F32)
    buf_ref[0:CONV_HALO, :] = prev_ref[CONV_TILE - CONV_HALO:CONV_TILE, :] * has_prev
    buf_ref[CONV_HALO:CONV_HALO + CONV_TILE, :] = cur_ref[...]
    first = CONV_HALO - CONV_STATE
    shifted_rows = sh_ref.shape[1]
    for tap in range(CONV_WIDTH):
        wt_ref[tap] = jnp.broadcast_to(w_ref[tap:tap + 1, :], (SUBLANES, D_CONV))
    for c in range(D_CONV // LANES):
        cs = slice(c * LANES, (c + 1) * LANES)
        for r in range(1, SUBLANES):
            sh_ref[r - 1] = buf_ref[r:r + shifted_rows, cs]
        weights = [wt_ref[tap, :, cs] for tap in range(CONV_WIDTH)]
        bias = jnp.broadcast_to(cb_ref[:, cs], (SUBLANES, LANES))

        def row_tiles(i, carry, cs=cs, weights=weights, bias=bias):
            row0 = pl.multiple_of(i * (CONV_UNROLL * SUBLANES), CONV_UNROLL * SUBLANES)
            acc = [bias] * CONV_UNROLL
            for tap in range(CONV_WIDTH):
                r = (first + tap) % SUBLANES
                for j in range(CONV_UNROLL):
                    rows = pl.ds(row0 + (j * SUBLANES + first + tap - r), SUBLANES)
                    tile = buf_ref[rows, cs] if r == 0 else sh_ref[r - 1, rows, :]
                    acc[j] = acc[j] + tile * weights[tap]
            for j in range(CONV_UNROLL):
                y_ref[pl.ds(row0 + j * SUBLANES, SUBLANES), cs] = acc[j]
            return carry

        lax.fori_loop(0, CONV_TILE // (CONV_UNROLL * SUBLANES), row_tiles, 0)
    y = _layer_norm(y_ref[...], g_ref[...], b_ref[...])
    o_ref[...] = jax.nn.silu(y).astype(o_ref.dtype)


def _conv_prompt(u, conv_w, conv_b, ln_g, ln_b):
    t = u.shape[0]
    vec = pl.BlockSpec((1, D_CONV), lambda i: (0, 0))
    return pl.pallas_call(
        _conv_prompt_body,
        grid=(t // CONV_TILE,),
        in_specs=[pl.BlockSpec((CONV_TILE, D_CONV), lambda i: (i, 0)),
                  pl.BlockSpec((CONV_TILE, D_CONV), lambda i: (jnp.maximum(i - 1, 0), 0)),
                  pl.BlockSpec((CONV_WIDTH, D_CONV), lambda i: (0, 0)),
                  vec, vec, vec],
        out_specs=pl.BlockSpec((CONV_TILE, D_CONV), lambda i: (i, 0)),
        out_shape=jax.ShapeDtypeStruct((t, D_CONV), BF16),
        scratch_shapes=[pltpu.VMEM((CONV_HALO + CONV_TILE, D_CONV), F32),
                        pltpu.VMEM((SUBLANES - 1, CONV_HALO + CONV_TILE - SUBLANES, LANES), F32),
                        pltpu.VMEM((CONV_WIDTH, SUBLANES, D_CONV), F32),
                        pltpu.VMEM((CONV_TILE, D_CONV), F32)],
        compiler_params=_params(("parallel",), 32),
        name="conv_prompt",
    )(u, u, conv_w, conv_b.reshape(1, D_CONV), ln_g.reshape(1, D_CONV), ln_b.reshape(1, D_CONV))


def _conv_sample_body(st_ref, u_ref, w_ref, cb_ref, g_ref, b_ref, o_ref):
    acc = u_ref[...] * w_ref[CONV_STATE:CONV_WIDTH, :]
    for tap in range(CONV_STATE):
        acc = acc + st_ref[tap] * w_ref[tap:tap + 1, :]
    y = _layer_norm(acc + cb_ref[...], g_ref[...], b_ref[...])
    o_ref[...] = jax.nn.silu(y).astype(o_ref.dtype)


def _conv_sample(state_t, u, conv_w, conv_b, ln_g, ln_b):
    n_req = u.shape[0]
    return pl.pallas_call(
        _conv_sample_body,
        out_shape=jax.ShapeDtypeStruct((n_req, D_CONV), BF16),
        name="conv_sample",
    )(state_t, u, conv_w, conv_b.reshape(1, D_CONV), ln_g.reshape(1, D_CONV),
      ln_b.reshape(1, D_CONV))


def kernel(x_prompt, x_sample, mem_prompt, cache_k, cache_v, state_conv, cache_mem_k, cache_mem_v,
           page_table, w_in, conv_w, conv_b, conv_ln_g, conv_ln_b, w_mem_kv, rel_bias, w_branch,
           w_out, ln1_g, ln1_b, w_gate, w_up, w_down, ln2_g, ln2_b):
    depth = w_in.shape[0]
    assert depth == 1 and x_prompt.shape[0] == 1 and x_sample.shape[1] == 1
    alpha = (2 * depth) ** 0.25
    t, d_model = x_prompt.shape[1], x_prompt.shape[2]
    n_req, n_pages = page_table.shape
    page = cache_k.shape[2]
    n_mem = mem_prompt.shape[1]
    assert (n_pages * page) % MOBA_BLOCK == 0

    sizes = (D_CONV, D_CONV, ATTN_WIDTH, ATTN_WIDTH, ATTN_WIDTH, CROSS_WIDTH, N_BRANCH * d_model)
    col_a, col_b, col_q, col_k, col_v, col_qc, col_g = (
        int(c) for c in np.concatenate([[0], np.cumsum(sizes)[:-1]]))
    w_o = w_out[0].astype(BF16)
    w_d = w_down[0].astype(BF16)
    w_br = w_branch[0].astype(BF16)
    d_ff = w_gate.shape[2]

    tables = _bias_tables(rel_bias)

    xp = x_prompt[0]
    xs = x_sample[:, 0]
    q_p, q_s, xp_bf, xs_bf = _proj(xp, xs, [(w_in, col_q)], ATTN_WIDTH, "none", [F32],
                                   bn=ATTN_WIDTH, vmem_mib=56, keep_input=True)
    u_p, u_s = _proj(xp_bf, xs_bf, [(w_in, col_a), (w_in, col_b)], D_CONV, "glu", [F32])
    k_p, k_s = _proj(xp_bf, xs_bf, [(w_in, col_k)], ATTN_WIDTH, "none", [F32], bn=1024)
    v_p, v_s = _proj(xp_bf, xs_bf, [(w_in, col_v)], ATTN_WIDTH, "none", [F32], bn=1024)
    qc_p, qc_s = _proj(xp_bf, xs_bf, [(w_in, col_qc)], CROSS_WIDTH, "none", [BF16], bn=1024)
    gates_p, gates_s = _proj(xp_bf, xs_bf, [(w_in, col_g)], N_BRANCH * d_model, "sigmoid",
                             [BF16], bn=1024)

    conv_y_p = _conv_prompt(u_p, conv_w[0], conv_b[0], conv_ln_g[0], conv_ln_b[0])
    k_aug, v_pad = _moba_prep(k_p, v_p)
    pt_flat = page_table.reshape(-1)
    attn_p, k_mean_s = _moba_prompt(_moba_select(q_p, _block_mean(k_p)), k_aug, v_pad, tables,
                                    cache_k, pt_flat)
    mem_bf = mem_prompt[0].astype(BF16)
    mk, mk_bf = _proj(mem_bf, None, [(w_mem_kv, 0)], CROSS_WIDTH, "none", [F32, BF16], bn=1024)
    mv, mv_bf = _proj(mem_bf, None, [(w_mem_kv, CROSS_WIDTH)], CROSS_WIDTH, "none", [F32, BF16],
                      bn=1024)
    cross_p = _cross_prompt(qc_p, mk_bf, mv_bf)

    conv_y_s = _conv_sample(state_conv[0].transpose(1, 0, 2), u_s, conv_w[0], conv_b[0],
                            conv_ln_g[0], conv_ln_b[0])
    sel = _sample_select(q_s.reshape(n_req, N_HEADS, HEAD_DIM),
                         k_mean_s.reshape(n_req, -1, N_HEADS, HEAD_DIM))
    attn_s = _sample_attn(q_s.reshape(n_req, 1, ATTN_WIDTH), k_s.reshape(n_req, 1, ATTN_WIDTH),
                          v_s.reshape(n_req, 1, ATTN_WIDTH), cache_k, cache_v, pt_flat,
                          sel.reshape(-1), tables, n_pages)
    cross_s = _cross_sample(qc_s.reshape(n_req, 1, CROSS_WIDTH), cache_mem_k, cache_mem_v)

    mix_p, mix_s = _branch_mix(
        (conv_y_p, attn_p, cross_p),
        (conv_y_s, attn_s.reshape(n_req, ATTN_WIDTH), cross_s.reshape(n_req, CROSS_WIDTH)),
        w_br, gates_p, gates_s)
    h1_p, h1_p_bf, h1_s, h1_s_bf = _mm_res_ln(mix_p, mix_s, w_o, xp, xs, ln1_g[0], ln1_b[0], alpha,
                                              [F32, BF16], bm=512)
    act_p, act_s = _proj(h1_p_bf, h1_s_bf, [(w_gate, 0), (w_up, 0)], d_ff, "swiglu", [BF16])
    y_p, y_s = _mm_res_ln(act_p, act_s, w_d, h1_p, h1_s, ln2_g[0], ln2_b[0], alpha, [F32],
                          bm=512)

    new_conv_sample = jnp.concatenate([state_conv[0][:, 1:], u_s[:, None, :]], axis=1)
    return (
        y_p[None],
        y_s[:, None],
        k_p.reshape(1, 1, t, N_HEADS, HEAD_DIM),
        v_p.reshape(1, 1, t, N_HEADS, HEAD_DIM),
        u_p[t - CONV_STATE:][None, None],
        mk.reshape(1, 1, n_mem, CROSS_HEADS, CROSS_HEAD_DIM),
        mv.reshape(1, 1, n_mem, CROSS_HEADS, CROSS_HEAD_DIM),
        k_s.reshape(1, n_req, 1, N_HEADS, HEAD_DIM),
        v_s.reshape(1, n_req, 1, N_HEADS, HEAD_DIM),
        new_conv_sample[None],
    )
```

```python
import functools
import math

import numpy as np
import jax
import jax.numpy as jnp
from jax import lax
from jax.experimental import pallas as pl
from jax.experimental.pallas import tpu as pltpu

F32 = jnp.float32
BF16 = jnp.bfloat16

N_HEADS = 8
HEAD_DIM = 128
ATTN_WIDTH = N_HEADS * HEAD_DIM
MOBA_BLOCK = 256
MOBA_TOPK = 3
D_CONV = 1024
CONV_WIDTH = 31
CONV_STATE = CONV_WIDTH - 1
CROSS_HEADS = 4
CROSS_HEAD_DIM = 256
CROSS_WIDTH = CROSS_HEADS * CROSS_HEAD_DIM
N_BRANCH = 3
BRANCH_WIDTH = 1024
NUM_BUCKETS = 32
MAX_EXACT = NUM_BUCKETS // 2
MAX_DISTANCE = 128
LN_EPS = 1e-5
MASKED = -1e30
LANES = 128
SUBLANES = 8
MIB = 2 ** 20
LOG2_E = math.log2(math.e)
LN_CHUNK_ROWS = 256

NT_DIMS = (((1,), (1,)), ((), ()))


def _params(semantics, vmem_mib):
    return pltpu.CompilerParams(dimension_semantics=semantics,
                                vmem_limit_bytes=vmem_mib * MIB)


def _layer_norm(y, g, b):
    mu = jnp.mean(y, axis=-1, keepdims=True)
    yc = y - mu
    var = jnp.mean(yc * yc, axis=-1, keepdims=True)
    return yc * lax.rsqrt(var + LN_EPS) * g + b


def _proj_body(*refs, n_w, n_out, epilogue, tail, keep_input):
    x_ref = refs[0]
    refs = refs[1:]
    if tail:
        xs_ref = refs[0]
        refs = refs[1:]
    w_refs = refs[:n_w]
    o_refs = refs[n_w:n_w + n_out]
    refs = refs[n_w + n_out:]
    if tail:
        os_refs = refs[:n_out]
        refs = refs[n_out:]
    if keep_input:
        x_copy_ref = refs[0]
        refs = refs[1:]
        if tail:
            xs_copy_ref = refs[0]
            refs = refs[1:]
    wbf_refs = refs
    i = pl.program_id(1)

    @pl.when(i == 0)
    def _():
        for w, wbf in zip(w_refs, wbf_refs):
            wbf[...] = w[...].astype(BF16)

    def apply(x, outs):
        z = [jnp.dot(x, w[...], preferred_element_type=F32) for w in wbf_refs]
        if epilogue == "each":
            for o, zn in zip(outs, z):
                o[...] = zn.astype(o.dtype)
            return
        if epilogue == "glu":
            y = z[0] * jax.nn.sigmoid(z[1])
        elif epilogue == "swiglu":
            y = jax.nn.silu(z[0]) * z[1]
        elif epilogue == "sigmoid":
            y = jax.nn.sigmoid(z[0])
        else:
            y = z[0]
        for o in outs:
            o[...] = y.astype(o.dtype)

    x = x_ref[...].astype(BF16)
    if keep_input:
        x_copy_ref[...] = x
    apply(x, o_refs)

    if tail:
        @pl.when(i == pl.num_programs(1) - 1)
        def _():
            xs = xs_ref[...].astype(BF16)
            if keep_input:
                xs_copy_ref[...] = xs
            apply(xs, os_refs)


def _proj(x, xs, ws, n, epilogue, out_dtypes, bm=1024, bn=512, vmem_mib=48, keep_input=False):
    m, k = x.shape
    tail = xs is not None
    bm = min(bm, m)
    bn = min(bn, n)
    while n % bn or any(c0 % bn for _, c0 in ws):
        bn //= 2
    assert m % bm == 0 and bn % LANES == 0
    assert not keep_input or bn == n
    in_specs = [pl.BlockSpec((bm, k), lambda j, i: (i, 0))]
    out_specs = [pl.BlockSpec((bm, bn), lambda j, i: (i, j)) for _ in out_dtypes]
    out_shape = [jax.ShapeDtypeStruct((m, n), dt) for dt in out_dtypes]
    operands = [x]
    if tail:
        ms = xs.shape[0]
        in_specs.append(pl.BlockSpec((ms, k), lambda j, i: (0, 0)))
        out_specs += [pl.BlockSpec((ms, bn), lambda j, i: (0, j)) for _ in out_dtypes]
        out_shape += [jax.ShapeDtypeStruct((ms, n), dt) for dt in out_dtypes]
        operands.append(xs)
    if keep_input:
        out_specs.append(pl.BlockSpec((bm, k), lambda j, i: (i, 0)))
        out_shape.append(jax.ShapeDtypeStruct((m, k), BF16))
        if tail:
            out_specs.append(pl.BlockSpec((ms, k), lambda j, i: (0, 0)))
            out_shape.append(jax.ShapeDtypeStruct((ms, k), BF16))
    in_specs += [pl.BlockSpec((None, k, bn), lambda j, i, c=c0 // bn: (0, 0, c + j))
                 for _, c0 in ws]
    return pl.pallas_call(
        functools.partial(_proj_body, n_w=len(ws), n_out=len(out_dtypes), epilogue=epilogue,
                          tail=tail, keep_input=keep_input),
        grid=(n // bn, m // bm),
        in_specs=in_specs,
        out_specs=out_specs,
        out_shape=out_shape,
        scratch_shapes=[pltpu.VMEM((k, bn), BF16) for _ in ws],
        compiler_params=_params(("arbitrary" if keep_input else "parallel", "arbitrary"), vmem_mib),
        name="proj_" + epilogue,
    )(*operands, *[w for w, _ in ws])


def _mm_res_ln_body(a_ref, as_ref, w_in_ref, r_ref, rs_ref, g_ref, b_ref, *o_refs, alpha, n_out,
                    cast_weight):
    if cast_weight:
        w_ref = o_refs[2 * n_out]

        @pl.when(pl.program_id(0) == 0)
        def _():
            w_ref[...] = w_in_ref[...].astype(BF16)
    else:
        w_ref = w_in_ref

    def apply(a, res, outs):
        rows = a.shape[0]
        chunk = min(rows, LN_CHUNK_ROWS)
        for r0 in range(0, rows, chunk):
            rs = slice(r0, r0 + chunk)
            y = alpha * res[rs, :] + jnp.dot(a[rs, :], w_ref[...], preferred_element_type=F32)
            out = _layer_norm(y, g_ref[...], b_ref[...])
            for o in outs:
                o[rs, :] = out.astype(o.dtype)

    apply(a_ref, r_ref, o_refs[:n_out])

    @pl.when(pl.program_id(0) == pl.num_programs(0) - 1)
    def _():
        apply(as_ref, rs_ref, o_refs[n_out:2 * n_out])


def _mm_res_ln(a, a_s, w, res, res_s, g, b, alpha, out_dtypes, bm, vmem_mib=58):
    m, k = a.shape
    ms = a_s.shape[0]
    n = w.shape[1]
    bm = min(bm, m)
    assert m % bm == 0
    cast_weight = w.dtype != BF16
    return pl.pallas_call(
        functools.partial(_mm_res_ln_body, alpha=alpha, n_out=len(out_dtypes),
                          cast_weight=cast_weight),
        scratch_shapes=[pltpu.VMEM((k, n), BF16)] if cast_weight else [],
        grid=(m // bm,),
        in_specs=[
            pl.BlockSpec((bm, k), lambda i: (i, 0)),
            pl.BlockSpec((ms, k), lambda i: (0, 0)),
            pl.BlockSpec((k, n), lambda i: (0, 0), pipeline_mode=pl.Buffered(1)),
            pl.BlockSpec((bm, n), lambda i: (i, 0)),
            pl.BlockSpec((ms, n), lambda i: (0, 0)),
            pl.BlockSpec((1, n), lambda i: (0, 0)),
            pl.BlockSpec((1, n), lambda i: (0, 0)),
        ],
        out_specs=[pl.BlockSpec((bm, n), lambda i: (i, 0)) for _ in out_dtypes]
        + [pl.BlockSpec((ms, n), lambda i: (0, 0)) for _ in out_dtypes],
        out_shape=[jax.ShapeDtypeStruct((m, n), dt) for dt in out_dtypes]
        + [jax.ShapeDtypeStruct((ms, n), dt) for dt in out_dtypes],
        compiler_params=_params(("arbitrary",), vmem_mib),
        name="mm_res_ln",
    )(a, a_s, w, res, res_s, g.reshape(1, n), b.reshape(1, n))


def _branch_body(*refs):
    nb = N_BRANCH
    br_refs, brs_refs, w_refs = refs[0:nb], refs[nb:2 * nb], refs[2 * nb:3 * nb]
    g_refs, gs_refs = refs[3 * nb:4 * nb], refs[4 * nb:5 * nb]
    o_ref, os_ref = refs[5 * nb], refs[5 * nb + 1]

    def mix(branches, gates, out):
        y = None
        for br, w, g in zip(branches, w_refs, gates):
            term = g[...] * jnp.dot(br[...], w[...], preferred_element_type=F32)
            y = term if y is None else y + term
        out[...] = y.astype(out.dtype)

    mix(br_refs, g_refs, o_ref)

    @pl.when(pl.program_id(1) == pl.num_programs(1) - 1)
    def _():
        mix(brs_refs, gs_refs, os_ref)


def _branch_mix(branches, branches_s, w_branch, gates, gates_s, bm=1024, bn=1024):
    m = branches[0].shape[0]
    ms = branches_s[0].shape[0]
    d = w_branch.shape[2]
    bm = min(bm, m)
    bn = min(bn, d)
    nj = d // bn
    ids = range(N_BRANCH)
    return pl.pallas_call(
        _branch_body,
        grid=(nj, m // bm),
        in_specs=[pl.BlockSpec((bm, BRANCH_WIDTH), lambda j, i: (i, 0)) for _ in ids]
        + [pl.BlockSpec((ms, BRANCH_WIDTH), lambda j, i: (0, 0)) for _ in ids]
        + [pl.BlockSpec((None, BRANCH_WIDTH, bn), lambda j, i, n=n: (n, 0, j)) for n in ids]
        + [pl.BlockSpec((bm, bn), lambda j, i, n=n: (i, n * nj + j)) for n in ids]
        + [pl.BlockSpec((ms, bn), lambda j, i, n=n: (0, n * nj + j)) for n in ids],
        out_specs=[pl.BlockSpec((bm, bn), lambda j, i: (i, j)),
                   pl.BlockSpec((ms, bn), lambda j, i: (0, j))],
        out_shape=[jax.ShapeDtypeStruct((m, d), BF16), jax.ShapeDtypeStruct((ms, d), BF16)],
        compiler_params=_params(("parallel", "arbitrary"), 48),
        name="branch_mix",
    )(*branches, *branches_s, w_branch, w_branch, w_branch, gates, gates, gates,
      gates_s, gates_s, gates_s)


TAB_PREV = 0
TAB_OWN = 1
TAB_PREV_T = 2
TAB_OWN_T = 3


def _bias_tables_body(rb_ref, o_ref):
    h = pl.program_id(0)
    shape = (MOBA_BLOCK, MOBA_BLOCK)
    far = rb_ref[NUM_BUCKETS - 1, h]
    for slot, offset, q_axis in ((TAB_PREV, MOBA_BLOCK, 0), (TAB_OWN, 0, 0),
                                 (TAB_PREV_T, MOBA_BLOCK, 1), (TAB_OWN_T, 0, 1)):
        qi = lax.broadcasted_iota(jnp.int32, shape, q_axis)
        kj = lax.broadcasted_iota(jnp.int32, shape, 1 - q_axis)
        dist = jnp.maximum(qi - kj + offset, 0)
        large = MAX_EXACT + (jnp.log(jnp.maximum(dist, 1).astype(F32) / MAX_EXACT)
                             / math.log(MAX_DISTANCE / MAX_EXACT)
                             * (NUM_BUCKETS - MAX_EXACT)).astype(jnp.int32)
        large = jnp.minimum(large, NUM_BUCKETS - 1)
        bucket = jnp.where(dist < MAX_EXACT, dist, large)
        val = jnp.zeros(shape, F32)
        for bkt in range(NUM_BUCKETS):
            val = jnp.where(bucket == bkt, rb_ref[bkt, h], val)
        val = val - far
        if slot in (TAB_PREV_T, TAB_OWN_T):
            val = val * LOG2_E
        if slot in (TAB_OWN, TAB_OWN_T):
            val = jnp.where(kj > qi, MASKED, val)
        o_ref[0, slot] = val


def _bias_tables(rel_bias):
    return pl.pallas_call(
        _bias_tables_body,
        grid=(N_HEADS,),
        in_specs=[pl.BlockSpec(memory_space=pltpu.SMEM)],
        out_specs=pl.BlockSpec((1, 4, MOBA_BLOCK, MOBA_BLOCK), lambda h: (h, 0, 0, 0)),
        out_shape=jax.ShapeDtypeStruct((N_HEADS, 4, MOBA_BLOCK, MOBA_BLOCK), F32),
        compiler_params=_params(("arbitrary",), 32),
        name="bias_tables",
    )(rel_bias)


def _block_mean_body(k_ref, o_ref):
    rows = k_ref.shape[0]
    x = k_ref[...].reshape(rows // MOBA_BLOCK, MOBA_BLOCK, k_ref.shape[1])
    o_ref[...] = jnp.sum(x, axis=1) * (1.0 / MOBA_BLOCK)


def _block_mean(k):
    t, w = k.shape
    nb = t // MOBA_BLOCK
    per = min(8, nb)
    assert nb % per == 0
    return pl.pallas_call(
        _block_mean_body,
        grid=(nb // per,),
        in_specs=[pl.BlockSpec((per * MOBA_BLOCK, w), lambda i: (i, 0))],
        out_specs=pl.BlockSpec((per, w), lambda i: (i, 0)),
        out_shape=jax.ShapeDtypeStruct((nb, w), F32),
        compiler_params=_params(("parallel",), 40),
        name="block_mean",
    )(k)


MOBA_CHUNK = 4
MOBA_HEADS_PER_STEP = 2
MOBA_PAD = MOBA_CHUNK * MOBA_BLOCK
KAUG_WIDTH = HEAD_DIM + LANES


def _moba_prep_body(k_ref, v_ref, ka_ref, vt_ref):
    i = pl.program_id(0)
    rows = k_ref.shape[0]
    col = lax.broadcasted_iota(jnp.int32, (rows, LANES), 1)

    @pl.when(i == 0)
    def _():
        pad_hot = jnp.where(col == LANES - 1, 1.0, 0.0).astype(BF16)
        for h in range(N_HEADS):
            ka_ref[:, h * KAUG_WIDTH:h * KAUG_WIDTH + HEAD_DIM] = jnp.zeros((rows, HEAD_DIM), BF16)
            ka_ref[:, h * KAUG_WIDTH + HEAD_DIM:(h + 1) * KAUG_WIDTH] = pad_hot
        vt_ref[...] = jnp.zeros(vt_ref.shape, BF16)

    @pl.when(i > 0)
    def _():
        row = lax.broadcasted_iota(jnp.int32, (rows, LANES), 0) + (i - 1) * rows
        hot = jnp.where(jnp.right_shift(row, int(math.log2(MOBA_BLOCK))) == col,
                        1.0, 0.0).astype(BF16)
        for h in range(N_HEADS):
            ka_ref[:, h * KAUG_WIDTH:h * KAUG_WIDTH + HEAD_DIM] = (
                k_ref[:, h * HEAD_DIM:(h + 1) * HEAD_DIM].astype(BF16))
            ka_ref[:, h * KAUG_WIDTH + HEAD_DIM:(h + 1) * KAUG_WIDTH] = hot
        for blk in range(rows // MOBA_BLOCK):
            vt_ref[blk] = v_ref[blk * MOBA_BLOCK:(blk + 1) * MOBA_BLOCK, :].T.astype(BF16)


def _moba_prep(k, v):
    t = k.shape[0]
    rows = MOBA_PAD
    assert t % rows == 0
    data = pl.BlockSpec((rows, ATTN_WIDTH), lambda i: (jnp.maximum(i - 1, 0), 0))
    return pl.pallas_call(
        _moba_prep_body,
        grid=(t // rows + 1,),
        in_specs=[data, data],
        out_specs=[pl.BlockSpec((rows, N_HEADS * KAUG_WIDTH), lambda i: (i, 0)),
                   pl.BlockSpec((MOBA_CHUNK, ATTN_WIDTH, MOBA_BLOCK), lambda i: (i, 0, 0))],
        out_shape=[jax.ShapeDtypeStruct((t + rows, N_HEADS * KAUG_WIDTH), BF16),
                   jax.ShapeDtypeStruct((t // MOBA_BLOCK + MOBA_CHUNK, ATTN_WIDTH, MOBA_BLOCK),
                                        BF16)],
        compiler_params=_params(("parallel",), 40),
        name="moba_prep",
    )(k, v)


SELECT_ROWS = 1024


def _moba_select_body(q_ref, km_ref, o_ref, *, n_sub):
    q = q_ref[...]
    rows = q.shape[0]
    gate = lax.dot_general(km_ref[...], q, NT_DIMS, precision=lax.Precision.HIGHEST,
                           preferred_element_type=F32)[:n_sub]
    blk = lax.broadcasted_iota(jnp.int32, gate.shape, 0)
    blk_f = blk.astype(F32)
    pos = lax.broadcasted_iota(jnp.int32, gate.shape, 1) + pl.program_id(0) * rows
    own = jnp.right_shift(pos, int(math.log2(MOBA_BLOCK)))
    gate = jnp.where(blk < own, gate, -jnp.inf)
    picked = blk == own
    for _ in range(MOBA_TOPK):
        top = jnp.max(gate, axis=0, keepdims=True)
        first = jnp.min(jnp.where(gate == top, blk_f, float(LANES)), axis=0, keepdims=True)
        chosen = blk_f == first
        picked = picked | (chosen & (top > -jnp.inf))
        gate = jnp.where(chosen, -jnp.inf, gate)
    o_ref[:HEAD_DIM, :] = (q * (HEAD_DIM ** -0.5 * LOG2_E)).T.astype(BF16)
    o_ref[HEAD_DIM:HEAD_DIM + n_sub, :] = jnp.where(picked, 0.0, MASKED).astype(BF16)
    o_ref[HEAD_DIM + n_sub:, :] = jnp.full((LANES - n_sub, rows), MASKED, BF16)


def _moba_select(q, k_mean):
    t = q.shape[0]
    nb = t // MOBA_BLOCK
    rows = min(SELECT_ROWS, t)
    n_sub = -(-nb // 16) * 16
    assert t % rows == 0 and MOBA_TOPK <= nb and n_sub < LANES
    km = jnp.pad(k_mean, ((0, LANES - nb), (0, 0)))
    return pl.pallas_call(
        functools.partial(_moba_select_body, n_sub=n_sub),
        grid=(t // rows, N_HEADS),
        in_specs=[pl.BlockSpec((rows, HEAD_DIM), lambda i, h: (i, h)),
                  pl.BlockSpec((LANES, HEAD_DIM), lambda i, h: (0, h))],
        out_specs=pl.BlockSpec((KAUG_WIDTH, rows), lambda i, h: (h, i)),
        out_shape=jax.ShapeDtypeStruct((N_HEADS * KAUG_WIDTH, t), BF16),
        compiler_params=_params(("parallel", "parallel"), 40),
        name="moba_select",
    )(q, km)


def _moba_prompt_body(pt_ref, qa_ref, qn_ref, ka_ref, vt_ref, tab_ref, ck_ref, o_ref, km_ref, s_ref,
                      page_buf, page_sem, *, pages_per_step, per_block):
    qb_idx = pl.program_id(1)
    step = pl.program_id(0) * pl.num_programs(1) + qb_idx
    n_steps = pl.num_programs(0) * pl.num_programs(1)

    def page_copies(at_step, slot):
        return [pltpu.make_async_copy(ck_ref.at[0, pt_ref[at_step * pages_per_step + j]],
                                      page_buf.at[slot, j], page_sem.at[slot])
                for j in range(pages_per_step)]

    @pl.when(step == 0)
    def _():
        for cp in page_copies(step, 0):
            cp.start()

    @pl.when(step + 1 < n_steps)
    def _():
        for cp in page_copies(step + 1, (step + 1) % 2):
            cp.start()

    cur = step % 2
    for cp in page_copies(step, cur):
        cp.wait()

    def block_means(blocks):
        for blk in blocks:
            tot = jnp.sum(page_buf[cur, blk * per_block], axis=0)
            for extra in range(1, per_block):
                tot = tot + jnp.sum(page_buf[cur, blk * per_block + extra], axis=0)
            km_ref[0, blk] = tot * (1.0 / MOBA_BLOCK)

    n_blocks = pages_per_step // per_block
    heads = range(MOBA_HEADS_PER_STEP)
    tiles = [(hh, g) for hh in heads for g in range(MOBA_CHUNK)]
    q_aug = [qa_ref[hh * KAUG_WIDTH:(hh + 1) * KAUG_WIDTH, :] for hh in heads]
    q_aug_next = [qn_ref[hh * KAUG_WIDTH:(hh + 1) * KAUG_WIDTH, :] for hh in heads]

    def logits(hh, g, first_block, queries):
        start = pl.multiple_of((first_block + g) * MOBA_BLOCK, MOBA_BLOCK)
        s_ref[hh * MOBA_CHUNK + g] = jnp.dot(
            ka_ref[pl.ds(start, MOBA_BLOCK), hh * KAUG_WIDTH:(hh + 1) * KAUG_WIDTH],
            queries[hh], preferred_element_type=F32)

    def weights(state, bias, next_first, next_queries):
        m_new, scale, l_new, p = [], [], [], {}

        def tile(hh, g):
            s = s_ref[hh * MOBA_CHUNK + g]
            return s if bias[g] is None else s + tab_ref[hh, bias[g]]

        for hh in heads:
            m, l, _ = state[hh]
            top = m
            for g in range(MOBA_CHUNK):
                top = jnp.maximum(top, jnp.max(tile(hh, g), axis=0, keepdims=True))
            sc = jnp.exp2(m - top)
            tot = sc * l
            for g in range(MOBA_CHUNK):
                e = jnp.exp2(tile(hh, g) - top)
                logits(hh, g, next_first, next_queries)
                tot = tot + jnp.sum(e, axis=0, keepdims=True)
                p[hh, g] = e.astype(BF16)
            m_new.append(top)
            scale.append(sc)
            l_new.append(tot)
        return m_new, scale, l_new, p

    def accumulate(first_block, state, stats):
        m_new, scale, l_new, p = stats
        out = []
        for hh in heads:
            acc = scale[hh] * state[hh][2]
            for g in range(MOBA_CHUNK):
                acc = acc + jnp.dot(vt_ref[first_block + g, hh * HEAD_DIM:(hh + 1) * HEAD_DIM, :],
                                    p[hh, g], preferred_element_type=F32)
            out.append((m_new[hh], l_new[hh], acc))
        return tuple(out)

    init = tuple((jnp.full((1, MOBA_BLOCK), 0.1 * MASKED, F32), jnp.zeros((1, MOBA_BLOCK), F32),
                  jnp.zeros((HEAD_DIM, MOBA_BLOCK), F32)) for _ in heads)
    first = qb_idx % MOBA_CHUNK + 1

    @pl.when(qb_idx == 0)
    def _():
        for hh, g in tiles:
            logits(hh, g, first, q_aug)

    no_bias = [None] * MOBA_CHUNK

    def trip(c, state):
        stats = weights(state, no_bias, first + (c + 1) * MOBA_CHUNK, q_aug)
        return accumulate(first + c * MOBA_CHUNK, state, stats)

    state = lax.fori_loop(0, qb_idx // MOBA_CHUNK, trip, init)
    bias = [None] * (MOBA_CHUNK - 2) + [0, 1]
    block_means(range(0, n_blocks // 2))
    stats = weights(state, bias, (qb_idx + 1) % MOBA_CHUNK + 1, q_aug_next)
    block_means(range(n_blocks // 2, n_blocks))
    state = accumulate(qb_idx + 1, state, stats)
    for hh in heads:
        _, l, acc = state[hh]
        o_ref[:, hh * HEAD_DIM:(hh + 1) * HEAD_DIM] = (acc / l).T.astype(o_ref.dtype)


def _moba_prompt(q_aug_t, k_aug, v_t, tables, cache_k, page_table_flat):
    t = q_aug_t.shape[1]
    nb = t // MOBA_BLOCK
    hb = MOBA_HEADS_PER_STEP
    n_steps = (N_HEADS // hb) * nb
    page = cache_k.shape[2]
    per_block = MOBA_BLOCK // page
    total_pages = page_table_flat.shape[0]
    pages_per_step = total_pages // n_steps
    assert t % MOBA_BLOCK == 0 and MOBA_CHUNK >= 2 and MOBA_BLOCK % page == 0
    assert total_pages % n_steps == 0 and pages_per_step % per_block == 0
    blocks_per_step = pages_per_step // per_block
    attn, k_mean = pl.pallas_call(
        functools.partial(_moba_prompt_body, pages_per_step=pages_per_step, per_block=per_block),
        grid_spec=pltpu.PrefetchScalarGridSpec(
            num_scalar_prefetch=1,
            grid=(N_HEADS // hb, nb),
            in_specs=[
                pl.BlockSpec((hb * KAUG_WIDTH, MOBA_BLOCK), lambda h, i, pt: (h, i)),
                pl.BlockSpec((hb * KAUG_WIDTH, MOBA_BLOCK),
                             lambda h, i, pt: (h, jnp.minimum(i + 1, nb - 1))),
                pl.BlockSpec((t + MOBA_PAD, hb * KAUG_WIDTH), lambda h, i, pt: (0, h)),
                pl.BlockSpec((nb + MOBA_CHUNK, hb * HEAD_DIM, MOBA_BLOCK),
                             lambda h, i, pt: (0, h, 0)),
                pl.BlockSpec((hb, 2, MOBA_BLOCK, MOBA_BLOCK),
                             lambda h, i, pt: (h, TAB_PREV_T // 2, 0, 0)),
                pl.BlockSpec(memory_space=pl.ANY),
            ],
            out_specs=[
                pl.BlockSpec((MOBA_BLOCK, hb * HEAD_DIM), lambda h, i, pt: (i, h)),
                pl.BlockSpec((1, blocks_per_step, N_HEADS, HEAD_DIM),
                             lambda h, i, pt: (h * nb + i, 0, 0, 0)),
            ],
            scratch_shapes=[
                pltpu.VMEM((hb * MOBA_CHUNK, MOBA_BLOCK, MOBA_BLOCK), F32),
                pltpu.VMEM((2, pages_per_step, page, N_HEADS, HEAD_DIM), F32),
                pltpu.SemaphoreType.DMA((2,)),
            ],
        ),
        out_shape=[jax.ShapeDtypeStruct((t, ATTN_WIDTH), BF16),
                   jax.ShapeDtypeStruct((n_steps, blocks_per_step, N_HEADS, HEAD_DIM), F32)],
        compiler_params=_params(("arbitrary", "arbitrary"), 58),
        name="moba_prompt",
    )(page_table_flat, q_aug_t, q_aug_t, k_aug, v_t, tables, cache_k)
    return attn, k_mean.reshape(n_steps * blocks_per_step, N_HEADS, HEAD_DIM)


def _sample_select_body(q_ref, km_ref, o_ref):
    km = km_ref[0]
    gate = jnp.sum(km * q_ref[0], axis=-1, keepdims=True)
    gate = jnp.broadcast_to(gate, km.shape)
    blk_id = lax.broadcasted_iota(jnp.int32, km.shape, 0).astype(F32)
    for slot in range(MOBA_TOPK):
        top = jnp.max(gate, axis=0, keepdims=True)
        first = jnp.min(jnp.where(gate == top, blk_id, float(km.shape[0])),
                        axis=0, keepdims=True)
        o_ref[0, slot] = first[0].astype(jnp.int32)
        gate = jnp.where(blk_id == first, -jnp.inf, gate)


def _sample_select(q_heads, k_mean):
    n_req, nbp = k_mean.shape[0], k_mean.shape[1]
    assert nbp >= MOBA_TOPK
    out = pl.pallas_call(
        _sample_select_body,
        grid=(n_req,),
        in_specs=[pl.BlockSpec((1, N_HEADS, HEAD_DIM), lambda b: (b, 0, 0)),
                  pl.BlockSpec((1, nbp, N_HEADS, HEAD_DIM), lambda b: (b, 0, 0, 0))],
        out_specs=pl.BlockSpec((1, MOBA_TOPK, N_HEADS, HEAD_DIM), lambda b: (b, 0, 0, 0)),
        out_shape=jax.ShapeDtypeStruct((n_req, MOBA_TOPK, N_HEADS, HEAD_DIM), jnp.int32),
        compiler_params=_params(("parallel",), 32),
        name="sample_select",
    )(q_heads, k_mean)
    return out[:, :, :, 0]


def _sample_attn_body(pt_ref, sel_ref, q_ref, kn_ref, vn_ref, tprev_ref, town_ref, ck_ref, cv_ref,
                      o_ref, kbuf, vbuf, sem, *, n_pages, per_block):
    b = pl.program_id(0)
    page = kbuf.shape[2]
    last_block = n_pages // per_block - 1

    def copies(req, buf_slot):
        out = []
        for h in range(N_HEADS):
            for slot in range(MOBA_TOPK):
                blk = sel_ref[(req * MOBA_TOPK + slot) * N_HEADS + h]
                for i in range(per_block):
                    pg = pt_ref[req * n_pages + blk * per_block + i]
                    idx = (h * MOBA_TOPK + slot) * per_block + i
                    out.append(pltpu.make_async_copy(ck_ref.at[0, pg, :, h, :],
                                                     kbuf.at[buf_slot, idx], sem.at[buf_slot, 0]))
                    out.append(pltpu.make_async_copy(cv_ref.at[0, pg, :, h, :],
                                                     vbuf.at[buf_slot, idx], sem.at[buf_slot, 1]))
        return out

    @pl.when(b == 0)
    def _():
        for cp in copies(b, 0):
            cp.start()

    @pl.when(b + 1 < pl.num_programs(0))
    def _():
        for cp in copies(b + 1, (b + 1) % 2):
            cp.start()

    cur = b % 2
    for cp in copies(b, cur):
        cp.wait()

    hs = [slice(h * HEAD_DIM, (h + 1) * HEAD_DIM) for h in range(N_HEADS)]
    n_tiles = MOBA_TOPK * per_block
    q = [q_ref[0, :, hs[h]] * (HEAD_DIM ** -0.5) for h in range(N_HEADS)]
    logits = {}
    for h in range(N_HEADS):
        q8 = jnp.broadcast_to(q[h], (8, HEAD_DIM)).astype(BF16)
        prev_row = tprev_ref[h, 0, 0:1, :]
        for slot in range(MOBA_TOPK):
            near = (sel_ref[(b * MOBA_TOPK + slot) * N_HEADS + h] == last_block).astype(F32)
            for i in range(per_block):
                idx = slot * per_block + i
                kt = kbuf[cur, h * n_tiles + idx].astype(BF16)
                lg = lax.dot_general(q8, kt, NT_DIMS, preferred_element_type=F32)[0:1, :]
                logits[h, idx] = lg + near * prev_row[:, i * page:(i + 1) * page]
    p_own, den, weights = {}, {}, {}
    for h in range(N_HEADS):
        l_own = (jnp.sum(q[h] * kn_ref[0, :, hs[h]], axis=1, keepdims=True)
                 + town_ref[h, 0, 0:1, 0:1])
        m = l_own
        for idx in range(n_tiles):
            m = jnp.maximum(m, jnp.max(logits[h, idx], axis=1, keepdims=True))
        p_own[h] = jnp.exp(l_own - m)
        tot = p_own[h]
        for idx in range(n_tiles):
            p = jnp.exp(logits[h, idx] - m)
            tot = tot + jnp.sum(p, axis=1, keepdims=True)
            weights[h, idx] = jnp.broadcast_to(p, (8, page)).astype(BF16)
        den[h] = tot
    for h in range(N_HEADS):
        acc = p_own[h] * vn_ref[0, :, hs[h]]
        for idx in range(n_tiles):
            acc = acc + jnp.dot(weights[h, idx], vbuf[cur, h * n_tiles + idx].astype(BF16),
                                preferred_element_type=F32)[0:1, :]
        o_ref[0, :, hs[h]] = (acc / den[h]).astype(o_ref.dtype)


def _sample_attn(q3, kn3, vn3, cache_k, cache_v, page_table_flat, sel_flat, tables, n_pages):
    n_req = q3.shape[0]
    page = cache_k.shape[2]
    per_block = MOBA_BLOCK // page
    n_bufs = N_HEADS * MOBA_TOPK * per_block
    row_spec = pl.BlockSpec((1, 1, ATTN_WIDTH), lambda b, pt, sel: (b, 0, 0))
    return pl.pallas_call(
        functools.partial(_sample_attn_body, n_pages=n_pages, per_block=per_block),
        grid_spec=pltpu.PrefetchScalarGridSpec(
            num_scalar_prefetch=2,
            grid=(n_req,),
            in_specs=[row_spec, row_spec, row_spec,
                      pl.BlockSpec((N_HEADS, 1, 8, MOBA_BLOCK),
                                   lambda b, pt, sel: (0, TAB_PREV, 0, 0)),
                      pl.BlockSpec((N_HEADS, 1, 8, MOBA_BLOCK),
                                   lambda b, pt, sel: (0, TAB_OWN, 0, 0)),
                      pl.BlockSpec(memory_space=pl.ANY),
                      pl.BlockSpec(memory_space=pl.ANY)],
            out_specs=row_spec,
            scratch_shapes=[pltpu.VMEM((2, n_bufs, page, HEAD_DIM), F32),
                            pltpu.VMEM((2, n_bufs, page, HEAD_DIM), F32),
                            pltpu.SemaphoreType.DMA((2, 2))],
        ),
        out_shape=jax.ShapeDtypeStruct((n_req, 1, ATTN_WIDTH), BF16),
        compiler_params=_params(("arbitrary",), 40),
        name="sample_attn",
    )(page_table_flat, sel_flat, q3, kn3, vn3, tables, tables, cache_k, cache_v)


def _cross_prompt_body(q_ref, mk_ref, mv_ref, o_ref):
    for h in range(CROSS_HEADS):
        sl = slice(h * CROSS_HEAD_DIM, (h + 1) * CROSS_HEAD_DIM)
        s = lax.dot_general(q_ref[:, sl], mk_ref[:, sl], NT_DIMS,
                            preferred_element_type=F32) * (CROSS_HEAD_DIM ** -0.5)
        p = jnp.exp(s - jnp.max(s, axis=1, keepdims=True))
        den = jnp.sum(p, axis=1, keepdims=True)
        o = jnp.dot(p.astype(BF16), mv_ref[:, sl], preferred_element_type=F32) / den
        o_ref[:, sl] = o.astype(o_ref.dtype)


def _cross_prompt(qc, mk_bf, mv_bf, bm=512):
    t = qc.shape[0]
    n_mem = mk_bf.shape[0]
    return pl.pallas_call(
        _cross_prompt_body,
        grid=(t // bm,),
        in_specs=[pl.BlockSpec((bm, CROSS_WIDTH), lambda i: (i, 0)),
                  pl.BlockSpec((n_mem, CROSS_WIDTH), lambda i: (0, 0)),
                  pl.BlockSpec((n_mem, CROSS_WIDTH), lambda i: (0, 0))],
        out_specs=pl.BlockSpec((bm, CROSS_WIDTH), lambda i: (i, 0)),
        out_shape=jax.ShapeDtypeStruct((t, CROSS_WIDTH), BF16),
        compiler_params=_params(("parallel",), 32),
        name="cross_prompt",
    )(qc, mk_bf, mv_bf)


def _cross_sample_body(q_ref, mk_hbm, mv_hbm, o_ref, kbuf, vbuf, sem):
    b = pl.program_id(0)

    def copies(req, buf_slot):
        out = []
        for h in range(CROSS_HEADS):
            out.append(pltpu.make_async_copy(mk_hbm.at[0, req, :, h, :], kbuf.at[buf_slot, h],
                                             sem.at[buf_slot, 0]))
            out.append(pltpu.make_async_copy(mv_hbm.at[0, req, :, h, :], vbuf.at[buf_slot, h],
                                             sem.at[buf_slot, 1]))
        return out

    @pl.when(b == 0)
    def _():
        for cp in copies(b, 0):
            cp.start()

    @pl.when(b + 1 < pl.num_programs(0))
    def _():
        for cp in copies(b + 1, (b + 1) % 2):
            cp.start()

    cur = b % 2
    for cp in copies(b, cur):
        cp.wait()

    for h in range(CROSS_HEADS):
        sl = slice(h * CROSS_HEAD_DIM, (h + 1) * CROSS_HEAD_DIM)
        q8 = jnp.broadcast_to(q_ref[0, :, sl], (8, CROSS_HEAD_DIM))
        s = lax.dot_general(q8, kbuf[cur, h].astype(BF16), NT_DIMS,
                            preferred_element_type=F32) * (CROSS_HEAD_DIM ** -0.5)
        p = jnp.exp(s - jnp.max(s, axis=1, keepdims=True))
        den = jnp.sum(p, axis=1, keepdims=True)
        o = jnp.dot(p.astype(BF16), vbuf[cur, h].astype(BF16),
                    preferred_element_type=F32) / den
        o_ref[0, :, sl] = o[0:1, :].astype(o_ref.dtype)


def _cross_sample(qc3, mem_k, mem_v):
    _, n_req, n_mem, _, _ = mem_k.shape
    row_spec = pl.BlockSpec((1, 1, CROSS_WIDTH), lambda b: (b, 0, 0))
    any_spec = pl.BlockSpec(memory_space=pl.ANY)
    return pl.pallas_call(
        _cross_sample_body,
        grid=(n_req,),
        in_specs=[row_spec, any_spec, any_spec],
        out_specs=row_spec,
        out_shape=jax.ShapeDtypeStruct((n_req, 1, CROSS_WIDTH), BF16),
        scratch_shapes=[pltpu.VMEM((2, CROSS_HEADS, n_mem, CROSS_HEAD_DIM), F32),
                        pltpu.VMEM((2, CROSS_HEADS, n_mem, CROSS_HEAD_DIM), F32),
                        pltpu.SemaphoreType.DMA((2, 2))],
        compiler_params=_params(("arbitrary",), 32),
        name="cross_sample",
    )(qc3, mem_k, mem_v)


CONV_TILE = 256
CONV_HALO = 32
CONV_UNROLL = 4


def _conv_prompt_body(cur_ref, prev_ref, w_ref, cb_ref, g_ref, b_ref, o_ref, buf_ref, sh_ref,
                      wt_ref, y_ref):
    i = pl.program_id(0)
    has_prev = (i > 0).astype(F32)
    buf_ref[0:CONV_HALO, :] = prev_ref[CONV_TILE - CONV_HALO:CONV_TILE, :] * has_prev
    buf_ref[CONV_HALO:CONV_HALO + CONV_TILE, :] = cur_ref[...]
    first = CONV_HALO - CONV_STATE
    shifted_rows = sh_ref.shape[1]
    for tap in range(CONV_WIDTH):
        wt_ref[tap] = jnp.broadcast_to(w_ref[tap:tap + 1, :], (SUBLANES, D_CONV))
    for c in range(D_CONV // LANES):
        cs = slice(c * LANES, (c + 1) * LANES)
        for r in range(1, SUBLANES):
            sh_ref[r - 1] = buf_ref[r:r + shifted_rows, cs]
        weights = [wt_ref[tap, :, cs] for tap in range(CONV_WIDTH)]
        bias = jnp.broadcast_to(cb_ref[:, cs], (SUBLANES, LANES))

        def row_tiles(i, carry, cs=cs, weights=weights, bias=bias):
            row0 = pl.multiple_of(i * (CONV_UNROLL * SUBLANES), CONV_UNROLL * SUBLANES)
            acc = [bias] * CONV_UNROLL
            for tap in range(CONV_WIDTH):
                r = (first + tap) % SUBLANES
                for j in range(CONV_UNROLL):
                    rows = pl.ds(row0 + (j * SUBLANES + first + tap - r), SUBLANES)
                    tile = buf_ref[rows, cs] if r == 0 else sh_ref[r - 1, rows, :]
                    acc[j] = acc[j] + tile * weights[tap]
            for j in range(CONV_UNROLL):
                y_ref[pl.ds(row0 + j * SUBLANES, SUBLANES), cs] = acc[j]
            return carry

        lax.fori_loop(0, CONV_TILE // (CONV_UNROLL * SUBLANES), row_tiles, 0)
    y = _layer_norm(y_ref[...], g_ref[...], b_ref[...])
    o_ref[...] = jax.nn.silu(y).astype(o_ref.dtype)


def _conv_prompt(u, conv_w, conv_b, ln_g, ln_b):
    t = u.shape[0]
    vec = pl.BlockSpec((1, D_CONV), lambda i: (0, 0))
    return pl.pallas_call(
        _conv_prompt_body,
        grid=(t // CONV_TILE,),
        in_specs=[pl.BlockSpec((CONV_TILE, D_CONV), lambda i: (i, 0)),
                  pl.BlockSpec((CONV_TILE, D_CONV), lambda i: (jnp.maximum(i - 1, 0), 0)),
                  pl.BlockSpec((CONV_WIDTH, D_CONV), lambda i: (0, 0)),
                  vec, vec, vec],
        out_specs=pl.BlockSpec((CONV_TILE, D_CONV), lambda i: (i, 0)),
        out_shape=jax.ShapeDtypeStruct((t, D_CONV), BF16),
        scratch_shapes=[pltpu.VMEM((CONV_HALO + CONV_TILE, D_CONV), F32),
                        pltpu.VMEM((SUBLANES - 1, CONV_HALO + CONV_TILE - SUBLANES, LANES), F32),
                        pltpu.VMEM((CONV_WIDTH, SUBLANES, D_CONV), F32),
                        pltpu.VMEM((CONV_TILE, D_CONV), F32)],
        compiler_params=_params(("parallel",), 32),
        name="conv_prompt",
    )(u, u, conv_w, conv_b.reshape(1, D_CONV), ln_g.reshape(1, D_CONV), ln_b.reshape(1, D_CONV))


def _conv_sample_body(st_ref, u_ref, w_ref, cb_ref, g_ref, b_ref, o_ref):
    acc = u_ref[...] * w_ref[CONV_STATE:CONV_WIDTH, :]
    for tap in range(CONV_STATE):
        acc = acc + st_ref[tap] * w_ref[tap:tap + 1, :]
    y = _layer_norm(acc + cb_ref[...], g_ref[...], b_ref[...])
    o_ref[...] = jax.nn.silu(y).astype(o_ref.dtype)


def _conv_sample(state_t, u, conv_w, conv_b, ln_g, ln_b):
    n_req = u.shape[0]
    return pl.pallas_call(
        _conv_sample_body,
        out_shape=jax.ShapeDtypeStruct((n_req, D_CONV), BF16),
        name="conv_sample",
    )(state_t, u, conv_w, conv_b.reshape(1, D_CONV), ln_g.reshape(1, D_CONV),
      ln_b.reshape(1, D_CONV))


def kernel(x_prompt, x_sample, mem_prompt, cache_k, cache_v, state_conv, cache_mem_k, cache_mem_v,
           page_table, w_in, conv_w, conv_b, conv_ln_g, conv_ln_b, w_mem_kv, rel_bias, w_branch,
           w_out, ln1_g, ln1_b, w_gate, w_up, w_down, ln2_g, ln2_b):
    depth = w_in.shape[0]
    assert depth == 1 and x_prompt.shape[0] == 1 and x_sample.shape[1] == 1
    alpha = (2 * depth) ** 0.25
    t, d_model = x_prompt.shape[1], x_prompt.shape[2]
    n_req, n_pages = page_table.shape
    page = cache_k.shape[2]
    n_mem = mem_prompt.shape[1]
    assert (n_pages * page) % MOBA_BLOCK == 0

    sizes = (D_CONV, D_CONV, ATTN_WIDTH, ATTN_WIDTH, ATTN_WIDTH, CROSS_WIDTH, N_BRANCH * d_model)
    col_a, col_b, col_q, col_k, col_v, col_qc, col_g = (
        int(c) for c in np.concatenate([[0], np.cumsum(sizes)[:-1]]))
    w_o = w_out[0]
    w_d = w_down[0].astype(BF16)
    w_br = w_branch[0].astype(BF16)
    d_ff = w_gate.shape[2]

    tables = _bias_tables(rel_bias)

    xp = x_prompt[0]
    xs = x_sample[:, 0]
    q_p, q_s, xp_bf, xs_bf = _proj(xp, xs, [(w_in, col_q)], ATTN_WIDTH, "none", [F32],
                                   bn=ATTN_WIDTH, vmem_mib=56, keep_input=True)
    u_p, u_s = _proj(xp_bf, xs_bf, [(w_in, col_a), (w_in, col_b)], D_CONV, "glu", [F32])
    k_p, v_p, qc_p, k_s, v_s, qc_s = _proj(
        xp_bf, xs_bf, [(w_in, col_k), (w_in, col_v), (w_in, col_qc)], ATTN_WIDTH, "each",
        [F32, F32, BF16], vmem_mib=56)
    gates_p, gates_s = _proj(xp_bf, xs_bf, [(w_in, col_g)], N_BRANCH * d_model, "sigmoid",
                             [BF16], bn=1024)

    conv_y_p = _conv_prompt(u_p, conv_w[0], conv_b[0], conv_ln_g[0], conv_ln_b[0])
    k_aug, v_pad = _moba_prep(k_p, v_p)
    pt_flat = page_table.reshape(-1)
    attn_p, k_mean_s = _moba_prompt(_moba_select(q_p, _block_mean(k_p)), k_aug, v_pad, tables,
                                    cache_k, pt_flat)
    mem_bf = mem_prompt[0].astype(BF16)
    mk, mk_bf = _proj(mem_bf, None, [(w_mem_kv, 0)], CROSS_WIDTH, "none", [F32, BF16], bn=1024)
    mv, mv_bf = _proj(mem_bf, None, [(w_mem_kv, CROSS_WIDTH)], CROSS_WIDTH, "none", [F32, BF16],
                      bn=1024)
    cross_p = _cross_prompt(qc_p, mk_bf, mv_bf)

    conv_y_s = _conv_sample(state_conv[0].transpose(1, 0, 2), u_s, conv_w[0], conv_b[0],
                            conv_ln_g[0], conv_ln_b[0])
    sel = _sample_select(q_s.reshape(n_req, N_HEADS, HEAD_DIM),
                         k_mean_s.reshape(n_req, -1, N_HEADS, HEAD_DIM))
    attn_s = _sample_attn(q_s.reshape(n_req, 1, ATTN_WIDTH), k_s.reshape(n_req, 1, ATTN_WIDTH),
                          v_s.reshape(n_req, 1, ATTN_WIDTH), cache_k, cache_v, pt_flat,
                          sel.reshape(-1), tables, n_pages)
    cross_s = _cross_sample(qc_s.reshape(n_req, 1, CROSS_WIDTH), cache_mem_k, cache_mem_v)

    mix_p, mix_s = _branch_mix(
        (conv_y_p, attn_p, cross_p),
        (conv_y_s, attn_s.reshape(n_req, ATTN_WIDTH), cross_s.reshape(n_req, CROSS_WIDTH)),
        w_br, gates_p, gates_s)
    h1_p, h1_p_bf, h1_s, h1_s_bf = _mm_res_ln(mix_p, mix_s, w_o, xp, xs, ln1_g[0], ln1_b[0], alpha,
                                              [F32, BF16], bm=512)
    act_p, act_s = _proj(h1_p_bf, h1_s_bf, [(w_gate, 0), (w_up, 0)], d_ff, "swiglu", [BF16])
    y_p, y_s = _mm_res_ln(act_p, act_s, w_d, h1_p, h1_s, ln2_g[0], ln2_b[0], alpha, [F32],
                          bm=512)

    new_conv_sample = jnp.concatenate([state_conv[0][:, 1:], u_s[:, None, :]], axis=1)
    return (
        y_p[None],
        y_s[:, None],
        k_p.reshape(1, 1, t, N_HEADS, HEAD_DIM),
        v_p.reshape(1, 1, t, N_HEADS, HEAD_DIM),
        u_p[t - CONV_STATE:][None, None],
        mk.reshape(1, 1, n_mem, CROSS_HEADS, CROSS_HEAD_DIM),
        mv.reshape(1, 1, n_mem, CROSS_HEADS, CROSS_HEAD_DIM),
        k_s.reshape(1, n_req, 1, N_HEADS, HEAD_DIM),
        v_s.reshape(1, n_req, 1, N_HEADS, HEAD_DIM),
        new_conv_sample[None],
    )
```

```python
import functools
import math

import numpy as np
import jax
import jax.numpy as jnp
from jax import lax
from jax.experimental import pallas as pl
from jax.experimental.pallas import tpu as pltpu

F32 = jnp.float32
BF16 = jnp.bfloat16

N_HEADS = 8
HEAD_DIM = 128
ATTN_WIDTH = N_HEADS * HEAD_DIM
MOBA_BLOCK = 256
MOBA_TOPK = 3
D_CONV = 1024
CONV_WIDTH = 31
CONV_STATE = CONV_WIDTH - 1
CROSS_HEADS = 4
CROSS_HEAD_DIM = 256
CROSS_WIDTH = CROSS_HEADS * CROSS_HEAD_DIM
N_BRANCH = 3
BRANCH_WIDTH = 1024
NUM_BUCKETS = 32
MAX_EXACT = NUM_BUCKETS // 2
MAX_DISTANCE = 128
LN_EPS = 1e-5
MASKED = -1e30
LANES = 128
SUBLANES = 8
MIB = 2 ** 20
LOG2_E = math.log2(math.e)
LN_CHUNK_ROWS = 256

NT_DIMS = (((1,), (1,)), ((), ()))


def _params(semantics, vmem_mib):
    return pltpu.CompilerParams(dimension_semantics=semantics,
                                vmem_limit_bytes=vmem_mib * MIB)


def _layer_norm(y, g, b):
    mu = jnp.mean(y, axis=-1, keepdims=True)
    yc = y - mu
    var = jnp.mean(yc * yc, axis=-1, keepdims=True)
    return yc * lax.rsqrt(var + LN_EPS) * g + b


def _proj_body(*refs, n_w, n_out, epilogue, tail, keep_input):
    x_ref = refs[0]
    refs = refs[1:]
    if tail:
        xs_ref = refs[0]
        refs = refs[1:]
    w_refs = refs[:n_w]
    o_refs = refs[n_w:n_w + n_out]
    refs = refs[n_w + n_out:]
    if tail:
        os_refs = refs[:n_out]
        refs = refs[n_out:]
    if keep_input:
        x_copy_ref = refs[0]
        refs = refs[1:]
        if tail:
            xs_copy_ref = refs[0]
            refs = refs[1:]
    wbf_refs = refs
    i = pl.program_id(1)

    @pl.when(i == 0)
    def _():
        for w, wbf in zip(w_refs, wbf_refs):
            wbf[...] = w[...].astype(BF16)

    def apply(x, outs):
        z = [jnp.dot(x, w[...], preferred_element_type=F32) for w in wbf_refs]
        if epilogue == "each":
            for o, zn in zip(outs, z):
                o[...] = zn.astype(o.dtype)
            return
        if epilogue == "glu":
            y = z[0] * jax.nn.sigmoid(z[1])
        elif epilogue == "swiglu":
            y = jax.nn.silu(z[0]) * z[1]
        elif epilogue == "sigmoid":
            y = jax.nn.sigmoid(z[0])
        else:
            y = z[0]
        for o in outs:
            o[...] = y.astype(o.dtype)

    x = x_ref[...].astype(BF16)
    if keep_input:
        x_copy_ref[...] = x
    apply(x, o_refs)

    if tail:
        @pl.when(i == pl.num_programs(1) - 1)
        def _():
            xs = xs_ref[...].astype(BF16)
            if keep_input:
                xs_copy_ref[...] = xs
            apply(xs, os_refs)


def _proj(x, xs, ws, n, epilogue, out_dtypes, bm=1024, bn=512, vmem_mib=48, keep_input=False):
    m, k = x.shape
    tail = xs is not None
    bm = min(bm, m)
    bn = min(bn, n)
    while n % bn or any(c0 % bn for _, c0 in ws):
        bn //= 2
    assert m % bm == 0 and bn % LANES == 0
    assert not keep_input or bn == n
    in_specs = [pl.BlockSpec((bm, k), lambda j, i: (i, 0))]
    out_specs = [pl.BlockSpec((bm, bn), lambda j, i: (i, j)) for _ in out_dtypes]
    out_shape = [jax.ShapeDtypeStruct((m, n), dt) for dt in out_dtypes]
    operands = [x]
    if tail:
        ms = xs.shape[0]
        in_specs.append(pl.BlockSpec((ms, k), lambda j, i: (0, 0)))
        out_specs += [pl.BlockSpec((ms, bn), lambda j, i: (0, j)) for _ in out_dtypes]
        out_shape += [jax.ShapeDtypeStruct((ms, n), dt) for dt in out_dtypes]
        operands.append(xs)
    if keep_input:
        out_specs.append(pl.BlockSpec((bm, k), lambda j, i: (i, 0)))
        out_shape.append(jax.ShapeDtypeStruct((m, k), BF16))
        if tail:
            out_specs.append(pl.BlockSpec((ms, k), lambda j, i: (0, 0)))
            out_shape.append(jax.ShapeDtypeStruct((ms, k), BF16))
    in_specs += [pl.BlockSpec((None, k, bn), lambda j, i, c=c0 // bn: (0, 0, c + j))
                 for _, c0 in ws]
    return pl.pallas_call(
        functools.partial(_proj_body, n_w=len(ws), n_out=len(out_dtypes), epilogue=epilogue,
                          tail=tail, keep_input=keep_input),
        grid=(n // bn, m // bm),
        in_specs=in_specs,
        out_specs=out_specs,
        out_shape=out_shape,
        scratch_shapes=[pltpu.VMEM((k, bn), BF16) for _ in ws],
        compiler_params=_params(("arbitrary" if keep_input else "parallel", "arbitrary"), vmem_mib),
        name="proj_" + epilogue,
    )(*operands, *[w for w, _ in ws])


def _mm_res_ln_body(a_ref, as_ref, w_in_ref, r_ref, rs_ref, g_ref, b_ref, *o_refs, alpha, n_out,
                    cast_weight):
    if cast_weight:
        w_ref = o_refs[2 * n_out]

        @pl.when(pl.program_id(0) == 0)
        def _():
            w_ref[...] = w_in_ref[...].astype(BF16)
    else:
        w_ref = w_in_ref

    def apply(a, res, outs):
        rows = a.shape[0]
        chunk = min(rows, LN_CHUNK_ROWS)
        for r0 in range(0, rows, chunk):
            rs = slice(r0, r0 + chunk)
            y = alpha * res[rs, :] + jnp.dot(a[rs, :], w_ref[...], preferred_element_type=F32)
            out = _layer_norm(y, g_ref[...], b_ref[...])
            for o in outs:
                o[rs, :] = out.astype(o.dtype)

    apply(a_ref, r_ref, o_refs[:n_out])

    @pl.when(pl.program_id(0) == pl.num_programs(0) - 1)
    def _():
        apply(as_ref, rs_ref, o_refs[n_out:2 * n_out])


def _mm_res_ln(a, a_s, w, res, res_s, g, b, alpha, out_dtypes, bm, vmem_mib=58):
    m, k = a.shape
    ms = a_s.shape[0]
    n = w.shape[1]
    bm = min(bm, m)
    assert m % bm == 0
    cast_weight = w.dtype != BF16
    return pl.pallas_call(
        functools.partial(_mm_res_ln_body, alpha=alpha, n_out=len(out_dtypes),
                          cast_weight=cast_weight),
        scratch_shapes=[pltpu.VMEM((k, n), BF16)] if cast_weight else [],
        grid=(m // bm,),
        in_specs=[
            pl.BlockSpec((bm, k), lambda i: (i, 0)),
            pl.BlockSpec((ms, k), lambda i: (0, 0)),
            pl.BlockSpec((k, n), lambda i: (0, 0), pipeline_mode=pl.Buffered(1)),
            pl.BlockSpec((bm, n), lambda i: (i, 0)),
            pl.BlockSpec((ms, n), lambda i: (0, 0)),
            pl.BlockSpec((1, n), lambda i: (0, 0)),
            pl.BlockSpec((1, n), lambda i: (0, 0)),
        ],
        out_specs=[pl.BlockSpec((bm, n), lambda i: (i, 0)) for _ in out_dtypes]
        + [pl.BlockSpec((ms, n), lambda i: (0, 0)) for _ in out_dtypes],
        out_shape=[jax.ShapeDtypeStruct((m, n), dt) for dt in out_dtypes]
        + [jax.ShapeDtypeStruct((ms, n), dt) for dt in out_dtypes],
        compiler_params=_params(("arbitrary",), vmem_mib),
        name="mm_res_ln",
    )(a, a_s, w, res, res_s, g.reshape(1, n), b.reshape(1, n))


def _branch_body(*refs):
    nb = N_BRANCH
    br_refs, brs_refs, w_refs = refs[0:nb], refs[nb:2 * nb], refs[2 * nb:3 * nb]
    g_refs, gs_refs = refs[3 * nb:4 * nb], refs[4 * nb:5 * nb]
    o_ref, os_ref = refs[5 * nb], refs[5 * nb + 1]

    def mix(branches, gates, out):
        y = None
        for br, w, g in zip(branches, w_refs, gates):
            term = g[...] * jnp.dot(br[...], w[...], preferred_element_type=F32)
            y = term if y is None else y + term
        out[...] = y.astype(out.dtype)

    mix(br_refs, g_refs, o_ref)

    @pl.when(pl.program_id(1) == pl.num_programs(1) - 1)
    def _():
        mix(brs_refs, gs_refs, os_ref)


def _branch_mix(branches, branches_s, w_branch, gates, gates_s, bm=1024, bn=1024):
    m = branches[0].shape[0]
    ms = branches_s[0].shape[0]
    d = w_branch.shape[2]
    bm = min(bm, m)
    bn = min(bn, d)
    nj = d // bn
    ids = range(N_BRANCH)
    return pl.pallas_call(
        _branch_body,
        grid=(nj, m // bm),
        in_specs=[pl.BlockSpec((bm, BRANCH_WIDTH), lambda j, i: (i, 0)) for _ in ids]
        + [pl.BlockSpec((ms, BRANCH_WIDTH), lambda j, i: (0, 0)) for _ in ids]
        + [pl.BlockSpec((None, BRANCH_WIDTH, bn), lambda j, i, n=n: (n, 0, j)) for n in ids]
        + [pl.BlockSpec((bm, bn), lambda j, i, n=n: (i, n * nj + j)) for n in ids]
        + [pl.BlockSpec((ms, bn), lambda j, i, n=n: (0, n * nj + j)) for n in ids],
        out_specs=[pl.BlockSpec((bm, bn), lambda j, i: (i, j)),
                   pl.BlockSpec((ms, bn), lambda j, i: (0, j))],
        out_shape=[jax.ShapeDtypeStruct((m, d), BF16), jax.ShapeDtypeStruct((ms, d), BF16)],
        compiler_params=_params(("parallel", "arbitrary"), 48),
        name="branch_mix",
    )(*branches, *branches_s, w_branch, w_branch, w_branch, gates, gates, gates,
      gates_s, gates_s, gates_s)


TAB_PREV = 0
TAB_OWN = 1
TAB_PREV_T = 2
TAB_OWN_T = 3


def _bias_tables_body(rb_ref, o_ref):
    h = pl.program_id(0)
    shape = (MOBA_BLOCK, MOBA_BLOCK)
    far = rb_ref[NUM_BUCKETS - 1, h]
    for slot, offset, q_axis in ((TAB_PREV, MOBA_BLOCK, 0), (TAB_OWN, 0, 0),
                                 (TAB_PREV_T, MOBA_BLOCK, 1), (TAB_OWN_T, 0, 1)):
        qi = lax.broadcasted_iota(jnp.int32, shape, q_axis)
        kj = lax.broadcasted_iota(jnp.int32, shape, 1 - q_axis)
        dist = jnp.maximum(qi - kj + offset, 0)
        large = MAX_EXACT + (jnp.log(jnp.maximum(dist, 1).astype(F32) / MAX_EXACT)
                             / math.log(MAX_DISTANCE / MAX_EXACT)
                             * (NUM_BUCKETS - MAX_EXACT)).astype(jnp.int32)
        large = jnp.minimum(large, NUM_BUCKETS - 1)
        bucket = jnp.where(dist < MAX_EXACT, dist, large)
        val = jnp.zeros(shape, F32)
        for bkt in range(NUM_BUCKETS):
            val = jnp.where(bucket == bkt, rb_ref[bkt, h], val)
        val = val - far
        if slot in (TAB_PREV_T, TAB_OWN_T):
            val = val * LOG2_E
        if slot in (TAB_OWN, TAB_OWN_T):
            val = jnp.where(kj > qi, MASKED, val)
        o_ref[0, slot] = val


def _bias_tables(rel_bias):
    return pl.pallas_call(
        _bias_tables_body,
        grid=(N_HEADS,),
        in_specs=[pl.BlockSpec(memory_space=pltpu.SMEM)],
        out_specs=pl.BlockSpec((1, 4, MOBA_BLOCK, MOBA_BLOCK), lambda h: (h, 0, 0, 0)),
        out_shape=jax.ShapeDtypeStruct((N_HEADS, 4, MOBA_BLOCK, MOBA_BLOCK), F32),
        compiler_params=_params(("arbitrary",), 32),
        name="bias_tables",
    )(rel_bias)


def _block_mean_body(k_ref, o_ref):
    rows = k_ref.shape[0]
    x = k_ref[...].reshape(rows // MOBA_BLOCK, MOBA_BLOCK, k_ref.shape[1])
    o_ref[...] = jnp.sum(x, axis=1) * (1.0 / MOBA_BLOCK)


def _block_mean(k):
    t, w = k.shape
    nb = t // MOBA_BLOCK
    per = min(8, nb)
    assert nb % per == 0
    return pl.pallas_call(
        _block_mean_body,
        grid=(nb // per,),
        in_specs=[pl.BlockSpec((per * MOBA_BLOCK, w), lambda i: (i, 0))],
        out_specs=pl.BlockSpec((per, w), lambda i: (i, 0)),
        out_shape=jax.ShapeDtypeStruct((nb, w), F32),
        compiler_params=_params(("parallel",), 40),
        name="block_mean",
    )(k)


MOBA_CHUNK = 4
MOBA_HEADS_PER_STEP = 2
MOBA_PAD = MOBA_CHUNK * MOBA_BLOCK
KAUG_WIDTH = HEAD_DIM + LANES


def _moba_prep_body(k_ref, v_ref, ka_ref, vt_ref):
    i = pl.program_id(0)
    rows = k_ref.shape[0]
    col = lax.broadcasted_iota(jnp.int32, (rows, LANES), 1)

    @pl.when(i == 0)
    def _():
        pad_hot = jnp.where(col == LANES - 1, 1.0, 0.0).astype(BF16)
        for h in range(N_HEADS):
            ka_ref[:, h * KAUG_WIDTH:h * KAUG_WIDTH + HEAD_DIM] = jnp.zeros((rows, HEAD_DIM), BF16)
            ka_ref[:, h * KAUG_WIDTH + HEAD_DIM:(h + 1) * KAUG_WIDTH] = pad_hot
        vt_ref[...] = jnp.zeros(vt_ref.shape, BF16)

    @pl.when(i > 0)
    def _():
        row = lax.broadcasted_iota(jnp.int32, (rows, LANES), 0) + (i - 1) * rows
        hot = jnp.where(jnp.right_shift(row, int(math.log2(MOBA_BLOCK))) == col,
                        1.0, 0.0).astype(BF16)
        for h in range(N_HEADS):
            ka_ref[:, h * KAUG_WIDTH:h * KAUG_WIDTH + HEAD_DIM] = (
                k_ref[:, h * HEAD_DIM:(h + 1) * HEAD_DIM].astype(BF16))
            ka_ref[:, h * KAUG_WIDTH + HEAD_DIM:(h + 1) * KAUG_WIDTH] = hot
        for blk in range(rows // MOBA_BLOCK):
            vt_ref[blk] = v_ref[blk * MOBA_BLOCK:(blk + 1) * MOBA_BLOCK, :].T.astype(BF16)


def _moba_prep(k, v):
    t = k.shape[0]
    rows = MOBA_PAD
    assert t % rows == 0
    data = pl.BlockSpec((rows, ATTN_WIDTH), lambda i: (jnp.maximum(i - 1, 0), 0))
    return pl.pallas_call(
        _moba_prep_body,
        grid=(t // rows + 1,),
        in_specs=[data, data],
        out_specs=[pl.BlockSpec((rows, N_HEADS * KAUG_WIDTH), lambda i: (i, 0)),
                   pl.BlockSpec((MOBA_CHUNK, ATTN_WIDTH, MOBA_BLOCK), lambda i: (i, 0, 0))],
        out_shape=[jax.ShapeDtypeStruct((t + rows, N_HEADS * KAUG_WIDTH), BF16),
                   jax.ShapeDtypeStruct((t // MOBA_BLOCK + MOBA_CHUNK, ATTN_WIDTH, MOBA_BLOCK),
                                        BF16)],
        compiler_params=_params(("parallel",), 40),
        name="moba_prep",
    )(k, v)


SELECT_ROWS = 1024


def _moba_select_body(q_ref, km_ref, o_ref, *, n_sub):
    q = q_ref[...]
    rows = q.shape[0]
    gate = lax.dot_general(km_ref[...], q, NT_DIMS, precision=lax.Precision.HIGHEST,
                           preferred_element_type=F32)[:n_sub]
    blk = lax.broadcasted_iota(jnp.int32, gate.shape, 0)
    blk_f = blk.astype(F32)
    pos = lax.broadcasted_iota(jnp.int32, gate.shape, 1) + pl.program_id(0) * rows
    own = jnp.right_shift(pos, int(math.log2(MOBA_BLOCK)))
    gate = jnp.where(blk < own, gate, -jnp.inf)
    picked = blk == own
    for _ in range(MOBA_TOPK):
        top = jnp.max(gate, axis=0, keepdims=True)
        first = jnp.min(jnp.where(gate == top, blk_f, float(LANES)), axis=0, keepdims=True)
        chosen = blk_f == first
        picked = picked | (chosen & (top > -jnp.inf))
        gate = jnp.where(chosen, -jnp.inf, gate)
    o_ref[:HEAD_DIM, :] = (q * (HEAD_DIM ** -0.5 * LOG2_E)).T.astype(BF16)
    o_ref[HEAD_DIM:HEAD_DIM + n_sub, :] = jnp.where(picked, 0.0, MASKED).astype(BF16)
    o_ref[HEAD_DIM + n_sub:, :] = jnp.full((LANES - n_sub, rows), MASKED, BF16)


def _moba_select(q, k_mean):
    t = q.shape[0]
    nb = t // MOBA_BLOCK
    rows = min(SELECT_ROWS, t)
    n_sub = -(-nb // 16) * 16
    assert t % rows == 0 and MOBA_TOPK <= nb and n_sub < LANES
    km = jnp.pad(k_mean, ((0, LANES - nb), (0, 0)))
    return pl.pallas_call(
        functools.partial(_moba_select_body, n_sub=n_sub),
        grid=(t // rows, N_HEADS),
        in_specs=[pl.BlockSpec((rows, HEAD_DIM), lambda i, h: (i, h)),
                  pl.BlockSpec((LANES, HEAD_DIM), lambda i, h: (0, h))],
        out_specs=pl.BlockSpec((KAUG_WIDTH, rows), lambda i, h: (h, i)),
        out_shape=jax.ShapeDtypeStruct((N_HEADS * KAUG_WIDTH, t), BF16),
        compiler_params=_params(("parallel", "parallel"), 40),
        name="moba_select",
    )(q, km)


def _moba_prompt_body(pt_ref, qa_ref, qn_ref, ka_ref, vt_ref, tab_ref, ck_ref, o_ref, km_ref, s_ref,
                      page_buf, page_sem, *, pages_per_step, per_block):
    qb_idx = pl.program_id(1)
    step = pl.program_id(0) * pl.num_programs(1) + qb_idx
    n_steps = pl.num_programs(0) * pl.num_programs(1)

    def page_copies(at_step, slot):
        return [pltpu.make_async_copy(ck_ref.at[0, pt_ref[at_step * pages_per_step + j]],
                                      page_buf.at[slot, j], page_sem.at[slot])
                for j in range(pages_per_step)]

    @pl.when(step == 0)
    def _():
        for cp in page_copies(step, 0):
            cp.start()

    @pl.when(step + 1 < n_steps)
    def _():
        for cp in page_copies(step + 1, (step + 1) % 2):
            cp.start()

    cur = step % 2
    for cp in page_copies(step, cur):
        cp.wait()

    def block_means(blocks):
        for blk in blocks:
            tot = jnp.sum(page_buf[cur, blk * per_block], axis=0)
            for extra in range(1, per_block):
                tot = tot + jnp.sum(page_buf[cur, blk * per_block + extra], axis=0)
            km_ref[0, blk] = tot * (1.0 / MOBA_BLOCK)

    n_blocks = pages_per_step // per_block
    heads = range(MOBA_HEADS_PER_STEP)
    tiles = [(hh, g) for hh in heads for g in range(MOBA_CHUNK)]
    q_aug = [qa_ref[hh * KAUG_WIDTH:(hh + 1) * KAUG_WIDTH, :] for hh in heads]
    q_aug_next = [qn_ref[hh * KAUG_WIDTH:(hh + 1) * KAUG_WIDTH, :] for hh in heads]

    def logits(hh, g, first_block, queries):
        start = pl.multiple_of((first_block + g) * MOBA_BLOCK, MOBA_BLOCK)
        s_ref[hh * MOBA_CHUNK + g] = jnp.dot(
            ka_ref[pl.ds(start, MOBA_BLOCK), hh * KAUG_WIDTH:(hh + 1) * KAUG_WIDTH],
            queries[hh], preferred_element_type=F32)

    def weights(state, bias, next_first, next_queries):
        m_new, scale, l_new, p = [], [], [], {}

        def tile(hh, g):
            s = s_ref[hh * MOBA_CHUNK + g]
            return s if bias[g] is None else s + tab_ref[hh, bias[g]]

        for hh in heads:
            m, l, _ = state[hh]
            top = m
            for g in range(MOBA_CHUNK):
                top = jnp.maximum(top, jnp.max(tile(hh, g), axis=0, keepdims=True))
            sc = jnp.exp2(m - top)
            tot = sc * l
            for g in range(MOBA_CHUNK):
                e = jnp.exp2(tile(hh, g) - top)
                logits(hh, g, next_first, next_queries)
                tot = tot + jnp.sum(e, axis=0, keepdims=True)
                p[hh, g] = e.astype(BF16)
            m_new.append(top)
            scale.append(sc)
            l_new.append(tot)
        return m_new, scale, l_new, p

    def accumulate(first_block, state, stats):
        m_new, scale, l_new, p = stats
        out = []
        for hh in heads:
            acc = scale[hh] * state[hh][2]
            for g in range(MOBA_CHUNK):
                acc = acc + jnp.dot(vt_ref[first_block + g, hh * HEAD_DIM:(hh + 1) * HEAD_DIM, :],
                                    p[hh, g], preferred_element_type=F32)
            out.append((m_new[hh], l_new[hh], acc))
        return tuple(out)

    init = tuple((jnp.full((1, MOBA_BLOCK), 0.1 * MASKED, F32), jnp.zeros((1, MOBA_BLOCK), F32),
                  jnp.zeros((HEAD_DIM, MOBA_BLOCK), F32)) for _ in heads)
    first = qb_idx % MOBA_CHUNK + 1

    @pl.when(qb_idx == 0)
    def _():
        for hh, g in tiles:
            logits(hh, g, first, q_aug)

    no_bias = [None] * MOBA_CHUNK

    def trip(c, state):
        stats = weights(state, no_bias, first + (c + 1) * MOBA_CHUNK, q_aug)
        return accumulate(first + c * MOBA_CHUNK, state, stats)

    n_early = qb_idx // MOBA_CHUNK
    odd = n_early % 2
    state = lax.cond(odd == 1, lambda st: trip(0, st), lambda st: st, init)
    state = lax.fori_loop(0, n_early // 2,
                          lambda c, st: trip(odd + 2 * c + 1, trip(odd + 2 * c, st)), state)
    bias = [None] * (MOBA_CHUNK - 2) + [0, 1]
    block_means(range(0, n_blocks // 2))
    stats = weights(state, bias, (qb_idx + 1) % MOBA_CHUNK + 1, q_aug_next)
    block_means(range(n_blocks // 2, n_blocks))
    state = accumulate(qb_idx + 1, state, stats)
    for hh in heads:
        _, l, acc = state[hh]
        o_ref[:, hh * HEAD_DIM:(hh + 1) * HEAD_DIM] = (acc / l).T.astype(o_ref.dtype)


def _moba_prompt(q_aug_t, k_aug, v_t, tables, cache_k, page_table_flat):
    t = q_aug_t.shape[1]
    nb = t // MOBA_BLOCK
    hb = MOBA_HEADS_PER_STEP
    n_steps = (N_HEADS // hb) * nb
    page = cache_k.shape[2]
    per_block = MOBA_BLOCK // page
    total_pages = page_table_flat.shape[0]
    pages_per_step = total_pages // n_steps
    assert t % MOBA_BLOCK == 0 and MOBA_CHUNK >= 2 and MOBA_BLOCK % page == 0
    assert total_pages % n_steps == 0 and pages_per_step % per_block == 0
    blocks_per_step = pages_per_step // per_block
    attn, k_mean = pl.pallas_call(
        functools.partial(_moba_prompt_body, pages_per_step=pages_per_step, per_block=per_block),
        grid_spec=pltpu.PrefetchScalarGridSpec(
            num_scalar_prefetch=1,
            grid=(N_HEADS // hb, nb),
            in_specs=[
                pl.BlockSpec((hb * KAUG_WIDTH, MOBA_BLOCK), lambda h, i, pt: (h, i)),
                pl.BlockSpec((hb * KAUG_WIDTH, MOBA_BLOCK),
                             lambda h, i, pt: (h, jnp.minimum(i + 1, nb - 1))),
                pl.BlockSpec((t + MOBA_PAD, hb * KAUG_WIDTH), lambda h, i, pt: (0, h)),
                pl.BlockSpec((nb + MOBA_CHUNK, hb * HEAD_DIM, MOBA_BLOCK),
                             lambda h, i, pt: (0, h, 0)),
                pl.BlockSpec((hb, 2, MOBA_BLOCK, MOBA_BLOCK),
                             lambda h, i, pt: (h, TAB_PREV_T // 2, 0, 0)),
                pl.BlockSpec(memory_space=pl.ANY),
            ],
            out_specs=[
                pl.BlockSpec((MOBA_BLOCK, hb * HEAD_DIM), lambda h, i, pt: (i, h)),
                pl.BlockSpec((1, blocks_per_step, N_HEADS, HEAD_DIM),
                             lambda h, i, pt: (h * nb + i, 0, 0, 0)),
            ],
            scratch_shapes=[
                pltpu.VMEM((hb * MOBA_CHUNK, MOBA_BLOCK, MOBA_BLOCK), F32),
                pltpu.VMEM((2, pages_per_step, page, N_HEADS, HEAD_DIM), F32),
                pltpu.SemaphoreType.DMA((2,)),
            ],
        ),
        out_shape=[jax.ShapeDtypeStruct((t, ATTN_WIDTH), BF16),
                   jax.ShapeDtypeStruct((n_steps, blocks_per_step, N_HEADS, HEAD_DIM), F32)],
        compiler_params=_params(("arbitrary", "arbitrary"), 58),
        name="moba_prompt",
    )(page_table_flat, q_aug_t, q_aug_t, k_aug, v_t, tables, cache_k)
    return attn, k_mean.reshape(n_steps * blocks_per_step, N_HEADS, HEAD_DIM)


def _sample_select_body(q_ref, km_ref, o_ref):
    km = km_ref[0]
    gate = jnp.sum(km * q_ref[0], axis=-1, keepdims=True)
    gate = jnp.broadcast_to(gate, km.shape)
    blk_id = lax.broadcasted_iota(jnp.int32, km.shape, 0).astype(F32)
    for slot in range(MOBA_TOPK):
        top = jnp.max(gate, axis=0, keepdims=True)
        first = jnp.min(jnp.where(gate == top, blk_id, float(km.shape[0])),
                        axis=0, keepdims=True)
        o_ref[0, slot] = first[0].astype(jnp.int32)
        gate = jnp.where(blk_id == first, -jnp.inf, gate)


def _sample_select(q_heads, k_mean):
    n_req, nbp = k_mean.shape[0], k_mean.shape[1]
    assert nbp >= MOBA_TOPK
    out = pl.pallas_call(
        _sample_select_body,
        grid=(n_req,),
        in_specs=[pl.BlockSpec((1, N_HEADS, HEAD_DIM), lambda b: (b, 0, 0)),
                  pl.BlockSpec((1, nbp, N_HEADS, HEAD_DIM), lambda b: (b, 0, 0, 0))],
        out_specs=pl.BlockSpec((1, MOBA_TOPK, N_HEADS, HEAD_DIM), lambda b: (b, 0, 0, 0)),
        out_shape=jax.ShapeDtypeStruct((n_req, MOBA_TOPK, N_HEADS, HEAD_DIM), jnp.int32),
        compiler_params=_params(("parallel",), 32),
        name="sample_select",
    )(q_heads, k_mean)
    return out[:, :, :, 0]


def _sample_attn_body(pt_ref, sel_ref, q_ref, kn_ref, vn_ref, tprev_ref, town_ref, ck_ref, cv_ref,
                      o_ref, kbuf, vbuf, sem, *, n_pages, per_block):
    b = pl.program_id(0)
    page = kbuf.shape[2]
    last_block = n_pages // per_block - 1

    def copies(req, buf_slot):
        out = []
        for h in range(N_HEADS):
            for slot in range(MOBA_TOPK):
                blk = sel_ref[(req * MOBA_TOPK + slot) * N_HEADS + h]
                for i in range(per_block):
                    pg = pt_ref[req * n_pages + blk * per_block + i]
                    idx = (h * MOBA_TOPK + slot) * per_block + i
                    out.append(pltpu.make_async_copy(ck_ref.at[0, pg, :, h, :],
                                                     kbuf.at[buf_slot, idx], sem.at[buf_slot, 0]))
                    out.append(pltpu.make_async_copy(cv_ref.at[0, pg, :, h, :],
                                                     vbuf.at[buf_slot, idx], sem.at[buf_slot, 1]))
        return out

    @pl.when(b == 0)
    def _():
        for cp in copies(b, 0):
            cp.start()

    @pl.when(b + 1 < pl.num_programs(0))
    def _():
        for cp in copies(b + 1, (b + 1) % 2):
            cp.start()

    cur = b % 2
    for cp in copies(b, cur):
        cp.wait()

    hs = [slice(h * HEAD_DIM, (h + 1) * HEAD_DIM) for h in range(N_HEADS)]
    n_tiles = MOBA_TOPK * per_block
    q = [q_ref[0, :, hs[h]] * (HEAD_DIM ** -0.5) for h in range(N_HEADS)]
    logits = {}
    for h in range(N_HEADS):
        q8 = jnp.broadcast_to(q[h], (8, HEAD_DIM)).astype(BF16)
        prev_row = tprev_ref[h, 0, 0:1, :]
        for slot in range(MOBA_TOPK):
            near = (sel_ref[(b * MOBA_TOPK + slot) * N_HEADS + h] == last_block).astype(F32)
            for i in range(per_block):
                idx = slot * per_block + i
                kt = kbuf[cur, h * n_tiles + idx].astype(BF16)
                lg = lax.dot_general(q8, kt, NT_DIMS, preferred_element_type=F32)[0:1, :]
                logits[h, idx] = lg + near * prev_row[:, i * page:(i + 1) * page]
    p_own, den, weights = {}, {}, {}
    for h in range(N_HEADS):
        l_own = (jnp.sum(q[h] * kn_ref[0, :, hs[h]], axis=1, keepdims=True)
                 + town_ref[h, 0, 0:1, 0:1])
        m = l_own
        for idx in range(n_tiles):
            m = jnp.maximum(m, jnp.max(logits[h, idx], axis=1, keepdims=True))
        p_own[h] = jnp.exp(l_own - m)
        tot = p_own[h]
        for idx in range(n_tiles):
            p = jnp.exp(logits[h, idx] - m)
            tot = tot + jnp.sum(p, axis=1, keepdims=True)
            weights[h, idx] = jnp.broadcast_to(p, (8, page)).astype(BF16)
        den[h] = tot
    for h in range(N_HEADS):
        acc = p_own[h] * vn_ref[0, :, hs[h]]
        for idx in range(n_tiles):
            acc = acc + jnp.dot(weights[h, idx], vbuf[cur, h * n_tiles + idx].astype(BF16),
                                preferred_element_type=F32)[0:1, :]
        o_ref[0, :, hs[h]] = (acc / den[h]).astype(o_ref.dtype)


def _sample_attn(q3, kn3, vn3, cache_k, cache_v, page_table_flat, sel_flat, tables, n_pages):
    n_req = q3.shape[0]
    page = cache_k.shape[2]
    per_block = MOBA_BLOCK // page
    n_bufs = N_HEADS * MOBA_TOPK * per_block
    row_spec = pl.BlockSpec((1, 1, ATTN_WIDTH), lambda b, pt, sel: (b, 0, 0))
    return pl.pallas_call(
        functools.partial(_sample_attn_body, n_pages=n_pages, per_block=per_block),
        grid_spec=pltpu.PrefetchScalarGridSpec(
            num_scalar_prefetch=2,
            grid=(n_req,),
            in_specs=[row_spec, row_spec, row_spec,
                      pl.BlockSpec((N_HEADS, 1, 8, MOBA_BLOCK),
                                   lambda b, pt, sel: (0, TAB_PREV, 0, 0)),
                      pl.BlockSpec((N_HEADS, 1, 8, MOBA_BLOCK),
                                   lambda b, pt, sel: (0, TAB_OWN, 0, 0)),
                      pl.BlockSpec(memory_space=pl.ANY),
                      pl.BlockSpec(memory_space=pl.ANY)],
            out_specs=row_spec,
            scratch_shapes=[pltpu.VMEM((2, n_bufs, page, HEAD_DIM), F32),
                            pltpu.VMEM((2, n_bufs, page, HEAD_DIM), F32),
                            pltpu.SemaphoreType.DMA((2, 2))],
        ),
        out_shape=jax.ShapeDtypeStruct((n_req, 1, ATTN_WIDTH), BF16),
        compiler_params=_params(("arbitrary",), 40),
        name="sample_attn",
    )(page_table_flat, sel_flat, q3, kn3, vn3, tables, tables, cache_k, cache_v)


def _cross_prompt_body(q_ref, mk_ref, mv_ref, o_ref):
    for h in range(CROSS_HEADS):
        sl = slice(h * CROSS_HEAD_DIM, (h + 1) * CROSS_HEAD_DIM)
        s = lax.dot_general(q_ref[:, sl], mk_ref[:, sl], NT_DIMS,
                            preferred_element_type=F32) * (CROSS_HEAD_DIM ** -0.5)
        p = jnp.exp(s - jnp.max(s, axis=1, keepdims=True))
        den = jnp.sum(p, axis=1, keepdims=True)
        o = jnp.dot(p.astype(BF16), mv_ref[:, sl], preferred_element_type=F32) / den
        o_ref[:, sl] = o.astype(o_ref.dtype)


def _cross_prompt(qc, mk_bf, mv_bf, bm=512):
    t = qc.shape[0]
    n_mem = mk_bf.shape[0]
    return pl.pallas_call(
        _cross_prompt_body,
        grid=(t // bm,),
        in_specs=[pl.BlockSpec((bm, CROSS_WIDTH), lambda i: (i, 0)),
                  pl.BlockSpec((n_mem, CROSS_WIDTH), lambda i: (0, 0)),
                  pl.BlockSpec((n_mem, CROSS_WIDTH), lambda i: (0, 0))],
        out_specs=pl.BlockSpec((bm, CROSS_WIDTH), lambda i: (i, 0)),
        out_shape=jax.ShapeDtypeStruct((t, CROSS_WIDTH), BF16),
        compiler_params=_params(("parallel",), 32),
        name="cross_prompt",
    )(qc, mk_bf, mv_bf)


def _cross_sample_body(q_ref, mk_hbm, mv_hbm, o_ref, kbuf, vbuf, sem):
    b = pl.program_id(0)

    def copies(req, buf_slot):
        out = []
        for h in range(CROSS_HEADS):
            out.append(pltpu.make_async_copy(mk_hbm.at[0, req, :, h, :], kbuf.at[buf_slot, h],
                                             sem.at[buf_slot, 0]))
            out.append(pltpu.make_async_copy(mv_hbm.at[0, req, :, h, :], vbuf.at[buf_slot, h],
                                             sem.at[buf_slot, 1]))
        return out

    @pl.when(b == 0)
    def _():
        for cp in copies(b, 0):
            cp.start()

    @pl.when(b + 1 < pl.num_programs(0))
    def _():
        for cp in copies(b + 1, (b + 1) % 2):
            cp.start()

    cur = b % 2
    for cp in copies(b, cur):
        cp.wait()

    for h in range(CROSS_HEADS):
        sl = slice(h * CROSS_HEAD_DIM, (h + 1) * CROSS_HEAD_DIM)
        q8 = jnp.broadcast_to(q_ref[0, :, sl], (8, CROSS_HEAD_DIM))
        s = lax.dot_general(q8, kbuf[cur, h].astype(BF16), NT_DIMS,
                            preferred_element_type=F32) * (CROSS_HEAD_DIM ** -0.5)
        p = jnp.exp(s - jnp.max(s, axis=1, keepdims=True))
        den = jnp.sum(p, axis=1, keepdims=True)
        o = jnp.dot(p.astype(BF16), vbuf[cur, h].astype(BF16),
                    preferred_element_type=F32) / den
        o_ref[0, :, sl] = o[0:1, :].astype(o_ref.dtype)


def _cross_sample(qc3, mem_k, mem_v):
    _, n_req, n_mem, _, _ = mem_k.shape
    row_spec = pl.BlockSpec((1, 1, CROSS_WIDTH), lambda b: (b, 0, 0))
    any_spec = pl.BlockSpec(memory_space=pl.ANY)
    return pl.pallas_call(
        _cross_sample_body,
        grid=(n_req,),
        in_specs=[row_spec, any_spec, any_spec],
        out_specs=row_spec,
        out_shape=jax.ShapeDtypeStruct((n_req, 1, CROSS_WIDTH), BF16),
        scratch_shapes=[pltpu.VMEM((2, CROSS_HEADS, n_mem, CROSS_HEAD_DIM), F32),
                        pltpu.VMEM((2, CROSS_HEADS, n_mem, CROSS_HEAD_DIM), F32),
                        pltpu.SemaphoreType.DMA((2, 2))],
        compiler_params=_params(("arbitrary",), 32),
        name="cross_sample",
    )(qc3, mem_k, mem_v)


CONV_TILE = 256
CONV_HALO = 32
CONV_UNROLL = 4


def _conv_prompt_body(cur_ref, prev_ref, w_ref, cb_ref, g_ref, b_ref, o_ref, buf_ref, sh_ref,
                      wt_ref, y_ref):
    i = pl.program_id(0)
    has_prev = (i > 0).astype(F32)
    buf_ref[0:CONV_HALO, :] = prev_ref[CONV_TILE - CONV_HALO:CONV_TILE, :] * has_prev
    buf_ref[CONV_HALO:CONV_HALO + CONV_TILE, :] = cur_ref[...]
    first = CONV_HALO - CONV_STATE
    shifted_rows = sh_ref.shape[1]
    for tap in range(CONV_WIDTH):
        wt_ref[tap] = jnp.broadcast_to(w_ref[tap:tap + 1, :], (SUBLANES, D_CONV))
    for c in range(D_CONV // LANES):
        cs = slice(c * LANES, (c + 1) * LANES)
        for r in range(1, SUBLANES):
            sh_ref[r - 1] = buf_ref[r:r + shifted_rows, cs]
        weights = [wt_ref[tap, :, cs] for tap in range(CONV_WIDTH)]
        bias = jnp.broadcast_to(cb_ref[:, cs], (SUBLANES, LANES))

        def row_tiles(i, carry, cs=cs, weights=weights, bias=bias):
            row0 = pl.multiple_of(i * (CONV_UNROLL * SUBLANES), CONV_UNROLL * SUBLANES)
            acc = [bias] * CONV_UNROLL
            for tap in range(CONV_WIDTH):
                r = (first + tap) % SUBLANES
                for j in range(CONV_UNROLL):
                    rows = pl.ds(row0 + (j * SUBLANES + first + tap - r), SUBLANES)
                    tile = buf_ref[rows, cs] if r == 0 else sh_ref[r - 1, rows, :]
                    acc[j] = acc[j] + tile * weights[tap]
            for j in range(CONV_UNROLL):
                y_ref[pl.ds(row0 + j * SUBLANES, SUBLANES), cs] = acc[j]
            return carry

        lax.fori_loop(0, CONV_TILE // (CONV_UNROLL * SUBLANES), row_tiles, 0)
    y = _layer_norm(y_ref[...], g_ref[...], b_ref[...])
    o_ref[...] = jax.nn.silu(y).astype(o_ref.dtype)


def _conv_prompt(u, conv_w, conv_b, ln_g, ln_b):
    t = u.shape[0]
    vec = pl.BlockSpec((1, D_CONV), lambda i: (0, 0))
    return pl.pallas_call(
        _conv_prompt_body,
        grid=(t // CONV_TILE,),
        in_specs=[pl.BlockSpec((CONV_TILE, D_CONV), lambda i: (i, 0)),
                  pl.BlockSpec((CONV_TILE, D_CONV), lambda i: (jnp.maximum(i - 1, 0), 0)),
                  pl.BlockSpec((CONV_WIDTH, D_CONV), lambda i: (0, 0)),
                  vec, vec, vec],
        out_specs=pl.BlockSpec((CONV_TILE, D_CONV), lambda i: (i, 0)),
        out_shape=jax.ShapeDtypeStruct((t, D_CONV), BF16),
        scratch_shapes=[pltpu.VMEM((CONV_HALO + CONV_TILE, D_CONV), F32),
                        pltpu.VMEM((SUBLANES - 1, CONV_HALO + CONV_TILE - SUBLANES, LANES), F32),
                        pltpu.VMEM((CONV_WIDTH, SUBLANES, D_CONV), F32),
                        pltpu.VMEM((CONV_TILE, D_CONV), F32)],
        compiler_params=_params(("parallel",), 32),
        name="conv_prompt",
    )(u, u, conv_w, conv_b.reshape(1, D_CONV), ln_g.reshape(1, D_CONV), ln_b.reshape(1, D_CONV))


def _conv_sample_body(st_ref, u_ref, w_ref, cb_ref, g_ref, b_ref, o_ref):
    acc = u_ref[...] * w_ref[CONV_STATE:CONV_WIDTH, :]
    for tap in range(CONV_STATE):
        acc = acc + st_ref[tap] * w_ref[tap:tap + 1, :]
    y = _layer_norm(acc + cb_ref[...], g_ref[...], b_ref[...])
    o_ref[...] = jax.nn.silu(y).astype(o_ref.dtype)


def _conv_sample(state_t, u, conv_w, conv_b, ln_g, ln_b):
    n_req = u.shape[0]
    return pl.pallas_call(
        _conv_sample_body,
        out_shape=jax.ShapeDtypeStruct((n_req, D_CONV), BF16),
        name="conv_sample",
    )(state_t, u, conv_w, conv_b.reshape(1, D_CONV), ln_g.reshape(1, D_CONV),
      ln_b.reshape(1, D_CONV))


def kernel(x_prompt, x_sample, mem_prompt, cache_k, cache_v, state_conv, cache_mem_k, cache_mem_v,
           page_table, w_in, conv_w, conv_b, conv_ln_g, conv_ln_b, w_mem_kv, rel_bias, w_branch,
           w_out, ln1_g, ln1_b, w_gate, w_up, w_down, ln2_g, ln2_b):
    depth = w_in.shape[0]
    assert depth == 1 and x_prompt.shape[0] == 1 and x_sample.shape[1] == 1
    alpha = (2 * depth) ** 0.25
    t, d_model = x_prompt.shape[1], x_prompt.shape[2]
    n_req, n_pages = page_table.shape
    page = cache_k.shape[2]
    n_mem = mem_prompt.shape[1]
    assert (n_pages * page) % MOBA_BLOCK == 0

    sizes = (D_CONV, D_CONV, ATTN_WIDTH, ATTN_WIDTH, ATTN_WIDTH, CROSS_WIDTH, N_BRANCH * d_model)
    col_a, col_b, col_q, col_k, col_v, col_qc, col_g = (
        int(c) for c in np.concatenate([[0], np.cumsum(sizes)[:-1]]))
    w_o = w_out[0]
    w_d = w_down[0].astype(BF16)
    w_br = w_branch[0].astype(BF16)
    d_ff = w_gate.shape[2]

    tables = _bias_tables(rel_bias)

    xp = x_prompt[0]
    xs = x_sample[:, 0]
    q_p, q_s, xp_bf, xs_bf = _proj(xp, xs, [(w_in, col_q)], ATTN_WIDTH, "none", [F32],
                                   bn=ATTN_WIDTH, vmem_mib=56, keep_input=True)
    u_p, u_s = _proj(xp_bf, xs_bf, [(w_in, col_a), (w_in, col_b)], D_CONV, "glu", [F32])
    k_p, v_p, qc_p, k_s, v_s, qc_s = _proj(
        xp_bf, xs_bf, [(w_in, col_k), (w_in, col_v), (w_in, col_qc)], ATTN_WIDTH, "each",
        [F32, F32, BF16], vmem_mib=56)
    gates_p, gates_s = _proj(xp_bf, xs_bf, [(w_in, col_g)], N_BRANCH * d_model, "sigmoid",
                             [BF16], bn=1024)

    conv_y_p = _conv_prompt(u_p, conv_w[0], conv_b[0], conv_ln_g[0], conv_ln_b[0])
    k_aug, v_pad = _moba_prep(k_p, v_p)
    pt_flat = page_table.reshape(-1)
    attn_p, k_mean_s = _moba_prompt(_moba_select(q_p, _block_mean(k_p)), k_aug, v_pad, tables,
                                    cache_k, pt_flat)
    mem_bf = mem_prompt[0].astype(BF16)
    mk, mk_bf = _proj(mem_bf, None, [(w_mem_kv, 0)], CROSS_WIDTH, "none", [F32, BF16], bn=1024)
    mv, mv_bf = _proj(mem_bf, None, [(w_mem_kv, CROSS_WIDTH)], CROSS_WIDTH, "none", [F32, BF16],
                      bn=1024)
    cross_p = _cross_prompt(qc_p, mk_bf, mv_bf)

    conv_y_s = _conv_sample(state_conv[0].transpose(1, 0, 2), u_s, conv_w[0], conv_b[0],
                            conv_ln_g[0], conv_ln_b[0])
    sel = _sample_select(q_s.reshape(n_req, N_HEADS, HEAD_DIM),
                         k_mean_s.reshape(n_req, -1, N_HEADS, HEAD_DIM))
    attn_s = _sample_attn(q_s.reshape(n_req, 1, ATTN_WIDTH), k_s.reshape(n_req, 1, ATTN_WIDTH),
                          v_s.reshape(n_req, 1, ATTN_WIDTH), cache_k, cache_v, pt_flat,
                          sel.reshape(-1), tables, n_pages)
    cross_s = _cross_sample(qc_s.reshape(n_req, 1, CROSS_WIDTH), cache_mem_k, cache_mem_v)

    mix_p, mix_s = _branch_mix(
        (conv_y_p, attn_p, cross_p),
        (conv_y_s, attn_s.reshape(n_req, ATTN_WIDTH), cross_s.reshape(n_req, CROSS_WIDTH)),
        w_br, gates_p, gates_s)
    h1_p, h1_p_bf, h1_s, h1_s_bf = _mm_res_ln(mix_p, mix_s, w_o, xp, xs, ln1_g[0], ln1_b[0], alpha,
                                              [F32, BF16], bm=512)
    act_p, act_s = _proj(h1_p_bf, h1_s_bf, [(w_gate, 0), (w_up, 0)], d_ff, "swiglu", [BF16])
    y_p, y_s = _mm_res_ln(act_p, act_s, w_d, h1_p, h1_s, ln2_g[0], ln2_b[0], alpha, [F32],
                          bm=512)

    new_conv_sample = jnp.concatenate([state_conv[0][:, 1:], u_s[:, None, :]], axis=1)
    return (
        y_p[None],
        y_s[:, None],
        k_p.reshape(1, 1, t, N_HEADS, HEAD_DIM),
        v_p.reshape(1, 1, t, N_HEADS, HEAD_DIM),
        u_p[t - CONV_STATE:][None, None],
        mk.reshape(1, 1, n_mem, CROSS_HEADS, CROSS_HEAD_DIM),
        mv.reshape(1, 1, n_mem, CROSS_HEADS, CROSS_HEAD_DIM),
        k_s.reshape(1, n_req, 1, N_HEADS, HEAD_DIM),
        v_s.reshape(1, n_req, 1, N_HEADS, HEAD_DIM),
        new_conv_sample[None],
    )
```

```python
import functools
import math

import numpy as np
import jax
import jax.numpy as jnp
from jax import lax
from jax.experimental import pallas as pl
from jax.experimental.pallas import tpu as pltpu

F32 = jnp.float32
BF16 = jnp.bfloat16

N_HEADS = 8
HEAD_DIM = 128
ATTN_WIDTH = N_HEADS * HEAD_DIM
MOBA_BLOCK = 256
MOBA_TOPK = 3
D_CONV = 1024
CONV_WIDTH = 31
CONV_STATE = CONV_WIDTH - 1
CROSS_HEADS = 4
CROSS_HEAD_DIM = 256
CROSS_WIDTH = CROSS_HEADS * CROSS_HEAD_DIM
N_BRANCH = 3
BRANCH_WIDTH = 1024
NUM_BUCKETS = 32
MAX_EXACT = NUM_BUCKETS // 2
MAX_DISTANCE = 128
LN_EPS = 1e-5
MASKED = -1e30
LANES = 128
SUBLANES = 8
MIB = 2 ** 20
LOG2_E = math.log2(math.e)
LN_CHUNK_ROWS = 256

NT_DIMS = (((1,), (1,)), ((), ()))


def _params(semantics, vmem_mib):
    return pltpu.CompilerParams(dimension_semantics=semantics,
                                vmem_limit_bytes=vmem_mib * MIB)


def _layer_norm(y, g, b):
    mu = jnp.mean(y, axis=-1, keepdims=True)
    yc = y - mu
    var = jnp.mean(yc * yc, axis=-1, keepdims=True)
    return yc * lax.rsqrt(var + LN_EPS) * g + b


def _proj_body(*refs, n_w, n_out, epilogue, tail, keep_input):
    x_ref = refs[0]
    refs = refs[1:]
    if tail:
        xs_ref = refs[0]
        refs = refs[1:]
    w_refs = refs[:n_w]
    o_refs = refs[n_w:n_w + n_out]
    refs = refs[n_w + n_out:]
    if tail:
        os_refs = refs[:n_out]
        refs = refs[n_out:]
    if keep_input:
        x_copy_ref = refs[0]
        refs = refs[1:]
        if tail:
            xs_copy_ref = refs[0]
            refs = refs[1:]
    wbf_refs = refs
    i = pl.program_id(1)

    @pl.when(i == 0)
    def _():
        for w, wbf in zip(w_refs, wbf_refs):
            wbf[...] = w[...].astype(BF16)

    def apply(x, outs):
        z = [jnp.dot(x, w[...], preferred_element_type=F32) for w in wbf_refs]
        if epilogue == "each":
            for o, zn in zip(outs, z):
                o[...] = zn.astype(o.dtype)
            return
        if epilogue == "glu":
            y = z[0] * jax.nn.sigmoid(z[1])
        elif epilogue == "swiglu":
            y = jax.nn.silu(z[0]) * z[1]
        elif epilogue == "sigmoid":
            y = jax.nn.sigmoid(z[0])
        else:
            y = z[0]
        for o in outs:
            o[...] = y.astype(o.dtype)

    x = x_ref[...].astype(BF16)
    if keep_input:
        x_copy_ref[...] = x
    apply(x, o_refs)

    if tail:
        @pl.when(i == pl.num_programs(1) - 1)
        def _():
            xs = xs_ref[...].astype(BF16)
            if keep_input:
                xs_copy_ref[...] = xs
            apply(xs, os_refs)


def _proj(x, xs, ws, n, epilogue, out_dtypes, bm=1024, bn=512, vmem_mib=48, keep_input=False):
    m, k = x.shape
    tail = xs is not None
    bm = min(bm, m)
    bn = min(bn, n)
    while n % bn or any(c0 % bn for _, c0 in ws):
        bn //= 2
    assert m % bm == 0 and bn % LANES == 0
    assert not keep_input or bn == n
    in_specs = [pl.BlockSpec((bm, k), lambda j, i: (i, 0))]
    out_specs = [pl.BlockSpec((bm, bn), lambda j, i: (i, j)) for _ in out_dtypes]
    out_shape = [jax.ShapeDtypeStruct((m, n), dt) for dt in out_dtypes]
    operands = [x]
    if tail:
        ms = xs.shape[0]
        in_specs.append(pl.BlockSpec((ms, k), lambda j, i: (0, 0)))
        out_specs += [pl.BlockSpec((ms, bn), lambda j, i: (0, j)) for _ in out_dtypes]
        out_shape += [jax.ShapeDtypeStruct((ms, n), dt) for dt in out_dtypes]
        operands.append(xs)
    if keep_input:
        out_specs.append(pl.BlockSpec((bm, k), lambda j, i: (i, 0)))
        out_shape.append(jax.ShapeDtypeStruct((m, k), BF16))
        if tail:
            out_specs.append(pl.BlockSpec((ms, k), lambda j, i: (0, 0)))
            out_shape.append(jax.ShapeDtypeStruct((ms, k), BF16))
    in_specs += [pl.BlockSpec((None, k, bn), lambda j, i, c=c0 // bn: (0, 0, c + j))
                 for _, c0 in ws]
    return pl.pallas_call(
        functools.partial(_proj_body, n_w=len(ws), n_out=len(out_dtypes), epilogue=epilogue,
                          tail=tail, keep_input=keep_input),
        grid=(n // bn, m // bm),
        in_specs=in_specs,
        out_specs=out_specs,
        out_shape=out_shape,
        scratch_shapes=[pltpu.VMEM((k, bn), BF16) for _ in ws],
        compiler_params=_params(("arbitrary" if keep_input else "parallel", "arbitrary"), vmem_mib),
        name="proj_" + epilogue,
    )(*operands, *[w for w, _ in ws])


def _mm_res_ln_body(a_ref, as_ref, w_in_ref, r_ref, rs_ref, g_ref, b_ref, *o_refs, alpha, n_out,
                    cast_weight):
    if cast_weight:
        w_ref = o_refs[2 * n_out]

        @pl.when(pl.program_id(0) == 0)
        def _():
            w_ref[...] = w_in_ref[...].astype(BF16)
    else:
        w_ref = w_in_ref

    def apply(a, res, outs):
        rows = a.shape[0]
        chunk = min(rows, LN_CHUNK_ROWS)
        for r0 in range(0, rows, chunk):
            rs = slice(r0, r0 + chunk)
            y = alpha * res[rs, :] + jnp.dot(a[rs, :], w_ref[...], preferred_element_type=F32)
            out = _layer_norm(y, g_ref[...], b_ref[...])
            for o in outs:
                o[rs, :] = out.astype(o.dtype)

    apply(a_ref, r_ref, o_refs[:n_out])

    @pl.when(pl.program_id(0) == pl.num_programs(0) - 1)
    def _():
        apply(as_ref, rs_ref, o_refs[n_out:2 * n_out])


def _mm_res_ln(a, a_s, w, res, res_s, g, b, alpha, out_dtypes, bm, vmem_mib=58):
    m, k = a.shape
    ms = a_s.shape[0]
    n = w.shape[1]
    bm = min(bm, m)
    assert m % bm == 0
    cast_weight = w.dtype != BF16
    return pl.pallas_call(
        functools.partial(_mm_res_ln_body, alpha=alpha, n_out=len(out_dtypes),
                          cast_weight=cast_weight),
        scratch_shapes=[pltpu.VMEM((k, n), BF16)] if cast_weight else [],
        grid=(m // bm,),
        in_specs=[
            pl.BlockSpec((bm, k), lambda i: (i, 0)),
            pl.BlockSpec((ms, k), lambda i: (0, 0)),
            pl.BlockSpec((k, n), lambda i: (0, 0), pipeline_mode=pl.Buffered(1)),
            pl.BlockSpec((bm, n), lambda i: (i, 0)),
            pl.BlockSpec((ms, n), lambda i: (0, 0)),
            pl.BlockSpec((1, n), lambda i: (0, 0)),
            pl.BlockSpec((1, n), lambda i: (0, 0)),
        ],
        out_specs=[pl.BlockSpec((bm, n), lambda i: (i, 0)) for _ in out_dtypes]
        + [pl.BlockSpec((ms, n), lambda i: (0, 0)) for _ in out_dtypes],
        out_shape=[jax.ShapeDtypeStruct((m, n), dt) for dt in out_dtypes]
        + [jax.ShapeDtypeStruct((ms, n), dt) for dt in out_dtypes],
        compiler_params=_params(("arbitrary",), vmem_mib),
        name="mm_res_ln",
    )(a, a_s, w, res, res_s, g.reshape(1, n), b.reshape(1, n))


def _branch_body(*refs):
    nb = N_BRANCH
    br_refs, brs_refs, w_refs = refs[0:nb], refs[nb:2 * nb], refs[2 * nb:3 * nb]
    g_refs, gs_refs = refs[3 * nb:4 * nb], refs[4 * nb:5 * nb]
    o_ref, os_ref = refs[5 * nb], refs[5 * nb + 1]

    def mix(branches, gates, out):
        y = None
        for br, w, g in zip(branches, w_refs, gates):
            term = g[...] * jnp.dot(br[...], w[...], preferred_element_type=F32)
            y = term if y is None else y + term
        out[...] = y.astype(out.dtype)

    mix(br_refs, g_refs, o_ref)

    @pl.when(pl.program_id(1) == pl.num_programs(1) - 1)
    def _():
        mix(brs_refs, gs_refs, os_ref)


def _branch_mix(branches, branches_s, w_branch, gates, gates_s, bm=1024, bn=1024):
    m = branches[0].shape[0]
    ms = branches_s[0].shape[0]
    d = w_branch.shape[2]
    bm = min(bm, m)
    bn = min(bn, d)
    nj = d // bn
    ids = range(N_BRANCH)
    return pl.pallas_call(
        _branch_body,
        grid=(nj, m // bm),
        in_specs=[pl.BlockSpec((bm, BRANCH_WIDTH), lambda j, i: (i, 0)) for _ in ids]
        + [pl.BlockSpec((ms, BRANCH_WIDTH), lambda j, i: (0, 0)) for _ in ids]
        + [pl.BlockSpec((None, BRANCH_WIDTH, bn), lambda j, i, n=n: (n, 0, j)) for n in ids]
        + [pl.BlockSpec((bm, bn), lambda j, i, n=n: (i, n * nj + j)) for n in ids]
        + [pl.BlockSpec((ms, bn), lambda j, i, n=n: (0, n * nj + j)) for n in ids],
        out_specs=[pl.BlockSpec((bm, bn), lambda j, i: (i, j)),
                   pl.BlockSpec((ms, bn), lambda j, i: (0, j))],
        out_shape=[jax.ShapeDtypeStruct((m, d), BF16), jax.ShapeDtypeStruct((ms, d), BF16)],
        compiler_params=_params(("parallel", "arbitrary"), 48),
        name="branch_mix",
    )(*branches, *branches_s, w_branch, w_branch, w_branch, gates, gates, gates,
      gates_s, gates_s, gates_s)


TAB_PREV = 0
TAB_OWN = 1
TAB_PREV_T = 2
TAB_OWN_T = 3


def _bias_tables_body(rb_ref, o_ref):
    h = pl.program_id(0)
    shape = (MOBA_BLOCK, MOBA_BLOCK)
    far = rb_ref[NUM_BUCKETS - 1, h]
    for slot, offset, q_axis in ((TAB_PREV, MOBA_BLOCK, 0), (TAB_OWN, 0, 0),
                                 (TAB_PREV_T, MOBA_BLOCK, 1), (TAB_OWN_T, 0, 1)):
        qi = lax.broadcasted_iota(jnp.int32, shape, q_axis)
        kj = lax.broadcasted_iota(jnp.int32, shape, 1 - q_axis)
        dist = jnp.maximum(qi - kj + offset, 0)
        large = MAX_EXACT + (jnp.log(jnp.maximum(dist, 1).astype(F32) / MAX_EXACT)
                             / math.log(MAX_DISTANCE / MAX_EXACT)
                             * (NUM_BUCKETS - MAX_EXACT)).astype(jnp.int32)
        large = jnp.minimum(large, NUM_BUCKETS - 1)
        bucket = jnp.where(dist < MAX_EXACT, dist, large)
        val = jnp.zeros(shape, F32)
        for bkt in range(NUM_BUCKETS):
            val = jnp.where(bucket == bkt, rb_ref[bkt, h], val)
        val = val - far
        if slot in (TAB_PREV_T, TAB_OWN_T):
            val = val * LOG2_E
        if slot in (TAB_OWN, TAB_OWN_T):
            val = jnp.where(kj > qi, MASKED, val)
        o_ref[0, slot] = val


def _bias_tables(rel_bias):
    return pl.pallas_call(
        _bias_tables_body,
        grid=(N_HEADS,),
        in_specs=[pl.BlockSpec(memory_space=pltpu.SMEM)],
        out_specs=pl.BlockSpec((1, 4, MOBA_BLOCK, MOBA_BLOCK), lambda h: (h, 0, 0, 0)),
        out_shape=jax.ShapeDtypeStruct((N_HEADS, 4, MOBA_BLOCK, MOBA_BLOCK), F32),
        compiler_params=_params(("arbitrary",), 32),
        name="bias_tables",
    )(rel_bias)


def _block_mean_body(k_ref, o_ref):
    rows = k_ref.shape[0]
    x = k_ref[...].reshape(rows // MOBA_BLOCK, MOBA_BLOCK, k_ref.shape[1])
    o_ref[...] = jnp.sum(x, axis=1) * (1.0 / MOBA_BLOCK)


def _block_mean(k):
    t, w = k.shape
    nb = t // MOBA_BLOCK
    per = min(8, nb)
    assert nb % per == 0
    return pl.pallas_call(
        _block_mean_body,
        grid=(nb // per,),
        in_specs=[pl.BlockSpec((per * MOBA_BLOCK, w), lambda i: (i, 0))],
        out_specs=pl.BlockSpec((per, w), lambda i: (i, 0)),
        out_shape=jax.ShapeDtypeStruct((nb, w), F32),
        compiler_params=_params(("parallel",), 40),
        name="block_mean",
    )(k)


MOBA_CHUNK = 4
MOBA_HEADS_PER_STEP = 2
MOBA_PAD = MOBA_CHUNK * MOBA_BLOCK
KAUG_WIDTH = HEAD_DIM + LANES


def _moba_prep_body(k_ref, v_ref, ka_ref, vt_ref):
    i = pl.program_id(0)
    rows = k_ref.shape[0]
    col = lax.broadcasted_iota(jnp.int32, (rows, LANES), 1)

    @pl.when(i == 0)
    def _():
        pad_hot = jnp.where(col == LANES - 1, 1.0, 0.0).astype(BF16)
        for h in range(N_HEADS):
            ka_ref[:, h * KAUG_WIDTH:h * KAUG_WIDTH + HEAD_DIM] = jnp.zeros((rows, HEAD_DIM), BF16)
            ka_ref[:, h * KAUG_WIDTH + HEAD_DIM:(h + 1) * KAUG_WIDTH] = pad_hot
        vt_ref[...] = jnp.zeros(vt_ref.shape, BF16)

    @pl.when(i > 0)
    def _():
        row = lax.broadcasted_iota(jnp.int32, (rows, LANES), 0) + (i - 1) * rows
        hot = jnp.where(jnp.right_shift(row, int(math.log2(MOBA_BLOCK))) == col,
                        1.0, 0.0).astype(BF16)
        for h in range(N_HEADS):
            ka_ref[:, h * KAUG_WIDTH:h * KAUG_WIDTH + HEAD_DIM] = (
                k_ref[:, h * HEAD_DIM:(h + 1) * HEAD_DIM].astype(BF16))
            ka_ref[:, h * KAUG_WIDTH + HEAD_DIM:(h + 1) * KAUG_WIDTH] = hot
        for blk in range(rows // MOBA_BLOCK):
            vt_ref[blk] = v_ref[blk * MOBA_BLOCK:(blk + 1) * MOBA_BLOCK, :].T.astype(BF16)


def _moba_prep(k, v):
    t = k.shape[0]
    rows = MOBA_PAD
    assert t % rows == 0
    data = pl.BlockSpec((rows, ATTN_WIDTH), lambda i: (jnp.maximum(i - 1, 0), 0))
    return pl.pallas_call(
        _moba_prep_body,
        grid=(t // rows + 1,),
        in_specs=[data, data],
        out_specs=[pl.BlockSpec((rows, N_HEADS * KAUG_WIDTH), lambda i: (i, 0)),
                   pl.BlockSpec((MOBA_CHUNK, ATTN_WIDTH, MOBA_BLOCK), lambda i: (i, 0, 0))],
        out_shape=[jax.ShapeDtypeStruct((t + rows, N_HEADS * KAUG_WIDTH), BF16),
                   jax.ShapeDtypeStruct((t // MOBA_BLOCK + MOBA_CHUNK, ATTN_WIDTH, MOBA_BLOCK),
                                        BF16)],
        compiler_params=_params(("parallel",), 40),
        name="moba_prep",
    )(k, v)


SELECT_ROWS = 1024


def _moba_select_body(q_ref, km_ref, o_ref, *, n_sub):
    q = q_ref[...]
    rows = q.shape[0]
    gate = lax.dot_general(km_ref[...], q, NT_DIMS, precision=lax.Precision.HIGHEST,
                           preferred_element_type=F32)[:n_sub]
    blk = lax.broadcasted_iota(jnp.int32, gate.shape, 0)
    blk_f = blk.astype(F32)
    pos = lax.broadcasted_iota(jnp.int32, gate.shape, 1) + pl.program_id(0) * rows
    own = jnp.right_shift(pos, int(math.log2(MOBA_BLOCK)))
    gate = jnp.where(blk < own, gate, -jnp.inf)
    picked = blk == own
    for _ in range(MOBA_TOPK):
        top = jnp.max(gate, axis=0, keepdims=True)
        first = jnp.min(jnp.where(gate == top, blk_f, float(LANES)), axis=0, keepdims=True)
        chosen = blk_f == first
        picked = picked | (chosen & (top > -jnp.inf))
        gate = jnp.where(chosen, -jnp.inf, gate)
    o_ref[:HEAD_DIM, :] = (q * (HEAD_DIM ** -0.5 * LOG2_E)).T.astype(BF16)
    o_ref[HEAD_DIM:HEAD_DIM + n_sub, :] = jnp.where(picked, 0.0, MASKED).astype(BF16)
    o_ref[HEAD_DIM + n_sub:, :] = jnp.full((LANES - n_sub, rows), MASKED, BF16)


def _moba_select(q, k_mean):
    t = q.shape[0]
    nb = t // MOBA_BLOCK
    rows = min(SELECT_ROWS, t)
    n_sub = -(-nb // 16) * 16
    assert t % rows == 0 and MOBA_TOPK <= nb and n_sub < LANES
    km = jnp.pad(k_mean, ((0, LANES - nb), (0, 0)))
    return pl.pallas_call(
        functools.partial(_moba_select_body, n_sub=n_sub),
        grid=(t // rows, N_HEADS),
        in_specs=[pl.BlockSpec((rows, HEAD_DIM), lambda i, h: (i, h)),
                  pl.BlockSpec((LANES, HEAD_DIM), lambda i, h: (0, h))],
        out_specs=pl.BlockSpec((KAUG_WIDTH, rows), lambda i, h: (h, i)),
        out_shape=jax.ShapeDtypeStruct((N_HEADS * KAUG_WIDTH, t), BF16),
        compiler_params=_params(("parallel", "parallel"), 40),
        name="moba_select",
    )(q, km)


def _moba_prompt_body(pt_ref, qa_ref, qn_ref, ka_ref, vt_ref, tab_ref, ck_ref, o_ref, km_ref, s_ref,
                      page_buf, page_sem, *, pages_per_step, per_block):
    qb_idx = pl.program_id(1)
    step = pl.program_id(0) * pl.num_programs(1) + qb_idx
    n_steps = pl.num_programs(0) * pl.num_programs(1)

    def page_copies(at_step, slot):
        return [pltpu.make_async_copy(ck_ref.at[0, pt_ref[at_step * pages_per_step + j]],
                                      page_buf.at[slot, j], page_sem.at[slot])
                for j in range(pages_per_step)]

    @pl.when(step == 0)
    def _():
        for cp in page_copies(step, 0):
            cp.start()

    @pl.when(step + 1 < n_steps)
    def _():
        for cp in page_copies(step + 1, (step + 1) % 2):
            cp.start()

    cur = step % 2
    for cp in page_copies(step, cur):
        cp.wait()

    def block_means(blocks):
        for blk in blocks:
            tot = jnp.sum(page_buf[cur, blk * per_block], axis=0)
            for extra in range(1, per_block):
                tot = tot + jnp.sum(page_buf[cur, blk * per_block + extra], axis=0)
            km_ref[0, blk] = tot * (1.0 / MOBA_BLOCK)

    n_blocks = pages_per_step // per_block
    heads = range(MOBA_HEADS_PER_STEP)
    tiles = [(hh, g) for hh in heads for g in range(MOBA_CHUNK)]
    q_aug = [qa_ref[hh * KAUG_WIDTH:(hh + 1) * KAUG_WIDTH, :] for hh in heads]
    q_aug_next = [qn_ref[hh * KAUG_WIDTH:(hh + 1) * KAUG_WIDTH, :] for hh in heads]

    def logits(hh, g, first_block, queries):
        start = pl.multiple_of((first_block + g) * MOBA_BLOCK, MOBA_BLOCK)
        s_ref[hh * MOBA_CHUNK + g] = jnp.dot(
            ka_ref[pl.ds(start, MOBA_BLOCK), hh * KAUG_WIDTH:(hh + 1) * KAUG_WIDTH],
            queries[hh], preferred_element_type=F32)

    def weights(state, bias, next_first, next_queries):
        m_new, scale, l_new, p = [], [], [], {}

        def tile(hh, g):
            s = s_ref[hh * MOBA_CHUNK + g]
            return s if bias[g] is None else s + tab_ref[hh, bias[g]]

        for hh in heads:
            m, l, _ = state[hh]
            top = m
            for g in range(MOBA_CHUNK):
                top = jnp.maximum(top, jnp.max(tile(hh, g), axis=0, keepdims=True))
            sc = jnp.exp2(m - top)
            tot = sc * l
            for g in range(MOBA_CHUNK):
                e = jnp.exp2(tile(hh, g) - top)
                logits(hh, g, next_first, next_queries)
                tot = tot + jnp.sum(e, axis=0, keepdims=True)
                p[hh, g] = e.astype(BF16)
            m_new.append(top)
            scale.append(sc)
            l_new.append(tot)
        return m_new, scale, l_new, p

    def accumulate(first_block, state, stats):
        m_new, scale, l_new, p = stats
        out = []
        for hh in heads:
            acc = scale[hh] * state[hh][2]
            for g in range(MOBA_CHUNK):
                acc = acc + jnp.dot(vt_ref[first_block + g, hh * HEAD_DIM:(hh + 1) * HEAD_DIM, :],
                                    p[hh, g], preferred_element_type=F32)
            out.append((m_new[hh], l_new[hh], acc))
        return tuple(out)

    init = tuple((jnp.full((1, MOBA_BLOCK), 0.1 * MASKED, F32), jnp.zeros((1, MOBA_BLOCK), F32),
                  jnp.zeros((HEAD_DIM, MOBA_BLOCK), F32)) for _ in heads)
    first = qb_idx % MOBA_CHUNK + 1

    @pl.when(qb_idx == 0)
    def _():
        for hh, g in tiles:
            logits(hh, g, first, q_aug)

    no_bias = [None] * MOBA_CHUNK

    def trip(c, state):
        stats = weights(state, no_bias, first + (c + 1) * MOBA_CHUNK, q_aug)
        return accumulate(first + c * MOBA_CHUNK, state, stats)

    n_early = qb_idx // MOBA_CHUNK
    odd = n_early % 2
    state = lax.cond(odd == 1, lambda st: trip(0, st), lambda st: st, init)
    state = lax.fori_loop(0, n_early // 2,
                          lambda c, st: trip(odd + 2 * c + 1, trip(odd + 2 * c, st)), state)
    bias = [None] * (MOBA_CHUNK - 2) + [0, 1]
    block_means(range(0, n_blocks // 2))
    stats = weights(state, bias, (qb_idx + 1) % MOBA_CHUNK + 1, q_aug_next)
    block_means(range(n_blocks // 2, n_blocks))
    state = accumulate(qb_idx + 1, state, stats)
    for hh in heads:
        _, l, acc = state[hh]
        o_ref[:, hh * HEAD_DIM:(hh + 1) * HEAD_DIM] = (acc / l).T.astype(o_ref.dtype)


def _moba_prompt(q_aug_t, k_aug, v_t, tables, cache_k, page_table_flat):
    t = q_aug_t.shape[1]
    nb = t // MOBA_BLOCK
    hb = MOBA_HEADS_PER_STEP
    n_steps = (N_HEADS // hb) * nb
    page = cache_k.shape[2]
    per_block = MOBA_BLOCK // page
    total_pages = page_table_flat.shape[0]
    pages_per_step = total_pages // n_steps
    assert t % MOBA_BLOCK == 0 and MOBA_CHUNK >= 2 and MOBA_BLOCK % page == 0
    assert total_pages % n_steps == 0 and pages_per_step % per_block == 0
    blocks_per_step = pages_per_step // per_block
    attn, k_mean = pl.pallas_call(
        functools.partial(_moba_prompt_body, pages_per_step=pages_per_step, per_block=per_block),
        grid_spec=pltpu.PrefetchScalarGridSpec(
            num_scalar_prefetch=1,
            grid=(N_HEADS // hb, nb),
            in_specs=[
                pl.BlockSpec((hb * KAUG_WIDTH, MOBA_BLOCK), lambda h, i, pt: (h, i)),
                pl.BlockSpec((hb * KAUG_WIDTH, MOBA_BLOCK),
                             lambda h, i, pt: (h, jnp.minimum(i + 1, nb - 1))),
                pl.BlockSpec((t + MOBA_PAD, hb * KAUG_WIDTH), lambda h, i, pt: (0, h)),
                pl.BlockSpec((nb + MOBA_CHUNK, hb * HEAD_DIM, MOBA_BLOCK),
                             lambda h, i, pt: (0, h, 0)),
                pl.BlockSpec((hb, 2, MOBA_BLOCK, MOBA_BLOCK),
                             lambda h, i, pt: (h, TAB_PREV_T // 2, 0, 0)),
                pl.BlockSpec(memory_space=pl.ANY),
            ],
            out_specs=[
                pl.BlockSpec((MOBA_BLOCK, hb * HEAD_DIM), lambda h, i, pt: (i, h)),
                pl.BlockSpec((1, blocks_per_step, N_HEADS, HEAD_DIM),
                             lambda h, i, pt: (h * nb + i, 0, 0, 0)),
            ],
            scratch_shapes=[
                pltpu.VMEM((hb * MOBA_CHUNK, MOBA_BLOCK, MOBA_BLOCK), F32),
                pltpu.VMEM((2, pages_per_step, page, N_HEADS, HEAD_DIM), F32),
                pltpu.SemaphoreType.DMA((2,)),
            ],
        ),
        out_shape=[jax.ShapeDtypeStruct((t, ATTN_WIDTH), BF16),
                   jax.ShapeDtypeStruct((n_steps, blocks_per_step, N_HEADS, HEAD_DIM), F32)],
        compiler_params=_params(("arbitrary", "arbitrary"), 58),
        name="moba_prompt",
    )(page_table_flat, q_aug_t, q_aug_t, k_aug, v_t, tables, cache_k)
    return attn, k_mean.reshape(n_steps * blocks_per_step, N_HEADS, HEAD_DIM)


def _sample_select_body(q_ref, km_ref, o_ref):
    km = km_ref[0]
    gate = jnp.sum(km * q_ref[0], axis=-1, keepdims=True)
    gate = jnp.broadcast_to(gate, km.shape)
    blk_id = lax.broadcasted_iota(jnp.int32, km.shape, 0).astype(F32)
    for slot in range(MOBA_TOPK):
        top = jnp.max(gate, axis=0, keepdims=True)
        first = jnp.min(jnp.where(gate == top, blk_id, float(km.shape[0])),
                        axis=0, keepdims=True)
        o_ref[0, slot] = first[0].astype(jnp.int32)
        gate = jnp.where(blk_id == first, -jnp.inf, gate)


def _sample_select(q_heads, k_mean):
    n_req, nbp = k_mean.shape[0], k_mean.shape[1]
    assert nbp >= MOBA_TOPK
    out = pl.pallas_call(
        _sample_select_body,
        grid=(n_req,),
        in_specs=[pl.BlockSpec((1, N_HEADS, HEAD_DIM), lambda b: (b, 0, 0)),
                  pl.BlockSpec((1, nbp, N_HEADS, HEAD_DIM), lambda b: (b, 0, 0, 0))],
        out_specs=pl.BlockSpec((1, MOBA_TOPK, N_HEADS, HEAD_DIM), lambda b: (b, 0, 0, 0)),
        out_shape=jax.ShapeDtypeStruct((n_req, MOBA_TOPK, N_HEADS, HEAD_DIM), jnp.int32),
        compiler_params=_params(("parallel",), 32),
        name="sample_select",
    )(q_heads, k_mean)
    return out[:, :, :, 0]


def _sample_attn_body(pt_ref, sel_ref, q_ref, kn_ref, vn_ref, tprev_ref, town_ref, ck_ref, cv_ref,
                      qc_ref, mk_hbm, mv_hbm, o_ref, oc_ref, kbuf, vbuf, sem, mkbuf, mvbuf, msem,
                      *, n_pages, per_block):
    b = pl.program_id(0)
    page = kbuf.shape[2]
    last_block = n_pages // per_block - 1

    def copies(req, buf_slot):
        out = []
        for h in range(N_HEADS):
            for slot in range(MOBA_TOPK):
                blk = sel_ref[(req * MOBA_TOPK + slot) * N_HEADS + h]
                for i in range(per_block):
                    pg = pt_ref[req * n_pages + blk * per_block + i]
                    idx = (h * MOBA_TOPK + slot) * per_block + i
                    out.append(pltpu.make_async_copy(ck_ref.at[0, pg, :, h, :],
                                                     kbuf.at[buf_slot, idx], sem.at[buf_slot, 0]))
                    out.append(pltpu.make_async_copy(cv_ref.at[0, pg, :, h, :],
                                                     vbuf.at[buf_slot, idx], sem.at[buf_slot, 1]))
        for h in range(CROSS_HEADS):
            out.append(pltpu.make_async_copy(mk_hbm.at[0, req, :, h, :], mkbuf.at[buf_slot, h],
                                             msem.at[buf_slot, 0]))
            out.append(pltpu.make_async_copy(mv_hbm.at[0, req, :, h, :], mvbuf.at[buf_slot, h],
                                             msem.at[buf_slot, 1]))
        return out

    @pl.when(b == 0)
    def _():
        for cp in copies(b, 0):
            cp.start()

    @pl.when(b + 1 < pl.num_programs(0))
    def _():
        for cp in copies(b + 1, (b + 1) % 2):
            cp.start()

    cur = b % 2
    for cp in copies(b, cur):
        cp.wait()

    hs = [slice(h * HEAD_DIM, (h + 1) * HEAD_DIM) for h in range(N_HEADS)]
    n_tiles = MOBA_TOPK * per_block
    q = [q_ref[0, :, hs[h]] * (HEAD_DIM ** -0.5) for h in range(N_HEADS)]
    logits = {}
    for h in range(N_HEADS):
        q8 = jnp.broadcast_to(q[h], (8, HEAD_DIM)).astype(BF16)
        prev_row = tprev_ref[h, 0, 0:1, :]
        for slot in range(MOBA_TOPK):
            near = (sel_ref[(b * MOBA_TOPK + slot) * N_HEADS + h] == last_block).astype(F32)
            for i in range(per_block):
                idx = slot * per_block + i
                kt = kbuf[cur, h * n_tiles + idx].astype(BF16)
                lg = lax.dot_general(q8, kt, NT_DIMS, preferred_element_type=F32)[0:1, :]
                logits[h, idx] = lg + near * prev_row[:, i * page:(i + 1) * page]
    p_own, den, weights = {}, {}, {}
    for h in range(N_HEADS):
        l_own = (jnp.sum(q[h] * kn_ref[0, :, hs[h]], axis=1, keepdims=True)
                 + town_ref[h, 0, 0:1, 0:1])
        m = l_own
        for idx in range(n_tiles):
            m = jnp.maximum(m, jnp.max(logits[h, idx], axis=1, keepdims=True))
        p_own[h] = jnp.exp(l_own - m)
        tot = p_own[h]
        for idx in range(n_tiles):
            p = jnp.exp(logits[h, idx] - m)
            tot = tot + jnp.sum(p, axis=1, keepdims=True)
            weights[h, idx] = jnp.broadcast_to(p, (8, page)).astype(BF16)
        den[h] = tot
    for h in range(N_HEADS):
        acc = p_own[h] * vn_ref[0, :, hs[h]]
        for idx in range(n_tiles):
            acc = acc + jnp.dot(weights[h, idx], vbuf[cur, h * n_tiles + idx].astype(BF16),
                                preferred_element_type=F32)[0:1, :]
        o_ref[0, :, hs[h]] = (acc / den[h]).astype(o_ref.dtype)

    for h in range(CROSS_HEADS):
        sl = slice(h * CROSS_HEAD_DIM, (h + 1) * CROSS_HEAD_DIM)
        q8 = jnp.broadcast_to(qc_ref[0, :, sl], (8, CROSS_HEAD_DIM))
        s = lax.dot_general(q8, mkbuf[cur, h].astype(BF16), NT_DIMS,
                            preferred_element_type=F32) * (CROSS_HEAD_DIM ** -0.5)
        p = jnp.exp(s - jnp.max(s, axis=1, keepdims=True))
        tot = jnp.sum(p, axis=1, keepdims=True)
        o = jnp.dot(p.astype(BF16), mvbuf[cur, h].astype(BF16),
                    preferred_element_type=F32) / tot
        oc_ref[0, :, sl] = o[0:1, :].astype(oc_ref.dtype)


def _sample_attn(q3, kn3, vn3, qc3, cache_k, cache_v, mem_k, mem_v, page_table_flat, sel_flat,
                 tables, n_pages):
    n_req = q3.shape[0]
    page = cache_k.shape[2]
    n_mem = mem_k.shape[2]
    per_block = MOBA_BLOCK // page
    n_bufs = N_HEADS * MOBA_TOPK * per_block
    row_spec = pl.BlockSpec((1, 1, ATTN_WIDTH), lambda b, pt, sel: (b, 0, 0))
    cross_spec = pl.BlockSpec((1, 1, CROSS_WIDTH), lambda b, pt, sel: (b, 0, 0))
    any_spec = pl.BlockSpec(memory_space=pl.ANY)
    return pl.pallas_call(
        functools.partial(_sample_attn_body, n_pages=n_pages, per_block=per_block),
        grid_spec=pltpu.PrefetchScalarGridSpec(
            num_scalar_prefetch=2,
            grid=(n_req,),
            in_specs=[row_spec, row_spec, row_spec,
                      pl.BlockSpec((N_HEADS, 1, 8, MOBA_BLOCK),
                                   lambda b, pt, sel: (0, TAB_PREV, 0, 0)),
                      pl.BlockSpec((N_HEADS, 1, 8, MOBA_BLOCK),
                                   lambda b, pt, sel: (0, TAB_OWN, 0, 0)),
                      any_spec, any_spec, cross_spec, any_spec, any_spec],
            out_specs=[row_spec, cross_spec],
            scratch_shapes=[pltpu.VMEM((2, n_bufs, page, HEAD_DIM), F32),
                            pltpu.VMEM((2, n_bufs, page, HEAD_DIM), F32),
                            pltpu.SemaphoreType.DMA((2, 2)),
                            pltpu.VMEM((2, CROSS_HEADS, n_mem, CROSS_HEAD_DIM), F32),
                            pltpu.VMEM((2, CROSS_HEADS, n_mem, CROSS_HEAD_DIM), F32),
                            pltpu.SemaphoreType.DMA((2, 2))],
        ),
        out_shape=[jax.ShapeDtypeStruct((n_req, 1, ATTN_WIDTH), BF16),
                   jax.ShapeDtypeStruct((n_req, 1, CROSS_WIDTH), BF16)],
        compiler_params=_params(("arbitrary",), 40),
        name="sample_attn",
    )(page_table_flat, sel_flat, q3, kn3, vn3, tables, tables, cache_k, cache_v, qc3, mem_k, mem_v)


def _cross_prompt_body(q_ref, mk_ref, mv_ref, o_ref):
    for h in range(CROSS_HEADS):
        sl = slice(h * CROSS_HEAD_DIM, (h + 1) * CROSS_HEAD_DIM)
        s = lax.dot_general(q_ref[:, sl], mk_ref[:, sl], NT_DIMS,
                            preferred_element_type=F32) * (CROSS_HEAD_DIM ** -0.5)
        p = jnp.exp(s - jnp.max(s, axis=1, keepdims=True))
        den = jnp.sum(p, axis=1, keepdims=True)
        o = jnp.dot(p.astype(BF16), mv_ref[:, sl], preferred_element_type=F32) / den
        o_ref[:, sl] = o.astype(o_ref.dtype)


def _cross_prompt(qc, mk_bf, mv_bf, bm=512):
    t = qc.shape[0]
    n_mem = mk_bf.shape[0]
    return pl.pallas_call(
        _cross_prompt_body,
        grid=(t // bm,),
        in_specs=[pl.BlockSpec((bm, CROSS_WIDTH), lambda i: (i, 0)),
                  pl.BlockSpec((n_mem, CROSS_WIDTH), lambda i: (0, 0)),
                  pl.BlockSpec((n_mem, CROSS_WIDTH), lambda i: (0, 0))],
        out_specs=pl.BlockSpec((bm, CROSS_WIDTH), lambda i: (i, 0)),
        out_shape=jax.ShapeDtypeStruct((t, CROSS_WIDTH), BF16),
        compiler_params=_params(("parallel",), 32),
        name="cross_prompt",
    )(qc, mk_bf, mv_bf)


CONV_TILE = 256
CONV_HALO = 32
CONV_UNROLL = 4


def _conv_prompt_body(cur_ref, prev_ref, w_ref, cb_ref, g_ref, b_ref, o_ref, buf_ref, sh_ref,
                      wt_ref, y_ref):
    i = pl.program_id(0)
    has_prev = (i > 0).astype(F32)
    buf_ref[0:CONV_HALO, :] = prev_ref[CONV_TILE - CONV_HALO:CONV_TILE, :] * has_prev
    buf_ref[CONV_HALO:CONV_HALO + CONV_TILE, :] = cur_ref[...]
    first = CONV_HALO - CONV_STATE
    shifted_rows = sh_ref.shape[1]
    for tap in range(CONV_WIDTH):
        wt_ref[tap] = jnp.broadcast_to(w_ref[tap:tap + 1, :], (SUBLANES, D_CONV))
    for c in range(D_CONV // LANES):
        cs = slice(c * LANES, (c + 1) * LANES)
        for r in range(1, SUBLANES):
            sh_ref[r - 1] = buf_ref[r:r + shifted_rows, cs]
        weights = [wt_ref[tap, :, cs] for tap in range(CONV_WIDTH)]
        bias = jnp.broadcast_to(cb_ref[:, cs], (SUBLANES, LANES))

        def row_tiles(i, carry, cs=cs, weights=weights, bias=bias):
            row0 = pl.multiple_of(i * (CONV_UNROLL * SUBLANES), CONV_UNROLL * SUBLANES)
            acc = [bias] * CONV_UNROLL
            for tap in range(CONV_WIDTH):
                r = (first + tap) % SUBLANES
                for j in range(CONV_UNROLL):
                    rows = pl.ds(row0 + (j * SUBLANES + first + tap - r), SUBLANES)
                    tile = buf_ref[rows, cs] if r == 0 else sh_ref[r - 1, rows, :]
                    acc[j] = acc[j] + tile * weights[tap]
            for j in range(CONV_UNROLL):
                y_ref[pl.ds(row0 + j * SUBLANES, SUBLANES), cs] = acc[j]
            return carry

        lax.fori_loop(0, CONV_TILE // (CONV_UNROLL * SUBLANES), row_tiles, 0)
    y = _layer_norm(y_ref[...], g_ref[...], b_ref[...])
    o_ref[...] = jax.nn.silu(y).astype(o_ref.dtype)


def _conv_prompt(u, conv_w, conv_b, ln_g, ln_b):
    t = u.shape[0]
    vec = pl.BlockSpec((1, D_CONV), lambda i: (0, 0))
    return pl.pallas_call(
        _conv_prompt_body,
        grid=(t // CONV_TILE,),
        in_specs=[pl.BlockSpec((CONV_TILE, D_CONV), lambda i: (i, 0)),
                  pl.BlockSpec((CONV_TILE, D_CONV), lambda i: (jnp.maximum(i - 1, 0), 0)),
                  pl.BlockSpec((CONV_WIDTH, D_CONV), lambda i: (0, 0)),
                  vec, vec, vec],
        out_specs=pl.BlockSpec((CONV_TILE, D_CONV), lambda i: (i, 0)),
        out_shape=jax.ShapeDtypeStruct((t, D_CONV), BF16),
        scratch_shapes=[pltpu.VMEM((CONV_HALO + CONV_TILE, D_CONV), F32),
                        pltpu.VMEM((SUBLANES - 1, CONV_HALO + CONV_TILE - SUBLANES, LANES), F32),
                        pltpu.VMEM((CONV_WIDTH, SUBLANES, D_CONV), F32),
                        pltpu.VMEM((CONV_TILE, D_CONV), F32)],
        compiler_params=_params(("parallel",), 32),
        name="conv_prompt",
    )(u, u, conv_w, conv_b.reshape(1, D_CONV), ln_g.reshape(1, D_CONV), ln_b.reshape(1, D_CONV))


def _conv_sample_body(st_ref, u_ref, w_ref, cb_ref, g_ref, b_ref, o_ref):
    acc = u_ref[...] * w_ref[CONV_STATE:CONV_WIDTH, :]
    for tap in range(CONV_STATE):
        acc = acc + st_ref[tap] * w_ref[tap:tap + 1, :]
    y = _layer_norm(acc + cb_ref[...], g_ref[...], b_ref[...])
    o_ref[...] = jax.nn.silu(y).astype(o_ref.dtype)


def _conv_sample(state_t, u, conv_w, conv_b, ln_g, ln_b):
    n_req = u.shape[0]
    return pl.pallas_call(
        _conv_sample_body,
        out_shape=jax.ShapeDtypeStruct((n_req, D_CONV), BF16),
        name="conv_sample",
    )(state_t, u, conv_w, conv_b.reshape(1, D_CONV), ln_g.reshape(1, D_CONV),
      ln_b.reshape(1, D_CONV))


def kernel(x_prompt, x_sample, mem_prompt, cache_k, cache_v, state_conv, cache_mem_k, cache_mem_v,
           page_table, w_in, conv_w, conv_b, conv_ln_g, conv_ln_b, w_mem_kv, rel_bias, w_branch,
           w_out, ln1_g, ln1_b, w_gate, w_up, w_down, ln2_g, ln2_b):
    depth = w_in.shape[0]
    assert depth == 1 and x_prompt.shape[0] == 1 and x_sample.shape[1] == 1
    alpha = (2 * depth) ** 0.25
    t, d_model = x_prompt.shape[1], x_prompt.shape[2]
    n_req, n_pages = page_table.shape
    page = cache_k.shape[2]
    n_mem = mem_prompt.shape[1]
    assert (n_pages * page) % MOBA_BLOCK == 0

    sizes = (D_CONV, D_CONV, ATTN_WIDTH, ATTN_WIDTH, ATTN_WIDTH, CROSS_WIDTH, N_BRANCH * d_model)
    col_a, col_b, col_q, col_k, col_v, col_qc, col_g = (
        int(c) for c in np.concatenate([[0], np.cumsum(sizes)[:-1]]))
    w_o = w_out[0]
    w_d = w_down[0].astype(BF16)
    w_br = w_branch[0].astype(BF16)
    d_ff = w_gate.shape[2]

    tables = _bias_tables(rel_bias)

    xp = x_prompt[0]
    xs = x_sample[:, 0]
    q_p, q_s, xp_bf, xs_bf = _proj(xp, xs, [(w_in, col_q)], ATTN_WIDTH, "none", [F32],
                                   bn=ATTN_WIDTH, vmem_mib=56, keep_input=True)
    u_p, u_s = _proj(xp_bf, xs_bf, [(w_in, col_a), (w_in, col_b)], D_CONV, "glu", [F32])
    k_p, v_p, qc_p, k_s, v_s, qc_s = _proj(
        xp_bf, xs_bf, [(w_in, col_k), (w_in, col_v), (w_in, col_qc)], ATTN_WIDTH, "each",
        [F32, F32, BF16], vmem_mib=56)
    gates_p, gates_s = _proj(xp_bf, xs_bf, [(w_in, col_g)], N_BRANCH * d_model, "sigmoid",
                             [BF16], bn=1024)

    conv_y_p = _conv_prompt(u_p, conv_w[0], conv_b[0], conv_ln_g[0], conv_ln_b[0])
    k_aug, v_pad = _moba_prep(k_p, v_p)
    pt_flat = page_table.reshape(-1)
    attn_p, k_mean_s = _moba_prompt(_moba_select(q_p, _block_mean(k_p)), k_aug, v_pad, tables,
                                    cache_k, pt_flat)
    mem_bf = mem_prompt[0].astype(BF16)
    mk, mk_bf = _proj(mem_bf, None, [(w_mem_kv, 0)], CROSS_WIDTH, "none", [F32, BF16], bn=1024)
    mv, mv_bf = _proj(mem_bf, None, [(w_mem_kv, CROSS_WIDTH)], CROSS_WIDTH, "none", [F32, BF16],
                      bn=1024)
    cross_p = _cross_prompt(qc_p, mk_bf, mv_bf)

    conv_y_s = _conv_sample(state_conv[0].transpose(1, 0, 2), u_s, conv_w[0], conv_b[0],
                            conv_ln_g[0], conv_ln_b[0])
    sel = _sample_select(q_s.reshape(n_req, N_HEADS, HEAD_DIM),
                         k_mean_s.reshape(n_req, -1, N_HEADS, HEAD_DIM))
    attn_s, cross_s = _sample_attn(
        q_s.reshape(n_req, 1, ATTN_WIDTH), k_s.reshape(n_req, 1, ATTN_WIDTH),
        v_s.reshape(n_req, 1, ATTN_WIDTH), qc_s.reshape(n_req, 1, CROSS_WIDTH),
        cache_k, cache_v, cache_mem_k, cache_mem_v, pt_flat, sel.reshape(-1), tables, n_pages)

    mix_p, mix_s = _branch_mix(
        (conv_y_p, attn_p, cross_p),
        (conv_y_s, attn_s.reshape(n_req, ATTN_WIDTH), cross_s.reshape(n_req, CROSS_WIDTH)),
        w_br, gates_p, gates_s)
    h1_p, h1_p_bf, h1_s, h1_s_bf = _mm_res_ln(mix_p, mix_s, w_o, xp, xs, ln1_g[0], ln1_b[0], alpha,
                                              [F32, BF16], bm=512)
    act_p, act_s = _proj(h1_p_bf, h1_s_bf, [(w_gate, 0), (w_up, 0)], d_ff, "swiglu", [BF16])
    y_p, y_s = _mm_res_ln(act_p, act_s, w_d, h1_p, h1_s, ln2_g[0], ln2_b[0], alpha, [F32],
                          bm=512)

    new_conv_sample = jnp.concatenate([state_conv[0][:, 1:], u_s[:, None, :]], axis=1)
    return (
        y_p[None],
        y_s[:, None],
        k_p.reshape(1, 1, t, N_HEADS, HEAD_DIM),
        v_p.reshape(1, 1, t, N_HEADS, HEAD_DIM),
        u_p[t - CONV_STATE:][None, None],
        mk.reshape(1, 1, n_mem, CROSS_HEADS, CROSS_HEAD_DIM),
        mv.reshape(1, 1, n_mem, CROSS_HEADS, CROSS_HEAD_DIM),
        k_s.reshape(1, n_req, 1, N_HEADS, HEAD_DIM),
        v_s.reshape(1, n_req, 1, N_HEADS, HEAD_DIM),
        new_conv_sample[None],
    )
```

```python
import functools
import math

import numpy as np
import jax
import jax.numpy as jnp
from jax import lax
from jax.experimental import pallas as pl
from jax.experimental.pallas import tpu as pltpu

F32 = jnp.float32
BF16 = jnp.bfloat16

N_HEADS = 8
HEAD_DIM = 128
ATTN_WIDTH = N_HEADS * HEAD_DIM
MOBA_BLOCK = 256
MOBA_TOPK = 3
D_CONV = 1024
CONV_WIDTH = 31
CONV_STATE = CONV_WIDTH - 1
CROSS_HEADS = 4
CROSS_HEAD_DIM = 256
CROSS_WIDTH = CROSS_HEADS * CROSS_HEAD_DIM
N_BRANCH = 3
BRANCH_WIDTH = 1024
NUM_BUCKETS = 32
MAX_EXACT = NUM_BUCKETS // 2
MAX_DISTANCE = 128
LN_EPS = 1e-5
MASKED = -1e30
LANES = 128
SUBLANES = 8
MIB = 2 ** 20
LOG2_E = math.log2(math.e)
LN_CHUNK_ROWS = 256

NT_DIMS = (((1,), (1,)), ((), ()))


def _params(semantics, vmem_mib):
    return pltpu.CompilerParams(dimension_semantics=semantics,
                                vmem_limit_bytes=vmem_mib * MIB)


def _layer_norm(y, g, b):
    mu = jnp.mean(y, axis=-1, keepdims=True)
    yc = y - mu
    var = jnp.mean(yc * yc, axis=-1, keepdims=True)
    return yc * lax.rsqrt(var + LN_EPS) * g + b


def _proj_body(*refs, n_w, n_out, epilogue, tail, keep_input):
    x_ref = refs[0]
    refs = refs[1:]
    if tail:
        xs_ref = refs[0]
        refs = refs[1:]
    w_refs = refs[:n_w]
    o_refs = refs[n_w:n_w + n_out]
    refs = refs[n_w + n_out:]
    if tail:
        os_refs = refs[:n_out]
        refs = refs[n_out:]
    if keep_input:
        x_copy_ref = refs[0]
        refs = refs[1:]
        if tail:
            xs_copy_ref = refs[0]
            refs = refs[1:]
    wbf_refs = refs
    i = pl.program_id(1)

    @pl.when(i == 0)
    def _():
        for w, wbf in zip(w_refs, wbf_refs):
            wbf[...] = w[...].astype(BF16)

    def apply(x, outs):
        z = [jnp.dot(x, w[...], preferred_element_type=F32) for w in wbf_refs]
        if epilogue == "each":
            for o, zn in zip(outs, z):
                o[...] = zn.astype(o.dtype)
            return
        if epilogue == "glu":
            y = z[0] * jax.nn.sigmoid(z[1])
        elif epilogue == "swiglu":
            y = jax.nn.silu(z[0]) * z[1]
        elif epilogue == "sigmoid":
            y = jax.nn.sigmoid(z[0])
        else:
            y = z[0]
        for o in outs:
            o[...] = y.astype(o.dtype)

    x = x_ref[...].astype(BF16)
    if keep_input:
        x_copy_ref[...] = x
    apply(x, o_refs)

    if tail:
        @pl.when(i == pl.num_programs(1) - 1)
        def _():
            xs = xs_ref[...].astype(BF16)
            if keep_input:
                xs_copy_ref[...] = xs
            apply(xs, os_refs)


def _proj(x, xs, ws, n, epilogue, out_dtypes, bm=1024, bn=512, vmem_mib=48, keep_input=False):
    m, k = x.shape
    tail = xs is not None
    bm = min(bm, m)
    bn = min(bn, n)
    while n % bn or any(c0 % bn for _, c0 in ws):
        bn //= 2
    assert m % bm == 0 and bn % LANES == 0
    assert not keep_input or bn == n
    in_specs = [pl.BlockSpec((bm, k), lambda j, i: (i, 0))]
    out_specs = [pl.BlockSpec((bm, bn), lambda j, i: (i, j)) for _ in out_dtypes]
    out_shape = [jax.ShapeDtypeStruct((m, n), dt) for dt in out_dtypes]
    operands = [x]
    if tail:
        ms = xs.shape[0]
        in_specs.append(pl.BlockSpec((ms, k), lambda j, i: (0, 0)))
        out_specs += [pl.BlockSpec((ms, bn), lambda j, i: (0, j)) for _ in out_dtypes]
        out_shape += [jax.ShapeDtypeStruct((ms, n), dt) for dt in out_dtypes]
        operands.append(xs)
    if keep_input:
        out_specs.append(pl.BlockSpec((bm, k), lambda j, i: (i, 0)))
        out_shape.append(jax.ShapeDtypeStruct((m, k), BF16))
        if tail:
            out_specs.append(pl.BlockSpec((ms, k), lambda j, i: (0, 0)))
            out_shape.append(jax.ShapeDtypeStruct((ms, k), BF16))
    in_specs += [pl.BlockSpec((None, k, bn), lambda j, i, c=c0 // bn: (0, 0, c + j))
                 for _, c0 in ws]
    return pl.pallas_call(
        functools.partial(_proj_body, n_w=len(ws), n_out=len(out_dtypes), epilogue=epilogue,
                          tail=tail, keep_input=keep_input),
        grid=(n // bn, m // bm),
        in_specs=in_specs,
        out_specs=out_specs,
        out_shape=out_shape,
        scratch_shapes=[pltpu.VMEM((k, bn), BF16) for _ in ws],
        compiler_params=_params(("arbitrary" if keep_input else "parallel", "arbitrary"), vmem_mib),
        name="proj_" + epilogue,
    )(*operands, *[w for w, _ in ws])


def _mm_res_ln_body(a_ref, as_ref, w_in_ref, r_ref, rs_ref, g_ref, b_ref, *o_refs, alpha, n_out,
                    cast_weight):
    if cast_weight:
        w_ref = o_refs[2 * n_out]

        @pl.when(pl.program_id(0) == 0)
        def _():
            w_ref[...] = w_in_ref[...].astype(BF16)
    else:
        w_ref = w_in_ref

    def apply(a, res, outs):
        rows = a.shape[0]
        chunk = min(rows, LN_CHUNK_ROWS)
        for r0 in range(0, rows, chunk):
            rs = slice(r0, r0 + chunk)
            y = alpha * res[rs, :] + jnp.dot(a[rs, :], w_ref[...], preferred_element_type=F32)
            out = _layer_norm(y, g_ref[...], b_ref[...])
            for o in outs:
                o[rs, :] = out.astype(o.dtype)

    apply(a_ref, r_ref, o_refs[:n_out])

    @pl.when(pl.program_id(0) == pl.num_programs(0) - 1)
    def _():
        apply(as_ref, rs_ref, o_refs[n_out:2 * n_out])


def _mm_res_ln(a, a_s, w, res, res_s, g, b, alpha, out_dtypes, bm, vmem_mib=58):
    m, k = a.shape
    ms = a_s.shape[0]
    n = w.shape[1]
    bm = min(bm, m)
    assert m % bm == 0
    cast_weight = w.dtype != BF16
    return pl.pallas_call(
        functools.partial(_mm_res_ln_body, alpha=alpha, n_out=len(out_dtypes),
                          cast_weight=cast_weight),
        scratch_shapes=[pltpu.VMEM((k, n), BF16)] if cast_weight else [],
        grid=(m // bm,),
        in_specs=[
            pl.BlockSpec((bm, k), lambda i: (i, 0)),
            pl.BlockSpec((ms, k), lambda i: (0, 0)),
            pl.BlockSpec((k, n), lambda i: (0, 0), pipeline_mode=pl.Buffered(1)),
            pl.BlockSpec((bm, n), lambda i: (i, 0)),
            pl.BlockSpec((ms, n), lambda i: (0, 0)),
            pl.BlockSpec((1, n), lambda i: (0, 0)),
            pl.BlockSpec((1, n), lambda i: (0, 0)),
        ],
        out_specs=[pl.BlockSpec((bm, n), lambda i: (i, 0)) for _ in out_dtypes]
        + [pl.BlockSpec((ms, n), lambda i: (0, 0)) for _ in out_dtypes],
        out_shape=[jax.ShapeDtypeStruct((m, n), dt) for dt in out_dtypes]
        + [jax.ShapeDtypeStruct((ms, n), dt) for dt in out_dtypes],
        compiler_params=_params(("arbitrary",), vmem_mib),
        name="mm_res_ln",
    )(a, a_s, w, res, res_s, g.reshape(1, n), b.reshape(1, n))


def _branch_body(*refs):
    nb = N_BRANCH
    br_refs, brs_refs, w_refs = refs[0:nb], refs[nb:2 * nb], refs[2 * nb:3 * nb]
    g_refs, gs_refs = refs[3 * nb:4 * nb], refs[4 * nb:5 * nb]
    o_ref, os_ref = refs[5 * nb], refs[5 * nb + 1]

    def mix(branches, gates, out):
        y = None
        for br, w, g in zip(branches, w_refs, gates):
            term = g[...] * jnp.dot(br[...], w[...], preferred_element_type=F32)
            y = term if y is None else y + term
        out[...] = y.astype(out.dtype)

    mix(br_refs, g_refs, o_ref)

    @pl.when(pl.program_id(1) == pl.num_programs(1) - 1)
    def _():
        mix(brs_refs, gs_refs, os_ref)


def _branch_mix(branches, branches_s, w_branch, gates, gates_s, bm=1024, bn=1024):
    m = branches[0].shape[0]
    ms = branches_s[0].shape[0]
    d = w_branch.shape[2]
    bm = min(bm, m)
    bn = min(bn, d)
    nj = d // bn
    ids = range(N_BRANCH)
    return pl.pallas_call(
        _branch_body,
        grid=(nj, m // bm),
        in_specs=[pl.BlockSpec((bm, BRANCH_WIDTH), lambda j, i: (i, 0)) for _ in ids]
        + [pl.BlockSpec((ms, BRANCH_WIDTH), lambda j, i: (0, 0)) for _ in ids]
        + [pl.BlockSpec((None, BRANCH_WIDTH, bn), lambda j, i, n=n: (n, 0, j)) for n in ids]
        + [pl.BlockSpec((bm, bn), lambda j, i, n=n: (i, n * nj + j)) for n in ids]
        + [pl.BlockSpec((ms, bn), lambda j, i, n=n: (0, n * nj + j)) for n in ids],
        out_specs=[pl.BlockSpec((bm, bn), lambda j, i: (i, j)),
                   pl.BlockSpec((ms, bn), lambda j, i: (0, j))],
        out_shape=[jax.ShapeDtypeStruct((m, d), BF16), jax.ShapeDtypeStruct((ms, d), BF16)],
        compiler_params=_params(("parallel", "arbitrary"), 48),
        name="branch_mix",
    )(*branches, *branches_s, w_branch, w_branch, w_branch, gates, gates, gates,
      gates_s, gates_s, gates_s)


TAB_PREV = 0
TAB_OWN = 1
TAB_PREV_T = 2
TAB_OWN_T = 3


def _bias_tables_body(rb_ref, o_ref):
    h = pl.program_id(0)
    shape = (MOBA_BLOCK, MOBA_BLOCK)
    far = rb_ref[NUM_BUCKETS - 1, h]
    for slot, offset, q_axis in ((TAB_PREV, MOBA_BLOCK, 0), (TAB_OWN, 0, 0),
                                 (TAB_PREV_T, MOBA_BLOCK, 1), (TAB_OWN_T, 0, 1)):
        qi = lax.broadcasted_iota(jnp.int32, shape, q_axis)
        kj = lax.broadcasted_iota(jnp.int32, shape, 1 - q_axis)
        dist = jnp.maximum(qi - kj + offset, 0)
        large = MAX_EXACT + (jnp.log(jnp.maximum(dist, 1).astype(F32) / MAX_EXACT)
                             / math.log(MAX_DISTANCE / MAX_EXACT)
                             * (NUM_BUCKETS - MAX_EXACT)).astype(jnp.int32)
        large = jnp.minimum(large, NUM_BUCKETS - 1)
        bucket = jnp.where(dist < MAX_EXACT, dist, large)
        val = jnp.zeros(shape, F32)
        for bkt in range(NUM_BUCKETS):
            val = jnp.where(bucket == bkt, rb_ref[bkt, h], val)
        val = val - far
        if slot in (TAB_PREV_T, TAB_OWN_T):
            val = val * LOG2_E
        if slot in (TAB_OWN, TAB_OWN_T):
            val = jnp.where(kj > qi, MASKED, val)
        o_ref[0, slot] = val


def _bias_tables(rel_bias):
    return pl.pallas_call(
        _bias_tables_body,
        grid=(N_HEADS,),
        in_specs=[pl.BlockSpec(memory_space=pltpu.SMEM)],
        out_specs=pl.BlockSpec((1, 4, MOBA_BLOCK, MOBA_BLOCK), lambda h: (h, 0, 0, 0)),
        out_shape=jax.ShapeDtypeStruct((N_HEADS, 4, MOBA_BLOCK, MOBA_BLOCK), F32),
        compiler_params=_params(("arbitrary",), 32),
        name="bias_tables",
    )(rel_bias)


def _block_mean_body(k_ref, o_ref):
    rows = k_ref.shape[0]
    x = k_ref[...].reshape(rows // MOBA_BLOCK, MOBA_BLOCK, k_ref.shape[1])
    o_ref[...] = jnp.sum(x, axis=1) * (1.0 / MOBA_BLOCK)


def _block_mean(k):
    t, w = k.shape
    nb = t // MOBA_BLOCK
    per = min(8, nb)
    assert nb % per == 0
    return pl.pallas_call(
        _block_mean_body,
        grid=(nb // per,),
        in_specs=[pl.BlockSpec((per * MOBA_BLOCK, w), lambda i: (i, 0))],
        out_specs=pl.BlockSpec((per, w), lambda i: (i, 0)),
        out_shape=jax.ShapeDtypeStruct((nb, w), F32),
        compiler_params=_params(("parallel",), 40),
        name="block_mean",
    )(k)


MOBA_CHUNK = 4
MOBA_HEADS_PER_STEP = 2
MOBA_PAD = MOBA_CHUNK * MOBA_BLOCK
KAUG_WIDTH = HEAD_DIM + LANES


def _moba_prep_body(k_ref, v_ref, ka_ref, vt_ref):
    i = pl.program_id(0)
    rows = k_ref.shape[0]
    col = lax.broadcasted_iota(jnp.int32, (rows, LANES), 1)

    @pl.when(i == 0)
    def _():
        pad_hot = jnp.where(col == LANES - 1, 1.0, 0.0).astype(BF16)
        for h in range(N_HEADS):
            ka_ref[:, h * KAUG_WIDTH:h * KAUG_WIDTH + HEAD_DIM] = jnp.zeros((rows, HEAD_DIM), BF16)
            ka_ref[:, h * KAUG_WIDTH + HEAD_DIM:(h + 1) * KAUG_WIDTH] = pad_hot
        vt_ref[...] = jnp.zeros(vt_ref.shape, BF16)

    @pl.when(i > 0)
    def _():
        row = lax.broadcasted_iota(jnp.int32, (rows, LANES), 0) + (i - 1) * rows
        hot = jnp.where(jnp.right_shift(row, int(math.log2(MOBA_BLOCK))) == col,
                        1.0, 0.0).astype(BF16)
        for h in range(N_HEADS):
            ka_ref[:, h * KAUG_WIDTH:h * KAUG_WIDTH + HEAD_DIM] = (
                k_ref[:, h * HEAD_DIM:(h + 1) * HEAD_DIM].astype(BF16))
            ka_ref[:, h * KAUG_WIDTH + HEAD_DIM:(h + 1) * KAUG_WIDTH] = hot
        for blk in range(rows // MOBA_BLOCK):
            vt_ref[blk] = v_ref[blk * MOBA_BLOCK:(blk + 1) * MOBA_BLOCK, :].T.astype(BF16)


def _moba_prep(k, v):
    t = k.shape[0]
    rows = MOBA_PAD
    assert t % rows == 0
    data = pl.BlockSpec((rows, ATTN_WIDTH), lambda i: (jnp.maximum(i - 1, 0), 0))
    return pl.pallas_call(
        _moba_prep_body,
        grid=(t // rows + 1,),
        in_specs=[data, data],
        out_specs=[pl.BlockSpec((rows, N_HEADS * KAUG_WIDTH), lambda i: (i, 0)),
                   pl.BlockSpec((MOBA_CHUNK, ATTN_WIDTH, MOBA_BLOCK), lambda i: (i, 0, 0))],
        out_shape=[jax.ShapeDtypeStruct((t + rows, N_HEADS * KAUG_WIDTH), BF16),
                   jax.ShapeDtypeStruct((t // MOBA_BLOCK + MOBA_CHUNK, ATTN_WIDTH, MOBA_BLOCK),
                                        BF16)],
        compiler_params=_params(("parallel",), 40),
        name="moba_prep",
    )(k, v)


SELECT_ROWS = 1024


def _moba_select_body(q_ref, km_ref, o_ref, *, n_sub):
    q = q_ref[...]
    rows = q.shape[0]
    gate = lax.dot_general(km_ref[...], q, NT_DIMS, precision=lax.Precision.HIGHEST,
                           preferred_element_type=F32)[:n_sub]
    blk = lax.broadcasted_iota(jnp.int32, gate.shape, 0)
    blk_f = blk.astype(F32)
    pos = lax.broadcasted_iota(jnp.int32, gate.shape, 1) + pl.program_id(0) * rows
    own = jnp.right_shift(pos, int(math.log2(MOBA_BLOCK)))
    gate = jnp.where(blk < own, gate, -jnp.inf)
    picked = blk == own
    for _ in range(MOBA_TOPK):
        top = jnp.max(gate, axis=0, keepdims=True)
        first = jnp.min(jnp.where(gate == top, blk_f, float(LANES)), axis=0, keepdims=True)
        chosen = blk_f == first
        picked = picked | (chosen & (top > -jnp.inf))
        gate = jnp.where(chosen, -jnp.inf, gate)
    o_ref[:HEAD_DIM, :] = (q * (HEAD_DIM ** -0.5 * LOG2_E)).T.astype(BF16)
    o_ref[HEAD_DIM:HEAD_DIM + n_sub, :] = jnp.where(picked, 0.0, MASKED).astype(BF16)
    o_ref[HEAD_DIM + n_sub:, :] = jnp.full((LANES - n_sub, rows), MASKED, BF16)


def _moba_select(q, k_mean):
    t = q.shape[0]
    nb = t // MOBA_BLOCK
    rows = min(SELECT_ROWS, t)
    n_sub = -(-nb // 16) * 16
    assert t % rows == 0 and MOBA_TOPK <= nb and n_sub < LANES
    km = jnp.pad(k_mean, ((0, LANES - nb), (0, 0)))
    return pl.pallas_call(
        functools.partial(_moba_select_body, n_sub=n_sub),
        grid=(t // rows, N_HEADS),
        in_specs=[pl.BlockSpec((rows, HEAD_DIM), lambda i, h: (i, h)),
                  pl.BlockSpec((LANES, HEAD_DIM), lambda i, h: (0, h))],
        out_specs=pl.BlockSpec((KAUG_WIDTH, rows), lambda i, h: (h, i)),
        out_shape=jax.ShapeDtypeStruct((N_HEADS * KAUG_WIDTH, t), BF16),
        compiler_params=_params(("parallel", "parallel"), 40),
        name="moba_select",
    )(q, km)


def _moba_prompt_body(pt_ref, qa_ref, qn_ref, ka_ref, vt_ref, tab_ref, ck_ref, o_ref, km_ref, s_ref,
                      page_buf, page_sem, *, pages_per_step, per_block):
    qb_idx = pl.program_id(1)
    step = pl.program_id(0) * pl.num_programs(1) + qb_idx
    n_steps = pl.num_programs(0) * pl.num_programs(1)

    def page_copies(at_step, slot):
        return [pltpu.make_async_copy(ck_ref.at[0, pt_ref[at_step * pages_per_step + j]],
                                      page_buf.at[slot, j], page_sem.at[slot])
                for j in range(pages_per_step)]

    @pl.when(step == 0)
    def _():
        for cp in page_copies(step, 0):
            cp.start()

    @pl.when(step + 1 < n_steps)
    def _():
        for cp in page_copies(step + 1, (step + 1) % 2):
            cp.start()

    cur = step % 2
    for cp in page_copies(step, cur):
        cp.wait()

    def block_means(blocks):
        for blk in blocks:
            tot = jnp.sum(page_buf[cur, blk * per_block], axis=0)
            for extra in range(1, per_block):
                tot = tot + jnp.sum(page_buf[cur, blk * per_block + extra], axis=0)
            km_ref[0, blk] = tot * (1.0 / MOBA_BLOCK)

    n_blocks = pages_per_step // per_block
    heads = range(MOBA_HEADS_PER_STEP)
    tiles = [(hh, g) for hh in heads for g in range(MOBA_CHUNK)]
    q_aug = [qa_ref[hh * KAUG_WIDTH:(hh + 1) * KAUG_WIDTH, :] for hh in heads]
    q_aug_next = [qn_ref[hh * KAUG_WIDTH:(hh + 1) * KAUG_WIDTH, :] for hh in heads]

    def logits(hh, g, first_block, queries):
        start = pl.multiple_of((first_block + g) * MOBA_BLOCK, MOBA_BLOCK)
        s_ref[hh * MOBA_CHUNK + g] = jnp.dot(
            ka_ref[pl.ds(start, MOBA_BLOCK), hh * KAUG_WIDTH:(hh + 1) * KAUG_WIDTH],
            queries[hh], preferred_element_type=F32)

    def weights(state, bias, next_first, next_queries):
        m_new, scale, l_new, p = [], [], [], {}

        def tile(hh, g):
            s = s_ref[hh * MOBA_CHUNK + g]
            return s if bias[g] is None else s + tab_ref[hh, bias[g]]

        for hh in heads:
            m, l, _ = state[hh]
            top = m
            for g in range(MOBA_CHUNK):
                top = jnp.maximum(top, jnp.max(tile(hh, g), axis=0, keepdims=True))
            sc = jnp.exp2(m - top)
            tot = sc * l
            for g in range(MOBA_CHUNK):
                e = jnp.exp2(tile(hh, g) - top)
                logits(hh, g, next_first, next_queries)
                tot = tot + jnp.sum(e, axis=0, keepdims=True)
                p[hh, g] = e.astype(BF16)
            m_new.append(top)
            scale.append(sc)
            l_new.append(tot)
        return m_new, scale, l_new, p

    def accumulate(first_block, state, stats):
        m_new, scale, l_new, p = stats
        out = []
        for hh in heads:
            acc = scale[hh] * state[hh][2]
            for g in range(MOBA_CHUNK):
                acc = acc + jnp.dot(vt_ref[first_block + g, hh * HEAD_DIM:(hh + 1) * HEAD_DIM, :],
                                    p[hh, g], preferred_element_type=F32)
            out.append((m_new[hh], l_new[hh], acc))
        return tuple(out)

    init = tuple((jnp.full((1, MOBA_BLOCK), 0.1 * MASKED, F32), jnp.zeros((1, MOBA_BLOCK), F32),
                  jnp.zeros((HEAD_DIM, MOBA_BLOCK), F32)) for _ in heads)
    first = qb_idx % MOBA_CHUNK + 1

    @pl.when(qb_idx == 0)
    def _():
        for hh, g in tiles:
            logits(hh, g, first, q_aug)

    no_bias = [None] * MOBA_CHUNK

    def trip(c, state):
        stats = weights(state, no_bias, first + (c + 1) * MOBA_CHUNK, q_aug)
        return accumulate(first + c * MOBA_CHUNK, state, stats)

    n_early = qb_idx // MOBA_CHUNK
    odd = n_early % 2
    state = lax.cond(odd == 1, lambda st: trip(0, st), lambda st: st, init)
    state = lax.fori_loop(0, n_early // 2,
                          lambda c, st: trip(odd + 2 * c + 1, trip(odd + 2 * c, st)), state)
    bias = [None] * (MOBA_CHUNK - 2) + [0, 1]
    block_means(range(0, n_blocks // 2))
    stats = weights(state, bias, (qb_idx + 1) % MOBA_CHUNK + 1, q_aug_next)
    block_means(range(n_blocks // 2, n_blocks))
    state = accumulate(qb_idx + 1, state, stats)
    for hh in heads:
        _, l, acc = state[hh]
        o_ref[:, hh * HEAD_DIM:(hh + 1) * HEAD_DIM] = (acc / l).T.astype(o_ref.dtype)


def _moba_prompt(q_aug_t, k_aug, v_t, tables, cache_k, page_table_flat):
    t = q_aug_t.shape[1]
    nb = t // MOBA_BLOCK
    hb = MOBA_HEADS_PER_STEP
    n_steps = (N_HEADS // hb) * nb
    page = cache_k.shape[2]
    per_block = MOBA_BLOCK // page
    total_pages = page_table_flat.shape[0]
    pages_per_step = total_pages // n_steps
    assert t % MOBA_BLOCK == 0 and MOBA_CHUNK >= 2 and MOBA_BLOCK % page == 0
    assert total_pages % n_steps == 0 and pages_per_step % per_block == 0
    blocks_per_step = pages_per_step // per_block
    attn, k_mean = pl.pallas_call(
        functools.partial(_moba_prompt_body, pages_per_step=pages_per_step, per_block=per_block),
        grid_spec=pltpu.PrefetchScalarGridSpec(
            num_scalar_prefetch=1,
            grid=(N_HEADS // hb, nb),
            in_specs=[
                pl.BlockSpec((hb * KAUG_WIDTH, MOBA_BLOCK), lambda h, i, pt: (h, i)),
                pl.BlockSpec((hb * KAUG_WIDTH, MOBA_BLOCK),
                             lambda h, i, pt: (h, jnp.minimum(i + 1, nb - 1))),
                pl.BlockSpec((t + MOBA_PAD, hb * KAUG_WIDTH), lambda h, i, pt: (0, h)),
                pl.BlockSpec((nb + MOBA_CHUNK, hb * HEAD_DIM, MOBA_BLOCK),
                             lambda h, i, pt: (0, h, 0)),
                pl.BlockSpec((hb, 2, MOBA_BLOCK, MOBA_BLOCK),
                             lambda h, i, pt: (h, TAB_PREV_T // 2, 0, 0)),
                pl.BlockSpec(memory_space=pl.ANY),
            ],
            out_specs=[
                pl.BlockSpec((MOBA_BLOCK, hb * HEAD_DIM), lambda h, i, pt: (i, h)),
                pl.BlockSpec((1, blocks_per_step, N_HEADS, HEAD_DIM),
                             lambda h, i, pt: (h * nb + i, 0, 0, 0)),
            ],
            scratch_shapes=[
                pltpu.VMEM((hb * MOBA_CHUNK, MOBA_BLOCK, MOBA_BLOCK), F32),
                pltpu.VMEM((2, pages_per_step, page, N_HEADS, HEAD_DIM), F32),
                pltpu.SemaphoreType.DMA((2,)),
            ],
        ),
        out_shape=[jax.ShapeDtypeStruct((t, ATTN_WIDTH), BF16),
                   jax.ShapeDtypeStruct((n_steps, blocks_per_step, N_HEADS, HEAD_DIM), F32)],
        compiler_params=_params(("arbitrary", "arbitrary"), 58),
        name="moba_prompt",
    )(page_table_flat, q_aug_t, q_aug_t, k_aug, v_t, tables, cache_k)
    return attn, k_mean.reshape(n_steps * blocks_per_step, N_HEADS, HEAD_DIM)


def _sample_select_body(q_ref, km_ref, o_ref):
    km = km_ref[0]
    gate = jnp.sum(km * q_ref[0], axis=-1, keepdims=True)
    gate = jnp.broadcast_to(gate, km.shape)
    blk_id = lax.broadcasted_iota(jnp.int32, km.shape, 0).astype(F32)
    for slot in range(MOBA_TOPK):
        top = jnp.max(gate, axis=0, keepdims=True)
        first = jnp.min(jnp.where(gate == top, blk_id, float(km.shape[0])),
                        axis=0, keepdims=True)
        o_ref[0, slot] = first[0].astype(jnp.int32)
        gate = jnp.where(blk_id == first, -jnp.inf, gate)


def _sample_select(q_heads, k_mean):
    n_req, nbp = k_mean.shape[0], k_mean.shape[1]
    assert nbp >= MOBA_TOPK
    out = pl.pallas_call(
        _sample_select_body,
        grid=(n_req,),
        in_specs=[pl.BlockSpec((1, N_HEADS, HEAD_DIM), lambda b: (b, 0, 0)),
                  pl.BlockSpec((1, nbp, N_HEADS, HEAD_DIM), lambda b: (b, 0, 0, 0))],
        out_specs=pl.BlockSpec((1, MOBA_TOPK, N_HEADS, HEAD_DIM), lambda b: (b, 0, 0, 0)),
        out_shape=jax.ShapeDtypeStruct((n_req, MOBA_TOPK, N_HEADS, HEAD_DIM), jnp.int32),
        compiler_params=_params(("parallel",), 32),
        name="sample_select",
    )(q_heads, k_mean)
    return out[:, :, :, 0]


def _sample_attn_body(pt_ref, sel_ref, q_ref, kn_ref, vn_ref, tprev_ref, town_ref, ck_ref, cv_ref,
                      qc_ref, mk_hbm, mv_hbm, o_ref, oc_ref, kbuf, vbuf, sem, mkbuf, mvbuf, msem,
                      *, n_pages, per_block):
    b = pl.program_id(0)
    page = kbuf.shape[2]
    last_block = n_pages // per_block - 1

    def copies(req, buf_slot):
        out = []
        for h in range(N_HEADS):
            for slot in range(MOBA_TOPK):
                blk = sel_ref[(req * MOBA_TOPK + slot) * N_HEADS + h]
                for i in range(per_block):
                    pg = pt_ref[req * n_pages + blk * per_block + i]
                    idx = (h * MOBA_TOPK + slot) * per_block + i
                    out.append(pltpu.make_async_copy(ck_ref.at[0, pg, :, h, :],
                                                     kbuf.at[buf_slot, idx], sem.at[buf_slot, 0]))
                    out.append(pltpu.make_async_copy(cv_ref.at[0, pg, :, h, :],
                                                     vbuf.at[buf_slot, idx], sem.at[buf_slot, 1]))
        for h in range(CROSS_HEADS):
            out.append(pltpu.make_async_copy(mk_hbm.at[0, req, :, h, :], mkbuf.at[buf_slot, h],
                                             msem.at[buf_slot, 0]))
            out.append(pltpu.make_async_copy(mv_hbm.at[0, req, :, h, :], mvbuf.at[buf_slot, h],
                                             msem.at[buf_slot, 1]))
        return out

    @pl.when(b == 0)
    def _():
        for cp in copies(b, 0):
            cp.start()

    @pl.when(b + 1 < pl.num_programs(0))
    def _():
        for cp in copies(b + 1, (b + 1) % 2):
            cp.start()

    cur = b % 2
    for cp in copies(b, cur):
        cp.wait()

    hs = [slice(h * HEAD_DIM, (h + 1) * HEAD_DIM) for h in range(N_HEADS)]
    n_tiles = MOBA_TOPK * per_block
    q = [q_ref[0, :, hs[h]] * (HEAD_DIM ** -0.5) for h in range(N_HEADS)]
    logits = {}
    for h in range(N_HEADS):
        q8 = jnp.broadcast_to(q[h], (8, HEAD_DIM)).astype(BF16)
        prev_row = tprev_ref[h, 0, 0:1, :]
        for slot in range(MOBA_TOPK):
            near = (sel_ref[(b * MOBA_TOPK + slot) * N_HEADS + h] == last_block).astype(F32)
            for i in range(per_block):
                idx = slot * per_block + i
                kt = kbuf[cur, h * n_tiles + idx].astype(BF16)
                lg = lax.dot_general(q8, kt, NT_DIMS, preferred_element_type=F32)[0:1, :]
                logits[h, idx] = lg + near * prev_row[:, i * page:(i + 1) * page]
    p_own, den, weights = {}, {}, {}
    for h in range(N_HEADS):
        l_own = (jnp.sum(q[h] * kn_ref[0, :, hs[h]], axis=1, keepdims=True)
                 + town_ref[h, 0, 0:1, 0:1])
        m = l_own
        for idx in range(n_tiles):
            m = jnp.maximum(m, jnp.max(logits[h, idx], axis=1, keepdims=True))
        p_own[h] = jnp.exp(l_own - m)
        tot = p_own[h]
        for idx in range(n_tiles):
            p = jnp.exp(logits[h, idx] - m)
            tot = tot + jnp.sum(p, axis=1, keepdims=True)
            weights[h, idx] = jnp.broadcast_to(p, (8, page)).astype(BF16)
        den[h] = tot
    for h in range(N_HEADS):
        acc = p_own[h] * vn_ref[0, :, hs[h]]
        for idx in range(n_tiles):
            acc = acc + jnp.dot(weights[h, idx], vbuf[cur, h * n_tiles + idx].astype(BF16),
                                preferred_element_type=F32)[0:1, :]
        o_ref[0, :, hs[h]] = (acc / den[h]).astype(o_ref.dtype)

    for h in range(CROSS_HEADS):
        sl = slice(h * CROSS_HEAD_DIM, (h + 1) * CROSS_HEAD_DIM)
        q8 = jnp.broadcast_to(qc_ref[0, :, sl], (8, CROSS_HEAD_DIM))
        s = lax.dot_general(q8, mkbuf[cur, h].astype(BF16), NT_DIMS,
                            preferred_element_type=F32) * (CROSS_HEAD_DIM ** -0.5)
        p = jnp.exp(s - jnp.max(s, axis=1, keepdims=True))
        tot = jnp.sum(p, axis=1, keepdims=True)
        o = jnp.dot(p.astype(BF16), mvbuf[cur, h].astype(BF16),
                    preferred_element_type=F32) / tot
        oc_ref[0, :, sl] = o[0:1, :].astype(oc_ref.dtype)


def _sample_attn(q3, kn3, vn3, qc3, cache_k, cache_v, mem_k, mem_v, page_table_flat, sel_flat,
                 tables, n_pages):
    n_req = q3.shape[0]
    page = cache_k.shape[2]
    n_mem = mem_k.shape[2]
    per_block = MOBA_BLOCK // page
    n_bufs = N_HEADS * MOBA_TOPK * per_block
    row_spec = pl.BlockSpec((1, 1, ATTN_WIDTH), lambda b, pt, sel: (b, 0, 0))
    cross_spec = pl.BlockSpec((1, 1, CROSS_WIDTH), lambda b, pt, sel: (b, 0, 0))
    any_spec = pl.BlockSpec(memory_space=pl.ANY)
    return pl.pallas_call(
        functools.partial(_sample_attn_body, n_pages=n_pages, per_block=per_block),
        grid_spec=pltpu.PrefetchScalarGridSpec(
            num_scalar_prefetch=2,
            grid=(n_req,),
            in_specs=[row_spec, row_spec, row_spec,
                      pl.BlockSpec((N_HEADS, 1, 8, MOBA_BLOCK),
                                   lambda b, pt, sel: (0, TAB_PREV, 0, 0)),
                      pl.BlockSpec((N_HEADS, 1, 8, MOBA_BLOCK),
                                   lambda b, pt, sel: (0, TAB_OWN, 0, 0)),
                      any_spec, any_spec, cross_spec, any_spec, any_spec],
            out_specs=[row_spec, cross_spec],
            scratch_shapes=[pltpu.VMEM((2, n_bufs, page, HEAD_DIM), F32),
                            pltpu.VMEM((2, n_bufs, page, HEAD_DIM), F32),
                            pltpu.SemaphoreType.DMA((2, 2)),
                            pltpu.VMEM((2, CROSS_HEADS, n_mem, CROSS_HEAD_DIM), F32),
                            pltpu.VMEM((2, CROSS_HEADS, n_mem, CROSS_HEAD_DIM), F32),
                            pltpu.SemaphoreType.DMA((2, 2))],
        ),
        out_shape=[jax.ShapeDtypeStruct((n_req, 1, ATTN_WIDTH), BF16),
                   jax.ShapeDtypeStruct((n_req, 1, CROSS_WIDTH), BF16)],
        compiler_params=_params(("arbitrary",), 40),
        name="sample_attn",
    )(page_table_flat, sel_flat, q3, kn3, vn3, tables, tables, cache_k, cache_v, qc3, mem_k, mem_v)


def _cross_prompt_body(q_ref, mk_ref, mv_ref, o_ref):
    for h in range(CROSS_HEADS):
        sl = slice(h * CROSS_HEAD_DIM, (h + 1) * CROSS_HEAD_DIM)
        s = lax.dot_general(q_ref[:, sl], mk_ref[:, sl].astype(BF16), NT_DIMS,
                            preferred_element_type=F32) * (CROSS_HEAD_DIM ** -0.5)
        p = jnp.exp(s - jnp.max(s, axis=1, keepdims=True))
        den = jnp.sum(p, axis=1, keepdims=True)
        o = jnp.dot(p.astype(BF16), mv_ref[:, sl].astype(BF16),
                    preferred_element_type=F32) / den
        o_ref[:, sl] = o.astype(o_ref.dtype)


def _cross_prompt(qc, mk_bf, mv_bf, bm=512):
    t = qc.shape[0]
    n_mem = mk_bf.shape[0]
    return pl.pallas_call(
        _cross_prompt_body,
        grid=(t // bm,),
        in_specs=[pl.BlockSpec((bm, CROSS_WIDTH), lambda i: (i, 0)),
                  pl.BlockSpec((n_mem, CROSS_WIDTH), lambda i: (0, 0)),
                  pl.BlockSpec((n_mem, CROSS_WIDTH), lambda i: (0, 0))],
        out_specs=pl.BlockSpec((bm, CROSS_WIDTH), lambda i: (i, 0)),
        out_shape=jax.ShapeDtypeStruct((t, CROSS_WIDTH), BF16),
        compiler_params=_params(("parallel",), 32),
        name="cross_prompt",
    )(qc, mk_bf, mv_bf)


CONV_TILE = 256
CONV_HALO = 32
CONV_UNROLL = 4


def _conv_prompt_body(cur_ref, prev_ref, w_ref, cb_ref, g_ref, b_ref, o_ref, buf_ref, sh_ref,
                      wt_ref, y_ref):
    i = pl.program_id(0)
    has_prev = (i > 0).astype(F32)
    buf_ref[0:CONV_HALO, :] = prev_ref[CONV_TILE - CONV_HALO:CONV_TILE, :] * has_prev
    buf_ref[CONV_HALO:CONV_HALO + CONV_TILE, :] = cur_ref[...]
    first = CONV_HALO - CONV_STATE
    shifted_rows = sh_ref.shape[1]
    for tap in range(CONV_WIDTH):
        wt_ref[tap] = jnp.broadcast_to(w_ref[tap:tap + 1, :], (SUBLANES, D_CONV))
    for c in range(D_CONV // LANES):
        cs = slice(c * LANES, (c + 1) * LANES)
        for r in range(1, SUBLANES):
            sh_ref[r - 1] = buf_ref[r:r + shifted_rows, cs]
        weights = [wt_ref[tap, :, cs] for tap in range(CONV_WIDTH)]
        bias = jnp.broadcast_to(cb_ref[:, cs], (SUBLANES, LANES))

        def row_tiles(i, carry, cs=cs, weights=weights, bias=bias):
            row0 = pl.multiple_of(i * (CONV_UNROLL * SUBLANES), CONV_UNROLL * SUBLANES)
            acc = [bias] * CONV_UNROLL
            for tap in range(CONV_WIDTH):
                r = (first + tap) % SUBLANES
                for j in range(CONV_UNROLL):
                    rows = pl.ds(row0 + (j * SUBLANES + first + tap - r), SUBLANES)
                    tile = buf_ref[rows, cs] if r == 0 else sh_ref[r - 1, rows, :]
                    acc[j] = acc[j] + tile * weights[tap]
            for j in range(CONV_UNROLL):
                y_ref[pl.ds(row0 + j * SUBLANES, SUBLANES), cs] = acc[j]
            return carry

        lax.fori_loop(0, CONV_TILE // (CONV_UNROLL * SUBLANES), row_tiles, 0)
    y = _layer_norm(y_ref[...], g_ref[...], b_ref[...])
    o_ref[...] = jax.nn.silu(y).astype(o_ref.dtype)


def _conv_prompt(u, conv_w, conv_b, ln_g, ln_b):
    t = u.shape[0]
    vec = pl.BlockSpec((1, D_CONV), lambda i: (0, 0))
    return pl.pallas_call(
        _conv_prompt_body,
        grid=(t // CONV_TILE,),
        in_specs=[pl.BlockSpec((CONV_TILE, D_CONV), lambda i: (i, 0)),
                  pl.BlockSpec((CONV_TILE, D_CONV), lambda i: (jnp.maximum(i - 1, 0), 0)),
                  pl.BlockSpec((CONV_WIDTH, D_CONV), lambda i: (0, 0)),
                  vec, vec, vec],
        out_specs=pl.BlockSpec((CONV_TILE, D_CONV), lambda i: (i, 0)),
        out_shape=jax.ShapeDtypeStruct((t, D_CONV), BF16),
        scratch_shapes=[pltpu.VMEM((CONV_HALO + CONV_TILE, D_CONV), F32),
                        pltpu.VMEM((SUBLANES - 1, CONV_HALO + CONV_TILE - SUBLANES, LANES), F32),
                        pltpu.VMEM((CONV_WIDTH, SUBLANES, D_CONV), F32),
                        pltpu.VMEM((CONV_TILE, D_CONV), F32)],
        compiler_params=_params(("parallel",), 32),
        name="conv_prompt",
    )(u, u, conv_w, conv_b.reshape(1, D_CONV), ln_g.reshape(1, D_CONV), ln_b.reshape(1, D_CONV))


def _conv_sample_body(st_ref, u_ref, w_ref, cb_ref, g_ref, b_ref, o_ref):
    acc = u_ref[...] * w_ref[CONV_STATE:CONV_WIDTH, :]
    for tap in range(CONV_STATE):
        acc = acc + st_ref[tap] * w_ref[tap:tap + 1, :]
    y = _layer_norm(acc + cb_ref[...], g_ref[...], b_ref[...])
    o_ref[...] = jax.nn.silu(y).astype(o_ref.dtype)


def _conv_sample(state_t, u, conv_w, conv_b, ln_g, ln_b):
    n_req = u.shape[0]
    return pl.pallas_call(
        _conv_sample_body,
        out_shape=jax.ShapeDtypeStruct((n_req, D_CONV), BF16),
        name="conv_sample",
    )(state_t, u, conv_w, conv_b.reshape(1, D_CONV), ln_g.reshape(1, D_CONV),
      ln_b.reshape(1, D_CONV))


def kernel(x_prompt, x_sample, mem_prompt, cache_k, cache_v, state_conv, cache_mem_k, cache_mem_v,
           page_table, w_in, conv_w, conv_b, conv_ln_g, conv_ln_b, w_mem_kv, rel_bias, w_branch,
           w_out, ln1_g, ln1_b, w_gate, w_up, w_down, ln2_g, ln2_b):
    depth = w_in.shape[0]
    assert depth == 1 and x_prompt.shape[0] == 1 and x_sample.shape[1] == 1
    alpha = (2 * depth) ** 0.25
    t, d_model = x_prompt.shape[1], x_prompt.shape[2]
    n_req, n_pages = page_table.shape
    page = cache_k.shape[2]
    n_mem = mem_prompt.shape[1]
    assert (n_pages * page) % MOBA_BLOCK == 0

    sizes = (D_CONV, D_CONV, ATTN_WIDTH, ATTN_WIDTH, ATTN_WIDTH, CROSS_WIDTH, N_BRANCH * d_model)
    col_a, col_b, col_q, col_k, col_v, col_qc, col_g = (
        int(c) for c in np.concatenate([[0], np.cumsum(sizes)[:-1]]))
    w_o = w_out[0]
    w_d = w_down[0].astype(BF16)
    w_br = w_branch[0].astype(BF16)
    d_ff = w_gate.shape[2]

    tables = _bias_tables(rel_bias)

    xp = x_prompt[0]
    xs = x_sample[:, 0]
    q_p, q_s, xp_bf, xs_bf = _proj(xp, xs, [(w_in, col_q)], ATTN_WIDTH, "none", [F32],
                                   bn=ATTN_WIDTH, vmem_mib=56, keep_input=True)
    u_p, u_s = _proj(xp_bf, xs_bf, [(w_in, col_a), (w_in, col_b)], D_CONV, "glu", [F32])
    k_p, v_p, qc_p, k_s, v_s, qc_s = _proj(
        xp_bf, xs_bf, [(w_in, col_k), (w_in, col_v), (w_in, col_qc)], ATTN_WIDTH, "each",
        [F32, F32, BF16], vmem_mib=56)
    gates_p, gates_s = _proj(xp_bf, xs_bf, [(w_in, col_g)], N_BRANCH * d_model, "sigmoid",
                             [BF16], bn=1024)

    conv_y_p = _conv_prompt(u_p, conv_w[0], conv_b[0], conv_ln_g[0], conv_ln_b[0])
    k_aug, v_pad = _moba_prep(k_p, v_p)
    pt_flat = page_table.reshape(-1)
    attn_p, k_mean_s = _moba_prompt(_moba_select(q_p, _block_mean(k_p)), k_aug, v_pad, tables,
                                    cache_k, pt_flat)
    mem_bf = mem_prompt[0].astype(BF16)
    mk, mv = _proj(mem_bf, None, [(w_mem_kv, 0), (w_mem_kv, CROSS_WIDTH)], CROSS_WIDTH, "each",
                   [F32, F32])
    cross_p = _cross_prompt(qc_p, mk, mv)

    conv_y_s = _conv_sample(state_conv[0].transpose(1, 0, 2), u_s, conv_w[0], conv_b[0],
                            conv_ln_g[0], conv_ln_b[0])
    sel = _sample_select(q_s.reshape(n_req, N_HEADS, HEAD_DIM),
                         k_mean_s.reshape(n_req, -1, N_HEADS, HEAD_DIM))
    attn_s, cross_s = _sample_attn(
        q_s.reshape(n_req, 1, ATTN_WIDTH), k_s.reshape(n_req, 1, ATTN_WIDTH),
        v_s.reshape(n_req, 1, ATTN_WIDTH), qc_s.reshape(n_req, 1, CROSS_WIDTH),
        cache_k, cache_v, cache_mem_k, cache_mem_v, pt_flat, sel.reshape(-1), tables, n_pages)

    mix_p, mix_s = _branch_mix(
        (conv_y_p, attn_p, cross_p),
        (conv_y_s, attn_s.reshape(n_req, ATTN_WIDTH), cross_s.reshape(n_req, CROSS_WIDTH)),
        w_br, gates_p, gates_s)
    h1_p, h1_p_bf, h1_s, h1_s_bf = _mm_res_ln(mix_p, mix_s, w_o, xp, xs, ln1_g[0], ln1_b[0], alpha,
                                              [F32, BF16], bm=512)
    act_p, act_s = _proj(h1_p_bf, h1_s_bf, [(w_gate, 0), (w_up, 0)], d_ff, "swiglu", [BF16])
    y_p, y_s = _mm_res_ln(act_p, act_s, w_d, h1_p, h1_s, ln2_g[0], ln2_b[0], alpha, [F32],
                          bm=512)

    new_conv_sample = jnp.concatenate([state_conv[0][:, 1:], u_s[:, None, :]], axis=1)
    return (
        y_p[None],
        y_s[:, None],
        k_p.reshape(1, 1, t, N_HEADS, HEAD_DIM),
        v_p.reshape(1, 1, t, N_HEADS, HEAD_DIM),
        u_p[t - CONV_STATE:][None, None],
        mk.reshape(1, 1, n_mem, CROSS_HEADS, CROSS_HEAD_DIM),
        mv.reshape(1, 1, n_mem, CROSS_HEADS, CROSS_HEAD_DIM),
        k_s.reshape(1, n_req, 1, N_HEADS, HEAD_DIM),
        v_s.reshape(1, n_req, 1, N_HEADS, HEAD_DIM),
        new_conv_sample[None],
    )
```

```python
import functools
import math

import numpy as np
import jax
import jax.numpy as jnp
from jax import lax
from jax.experimental import pallas as pl
from jax.experimental.pallas import tpu as pltpu

F32 = jnp.float32
BF16 = jnp.bfloat16

N_HEADS = 8
HEAD_DIM = 128
ATTN_WIDTH = N_HEADS * HEAD_DIM
MOBA_BLOCK = 256
MOBA_TOPK = 3
D_CONV = 1024
CONV_WIDTH = 31
CONV_STATE = CONV_WIDTH - 1
CROSS_HEADS = 4
CROSS_HEAD_DIM = 256
CROSS_WIDTH = CROSS_HEADS * CROSS_HEAD_DIM
N_BRANCH = 3
BRANCH_WIDTH = 1024
NUM_BUCKETS = 32
MAX_EXACT = NUM_BUCKETS // 2
MAX_DISTANCE = 128
LN_EPS = 1e-5
MASKED = -1e30
LANES = 128
SUBLANES = 8
MIB = 2 ** 20
LOG2_E = math.log2(math.e)
LN_CHUNK_ROWS = 256

NT_DIMS = (((1,), (1,)), ((), ()))


def _params(semantics, vmem_mib):
    return pltpu.CompilerParams(dimension_semantics=semantics,
                                vmem_limit_bytes=vmem_mib * MIB)


def _layer_norm(y, g, b):
    mu = jnp.mean(y, axis=-1, keepdims=True)
    yc = y - mu
    var = jnp.mean(yc * yc, axis=-1, keepdims=True)
    return yc * lax.rsqrt(var + LN_EPS) * g + b


def _proj_body(*refs, n_w, n_out, epilogue, tail, keep_input):
    x_ref = refs[0]
    refs = refs[1:]
    if tail:
        xs_ref = refs[0]
        refs = refs[1:]
    w_refs = refs[:n_w]
    o_refs = refs[n_w:n_w + n_out]
    refs = refs[n_w + n_out:]
    if tail:
        os_refs = refs[:n_out]
        refs = refs[n_out:]
    if keep_input:
        x_copy_ref = refs[0]
        refs = refs[1:]
        if tail:
            xs_copy_ref = refs[0]
            refs = refs[1:]
    wbf_refs = refs
    i = pl.program_id(1)

    @pl.when(i == 0)
    def _():
        for w, wbf in zip(w_refs, wbf_refs):
            wbf[...] = w[...].astype(BF16)

    def apply(x, outs):
        z = [jnp.dot(x, w[...], preferred_element_type=F32) for w in wbf_refs]
        if epilogue == "each":
            for o, zn in zip(outs, z):
                o[...] = zn.astype(o.dtype)
            return
        if epilogue == "glu":
            y = z[0] * jax.nn.sigmoid(z[1])
        elif epilogue == "swiglu":
            y = jax.nn.silu(z[0]) * z[1]
        elif epilogue == "sigmoid":
            y = jax.nn.sigmoid(z[0])
        else:
            y = z[0]
        for o in outs:
            o[...] = y.astype(o.dtype)

    x = x_ref[...].astype(BF16)
    if keep_input:
        x_copy_ref[...] = x
    apply(x, o_refs)

    if tail:
        @pl.when(i == pl.num_programs(1) - 1)
        def _():
            xs = xs_ref[...].astype(BF16)
            if keep_input:
                xs_copy_ref[...] = xs
            apply(xs, os_refs)


def _proj(x, xs, ws, n, epilogue, out_dtypes, bm=1024, bn=512, vmem_mib=48, keep_input=False):
    m, k = x.shape
    tail = xs is not None
    bm = min(bm, m)
    bn = min(bn, n)
    while n % bn or any(c0 % bn for _, c0 in ws):
        bn //= 2
    assert m % bm == 0 and bn % LANES == 0
    assert not keep_input or bn == n
    in_specs = [pl.BlockSpec((bm, k), lambda j, i: (i, 0))]
    out_specs = [pl.BlockSpec((bm, bn), lambda j, i: (i, j)) for _ in out_dtypes]
    out_shape = [jax.ShapeDtypeStruct((m, n), dt) for dt in out_dtypes]
    operands = [x]
    if tail:
        ms = xs.shape[0]
        in_specs.append(pl.BlockSpec((ms, k), lambda j, i: (0, 0)))
        out_specs += [pl.BlockSpec((ms, bn), lambda j, i: (0, j)) for _ in out_dtypes]
        out_shape += [jax.ShapeDtypeStruct((ms, n), dt) for dt in out_dtypes]
        operands.append(xs)
    if keep_input:
        out_specs.append(pl.BlockSpec((bm, k), lambda j, i: (i, 0)))
        out_shape.append(jax.ShapeDtypeStruct((m, k), BF16))
        if tail:
            out_specs.append(pl.BlockSpec((ms, k), lambda j, i: (0, 0)))
            out_shape.append(jax.ShapeDtypeStruct((ms, k), BF16))
    in_specs += [pl.BlockSpec((None, k, bn), lambda j, i, c=c0 // bn: (0, 0, c + j))
                 for _, c0 in ws]
    return pl.pallas_call(
        functools.partial(_proj_body, n_w=len(ws), n_out=len(out_dtypes), epilogue=epilogue,
                          tail=tail, keep_input=keep_input),
        grid=(n // bn, m // bm),
        in_specs=in_specs,
        out_specs=out_specs,
        out_shape=out_shape,
        scratch_shapes=[pltpu.VMEM((k, bn), BF16) for _ in ws],
        compiler_params=_params(("arbitrary" if keep_input else "parallel", "arbitrary"), vmem_mib),
        name="proj_" + epilogue,
    )(*operands, *[w for w, _ in ws])


def _mm_res_ln_body(a_ref, as_ref, w_in_ref, r_ref, rs_ref, g_ref, b_ref, *o_refs, alpha, n_out,
                    cast_weight):
    if cast_weight:
        w_ref = o_refs[2 * n_out]

        @pl.when(pl.program_id(0) == 0)
        def _():
            w_ref[...] = w_in_ref[...].astype(BF16)
    else:
        w_ref = w_in_ref

    def apply(a, res, outs):
        rows = a.shape[0]
        chunk = min(rows, LN_CHUNK_ROWS)
        for r0 in range(0, rows, chunk):
            rs = slice(r0, r0 + chunk)
            y = alpha * res[rs, :] + jnp.dot(a[rs, :], w_ref[...], preferred_element_type=F32)
            out = _layer_norm(y, g_ref[...], b_ref[...])
            for o in outs:
                o[rs, :] = out.astype(o.dtype)

    apply(a_ref, r_ref, o_refs[:n_out])

    @pl.when(pl.program_id(0) == pl.num_programs(0) - 1)
    def _():
        apply(as_ref, rs_ref, o_refs[n_out:2 * n_out])


def _mm_res_ln(a, a_s, w, res, res_s, g, b, alpha, out_dtypes, bm, vmem_mib=58):
    m, k = a.shape
    ms = a_s.shape[0]
    n = w.shape[1]
    bm = min(bm, m)
    assert m % bm == 0
    cast_weight = w.dtype != BF16
    return pl.pallas_call(
        functools.partial(_mm_res_ln_body, alpha=alpha, n_out=len(out_dtypes),
                          cast_weight=cast_weight),
        scratch_shapes=[pltpu.VMEM((k, n), BF16)] if cast_weight else [],
        grid=(m // bm,),
        in_specs=[
            pl.BlockSpec((bm, k), lambda i: (i, 0)),
            pl.BlockSpec((ms, k), lambda i: (0, 0)),
            pl.BlockSpec((k, n), lambda i: (0, 0), pipeline_mode=pl.Buffered(1)),
            pl.BlockSpec((bm, n), lambda i: (i, 0)),
            pl.BlockSpec((ms, n), lambda i: (0, 0)),
            pl.BlockSpec((1, n), lambda i: (0, 0)),
            pl.BlockSpec((1, n), lambda i: (0, 0)),
        ],
        out_specs=[pl.BlockSpec((bm, n), lambda i: (i, 0)) for _ in out_dtypes]
        + [pl.BlockSpec((ms, n), lambda i: (0, 0)) for _ in out_dtypes],
        out_shape=[jax.ShapeDtypeStruct((m, n), dt) for dt in out_dtypes]
        + [jax.ShapeDtypeStruct((ms, n), dt) for dt in out_dtypes],
        compiler_params=_params(("arbitrary",), vmem_mib),
        name="mm_res_ln",
    )(a, a_s, w, res, res_s, g.reshape(1, n), b.reshape(1, n))


def _branch_body(*refs):
    nb = N_BRANCH
    br_refs, brs_refs, w_refs = refs[0:nb], refs[nb:2 * nb], refs[2 * nb:3 * nb]
    g_refs, gs_refs = refs[3 * nb:4 * nb], refs[4 * nb:5 * nb]
    o_ref, os_ref = refs[5 * nb], refs[5 * nb + 1]

    def mix(branches, gates, out):
        y = None
        for br, w, g in zip(branches, w_refs, gates):
            term = g[...] * jnp.dot(br[...], w[...], preferred_element_type=F32)
            y = term if y is None else y + term
        out[...] = y.astype(out.dtype)

    mix(br_refs, g_refs, o_ref)

    @pl.when(pl.program_id(1) == pl.num_programs(1) - 1)
    def _():
        mix(brs_refs, gs_refs, os_ref)


def _branch_mix(branches, branches_s, w_branch, gates, gates_s, bm=1024, bn=1024):
    m = branches[0].shape[0]
    ms = branches_s[0].shape[0]
    d = w_branch.shape[2]
    bm = min(bm, m)
    bn = min(bn, d)
    nj = d // bn
    ids = range(N_BRANCH)
    return pl.pallas_call(
        _branch_body,
        grid=(nj, m // bm),
        in_specs=[pl.BlockSpec((bm, BRANCH_WIDTH), lambda j, i: (i, 0)) for _ in ids]
        + [pl.BlockSpec((ms, BRANCH_WIDTH), lambda j, i: (0, 0)) for _ in ids]
        + [pl.BlockSpec((None, BRANCH_WIDTH, bn), lambda j, i, n=n: (n, 0, j)) for n in ids]
        + [pl.BlockSpec((bm, bn), lambda j, i, n=n: (i, n * nj + j)) for n in ids]
        + [pl.BlockSpec((ms, bn), lambda j, i, n=n: (0, n * nj + j)) for n in ids],
        out_specs=[pl.BlockSpec((bm, bn), lambda j, i: (i, j)),
                   pl.BlockSpec((ms, bn), lambda j, i: (0, j))],
        out_shape=[jax.ShapeDtypeStruct((m, d), BF16), jax.ShapeDtypeStruct((ms, d), BF16)],
        compiler_params=_params(("parallel", "arbitrary"), 48),
        name="branch_mix",
    )(*branches, *branches_s, w_branch, w_branch, w_branch, gates, gates, gates,
      gates_s, gates_s, gates_s)


TAB_PREV = 0
TAB_OWN = 1
TAB_PREV_T = 2
TAB_OWN_T = 3


def _bias_tables_body(rb_ref, o_ref):
    h = pl.program_id(0)
    shape = (MOBA_BLOCK, MOBA_BLOCK)
    far = rb_ref[NUM_BUCKETS - 1, h]
    for slot, offset, q_axis in ((TAB_PREV, MOBA_BLOCK, 0), (TAB_OWN, 0, 0),
                                 (TAB_PREV_T, MOBA_BLOCK, 1), (TAB_OWN_T, 0, 1)):
        qi = lax.broadcasted_iota(jnp.int32, shape, q_axis)
        kj = lax.broadcasted_iota(jnp.int32, shape, 1 - q_axis)
        dist = jnp.maximum(qi - kj + offset, 0)
        large = MAX_EXACT + (jnp.log(jnp.maximum(dist, 1).astype(F32) / MAX_EXACT)
                             / math.log(MAX_DISTANCE / MAX_EXACT)
                             * (NUM_BUCKETS - MAX_EXACT)).astype(jnp.int32)
        large = jnp.minimum(large, NUM_BUCKETS - 1)
        bucket = jnp.where(dist < MAX_EXACT, dist, large)
        val = jnp.zeros(shape, F32)
        for bkt in range(NUM_BUCKETS):
            val = jnp.where(bucket == bkt, rb_ref[bkt, h], val)
        val = val - far
        if slot in (TAB_PREV_T, TAB_OWN_T):
            val = val * LOG2_E
        if slot in (TAB_OWN, TAB_OWN_T):
            val = jnp.where(kj > qi, MASKED, val)
        o_ref[0, slot] = val


def _bias_tables(rel_bias):
    return pl.pallas_call(
        _bias_tables_body,
        grid=(N_HEADS,),
        in_specs=[pl.BlockSpec(memory_space=pltpu.SMEM)],
        out_specs=pl.BlockSpec((1, 4, MOBA_BLOCK, MOBA_BLOCK), lambda h: (h, 0, 0, 0)),
        out_shape=jax.ShapeDtypeStruct((N_HEADS, 4, MOBA_BLOCK, MOBA_BLOCK), F32),
        compiler_params=_params(("arbitrary",), 32),
        name="bias_tables",
    )(rel_bias)


def _block_mean_body(k_ref, o_ref):
    rows = k_ref.shape[0]
    x = k_ref[...].reshape(rows // MOBA_BLOCK, MOBA_BLOCK, k_ref.shape[1])
    o_ref[...] = jnp.sum(x, axis=1) * (1.0 / MOBA_BLOCK)


def _block_mean(k):
    t, w = k.shape
    nb = t // MOBA_BLOCK
    per = min(8, nb)
    assert nb % per == 0
    return pl.pallas_call(
        _block_mean_body,
        grid=(nb // per,),
        in_specs=[pl.BlockSpec((per * MOBA_BLOCK, w), lambda i: (i, 0))],
        out_specs=pl.BlockSpec((per, w), lambda i: (i, 0)),
        out_shape=jax.ShapeDtypeStruct((nb, w), F32),
        compiler_params=_params(("parallel",), 40),
        name="block_mean",
    )(k)


MOBA_CHUNK = 4
MOBA_HEADS_PER_STEP = 2
MOBA_PAD = MOBA_CHUNK * MOBA_BLOCK
KAUG_WIDTH = HEAD_DIM + LANES


def _moba_prep_body(k_ref, v_ref, ka_ref, vt_ref):
    i = pl.program_id(0)
    rows = k_ref.shape[0]
    col = lax.broadcasted_iota(jnp.int32, (rows, LANES), 1)

    @pl.when(i == 0)
    def _():
        pad_hot = jnp.where(col == LANES - 1, 1.0, 0.0).astype(BF16)
        for h in range(N_HEADS):
            ka_ref[:, h * KAUG_WIDTH:h * KAUG_WIDTH + HEAD_DIM] = jnp.zeros((rows, HEAD_DIM), BF16)
            ka_ref[:, h * KAUG_WIDTH + HEAD_DIM:(h + 1) * KAUG_WIDTH] = pad_hot
        vt_ref[...] = jnp.zeros(vt_ref.shape, BF16)

    @pl.when(i > 0)
    def _():
        row = lax.broadcasted_iota(jnp.int32, (rows, LANES), 0) + (i - 1) * rows
        hot = jnp.where(jnp.right_shift(row, int(math.log2(MOBA_BLOCK))) == col,
                        1.0, 0.0).astype(BF16)
        for h in range(N_HEADS):
            ka_ref[:, h * KAUG_WIDTH:h * KAUG_WIDTH + HEAD_DIM] = (
                k_ref[:, h * HEAD_DIM:(h + 1) * HEAD_DIM].astype(BF16))
            ka_ref[:, h * KAUG_WIDTH + HEAD_DIM:(h + 1) * KAUG_WIDTH] = hot
        for blk in range(rows // MOBA_BLOCK):
            vt_ref[blk] = v_ref[blk * MOBA_BLOCK:(blk + 1) * MOBA_BLOCK, :].T.astype(BF16)


def _moba_prep(k, v):
    t = k.shape[0]
    rows = MOBA_PAD
    assert t % rows == 0
    data = pl.BlockSpec((rows, ATTN_WIDTH), lambda i: (jnp.maximum(i - 1, 0), 0))
    return pl.pallas_call(
        _moba_prep_body,
        grid=(t // rows + 1,),
        in_specs=[data, data],
        out_specs=[pl.BlockSpec((rows, N_HEADS * KAUG_WIDTH), lambda i: (i, 0)),
                   pl.BlockSpec((MOBA_CHUNK, ATTN_WIDTH, MOBA_BLOCK), lambda i: (i, 0, 0))],
        out_shape=[jax.ShapeDtypeStruct((t + rows, N_HEADS * KAUG_WIDTH), BF16),
                   jax.ShapeDtypeStruct((t // MOBA_BLOCK + MOBA_CHUNK, ATTN_WIDTH, MOBA_BLOCK),
                                        BF16)],
        compiler_params=_params(("parallel",), 40),
        name="moba_prep",
    )(k, v)


SELECT_ROWS = 1024


def _moba_select_body(q_ref, km_ref, o_ref, *, n_sub):
    q = q_ref[...]
    rows = q.shape[0]
    gate = lax.dot_general(km_ref[...], q, NT_DIMS, precision=lax.Precision.HIGHEST,
                           preferred_element_type=F32)[:n_sub]
    blk = lax.broadcasted_iota(jnp.int32, gate.shape, 0)
    blk_f = blk.astype(F32)
    pos = lax.broadcasted_iota(jnp.int32, gate.shape, 1) + pl.program_id(0) * rows
    own = jnp.right_shift(pos, int(math.log2(MOBA_BLOCK)))
    gate = jnp.where(blk < own, gate, -jnp.inf)
    picked = blk == own
    for _ in range(MOBA_TOPK):
        top = jnp.max(gate, axis=0, keepdims=True)
        first = jnp.min(jnp.where(gate == top, blk_f, float(LANES)), axis=0, keepdims=True)
        chosen = blk_f == first
        picked = picked | (chosen & (top > -jnp.inf))
        gate = jnp.where(chosen, -jnp.inf, gate)
    o_ref[:HEAD_DIM, :] = (q * (HEAD_DIM ** -0.5 * LOG2_E)).T.astype(BF16)
    o_ref[HEAD_DIM:HEAD_DIM + n_sub, :] = jnp.where(picked, 0.0, MASKED).astype(BF16)
    o_ref[HEAD_DIM + n_sub:, :] = jnp.full((LANES - n_sub, rows), MASKED, BF16)


def _moba_select(q, k_mean):
    t = q.shape[0]
    nb = t // MOBA_BLOCK
    rows = min(SELECT_ROWS, t)
    n_sub = -(-nb // 16) * 16
    assert t % rows == 0 and MOBA_TOPK <= nb and n_sub < LANES
    km = jnp.pad(k_mean, ((0, LANES - nb), (0, 0)))
    return pl.pallas_call(
        functools.partial(_moba_select_body, n_sub=n_sub),
        grid=(t // rows, N_HEADS),
        in_specs=[pl.BlockSpec((rows, HEAD_DIM), lambda i, h: (i, h)),
                  pl.BlockSpec((LANES, HEAD_DIM), lambda i, h: (0, h))],
        out_specs=pl.BlockSpec((KAUG_WIDTH, rows), lambda i, h: (h, i)),
        out_shape=jax.ShapeDtypeStruct((N_HEADS * KAUG_WIDTH, t), BF16),
        compiler_params=_params(("parallel", "parallel"), 40),
        name="moba_select",
    )(q, km)


def _moba_prompt_body(pt_ref, qa_ref, qn_ref, ka_ref, vt_ref, tab_ref, ck_ref, o_ref, km_ref, s_ref,
                      page_buf, page_sem, *, pages_per_step, per_block):
    qb_idx = pl.program_id(1)
    step = pl.program_id(0) * pl.num_programs(1) + qb_idx
    n_steps = pl.num_programs(0) * pl.num_programs(1)

    def page_copies(at_step, slot):
        return [pltpu.make_async_copy(ck_ref.at[0, pt_ref[at_step * pages_per_step + j]],
                                      page_buf.at[slot, j], page_sem.at[slot])
                for j in range(pages_per_step)]

    @pl.when(step == 0)
    def _():
        for cp in page_copies(step, 0):
            cp.start()

    @pl.when(step + 1 < n_steps)
    def _():
        for cp in page_copies(step + 1, (step + 1) % 2):
            cp.start()

    cur = step % 2
    for cp in page_copies(step, cur):
        cp.wait()

    def block_means(blocks):
        for blk in blocks:
            tot = jnp.sum(page_buf[cur, blk * per_block], axis=0)
            for extra in range(1, per_block):
                tot = tot + jnp.sum(page_buf[cur, blk * per_block + extra], axis=0)
            km_ref[0, blk] = tot * (1.0 / MOBA_BLOCK)

    n_blocks = pages_per_step // per_block
    heads = range(MOBA_HEADS_PER_STEP)
    tiles = [(hh, g) for hh in heads for g in range(MOBA_CHUNK)]
    q_aug = [qa_ref[hh * KAUG_WIDTH:(hh + 1) * KAUG_WIDTH, :] for hh in heads]
    q_aug_next = [qn_ref[hh * KAUG_WIDTH:(hh + 1) * KAUG_WIDTH, :] for hh in heads]

    def logits(hh, g, first_block, queries):
        start = pl.multiple_of((first_block + g) * MOBA_BLOCK, MOBA_BLOCK)
        s_ref[hh * MOBA_CHUNK + g] = jnp.dot(
            ka_ref[pl.ds(start, MOBA_BLOCK), hh * KAUG_WIDTH:(hh + 1) * KAUG_WIDTH],
            queries[hh], preferred_element_type=F32)

    def weights(state, bias, next_first, next_queries):
        m_new, scale, l_new, p = [], [], [], {}

        def tile(hh, g):
            s = s_ref[hh * MOBA_CHUNK + g]
            return s if bias[g] is None else s + tab_ref[hh, bias[g]]

        for hh in heads:
            m, l, _ = state[hh]
            top = m
            for g in range(MOBA_CHUNK):
                top = jnp.maximum(top, jnp.max(tile(hh, g), axis=0, keepdims=True))
            sc = jnp.exp2(m - top)
            tot = sc * l
            for g in range(MOBA_CHUNK):
                e = jnp.exp2(tile(hh, g) - top)
                logits(hh, g, next_first, next_queries)
                tot = tot + jnp.sum(e, axis=0, keepdims=True)
                p[hh, g] = e.astype(BF16)
            m_new.append(top)
            scale.append(sc)
            l_new.append(tot)
        return m_new, scale, l_new, p

    def accumulate(first_block, state, stats):
        m_new, scale, l_new, p = stats
        out = []
        for hh in heads:
            acc = scale[hh] * state[hh][2]
            for g in range(MOBA_CHUNK):
                acc = acc + jnp.dot(vt_ref[first_block + g, hh * HEAD_DIM:(hh + 1) * HEAD_DIM, :],
                                    p[hh, g], preferred_element_type=F32)
            out.append((m_new[hh], l_new[hh], acc))
        return tuple(out)

    init = tuple((jnp.full((1, MOBA_BLOCK), 0.1 * MASKED, F32), jnp.zeros((1, MOBA_BLOCK), F32),
                  jnp.zeros((HEAD_DIM, MOBA_BLOCK), F32)) for _ in heads)
    first = qb_idx % MOBA_CHUNK + 1

    @pl.when(qb_idx == 0)
    def _():
        for hh, g in tiles:
            logits(hh, g, first, q_aug)

    no_bias = [None] * MOBA_CHUNK

    def trip(c, state):
        stats = weights(state, no_bias, first + (c + 1) * MOBA_CHUNK, q_aug)
        return accumulate(first + c * MOBA_CHUNK, state, stats)

    n_early = qb_idx // MOBA_CHUNK
    odd = n_early % 2
    state = lax.cond(odd == 1, lambda st: trip(0, st), lambda st: st, init)
    state = lax.fori_loop(0, n_early // 2,
                          lambda c, st: trip(odd + 2 * c + 1, trip(odd + 2 * c, st)), state)
    bias = [None] * (MOBA_CHUNK - 2) + [0, 1]
    block_means(range(0, n_blocks // 2))
    stats = weights(state, bias, (qb_idx + 1) % MOBA_CHUNK + 1, q_aug_next)
    block_means(range(n_blocks // 2, n_blocks))
    state = accumulate(qb_idx + 1, state, stats)
    for hh in heads:
        _, l, acc = state[hh]
        o_ref[:, hh * HEAD_DIM:(hh + 1) * HEAD_DIM] = (acc / l).T.astype(o_ref.dtype)


def _moba_prompt(q_aug_t, k_aug, v_t, tables, cache_k, page_table_flat):
    t = q_aug_t.shape[1]
    nb = t // MOBA_BLOCK
    hb = MOBA_HEADS_PER_STEP
    n_steps = (N_HEADS // hb) * nb
    page = cache_k.shape[2]
    per_block = MOBA_BLOCK // page
    total_pages = page_table_flat.shape[0]
    pages_per_step = total_pages // n_steps
    assert t % MOBA_BLOCK == 0 and MOBA_CHUNK >= 2 and MOBA_BLOCK % page == 0
    assert total_pages % n_steps == 0 and pages_per_step % per_block == 0
    blocks_per_step = pages_per_step // per_block
    attn, k_mean = pl.pallas_call(
        functools.partial(_moba_prompt_body, pages_per_step=pages_per_step, per_block=per_block),
        grid_spec=pltpu.PrefetchScalarGridSpec(
            num_scalar_prefetch=1,
            grid=(N_HEADS // hb, nb),
            in_specs=[
                pl.BlockSpec((hb * KAUG_WIDTH, MOBA_BLOCK), lambda h, i, pt: (h, i)),
                pl.BlockSpec((hb * KAUG_WIDTH, MOBA_BLOCK),
                             lambda h, i, pt: (h, jnp.minimum(i + 1, nb - 1))),
                pl.BlockSpec((t + MOBA_PAD, hb * KAUG_WIDTH), lambda h, i, pt: (0, h)),
                pl.BlockSpec((nb + MOBA_CHUNK, hb * HEAD_DIM, MOBA_BLOCK),
                             lambda h, i, pt: (0, h, 0)),
                pl.BlockSpec((hb, 2, MOBA_BLOCK, MOBA_BLOCK),
                             lambda h, i, pt: (h, TAB_PREV_T // 2, 0, 0)),
                pl.BlockSpec(memory_space=pl.ANY),
            ],
            out_specs=[
                pl.BlockSpec((MOBA_BLOCK, hb * HEAD_DIM), lambda h, i, pt: (i, h)),
                pl.BlockSpec((1, blocks_per_step, N_HEADS, HEAD_DIM),
                             lambda h, i, pt: (h * nb + i, 0, 0, 0)),
            ],
            scratch_shapes=[
                pltpu.VMEM((hb * MOBA_CHUNK, MOBA_BLOCK, MOBA_BLOCK), F32),
                pltpu.VMEM((2, pages_per_step, page, N_HEADS, HEAD_DIM), F32),
                pltpu.SemaphoreType.DMA((2,)),
            ],
        ),
        out_shape=[jax.ShapeDtypeStruct((t, ATTN_WIDTH), BF16),
                   jax.ShapeDtypeStruct((n_steps, blocks_per_step, N_HEADS, HEAD_DIM), F32)],
        compiler_params=_params(("arbitrary", "arbitrary"), 58),
        name="moba_prompt",
    )(page_table_flat, q_aug_t, q_aug_t, k_aug, v_t, tables, cache_k)
    return attn, k_mean.reshape(n_steps * blocks_per_step, N_HEADS, HEAD_DIM)


def _sample_select_body(q_ref, km_ref, o_ref):
    km = km_ref[0]
    gate = jnp.sum(km * q_ref[0], axis=-1, keepdims=True)
    gate = jnp.broadcast_to(gate, km.shape)
    blk_id = lax.broadcasted_iota(jnp.int32, km.shape, 0).astype(F32)
    for slot in range(MOBA_TOPK):
        top = jnp.max(gate, axis=0, keepdims=True)
        first = jnp.min(jnp.where(gate == top, blk_id, float(km.shape[0])),
                        axis=0, keepdims=True)
        o_ref[0, slot] = first[0].astype(jnp.int32)
        gate = jnp.where(blk_id == first, -jnp.inf, gate)


def _sample_select(q_heads, k_mean):
    n_req, nbp = k_mean.shape[0], k_mean.shape[1]
    assert nbp >= MOBA_TOPK
    out = pl.pallas_call(
        _sample_select_body,
        grid=(n_req,),
        in_specs=[pl.BlockSpec((1, N_HEADS, HEAD_DIM), lambda b: (b, 0, 0)),
                  pl.BlockSpec((1, nbp, N_HEADS, HEAD_DIM), lambda b: (b, 0, 0, 0))],
        out_specs=pl.BlockSpec((1, MOBA_TOPK, N_HEADS, HEAD_DIM), lambda b: (b, 0, 0, 0)),
        out_shape=jax.ShapeDtypeStruct((n_req, MOBA_TOPK, N_HEADS, HEAD_DIM), jnp.int32),
        compiler_params=_params(("parallel",), 32),
        name="sample_select",
    )(q_heads, k_mean)
    return out[:, :, :, 0]


def _sample_attn_body(pt_ref, sel_ref, q_ref, kn_ref, vn_ref, tprev_ref, town_ref, ck_ref, cv_ref,
                      qc_ref, mk_hbm, mv_hbm, o_ref, oc_ref, kbuf, vbuf, sem, mkbuf, mvbuf, msem,
                      *, n_pages, per_block):
    b = pl.program_id(0)
    page = kbuf.shape[2]
    last_block = n_pages // per_block - 1

    def copies(req, buf_slot):
        out = []
        for h in range(N_HEADS):
            for slot in range(MOBA_TOPK):
                blk = sel_ref[(req * MOBA_TOPK + slot) * N_HEADS + h]
                for i in range(per_block):
                    pg = pt_ref[req * n_pages + blk * per_block + i]
                    idx = (h * MOBA_TOPK + slot) * per_block + i
                    out.append(pltpu.make_async_copy(ck_ref.at[0, pg, :, h, :],
                                                     kbuf.at[buf_slot, idx], sem.at[buf_slot, 0]))
                    out.append(pltpu.make_async_copy(cv_ref.at[0, pg, :, h, :],
                                                     vbuf.at[buf_slot, idx], sem.at[buf_slot, 1]))
        for h in range(CROSS_HEADS):
            out.append(pltpu.make_async_copy(mk_hbm.at[0, req, :, h, :], mkbuf.at[buf_slot, h],
                                             msem.at[buf_slot, 0]))
            out.append(pltpu.make_async_copy(mv_hbm.at[0, req, :, h, :], mvbuf.at[buf_slot, h],
                                             msem.at[buf_slot, 1]))
        return out

    @pl.when(b == 0)
    def _():
        for n, cp in enumerate(copies(b, 0)):
            cp.start(priority=n % 2)

    @pl.when(b + 1 < pl.num_programs(0))
    def _():
        for n, cp in enumerate(copies(b + 1, (b + 1) % 2)):
            cp.start(priority=n % 2)

    cur = b % 2
    for cp in copies(b, cur):
        cp.wait()

    hs = [slice(h * HEAD_DIM, (h + 1) * HEAD_DIM) for h in range(N_HEADS)]
    n_tiles = MOBA_TOPK * per_block
    q = [q_ref[0, :, hs[h]] * (HEAD_DIM ** -0.5) for h in range(N_HEADS)]
    logits = {}
    for h in range(N_HEADS):
        q8 = jnp.broadcast_to(q[h], (8, HEAD_DIM)).astype(BF16)
        prev_row = tprev_ref[h, 0, 0:1, :]
        for slot in range(MOBA_TOPK):
            near = (sel_ref[(b * MOBA_TOPK + slot) * N_HEADS + h] == last_block).astype(F32)
            for i in range(per_block):
                idx = slot * per_block + i
                kt = kbuf[cur, h * n_tiles + idx].astype(BF16)
                lg = lax.dot_general(q8, kt, NT_DIMS, preferred_element_type=F32)[0:1, :]
                logits[h, idx] = lg + near * prev_row[:, i * page:(i + 1) * page]
    p_own, den, weights = {}, {}, {}
    for h in range(N_HEADS):
        l_own = (jnp.sum(q[h] * kn_ref[0, :, hs[h]], axis=1, keepdims=True)
                 + town_ref[h, 0, 0:1, 0:1])
        m = l_own
        for idx in range(n_tiles):
            m = jnp.maximum(m, jnp.max(logits[h, idx], axis=1, keepdims=True))
        p_own[h] = jnp.exp(l_own - m)
        tot = p_own[h]
        for idx in range(n_tiles):
            p = jnp.exp(logits[h, idx] - m)
            tot = tot + jnp.sum(p, axis=1, keepdims=True)
            weights[h, idx] = jnp.broadcast_to(p, (8, page)).astype(BF16)
        den[h] = tot
    for h in range(N_HEADS):
        acc = p_own[h] * vn_ref[0, :, hs[h]]
        for idx in range(n_tiles):
            acc = acc + jnp.dot(weights[h, idx], vbuf[cur, h * n_tiles + idx].astype(BF16),
                                preferred_element_type=F32)[0:1, :]
        o_ref[0, :, hs[h]] = (acc / den[h]).astype(o_ref.dtype)

    for h in range(CROSS_HEADS):
        sl = slice(h * CROSS_HEAD_DIM, (h + 1) * CROSS_HEAD_DIM)
        q8 = jnp.broadcast_to(qc_ref[0, :, sl], (8, CROSS_HEAD_DIM))
        s = lax.dot_general(q8, mkbuf[cur, h].astype(BF16), NT_DIMS,
                            preferred_element_type=F32) * (CROSS_HEAD_DIM ** -0.5)
        p = jnp.exp(s - jnp.max(s, axis=1, keepdims=True))
        tot = jnp.sum(p, axis=1, keepdims=True)
        o = jnp.dot(p.astype(BF16), mvbuf[cur, h].astype(BF16),
                    preferred_element_type=F32) / tot
        oc_ref[0, :, sl] = o[0:1, :].astype(oc_ref.dtype)


def _sample_attn(q3, kn3, vn3, qc3, cache_k, cache_v, mem_k, mem_v, page_table_flat, sel_flat,
                 tables, n_pages):
    n_req = q3.shape[0]
    page = cache_k.shape[2]
    n_mem = mem_k.shape[2]
    per_block = MOBA_BLOCK // page
    n_bufs = N_HEADS * MOBA_TOPK * per_block
    row_spec = pl.BlockSpec((1, 1, ATTN_WIDTH), lambda b, pt, sel: (b, 0, 0))
    cross_spec = pl.BlockSpec((1, 1, CROSS_WIDTH), lambda b, pt, sel: (b, 0, 0))
    any_spec = pl.BlockSpec(memory_space=pl.ANY)
    return pl.pallas_call(
        functools.partial(_sample_attn_body, n_pages=n_pages, per_block=per_block),
        grid_spec=pltpu.PrefetchScalarGridSpec(
            num_scalar_prefetch=2,
            grid=(n_req,),
            in_specs=[row_spec, row_spec, row_spec,
                      pl.BlockSpec((N_HEADS, 1, 8, MOBA_BLOCK),
                                   lambda b, pt, sel: (0, TAB_PREV, 0, 0)),
                      pl.BlockSpec((N_HEADS, 1, 8, MOBA_BLOCK),
                                   lambda b, pt, sel: (0, TAB_OWN, 0, 0)),
                      any_spec, any_spec, cross_spec, any_spec, any_spec],
            out_specs=[row_spec, cross_spec],
            scratch_shapes=[pltpu.VMEM((2, n_bufs, page, HEAD_DIM), F32),
                            pltpu.VMEM((2, n_bufs, page, HEAD_DIM), F32),
                            pltpu.SemaphoreType.DMA((2, 2)),
                            pltpu.VMEM((2, CROSS_HEADS, n_mem, CROSS_HEAD_DIM), F32),
                            pltpu.VMEM((2, CROSS_HEADS, n_mem, CROSS_HEAD_DIM), F32),
                            pltpu.SemaphoreType.DMA((2, 2))],
        ),
        out_shape=[jax.ShapeDtypeStruct((n_req, 1, ATTN_WIDTH), BF16),
                   jax.ShapeDtypeStruct((n_req, 1, CROSS_WIDTH), BF16)],
        compiler_params=_params(("arbitrary",), 40),
        name="sample_attn",
    )(page_table_flat, sel_flat, q3, kn3, vn3, tables, tables, cache_k, cache_v, qc3, mem_k, mem_v)


def _cross_prompt_body(q_ref, mk_ref, mv_ref, o_ref):
    for h in range(CROSS_HEADS):
        sl = slice(h * CROSS_HEAD_DIM, (h + 1) * CROSS_HEAD_DIM)
        s = lax.dot_general(q_ref[:, sl], mk_ref[:, sl].astype(BF16), NT_DIMS,
                            preferred_element_type=F32) * (CROSS_HEAD_DIM ** -0.5)
        p = jnp.exp(s - jnp.max(s, axis=1, keepdims=True))
        den = jnp.sum(p, axis=1, keepdims=True)
        o = jnp.dot(p.astype(BF16), mv_ref[:, sl].astype(BF16),
                    preferred_element_type=F32) / den
        o_ref[:, sl] = o.astype(o_ref.dtype)


def _cross_prompt(qc, mk_bf, mv_bf, bm=512):
    t = qc.shape[0]
    n_mem = mk_bf.shape[0]
    return pl.pallas_call(
        _cross_prompt_body,
        grid=(t // bm,),
        in_specs=[pl.BlockSpec((bm, CROSS_WIDTH), lambda i: (i, 0)),
                  pl.BlockSpec((n_mem, CROSS_WIDTH), lambda i: (0, 0)),
                  pl.BlockSpec((n_mem, CROSS_WIDTH), lambda i: (0, 0))],
        out_specs=pl.BlockSpec((bm, CROSS_WIDTH), lambda i: (i, 0)),
        out_shape=jax.ShapeDtypeStruct((t, CROSS_WIDTH), BF16),
        compiler_params=_params(("parallel",), 32),
        name="cross_prompt",
    )(qc, mk_bf, mv_bf)


CONV_TILE = 256
CONV_HALO = 32
CONV_UNROLL = 4


def _conv_prompt_body(cur_ref, prev_ref, w_ref, cb_ref, g_ref, b_ref, o_ref, buf_ref, sh_ref,
                      wt_ref, y_ref):
    i = pl.program_id(0)
    has_prev = (i > 0).astype(F32)
    buf_ref[0:CONV_HALO, :] = prev_ref[CONV_TILE - CONV_HALO:CONV_TILE, :] * has_prev
    buf_ref[CONV_HALO:CONV_HALO + CONV_TILE, :] = cur_ref[...]
    first = CONV_HALO - CONV_STATE
    shifted_rows = sh_ref.shape[1]
    for tap in range(CONV_WIDTH):
        wt_ref[tap] = jnp.broadcast_to(w_ref[tap:tap + 1, :], (SUBLANES, D_CONV))
    for c in range(D_CONV // LANES):
        cs = slice(c * LANES, (c + 1) * LANES)
        for r in range(1, SUBLANES):
            sh_ref[r - 1] = buf_ref[r:r + shifted_rows, cs]
        weights = [wt_ref[tap, :, cs] for tap in range(CONV_WIDTH)]
        bias = jnp.broadcast_to(cb_ref[:, cs], (SUBLANES, LANES))

        def row_tiles(i, carry, cs=cs, weights=weights, bias=bias):
            row0 = pl.multiple_of(i * (CONV_UNROLL * SUBLANES), CONV_UNROLL * SUBLANES)
            acc = [bias] * CONV_UNROLL
            for tap in range(CONV_WIDTH):
                r = (first + tap) % SUBLANES
                for j in range(CONV_UNROLL):
                    rows = pl.ds(row0 + (j * SUBLANES + first + tap - r), SUBLANES)
                    tile = buf_ref[rows, cs] if r == 0 else sh_ref[r - 1, rows, :]
                    acc[j] = acc[j] + tile * weights[tap]
            for j in range(CONV_UNROLL):
                y_ref[pl.ds(row0 + j * SUBLANES, SUBLANES), cs] = acc[j]
            return carry

        lax.fori_loop(0, CONV_TILE // (CONV_UNROLL * SUBLANES), row_tiles, 0)
    y = _layer_norm(y_ref[...], g_ref[...], b_ref[...])
    o_ref[...] = jax.nn.silu(y).astype(o_ref.dtype)


def _conv_prompt(u, conv_w, conv_b, ln_g, ln_b):
    t = u.shape[0]
    vec = pl.BlockSpec((1, D_CONV), lambda i: (0, 0))
    return pl.pallas_call(
        _conv_prompt_body,
        grid=(t // CONV_TILE,),
        in_specs=[pl.BlockSpec((CONV_TILE, D_CONV), lambda i: (i, 0)),
                  pl.BlockSpec((CONV_TILE, D_CONV), lambda i: (jnp.maximum(i - 1, 0), 0)),
                  pl.BlockSpec((CONV_WIDTH, D_CONV), lambda i: (0, 0)),
                  vec, vec, vec],
        out_specs=pl.BlockSpec((CONV_TILE, D_CONV), lambda i: (i, 0)),
        out_shape=jax.ShapeDtypeStruct((t, D_CONV), BF16),
        scratch_shapes=[pltpu.VMEM((CONV_HALO + CONV_TILE, D_CONV), F32),
                        pltpu.VMEM((SUBLANES - 1, CONV_HALO + CONV_TILE - SUBLANES, LANES), F32),
                        pltpu.VMEM((CONV_WIDTH, SUBLANES, D_CONV), F32),
                        pltpu.VMEM((CONV_TILE, D_CONV), F32)],
        compiler_params=_params(("parallel",), 32),
        name="conv_prompt",
    )(u, u, conv_w, conv_b.reshape(1, D_CONV), ln_g.reshape(1, D_CONV), ln_b.reshape(1, D_CONV))


def _conv_sample_body(st_ref, u_ref, w_ref, cb_ref, g_ref, b_ref, o_ref):
    acc = u_ref[...] * w_ref[CONV_STATE:CONV_WIDTH, :]
    for tap in range(CONV_STATE):
        acc = acc + st_ref[tap] * w_ref[tap:tap + 1, :]
    y = _layer_norm(acc + cb_ref[...], g_ref[...], b_ref[...])
    o_ref[...] = jax.nn.silu(y).astype(o_ref.dtype)


def _conv_sample(state_t, u, conv_w, conv_b, ln_g, ln_b):
    n_req = u.shape[0]
    return pl.pallas_call(
        _conv_sample_body,
        out_shape=jax.ShapeDtypeStruct((n_req, D_CONV), BF16),
        name="conv_sample",
    )(state_t, u, conv_w, conv_b.reshape(1, D_CONV), ln_g.reshape(1, D_CONV),
      ln_b.reshape(1, D_CONV))


def kernel(x_prompt, x_sample, mem_prompt, cache_k, cache_v, state_conv, cache_mem_k, cache_mem_v,
           page_table, w_in, conv_w, conv_b, conv_ln_g, conv_ln_b, w_mem_kv, rel_bias, w_branch,
           w_out, ln1_g, ln1_b, w_gate, w_up, w_down, ln2_g, ln2_b):
    depth = w_in.shape[0]
    assert depth == 1 and x_prompt.shape[0] == 1 and x_sample.shape[1] == 1
    alpha = (2 * depth) ** 0.25
    t, d_model = x_prompt.shape[1], x_prompt.shape[2]
    n_req, n_pages = page_table.shape
    page = cache_k.shape[2]
    n_mem = mem_prompt.shape[1]
    assert (n_pages * page) % MOBA_BLOCK == 0

    sizes = (D_CONV, D_CONV, ATTN_WIDTH, ATTN_WIDTH, ATTN_WIDTH, CROSS_WIDTH, N_BRANCH * d_model)
    col_a, col_b, col_q, col_k, col_v, col_qc, col_g = (
        int(c) for c in np.concatenate([[0], np.cumsum(sizes)[:-1]]))
    w_o = w_out[0]
    w_d = w_down[0].astype(BF16)
    w_br = w_branch[0].astype(BF16)
    d_ff = w_gate.shape[2]

    tables = _bias_tables(rel_bias)

    xp = x_prompt[0]
    xs = x_sample[:, 0]
    q_p, q_s, xp_bf, xs_bf = _proj(xp, xs, [(w_in, col_q)], ATTN_WIDTH, "none", [F32],
                                   bn=ATTN_WIDTH, vmem_mib=56, keep_input=True)
    u_p, u_s = _proj(xp_bf, xs_bf, [(w_in, col_a), (w_in, col_b)], D_CONV, "glu", [F32])
    k_p, v_p, qc_p, k_s, v_s, qc_s = _proj(
        xp_bf, xs_bf, [(w_in, col_k), (w_in, col_v), (w_in, col_qc)], ATTN_WIDTH, "each",
        [F32, F32, BF16], vmem_mib=56)
    gates_p, gates_s = _proj(xp_bf, xs_bf, [(w_in, col_g)], N_BRANCH * d_model, "sigmoid",
                             [BF16], bn=1024)

    conv_y_p = _conv_prompt(u_p, conv_w[0], conv_b[0], conv_ln_g[0], conv_ln_b[0])
    k_aug, v_pad = _moba_prep(k_p, v_p)
    pt_flat = page_table.reshape(-1)
    attn_p, k_mean_s = _moba_prompt(_moba_select(q_p, _block_mean(k_p)), k_aug, v_pad, tables,
                                    cache_k, pt_flat)
    mem_bf = mem_prompt[0].astype(BF16)
    mk, mv = _proj(mem_bf, None, [(w_mem_kv, 0), (w_mem_kv, CROSS_WIDTH)], CROSS_WIDTH, "each",
                   [F32, F32])
    cross_p = _cross_prompt(qc_p, mk, mv)

    conv_y_s = _conv_sample(state_conv[0].transpose(1, 0, 2), u_s, conv_w[0], conv_b[0],
                            conv_ln_g[0], conv_ln_b[0])
    sel = _sample_select(q_s.reshape(n_req, N_HEADS, HEAD_DIM),
                         k_mean_s.reshape(n_req, -1, N_HEADS, HEAD_DIM))
    attn_s, cross_s = _sample_attn(
        q_s.reshape(n_req, 1, ATTN_WIDTH), k_s.reshape(n_req, 1, ATTN_WIDTH),
        v_s.reshape(n_req, 1, ATTN_WIDTH), qc_s.reshape(n_req, 1, CROSS_WIDTH),
        cache_k, cache_v, cache_mem_k, cache_mem_v, pt_flat, sel.reshape(-1), tables, n_pages)

    mix_p, mix_s = _branch_mix(
        (conv_y_p, attn_p, cross_p),
        (conv_y_s, attn_s.reshape(n_req, ATTN_WIDTH), cross_s.reshape(n_req, CROSS_WIDTH)),
        w_br, gates_p, gates_s)
    h1_p, h1_p_bf, h1_s, h1_s_bf = _mm_res_ln(mix_p, mix_s, w_o, xp, xs, ln1_g[0], ln1_b[0], alpha,
                                              [F32, BF16], bm=512)
    act_p, act_s = _proj(h1_p_bf, h1_s_bf, [(w_gate, 0), (w_up, 0)], d_ff, "swiglu", [BF16])
    y_p, y_s = _mm_res_ln(act_p, act_s, w_d, h1_p, h1_s, ln2_g[0], ln2_b[0], alpha, [F32],
                          bm=512)

    new_conv_sample = jnp.concatenate([state_conv[0][:, 1:], u_s[:, None, :]], axis=1)
    return (
        y_p[None],
        y_s[:, None],
        k_p.reshape(1, 1, t, N_HEADS, HEAD_DIM),
        v_p.reshape(1, 1, t, N_HEADS, HEAD_DIM),
        u_p[t - CONV_STATE:][None, None],
        mk.reshape(1, 1, n_mem, CROSS_HEADS, CROSS_HEAD_DIM),
        mv.reshape(1, 1, n_mem, CROSS_HEADS, CROSS_HEAD_DIM),
        k_s.reshape(1, n_req, 1, N_HEADS, HEAD_DIM),
        v_s.reshape(1, n_req, 1, N_HEADS, HEAD_DIM),
        new_conv_sample[None],
    )
```
